```python
import jax, jax.numpy as jnp
from jax import lax
import numpy as np

D_MODEL = 1024
BATCH = 16
SEQ = 2048
DEPTH = 1
DEC_BATCH = 32
DEC_SEQ = 4
PAST_LEN = 16384
PAGE_SIZE = 128

C_CONV = D_MODEL // 2
CONV_K = 31
N_HEADS = 8
HEAD_DIM = D_MODEL // 16
ATT_DIM = N_HEADS * HEAD_DIM
N_KV = 2
Q_PER_KV = N_HEADS // N_KV
KV_DIM = N_KV * HEAD_DIM
CMP_LEN = 32
CMP_STRIDE = 16
CMP_HID = 2 * HEAD_DIM
SEL_LEN = 64
SEL_RATIO = SEL_LEN // CMP_STRIDE
N_SEL = 16
WINDOW = 512
Q_BLOCK = 128
N_GATES = 3
N_EXPERTS = 32
TOP_K = 4
D_FF = D_MODEL
SWIGLU_LIMIT = 7.0
SWIGLU_ALPHA = 1.702
MOE_BLOCK = 256
N_IN = 2 * C_CONV + ATT_DIM + 6 * KV_DIM + N_GATES * N_HEADS
DN_ALPHA = (2 * DEPTH) ** 0.25
DN_BETA = (8 * DEPTH) ** -0.25
LN_EPS = 1e-5
NEG_INF = -1e30
FORCE = 1e9

kernel_name = 'hymba_conformer_nsa_moe_step'


def layer_norm(x, g, b):
    xf = x.astype(jnp.float32)
    mu = jnp.mean(xf, -1, keepdims=True)
    var = jnp.mean(jnp.square(xf - mu), -1, keepdims=True)
    return ((xf - mu) * lax.rsqrt(var + LN_EPS) * g + b).astype(x.dtype)


def alibi_slopes():
    return jnp.asarray([2.0 ** (-8.0 * (h + 1) / N_HEADS) for h in range(N_HEADS)], jnp.float32)


def masked_softmax(s, mask):
    s = jnp.where(mask, s, NEG_INF)
    e = jnp.exp(s - jnp.max(s, -1, keepdims=True)) * mask
    return e / jnp.maximum(jnp.sum(e, -1, keepdims=True), 1e-30)


def compress_blocks(rows, w1, w2, pe):
    b, t, g, d = rows.shape
    n_ch = t // CMP_STRIDE
    ch = rows.reshape(b, n_ch, CMP_STRIDE, g, d).transpose(0, 1, 3, 2, 4).reshape(b, n_ch, g, CMP_STRIDE * d)
    half = CMP_STRIDE * d
    lo = ch @ w1[:half]
    hi = ch @ w1[half:]
    h = lo[:, :-1] + hi[:, 1:] + pe.reshape(-1) @ w1
    return jax.nn.gelu(h) @ w2


def nsa_attend(q, gates, kc, vc, ks, vs, kw, vw, q_pos0, w_ck1, w_ck2, pe_k, w_cv1, w_cv2, pe_v):
    bsz, tq = q.shape[0], q.shape[1]
    tk = kc.shape[1]
    t_pad = -(-tk // SEL_LEN) * SEL_LEN
    pad = ((0, 0), (0, t_pad - tk), (0, 0), (0, 0))
    ck = compress_blocks(jnp.pad(kc, pad), w_ck1, w_ck2, pe_k)
    cv = compress_blocks(jnp.pad(vc, pad), w_cv1, w_cv2, pe_v)
    n_cmp = ck.shape[1]
    cmp_end = jnp.arange(n_cmp) * CMP_STRIDE + CMP_LEN - 1
    n_blk = t_pad // SEL_LEN
    k_sel = jnp.pad(ks, pad).reshape(bsz, n_blk, SEL_LEN, N_KV, HEAD_DIM).transpose(0, 3, 1, 2, 4)
    v_sel = jnp.pad(vs, pad).reshape(bsz, n_blk, SEL_LEN, N_KV, HEAD_DIM).transpose(0, 3, 1, 2, 4)
    n_pick = min(N_SEL, n_blk)
    qb = min(Q_BLOCK, tq)
    n_qb = tq // qb
    slopes = alibi_slopes().reshape(N_KV, Q_PER_KV)[None, :, :, None]
    scale = HEAD_DIM ** -0.5
    blk_ids = jnp.arange(n_blk)[None, :]
    g_ids = jnp.arange(N_KV)[None, :, None]

    def one_block(item):
        b, j = item
        qs = j * qb
        qh = lax.dynamic_slice(q, (b, qs, 0, 0), (1, qb, N_HEADS, HEAD_DIM))[0]
        qh = qh.astype(jnp.float32).reshape(qb, N_KV, Q_PER_KV, HEAD_DIM) * scale
        gb = lax.dynamic_slice(gates, (b, qs, 0, 0), (1, qb, N_HEADS, N_GATES))[0]
        gb = gb.reshape(qb, N_KV, Q_PER_KV, N_GATES)
        t = q_pos0 + qs + jnp.arange(qb)
        ckb, cvb = ck[b], cv[b]
        dist = t[:, None] - cmp_end[None, :]
        s = jnp.einsum('qgrd,cgd->qgrc', qh, ckb) - slopes * dist.astype(jnp.float32)[:, None, None, :]
        p_cmp = masked_softmax(s, (dist >= 0)[:, None, None, :])
        o_cmp = jnp.einsum('qgrc,cgd->qgrd', p_cmp, cvb)
        p_pad = jnp.pad(jnp.sum(p_cmp, 2), ((0, 0), (0, 0), (1, 1)))
        imp = p_pad[..., :SEL_RATIO * n_blk].reshape(qb, N_KV, n_blk, SEL_RATIO).sum(-1) + p_pad[..., SEL_RATIO::SEL_RATIO]
        cur = (t // SEL_LEN)[:, None]
        forced = (blk_ids == 0) | (blk_ids == cur) | (blk_ids == cur - 1)
        future = blk_ids * SEL_LEN > t[:, None]
        imp = jnp.where(future[:, None, :], -FORCE, jnp.where(forced[:, None, :], FORCE, imp))
        _, pick = lax.top_k(imp, n_pick)
        kb = k_sel[b][g_ids, pick].reshape(qb, N_KV, n_pick * SEL_LEN, HEAD_DIM)
        vb = v_sel[b][g_ids, pick].reshape(qb, N_KV, n_pick * SEL_LEN, HEAD_DIM)
        kpos = (pick[..., None] * SEL_LEN + jnp.arange(SEL_LEN)).reshape(qb, N_KV, n_pick * SEL_LEN)
        dist = t[:, None, None] - kpos
        s = jnp.einsum('qgrd,qgsd->qgrs', qh, kb) - slopes * dist.astype(jnp.float32)[:, :, None, :]
        p = masked_softmax(s, (dist >= 0)[:, :, None, :])
        o_slc = jnp.einsum('qgrs,qgsd->qgrd', p, vb)
        kwb = lax.dynamic_slice_in_dim(kw[b], qs, WINDOW + qb, 0)
        vwb = lax.dynamic_slice_in_dim(vw[b], qs, WINDOW + qb, 0)
        wpos = q_pos0 - WINDOW + qs + jnp.arange(WINDOW + qb)
        dist = t[:, None] - wpos[None, :]
        wmask = (dist >= 0) & (dist <= WINDOW) & (wpos >= 0)[None, :]
        s = jnp.einsum('qgrd,wgd->qgrw', qh, kwb) - slopes * dist.astype(jnp.float32)[:, None, None, :]
        p = masked_softmax(s, wmask[:, None, None, :])
        o_win = jnp.einsum('qgrw,wgd->qgrd', p, vwb)
        o = gb[..., 0:1] * o_cmp + gb[..., 1:2] * o_slc + gb[..., 2:3] * o_win
        return o.reshape(qb, N_HEADS, HEAD_DIM).astype(q.dtype)

    items = jnp.arange(bsz * n_qb)
    out = lax.map(one_block, (items // n_qb, items % n_qb))
    return out.reshape(bsz, tq, N_HEADS, HEAD_DIM)


def moe_ffn(h, w_router, b_router, w_gate_up, b_gate_up, w_down, b_down):
    n_tok, d = h.shape
    logits = (h @ w_router + b_router).astype(jnp.float32)
    top_val, top_idx = lax.top_k(logits, TOP_K)
    top_w = jax.nn.softmax(top_val, axis=-1)
    n_asg = n_tok * TOP_K
    bm = max(8, min(MOE_BLOCK, n_asg // N_EXPERTS))
    n_blocks = -(-n_asg // bm) + N_EXPERTS
    flat_e = top_idx.reshape(-1)
    order = jnp.argsort(flat_e)
    e_sorted = flat_e[order]
    tok_sorted = order // TOP_K
    counts = jnp.bincount(flat_e, length=N_EXPERTS)
    padded = (counts + bm - 1) // bm * bm
    pad_end = jnp.cumsum(padded)
    pad_start = pad_end - padded
    start = jnp.cumsum(counts) - counts
    dest = pad_start[e_sorted] + jnp.arange(n_asg) - start[e_sorted]
    slot_tok = jnp.full((n_blocks * bm,), n_tok, jnp.int32).at[dest].set(tok_sorted.astype(jnp.int32))
    h_pad = jnp.concatenate([h, jnp.zeros((1, d), h.dtype)], 0)
    xs = h_pad[slot_tok].reshape(n_blocks, bm, d)
    blk_e = jnp.minimum(jnp.searchsorted(pad_end, jnp.arange(n_blocks) * bm, side='right'), N_EXPERTS - 1)

    def expert_block(args):
        xb, e = args
        gu = xb @ w_gate_up[e] + b_gate_up[e]
        gl, lin = jnp.split(gu, 2, axis=-1)
        gl = jnp.minimum(gl, SWIGLU_LIMIT)
        lin = jnp.clip(lin, -SWIGLU_LIMIT, SWIGLU_LIMIT)
        return (gl * jax.nn.sigmoid(SWIGLU_ALPHA * gl) * (lin + 1.0)) @ w_down[e] + b_down[e]

    ys = lax.map(expert_block, (xs, blk_e)).reshape(n_blocks * bm, d)
    w_sorted = top_w.reshape(-1)[order].astype(h.dtype)
    return jax.ops.segment_sum(ys[dest] * w_sorted[:, None], tok_sorted, num_segments=n_tok)


def trunk_layer(x, c, conv_buf, past_ck, past_cv, past_sk, past_sv, win_k, win_v, pos0, win_keep, p):
    bsz, t_len, _ = x.shape
    mod = (jax.nn.silu(c) @ p['w_ada'] + p['b_ada'])[:, None, :]
    shift1, scale1, gate1, shift2, scale2, gate2 = jnp.split(mod, 6, axis=-1)
    u = x * (1.0 + scale1) + shift1
    z = u @ p['w_in']
    sizes = [C_CONV, C_CONV, ATT_DIM] + [KV_DIM] * 6
    cuts = [sum(sizes[:i + 1]) for i in range(len(sizes))]
    a_val, a_gate, q, kc, vc, ks, vs, kw, vw, g = jnp.split(z, cuts, axis=-1)
    glu = a_val * jax.nn.sigmoid(a_gate)
    xc = jnp.concatenate([conv_buf.astype(glu.dtype), glu], 1)
    conv = lax.conv_general_dilated(xc, p['w_dw'][:, None, :].astype(xc.dtype), (1,), 'VALID',
                                    dimension_numbers=('NWC', 'WIO', 'NWC'), feature_group_count=C_CONV) + p['b_dw']
    conv_out = jax.nn.silu(layer_norm(conv, p['conv_ln_g'], p['conv_ln_b']))
    new_conv = xc[:, -(CONV_K - 1):]
    heads = lambda a, n: a.reshape(bsz, t_len, n, HEAD_DIM)
    q = heads(q, N_HEADS)
    kc, vc, ks, vs, kw, vw = [heads(a, N_KV) for a in (kc, vc, ks, vs, kw, vw)]
    gates = jax.nn.sigmoid(g).reshape(bsz, t_len, N_HEADS, N_GATES)
    kw_all = jnp.concatenate([win_k.astype(kw.dtype), kw], 1)
    vw_all = jnp.concatenate([win_v.astype(vw.dtype), vw], 1)
    att = nsa_attend(q, gates,
                     jnp.concatenate([past_ck.astype(kc.dtype), kc], 1), jnp.concatenate([past_cv.astype(vc.dtype), vc], 1),
                     jnp.concatenate([past_sk.astype(ks.dtype), ks], 1), jnp.concatenate([past_sv.astype(vs.dtype), vs], 1),
                     kw_all, vw_all, pos0,
                     p['w_ck1'], p['w_ck2'], p['pe_k'], p['w_cv1'], p['w_cv2'], p['pe_v'])
    merged = jnp.concatenate([conv_out * p['beta_conv'], att.reshape(bsz, t_len, ATT_DIM) * p['beta_attn']], -1)
    x1 = layer_norm(DN_ALPHA * x + gate1 * (merged @ p['w_out']), p['ln1_g'], p['ln1_b'])
    u2 = x1 * (1.0 + scale2) + shift2
    f = moe_ffn(u2.reshape(bsz * t_len, D_MODEL), p['w_router'], p['b_router'], p['w_gate_up'],
                p['b_gate_up'], p['w_down'], p['b_down']).reshape(bsz, t_len, D_MODEL)
    y = layer_norm(DN_ALPHA * x1 + gate2 * f, p['ln2_g'], p['ln2_b'])
    return y, (kc, vc, ks, vs, kw_all[:, -win_keep:], vw_all[:, -win_keep:], new_conv)


def setup_inputs(seed: int = 0) -> dict:
    key = jax.random.key(seed)
    keys = iter(jax.random.split(key, 48))
    nrm = lambda shape, s: jax.random.normal(next(keys), shape, jnp.float32) * s
    n_pages = PAST_LEN // PAGE_SIZE
    n_used = DEC_BATCH * n_pages
    n_pool = n_used + max(1, n_used // 4)
    win_buf = min(WINDOW, PAST_LEN)
    L = DEPTH
    page_table = jax.random.permutation(next(keys), n_pool)[:n_used].reshape(DEC_BATCH, n_pages).astype(jnp.int32)
    return {
        'x_prompt': nrm((BATCH, SEQ, D_MODEL), 1.0),
        'x_sample': nrm((DEC_BATCH, DEC_SEQ, D_MODEL), 1.0),
        'c_prompt': nrm((BATCH, D_MODEL), 1.0),
        'c_sample': nrm((DEC_BATCH, D_MODEL), 1.0),
        'cache_cmp_k': nrm((L, n_pool, PAGE_SIZE, N_KV, HEAD_DIM), 1.0),
        'cache_cmp_v': nrm((L, n_pool, PAGE_SIZE, N_KV, HEAD_DIM), 1.0),
        'cache_slc_k': nrm((L, n_pool, PAGE_SIZE, N_KV, HEAD_DIM), 1.0),
        'cache_slc_v': nrm((L, n_pool, PAGE_SIZE, N_KV, HEAD_DIM), 1.0),
        'state_win_k': nrm((L, DEC_BATCH, win_buf, N_KV, HEAD_DIM), 1.0),
        'state_win_v': nrm((L, DEC_BATCH, win_buf, N_KV, HEAD_DIM), 1.0),
        'state_conv': nrm((L, DEC_BATCH, CONV_K - 1, C_CONV), 0.5),
        'page_table': page_table,
        'w_ada': nrm((L, D_MODEL, 6 * D_MODEL), 0.5 * D_MODEL ** -0.5),
        'b_ada': nrm((L, 6 * D_MODEL), 0.02),
        'w_in': nrm((L, D_MODEL, N_IN), D_MODEL ** -0.5),
        'w_dw': nrm((L, CONV_K, C_CONV), CONV_K ** -0.5),
        'b_dw': nrm((L, C_CONV), 0.02),
        'conv_ln_g': 1.0 + nrm((L, C_CONV), 0.02),
        'conv_ln_b': nrm((L, C_CONV), 0.02),
        'w_ck1': nrm((L, CMP_LEN * HEAD_DIM, CMP_HID), (CMP_LEN * HEAD_DIM) ** -0.5),
        'w_ck2': nrm((L, CMP_HID, HEAD_DIM), CMP_HID ** -0.5),
        'pe_k': nrm((L, CMP_LEN, HEAD_DIM), 0.1),
        'w_cv1': nrm((L, CMP_LEN * HEAD_DIM, CMP_HID), (CMP_LEN * HEAD_DIM) ** -0.5),
        'w_cv2': nrm((L, CMP_HID, HEAD_DIM), CMP_HID ** -0.5),
        'pe_v': nrm((L, CMP_LEN, HEAD_DIM), 0.1),
        'beta_conv': 1.0 + nrm((L, C_CONV), 0.02),
        'beta_attn': 1.0 + nrm((L, ATT_DIM), 0.02),
        'w_out': nrm((L, D_MODEL, D_MODEL), DN_BETA * D_MODEL ** -0.5),
        'ln1_g': 1.0 + nrm((L, D_MODEL), 0.02),
        'ln1_b': nrm((L, D_MODEL), 0.02),
        'w_router': nrm((L, D_MODEL, N_EXPERTS), D_MODEL ** -0.5),
        'b_router': nrm((L, N_EXPERTS), 0.01),
        'w_gate_up': nrm((L, N_EXPERTS, D_MODEL, 2 * D_FF), D_MODEL ** -0.5),
        'b_gate_up': nrm((L, N_EXPERTS, 2 * D_FF), 0.02),
        'w_down': nrm((L, N_EXPERTS, D_FF, D_MODEL), DN_BETA * D_FF ** -0.5),
        'b_down': nrm((L, N_EXPERTS, D_MODEL), 0.02),
        'ln2_g': 1.0 + nrm((L, D_MODEL), 0.02),
        'ln2_b': nrm((L, D_MODEL), 0.02),
    }


def reference(x_prompt, x_sample, c_prompt, c_sample, cache_cmp_k, cache_cmp_v, cache_slc_k, cache_slc_v,
              state_win_k, state_win_v, state_conv, page_table, w_ada, b_ada, w_in, w_dw, b_dw, conv_ln_g,
              conv_ln_b, w_ck1, w_ck2, pe_k, w_cv1, w_cv2, pe_v, beta_conv, beta_attn, w_out, ln1_g, ln1_b,
              w_router, b_router, w_gate_up, b_gate_up, w_down, b_down, ln2_g, ln2_b):
    n_dec = x_sample.shape[0]
    n_pro, seq = x_prompt.shape[0], x_prompt.shape[1]
    past_len = page_table.shape[1] * PAGE_SIZE
    win_buf = state_win_k.shape[2]

    def gather_past(pool):
        return pool[page_table].reshape(n_dec, past_len, N_KV, HEAD_DIM)

    def pad_window(buf):
        return jnp.pad(buf, ((0, 0), (WINDOW - win_buf, 0), (0, 0), (0, 0)))

    dt = x_prompt.dtype
    empty = jnp.zeros((n_pro, 0, N_KV, HEAD_DIM), dt)
    win0 = jnp.zeros((n_pro, WINDOW, N_KV, HEAD_DIM), dt)
    conv0 = jnp.zeros((n_pro, CONV_K - 1, C_CONV), dt)
    xp, xs = x_prompt, x_sample
    st_p = [[] for _ in range(7)]
    st_s = [[] for _ in range(7)]
    for l in range(DEPTH):
        p = {'w_ada': w_ada[l], 'b_ada': b_ada[l], 'w_in': w_in[l], 'w_dw': w_dw[l], 'b_dw': b_dw[l],
             'conv_ln_g': conv_ln_g[l], 'conv_ln_b': conv_ln_b[l], 'w_ck1': w_ck1[l], 'w_ck2': w_ck2[l],
             'pe_k': pe_k[l], 'w_cv1': w_cv1[l], 'w_cv2': w_cv2[l], 'pe_v': pe_v[l], 'beta_conv': beta_conv[l],
             'beta_attn': beta_attn[l], 'w_out': w_out[l], 'ln1_g': ln1_g[l], 'ln1_b': ln1_b[l],
             'w_router': w_router[l], 'b_router': b_router[l], 'w_gate_up': w_gate_up[l],
             'b_gate_up': b_gate_up[l], 'w_down': w_down[l], 'b_down': b_down[l], 'ln2_g': ln2_g[l],
             'ln2_b': ln2_b[l]}
        xp, new_p = trunk_layer(xp, c_prompt, conv0, empty, empty, empty, empty, win0, win0,
                                0, min(WINDOW, seq), p)
        xs, new_s = trunk_layer(xs, c_sample, state_conv[l], gather_past(cache_cmp_k[l]), gather_past(cache_cmp_v[l]),
                                gather_past(cache_slc_k[l]), gather_past(cache_slc_v[l]),
                                pad_window(state_win_k[l]), pad_window(state_win_v[l]), past_len, win_buf, p)
        for i in range(7):
            st_p[i].append(new_p[i])
            st_s[i].append(new_s[i])
    ck_p, cv_p, sk_p, sv_p, wk_p, wv_p, conv_p = [jnp.stack(a) for a in st_p]
    ck_s, cv_s, sk_s, sv_s, wk_s, wv_s, conv_s = [jnp.stack(a) for a in st_s]
    return (xp, xs, ck_p, cv_p, sk_p, sv_p, wk_p, wv_p, conv_p, ck_s, cv_s, sk_s, sv_s, wk_s, wv_s, conv_s)
```

```python
import functools

import jax
import jax.numpy as jnp
from jax import lax
from jax.experimental import pallas as pl
from jax.experimental.pallas import tpu as pltpu

F32 = jnp.float32
BF16 = jnp.bfloat16
I32 = jnp.int32

N_HEADS = 8
HEAD_DIM = 64
N_KV = 2
Q_PER_KV = N_HEADS // N_KV
KV_DIM = N_KV * HEAD_DIM
CMP_LEN = 32
CMP_STRIDE = 16
CMP_HID = 2 * HEAD_DIM
SEL_LEN = 64
SEL_RATIO = SEL_LEN // CMP_STRIDE
N_SEL = 16
WINDOW = 512
Q_BLOCK = 128
N_GATES = 3
N_EXPERTS = 32
TOP_K = 4
CONV_K = 31
SWIGLU_LIMIT = 7.0
SWIGLU_ALPHA = 1.702
LN_EPS = 1e-5
NEG_INF = -1e30
FORCE = 1e9
SLOPES = tuple(2.0 ** (-8.0 * (h + 1) / N_HEADS) for h in range(N_HEADS))

LANES = 128
SUBLANES = 8
VMEM_LIMIT_BYTES = 56 * 1024 * 1024

CONV_HIST = 32
CHUNK_ROW = CMP_STRIDE * KV_DIM
SLC_CHUNK = 512
DEC_CHUNK = 2048
DEC_Q = 8
HIGHEST = lax.Precision.HIGHEST


def _cparams(*sem):
    return pltpu.CompilerParams(dimension_semantics=sem, vmem_limit_bytes=VMEM_LIMIT_BYTES)


def _sigmoid(x):
    return 1.0 / (1.0 + jnp.exp(-x))


def _bdot(a, b):
    return jnp.dot(a.astype(BF16), b.astype(BF16), preferred_element_type=F32)


def _bdot_nt(a, b):
    return lax.dot_general(a.astype(BF16), b.astype(BF16), (((1,), (1,)), ((), ())), preferred_element_type=F32)


def _layer_norm(x, g, b):
    mu = jnp.mean(x, -1, keepdims=True)
    xc = x - mu
    var = jnp.mean(xc * xc, -1, keepdims=True)
    return xc * lax.rsqrt(var + LN_EPS) * g + b


def _gelu_tanh(x):
    return 0.5 * x * (1.0 + jnp.tanh(0.7978845608028654 * (x + 0.044715 * (x * x * x))))


def _masked_softmax(s, mask):
    s = jnp.where(mask, s, NEG_INF)
    e = jnp.where(mask, jnp.exp(s - jnp.max(s, -1, keepdims=True)), 0.0)
    return e * (1.0 / jnp.maximum(jnp.sum(e, -1, keepdims=True), 1e-30))


def _ada_kernel(c_ref, w_ref, b_ref, o_ref):
    c = c_ref[...]
    o_ref[...] = _bdot(c * _sigmoid(c), w_ref[...]) + b_ref[...]


def _ada(c_all, w_ada, b_ada):
    nb, d = c_all.shape
    n = w_ada.shape[1]
    bn = d
    return pl.pallas_call(
        _ada_kernel,
        grid=(n // bn,),
        in_specs=[pl.BlockSpec((nb, d), lambda i: (0, 0)),
                  pl.BlockSpec((d, bn), lambda i: (0, i)),
                  pl.BlockSpec((1, bn), lambda i: (0, i))],
        out_specs=pl.BlockSpec((nb, bn), lambda i: (0, i)),
        out_shape=jax.ShapeDtypeStruct((nb, n), F32),
        compiler_params=_cparams("arbitrary"),
        name="ada",
    )(c_all, w_ada, b_ada)


def _mod_rows(mod_ref, i, per_row):
    return mod_ref[i] if per_row else mod_ref[0, i:i + 1, :]


def _conv_ln_silu(xc_ref, row0, n, wdw_ref, bdw, lng, lnb):
    acc = jnp.zeros((n, xc_ref.shape[1]), F32) + bdw
    off = CONV_HIST - (CONV_K - 1)
    for k in range(CONV_K):
        acc = acc + xc_ref[pl.ds(row0 + off + k, n), :] * wdw_ref[k:k + 1, :]
    y = _layer_norm(acc, lng, lnb)
    return y * _sigmoid(y)


def _in_kernel(*refs, tt, c_conv, fuse_conv, per_row):
    if fuse_conv:
        (x_ref, mod_ref, wa_ref, wq_ref, wkv_ref, wg_ref, hist_ref, wdw_ref, bdw_ref, lng_ref, lnb_ref,
         conv_ref, tail_ref, q_ref, kc_ref, vc_ref, ks_ref, vs_ref, kw_ref, vw_ref, gates_ref, xc_ref) = refs
    else:
        (x_ref, mod_ref, wa_ref, wq_ref, wkv_ref, wg_ref,
         glu_ref, q_ref, kc_ref, vc_ref, ks_ref, vs_ref, kw_ref, vw_ref, gates_ref) = refs
    x = x_ref[0]
    shift1 = _mod_rows(mod_ref, 0, per_row)
    scale1 = _mod_rows(mod_ref, 1, per_row)
    u = (x * (1.0 + scale1) + shift1).astype(BF16)
    a = jnp.dot(u, wa_ref[...], preferred_element_type=F32)
    glu = a[:, :c_conv] * _sigmoid(a[:, c_conv:])
    q_ref[0] = jnp.dot(u, wq_ref[...], preferred_element_type=F32)
    kv = jnp.dot(u, wkv_ref[...], preferred_element_type=F32)
    for i, r in enumerate((kc_ref, vc_ref, ks_ref, vs_ref, kw_ref, vw_ref)):
        r[0] = kv[:, i * KV_DIM:(i + 1) * KV_DIM]
    gates_ref[0] = _sigmoid(jnp.dot(u, wg_ref[...], preferred_element_type=F32))
    if not fuse_conv:
        glu_ref[0] = glu
        return

    @pl.when(pl.program_id(1) == 0)
    def _():
        xc_ref[0:CONV_HIST, :] = hist_ref[0]

    xc_ref[CONV_HIST:CONV_HIST + tt, :] = glu
    rows = min(tt, 64)
    for r0 in range(0, tt, rows):
        conv_ref[0, r0:r0 + rows, :] = _conv_ln_silu(xc_ref, r0, rows, wdw_ref, bdw_ref[...], lng_ref[...], lnb_ref[...])
    tail = xc_ref[tt:tt + CONV_HIST, :]
    xc_ref[0:CONV_HIST, :] = tail
    tail_ref[0] = tail


def _in_proj(x, mod, wts, conv=None, *, tt, per_row):
    b, t, d = x.shape
    wa, wq, wkv, wg = wts
    c_conv = wa.shape[1] // 2
    fuse = conv is not None
    nt = t // tt
    const = lambda shape: pl.BlockSpec(shape, lambda i, j: (0,) * len(shape))
    mod_spec = (pl.BlockSpec((6, tt, d), lambda i, j: (0, j, 0)) if per_row
                else pl.BlockSpec((1, 6, d), lambda i, j: (i, 0, 0)))
    in_specs = [pl.BlockSpec((1, tt, d), lambda i, j: (i, j, 0)), mod_spec,
                const(wa.shape), const(wq.shape), const(wkv.shape), const(wg.shape)]
    args = [x, mod, wa, wq, wkv, wg]
    tok = lambda n: pl.BlockSpec((1, tt, n), lambda i, j: (i, j, 0))
    tok_shape = lambda n: jax.ShapeDtypeStruct((b, t, n), F32)
    out_specs, out_shape, scratch = [], [], []
    if fuse:
        hist, wdw, bdw, lng, lnb = conv
        in_specs += [pl.BlockSpec((1, CONV_HIST, c_conv), lambda i, j: (i, 0, 0)),
                     const(wdw.shape), const(bdw.shape), const(lng.shape), const(lnb.shape)]
        args += [hist, wdw, bdw, lng, lnb]
        out_specs += [tok(c_conv), pl.BlockSpec((1, CONV_HIST, c_conv), lambda i, j: (i, 0, 0))]
        out_shape += [tok_shape(c_conv), jax.ShapeDtypeStruct((b, CONV_HIST, c_conv), F32)]
        scratch = [pltpu.VMEM((CONV_HIST + tt, c_conv), F32)]
    else:
        out_specs += [tok(c_conv)]
        out_shape += [tok_shape(c_conv)]
    out_specs += [tok(wq.shape[1])] + [tok(KV_DIM)] * 6 + [tok(wg.shape[1])]
    out_shape += [tok_shape(wq.shape[1])] + [tok_shape(KV_DIM)] * 6 + [tok_shape(wg.shape[1])]
    return pl.pallas_call(
        functools.partial(_in_kernel, tt=tt, c_conv=c_conv, fuse_conv=fuse, per_row=per_row),
        grid=(b, nt), in_specs=in_specs, out_specs=out_specs, out_shape=out_shape, scratch_shapes=scratch,
        compiler_params=_cparams("arbitrary", "arbitrary"),
        name="in_proj_conv" if fuse else "in_proj",
    )(*args)


def _conv_small_kernel(xc_ref, wdw_ref, bdw_ref, lng_ref, lnb_ref, o_ref, *, tt):
    o_ref[0] = _conv_ln_silu(xc_ref.at[0], 0, tt, wdw_ref, bdw_ref[...], lng_ref[...], lnb_ref[...])


def _conv_small(xc, wdw, bdw, lng, lnb, *, tt):
    b, rows, c = xc.shape
    const = lambda shape: pl.BlockSpec(shape, lambda i: (0,) * len(shape))
    return pl.pallas_call(
        functools.partial(_conv_small_kernel, tt=tt),
        grid=(b,),
        in_specs=[pl.BlockSpec((1, rows, c), lambda i: (i, 0, 0)),
                  const(wdw.shape), const(bdw.shape), const(lng.shape), const(lnb.shape)],
        out_specs=pl.BlockSpec((1, tt, c), lambda i: (i, 0, 0)),
        out_shape=jax.ShapeDtypeStruct((b, tt, c), F32),
        compiler_params=_cparams("arbitrary"),
        name="conv_small",
    )(xc, wdw, bdw, lng, lnb)


def _compress_weights(w1, w2, pe):
    half = CMP_STRIDE * HEAD_DIM

    def expand(w):
        w = w.reshape(CMP_STRIDE, HEAD_DIM, CMP_HID)
        z = jnp.zeros_like(w)
        g0 = jnp.concatenate([w, z], axis=1).reshape(CHUNK_ROW, CMP_HID)
        g1 = jnp.concatenate([z, w], axis=1).reshape(CHUNK_ROW, CMP_HID)
        return jnp.concatenate([g0, g1], axis=1)

    w1p = jnp.concatenate([expand(w1[:half]), expand(w1[half:])], axis=1).astype(BF16)
    z2 = jnp.zeros_like(w2)
    w2p = jnp.concatenate([jnp.concatenate([w2, z2], axis=1), jnp.concatenate([z2, w2], axis=1)], axis=0).astype(BF16)
    return w1p, w1.astype(BF16), pe.reshape(1, -1), w2p


def _compress_tail(r, w1_ref, pe_ref, w2p_ref):
    n = r.shape[0]
    pe8 = jnp.broadcast_to(pe_ref[...], (SUBLANES, pe_ref.shape[1]))
    c = _bdot(pe8, w1_ref[...])[0:1]
    c2 = jnp.concatenate([c, c], axis=1)
    h = r[:, :2 * CMP_HID] + pltpu.roll(r[:, 2 * CMP_HID:], n - 1, 0) + c2
    return _bdot(_gelu_tanh(h), w2p_ref[...])


def _compress_prompt_kernel(ak_ref, av_ref, w1pk, w1k, pek, w2pk, w1pv, w1v, pev, w2pv, ck_ref, cv_ref):
    ck_ref[0] = _compress_tail(_bdot(ak_ref[0], w1pk[...]), w1k, pek, w2pk)
    cv_ref[0] = _compress_tail(_bdot(av_ref[0], w1pv[...]), w1v, pev, w2pv)


def _compress_prompt(ak, av, wk, wv):
    b, n, _ = ak.shape
    const = lambda a: pl.BlockSpec(a.shape, lambda i: (0,) * a.ndim)
    row = pl.BlockSpec((1, n, CHUNK_ROW), lambda i: (i, 0, 0))
    out = pl.BlockSpec((1, n, KV_DIM), lambda i: (i, 0, 0))
    return pl.pallas_call(
        _compress_prompt_kernel,
        grid=(b,),
        in_specs=[row, row] + [const(a) for a in wk] + [const(a) for a in wv],
        out_specs=[out, out],
        out_shape=[jax.ShapeDtypeStruct((b, n, KV_DIM), F32)] * 2,
        compiler_params=_cparams("arbitrary"),
        name="compress_prompt",
    )(ak, av, *wk, *wv)


def _compress_decode_kernel(pt_ref, pk_ref, pv_ref, w1pk, w1k, pek, w2pk, w1pv, w1v, pev, w2pv, ck_ref, cv_ref,
                            slab_k, slab_v, rbuf, sem, *, n_pages, rows_per_page):
    b = pl.program_id(0)

    def page_copies(pg):
        pid = pt_ref[b * n_pages + pg]
        dst = pl.ds(pl.multiple_of(pg * rows_per_page, rows_per_page), rows_per_page)
        return (pltpu.make_async_copy(pk_ref.at[pid], slab_k.at[dst], sem.at[0]),
                pltpu.make_async_copy(pv_ref.at[pid], slab_v.at[dst], sem.at[1]))

    def start(pg, carry):
        for cp in page_copies(pg):
            cp.start()
        return carry

    def wait(pg, carry):
        for cp in page_copies(pg):
            cp.wait()
        return carry

    lax.fori_loop(0, n_pages, start, 0)
    lax.fori_loop(0, n_pages, wait, 0)
    n = n_pages * rows_per_page
    step = min(n, 256)

    def project(slab, w1p):
        def body(i, carry):
            rows = pl.ds(pl.multiple_of(i * step, step), step)
            rbuf[rows, :] = _bdot(slab[rows, :], w1p[...])
            return carry
        lax.fori_loop(0, n // step, body, 0)
        return rbuf[...]

    ck_ref[0] = _compress_tail(project(slab_k, w1pk), w1k, pek, w2pk)
    cv_ref[0] = _compress_tail(project(slab_v, w1pv), w1v, pev, w2pv)


def _compress_decode(page_table, pool_k, pool_v, wk, wv):
    nb, n_pages = page_table.shape
    rpp = pool_k.shape[1]
    n = n_pages * rpp
    const = lambda a: pl.BlockSpec(a.shape, lambda i, pt: (0,) * a.ndim)
    any_spec = pl.BlockSpec(memory_space=pl.ANY)
    out = pl.BlockSpec((1, n, KV_DIM), lambda i, pt: (i, 0, 0))
    grid_spec = pltpu.PrefetchScalarGridSpec(
        num_scalar_prefetch=1, grid=(nb,),
        in_specs=[any_spec, any_spec] + [const(a) for a in wk] + [const(a) for a in wv],
        out_specs=[out, out],
        scratch_shapes=[pltpu.VMEM((n, CHUNK_ROW), F32), pltpu.VMEM((n, CHUNK_ROW), F32),
                        pltpu.VMEM((n, 4 * CMP_HID), F32), pltpu.SemaphoreType.DMA((2,))])
    return pl.pallas_call(
        functools.partial(_compress_decode_kernel, n_pages=n_pages, rows_per_page=rpp),
        grid_spec=grid_spec,
        out_shape=[jax.ShapeDtypeStruct((nb, n, KV_DIM), F32)] * 2,
        compiler_params=_cparams("arbitrary"),
        name="compress_decode",
    )(page_table.reshape(-1), pool_k, pool_v, *wk, *wv)


def _importance_matrix(n_cmp_rows, n_cmp, n_blk, cols):
    c = jnp.arange(n_cmp_rows)[:, None]
    b = jnp.arange(cols)[None, :]
    d = c - (SEL_RATIO * b - 1)
    return ((d >= 0) & (d <= SEL_RATIO) & (c < n_cmp) & (b < n_blk)).astype(F32)


def _expand_matrix(n_chunks, chunk, blk0):
    key = jnp.arange(n_chunks)[:, None, None] * chunk + jnp.arange(chunk)[None, None, :]
    blk = jnp.asarray(blk0)[:, None, None] + jnp.arange(LANES)[None, :, None]
    return (key // SEL_LEN == blk).astype(BF16)


def _flash_step(qh, kg, vg, bias, mask, m, l, acc):
    s = jnp.where(mask, _bdot_nt(qh, kg) + bias, NEG_INF)
    m_new = jnp.maximum(m, jnp.max(s, -1, keepdims=True))
    alpha = jnp.exp(m - m_new)
    e = jnp.where(mask, jnp.exp(s - m_new), 0.0)
    return m_new, alpha * l + jnp.sum(e, -1, keepdims=True), alpha * acc + _bdot(e, vg)


def _flash_out(l, acc):
    return acc * (1.0 / jnp.maximum(l, 1e-30))


def _attn_prompt_kernel(q_ref, g_ref, ck_ref, cv_ref, ks_ref, vs_ref, kw_ref, vw_ref, mimp_ref, esel_ref, o_ref,
                        *, seq, n_blk, n_pick, win_len):
    qb = Q_BLOCK
    qs = pl.program_id(1) * qb
    t_col = qs + lax.broadcasted_iota(I32, (qb, 1), 0)
    gates = g_ref[0]
    qh = [(q_ref[0, :, h * HEAD_DIM:(h + 1) * HEAD_DIM] * (HEAD_DIM ** -0.5)).astype(BF16) for h in range(N_HEADS)]
    nc = ck_ref.shape[1]
    cmp_end = lax.broadcasted_iota(I32, (1, nc), 1) * CMP_STRIDE + (CMP_LEN - 1)
    dist_c = t_col - cmp_end
    mask_c = dist_c >= 0
    dist_cf = dist_c.astype(F32)
    blk_pad = mimp_ref.shape[1]

    blk_i = lax.broadcasted_iota(I32, (n_blk, qb), 0)
    t_row = qs + lax.broadcasted_iota(I32, (n_blk, qb), 1)
    cur = t_row // SEL_LEN
    forced = (blk_i == 0) | (blk_i == cur) | (blk_i == cur - 1)
    future = blk_i * SEL_LEN > t_row

    win_start = pl.multiple_of(jnp.clip(qs - WINDOW, 0, seq - win_len), qb)
    wpos = win_start + lax.broadcasted_iota(I32, (1, win_len), 1)
    dist_w = t_col - wpos
    mask_w = (dist_w >= 0) & (dist_w <= WINDOW)
    dist_wf = dist_w.astype(F32)
    n_chunks = (qs + qb + SLC_CHUNK - 1) // SLC_CHUNK

    for g in range(N_KV):
        lanes = slice(g * HEAD_DIM, (g + 1) * HEAD_DIM)
        heads = range(g * Q_PER_KV, (g + 1) * Q_PER_KV)
        ckg = ck_ref[0, :, lanes].astype(BF16)
        cvg = cv_ref[0, :, lanes].astype(BF16)
        o_cmp = []
        p_sum = jnp.zeros((qb, nc), F32)
        for h in heads:
            p = _masked_softmax(_bdot_nt(qh[h], ckg) - SLOPES[h] * dist_cf, mask_c)
            p_sum = p_sum + p
            o_cmp.append(_bdot(p, cvg))
        imp = jnp.dot(p_sum, mimp_ref[...], precision=HIGHEST, preferred_element_type=F32)
        imp_t = jnp.transpose(imp)[0:n_blk, :]
        imp_t = jnp.where(future, -FORCE, jnp.where(forced, FORCE, imp_t))
        cnt = jnp.zeros((n_blk, qb), F32)
        for b2 in range(n_blk):
            row = imp_t[b2:b2 + 1, :]
            ahead = (row > imp_t) | ((row == imp_t) & (blk_i > b2))
            cnt = cnt + jnp.where(ahead, 1.0, 0.0)
        sel_t = jnp.where(cnt < n_pick, 1.0, 0.0)
        if n_blk < blk_pad:
            sel_t = jnp.concatenate([sel_t, jnp.zeros((blk_pad - n_blk, qb), F32)], axis=0)
        sel = jnp.transpose(sel_t).astype(BF16)

        def slc_step(c, carry):
            rows = pl.ds(pl.multiple_of(c * SLC_CHUNK, SLC_CHUNK), SLC_CHUNK)
            kg = ks_ref[0, rows, lanes].astype(BF16)
            vg = vs_ref[0, rows, lanes].astype(BF16)
            msel = jnp.dot(sel, esel_ref[c], preferred_element_type=F32)
            dist = t_col - (c * SLC_CHUNK + lax.broadcasted_iota(I32, (1, SLC_CHUNK), 1))
            mask = (dist >= 0) & (msel > 0.5)
            dist_f = dist.astype(F32)
            out = []
            for i, h in enumerate(heads):
                out += _flash_step(qh[h], kg, vg, -SLOPES[h] * dist_f, mask, *carry[3 * i:3 * i + 3])
            return tuple(out)

        init = (jnp.full((qb, 1), NEG_INF, F32), jnp.zeros((qb, 1), F32), jnp.zeros((qb, HEAD_DIM), F32)) * Q_PER_KV
        fin = lax.fori_loop(0, n_chunks, slc_step, init)
        rows_w = pl.ds(win_start, win_len)
        kwg = kw_ref[0, rows_w, lanes].astype(BF16)
        vwg = vw_ref[0, rows_w, lanes].astype(BF16)
        for i, h in enumerate(heads):
            o_slc = _flash_out(fin[3 * i + 1], fin[3 * i + 2])
            p = _masked_softmax(_bdot_nt(qh[h], kwg) - SLOPES[h] * dist_wf, mask_w)
            o_win = _bdot(p, vwg)
            c0 = h * N_GATES
            o_ref[0, :, h * HEAD_DIM:(h + 1) * HEAD_DIM] = (gates[:, c0:c0 + 1] * o_cmp[i] + gates[:, c0 + 1:c0 + 2] * o_slc
                                                           + gates[:, c0 + 2:c0 + 3] * o_win)


def _attn_prompt(q, gates, ck, cv, ks, vs, kw, vw):
    b, t, att = q.shape
    assert t % SLC_CHUNK == 0 and t % Q_BLOCK == 0
    n_blk = t // SEL_LEN
    n_cmp = t // CMP_STRIDE - 1
    assert n_blk <= LANES and ck.shape[1] == t // CMP_STRIDE
    n_pick = min(N_SEL, n_blk)
    win_len = min(WINDOW + Q_BLOCK, t)
    mimp = _importance_matrix(ck.shape[1], n_cmp, n_blk, LANES)
    n_ch = t // SLC_CHUNK
    esel = _expand_matrix(n_ch, SLC_CHUNK, [0] * n_ch)
    tok = lambda n: pl.BlockSpec((1, Q_BLOCK, n), lambda i, j: (i, j, 0))
    full = lambda a: pl.BlockSpec((1,) + a.shape[1:], lambda i, j: (i, 0, 0))
    const = lambda a: pl.BlockSpec(a.shape, lambda i, j: (0,) * a.ndim)
    return pl.pallas_call(
        functools.partial(_attn_prompt_kernel, seq=t, n_blk=n_blk, n_pick=n_pick, win_len=win_len),
        grid=(b, t // Q_BLOCK),
        in_specs=[tok(att), tok(gates.shape[2]), full(ck), full(cv), full(ks), full(vs), full(kw), full(vw),
                  const(mimp), const(esel)],
        out_specs=tok(att),
        out_shape=jax.ShapeDtypeStruct((b, t, att), F32),
        compiler_params=_cparams("arbitrary", "arbitrary"),
        name="attn_prompt",
    )(q, gates, ck, cv, ks, vs, kw, vw, mimp, esel)


def _attn_decode_kernel(pt_ref, q_ref, g_ref, ck_ref, cv_ref, ksn_ref, vsn_ref, kwn_ref, vwn_ref, wk_ref, wv_ref,
                        pk_ref, pv_ref, mimp_ref, esel_ref, o_ref, slab_k, slab_v, sem,
                        *, n_pages, page, past, n_blk, n_pick, tq):
    b = pl.program_id(0)

    def page_copies(pg):
        pid = pt_ref[b * n_pages + pg]
        dst = pl.ds(pl.multiple_of(pg * page, page), page)
        return (pltpu.make_async_copy(pk_ref.at[pid], slab_k.at[dst], sem.at[0]),
                pltpu.make_async_copy(pv_ref.at[pid], slab_v.at[dst], sem.at[1]))

    def start(pg, carry):
        for cp in page_copies(pg):
            cp.start()
        return carry

    def wait(pg, carry):
        for cp in page_copies(pg):
            cp.wait()
        return carry

    lax.fori_loop(0, n_pages, start, 0)

    rows = Q_PER_KV * DEC_Q
    qi = lax.broadcasted_iota(I32, (rows, 1), 0) % DEC_Q
    t_col = past + qi % tq
    hl = lax.broadcasted_iota(I32, (rows, 1), 0) // DEC_Q
    nc = ck_ref.shape[1]
    cmp_end = lax.broadcasted_iota(I32, (1, nc), 1) * CMP_STRIDE + (CMP_LEN - 1)
    dist_c = t_col - cmp_end
    mask_c = dist_c >= 0
    dist_cf = dist_c.astype(F32)
    blk_pad = mimp_ref.shape[1]
    t8 = past + lax.broadcasted_iota(I32, (DEC_Q, 1), 0) % tq
    blk_l = lax.broadcasted_iota(I32, (DEC_Q, blk_pad), 1)
    cur = t8 // SEL_LEN
    forced = (blk_l == 0) | (blk_l == cur) | (blk_l == cur - 1)
    future = blk_l * SEL_LEN > t8
    pad_blk = blk_l >= n_blk
    b_sub = lax.broadcasted_iota(I32, (blk_pad, blk_pad), 0)
    b_lane = lax.broadcasted_iota(I32, (blk_pad, blk_pad), 1)
    n_chunks = past // DEC_CHUNK
    new_rows = ksn_ref.shape[1]

    sels, slopes, o_cmps = [], [], []
    for g in range(N_KV):
        lanes = slice(g * HEAD_DIM, (g + 1) * HEAD_DIM)
        slope = jnp.zeros((rows, 1), F32)
        for i in range(Q_PER_KV):
            slope = jnp.where(hl == i, SLOPES[g * Q_PER_KV + i], slope)
        slopes.append(slope)
        qg = (q_ref[0, g] * (HEAD_DIM ** -0.5)).astype(BF16)
        p = _masked_softmax(_bdot_nt(qg, ck_ref[0, :, lanes]) - slope * dist_cf, mask_c)
        o_cmps.append(_bdot(p, cv_ref[0, :, lanes]))
        p_sum = p[0:DEC_Q]
        for i in range(1, Q_PER_KV):
            p_sum = p_sum + p[i * DEC_Q:(i + 1) * DEC_Q]
        imp = jnp.dot(p_sum, mimp_ref[...], precision=HIGHEST, preferred_element_type=F32)
        imp = jnp.where(pad_blk, -3e38, jnp.where(future, -FORCE, jnp.where(forced, FORCE, imp)))
        imp_t = jnp.transpose(jnp.concatenate([imp, jnp.zeros((LANES - DEC_Q, blk_pad), F32)], axis=0))
        sel_rows = []
        for r in range(DEC_Q):
            col = imp_t[:, r:r + 1]
            row = imp[r:r + 1, :]
            ahead = (col > row) | ((col == row) & (b_sub < b_lane))
            cnt = jnp.sum(jnp.where(ahead, 1.0, 0.0), axis=0, keepdims=True)
            sel_rows.append(jnp.where(cnt < n_pick, 1.0, 0.0))
        sel8 = jnp.concatenate(sel_rows, axis=0)
        sels.append(jnp.concatenate([sel8] * Q_PER_KV, axis=0))

    lax.fori_loop(0, n_pages, wait, 0)

    for g in range(N_KV):
        lanes = slice(g * HEAD_DIM, (g + 1) * HEAD_DIM)
        slope = slopes[g]
        qg = (q_ref[0, g] * (HEAD_DIM ** -0.5)).astype(BF16)
        sel = sels[g]
        sel_b = sel.astype(BF16)
        m = jnp.full((rows, 1), NEG_INF, F32)
        l = jnp.zeros((rows, 1), F32)
        acc = jnp.zeros((rows, HEAD_DIM), F32)
        per_mat = LANES * SEL_LEN // DEC_CHUNK
        for c in range(n_chunks):
            krows = slice(c * DEC_CHUNK, (c + 1) * DEC_CHUNK)
            lane0 = (c // per_mat) * LANES
            msel = jnp.dot(sel_b[:, lane0:lane0 + LANES], esel_ref[c % per_mat], preferred_element_type=F32)
            dist = t_col - (c * DEC_CHUNK + lax.broadcasted_iota(I32, (1, DEC_CHUNK), 1))
            mask = (dist >= 0) & (msel > 0.5)
            m, l, acc = _flash_step(qg, slab_k[krows, lanes], slab_v[krows, lanes], -slope * dist.astype(F32), mask, m, l, acc)
        dist = t_col - (past + lax.broadcasted_iota(I32, (1, new_rows), 1))
        cur_blk = past // SEL_LEN
        mask = (dist >= 0) & (sel[:, cur_blk:cur_blk + 1] > 0.5)
        m, l, acc = _flash_step(qg, ksn_ref[0, :, lanes], vsn_ref[0, :, lanes], -slope * dist.astype(F32), mask, m, l, acc)
        o_slc = _flash_out(l, acc)
        wlen = wk_ref.shape[1]
        dist = t_col - (past - wlen + lax.broadcasted_iota(I32, (1, wlen), 1))
        mask = (dist >= 0) & (dist <= WINDOW)
        m = jnp.full((rows, 1), NEG_INF, F32)
        l = jnp.zeros((rows, 1), F32)
        acc = jnp.zeros((rows, HEAD_DIM), F32)
        m, l, acc = _flash_step(qg, wk_ref[0, :, lanes], wv_ref[0, :, lanes], -slope * dist.astype(F32), mask, m, l, acc)
        dist = t_col - (past + lax.broadcasted_iota(I32, (1, new_rows), 1))
        mask = (dist >= 0) & (dist <= WINDOW)
        m, l, acc = _flash_step(qg, kwn_ref[0, :, lanes], vwn_ref[0, :, lanes], -slope * dist.astype(F32), mask, m, l, acc)
        o_win = _flash_out(l, acc)
        gt = g_ref[0, g]
        o_ref[0, g] = gt[:, 0:1] * o_cmps[g] + gt[:, 1:2] * o_slc + gt[:, 2:3] * o_win


def _attn_decode(page_table, q, gates, ck, cv, ksn, vsn, kwn, vwn, win_k, win_v, pool_k, pool_v):
    nb, tq, att = q.shape
    n_pages = page_table.shape[1]
    page = pool_k.shape[1]
    past = n_pages * page
    assert tq <= DEC_Q and past % DEC_CHUNK == 0 and past % SEL_LEN == 0 and (past + tq - 1) // SEL_LEN == past // SEL_LEN
    assert win_k.shape[1] == WINDOW and ck.shape[1] * CMP_STRIDE == past
    t_pad = -(-(past + tq) // SEL_LEN) * SEL_LEN
    n_blk = t_pad // SEL_LEN
    n_cmp = t_pad // CMP_STRIDE - 1
    n_pick = min(N_SEL, n_blk)
    blk_pad = -(-n_blk // LANES) * LANES
    mimp = _importance_matrix(ck.shape[1], n_cmp, n_blk, blk_pad)
    per_mat = LANES * SEL_LEN // DEC_CHUNK
    esel = _expand_matrix(per_mat, DEC_CHUNK, [0] * per_mat)

    def dec_rows(a, width):
        a = a.reshape(nb, tq, N_KV, Q_PER_KV, width)
        a = jnp.take(a, jnp.arange(DEC_Q) % tq, axis=1)
        return a.transpose(0, 2, 3, 1, 4).reshape(nb, N_KV, Q_PER_KV * DEC_Q, width)

    q_d = dec_rows(q, HEAD_DIM)
    g_d = dec_rows(gates[:, :, :N_HEADS * N_GATES], N_GATES)
    pad_new = lambda a: jnp.pad(a, ((0, 0), (0, DEC_Q - tq), (0, 0)))
    ksn, vsn, kwn, vwn = [pad_new(a) for a in (ksn, vsn, kwn, vwn)]
    rows = Q_PER_KV * DEC_Q
    blk = lambda a: pl.BlockSpec((1,) + a.shape[1:], lambda i, pt: (i,) + (0,) * (a.ndim - 1))
    const = lambda a: pl.BlockSpec(a.shape, lambda i, pt: (0,) * a.ndim)
    any_spec = pl.BlockSpec(memory_space=pl.ANY)
    grid_spec = pltpu.PrefetchScalarGridSpec(
        num_scalar_prefetch=1, grid=(nb,),
        in_specs=[blk(q_d), blk(g_d), blk(ck), blk(cv), blk(ksn), blk(vsn), blk(kwn), blk(vwn), blk(win_k), blk(win_v),
                  any_spec, any_spec, const(mimp), const(esel)],
        out_specs=pl.BlockSpec((1, N_KV, rows, HEAD_DIM), lambda i, pt: (i, 0, 0, 0)),
        scratch_shapes=[pltpu.VMEM((past, KV_DIM), F32), pltpu.VMEM((past, KV_DIM), F32), pltpu.SemaphoreType.DMA((2,))])
    o = pl.pallas_call(
        functools.partial(_attn_decode_kernel, n_pages=n_pages, page=page, past=past, n_blk=n_blk, n_pick=n_pick, tq=tq),
        grid_spec=grid_spec,
        out_shape=jax.ShapeDtypeStruct((nb, N_KV, rows, HEAD_DIM), F32),
        compiler_params=_cparams("arbitrary"),
        name="attn_decode",
    )(page_table.reshape(-1), q_d, g_d, ck, cv, ksn, vsn, kwn, vwn, win_k, win_v, pool_k, pool_v, mimp, esel)
    o = o.reshape(nb, N_KV, Q_PER_KV, DEC_Q, HEAD_DIM)[:, :, :, :tq]
    return o.transpose(0, 3, 1, 2, 4).reshape(nb, tq, att)


def _out_kernel(x_ref, conv_ref, att_ref, mod_ref, bc_ref, ba_ref, woc_ref, woa_ref, g1_ref, b1_ref, wr_ref, br_ref,
                cnt0_ref, x1_ref, u2_ref, idx_ref, w_ref, pos_ref, cnt_ref, carry_ref, *, tt, alpha, per_row):
    first = (pl.program_id(0) == 0) & (pl.program_id(1) == 0)

    @pl.when(first)
    def _():
        carry_ref[...] = cnt0_ref[...]

    gate1 = _mod_rows(mod_ref, 2, per_row)
    shift2 = _mod_rows(mod_ref, 3, per_row)
    scale2 = _mod_rows(mod_ref, 4, per_row)
    mix = _bdot(conv_ref[0] * bc_ref[...], woc_ref[...]) + _bdot(att_ref[0] * ba_ref[...], woa_ref[...])
    x1 = _layer_norm(alpha * x_ref[0] + gate1 * mix, g1_ref[...], b1_ref[...])
    x1_ref[0] = x1
    u2 = x1 * (1.0 + scale2) + shift2
    u2_ref[0] = u2
    logits = jnp.dot(u2, wr_ref[...], precision=HIGHEST, preferred_element_type=F32) + br_ref[...]
    lane = lax.broadcasted_iota(I32, logits.shape, 1)
    vals = logits
    top_v, top_i = [], []
    for _k in range(TOP_K):
        m = jnp.max(vals, axis=-1, keepdims=True)
        i = jnp.min(jnp.where(vals == m, lane, LANES), axis=-1, keepdims=True)
        top_v.append(m)
        top_i.append(i)
        vals = jnp.where(lane == i, -jnp.inf, vals)
    e = [jnp.exp(v - top_v[0]) for v in top_v]
    inv = 1.0 / (e[0] + e[1] + e[2] + e[3])
    onehot = [jnp.where(lane == i, 1.0, 0.0) for i in top_i]
    assigned = onehot[0] + onehot[1] + onehot[2] + onehot[3]
    r_i = lax.broadcasted_iota(I32, (tt, tt), 0)
    c_i = lax.broadcasted_iota(I32, (tt, tt), 1)
    before = jnp.where(r_i > c_i, 1.0, 0.0).astype(BF16)
    rank = carry_ref[...] + jnp.dot(before, assigned.astype(BF16), preferred_element_type=F32)
    idx_o = jnp.zeros(logits.shape, I32)
    w_o = jnp.zeros(logits.shape, F32)
    pos_o = jnp.zeros(logits.shape, F32)
    for k in range(TOP_K):
        idx_o = jnp.where(lane == k, top_i[k], idx_o)
        w_o = jnp.where(lane == k, e[k] * inv, w_o)
        pos_o = jnp.where(lane == k, jnp.sum(onehot[k] * rank, axis=-1, keepdims=True), pos_o)
    idx_ref[0] = idx_o
    w_ref[0] = w_o
    pos_ref[0] = pos_o.astype(I32)
    carry_ref[...] = carry_ref[...] + jnp.sum(assigned, axis=0, keepdims=True)
    cnt_ref[...] = carry_ref[...]


def _out_proj(x, conv_out, att, mod, vecs, wts, cnt0, *, tt, alpha, per_row):
    b, t, d = x.shape
    bc, ba, g1, b1, br = vecs
    woc, woa, wr = wts
    const = lambda a: pl.BlockSpec(a.shape, lambda i, j: (0,) * a.ndim)
    tok = lambda n: pl.BlockSpec((1, tt, n), lambda i, j: (i, j, 0))
    mod_spec = (pl.BlockSpec((6, tt, d), lambda i, j: (0, j, 0)) if per_row
                else pl.BlockSpec((1, 6, d), lambda i, j: (i, 0, 0)))
    tok_shape = lambda n, dt: jax.ShapeDtypeStruct((b, t, n), dt)
    return pl.pallas_call(
        functools.partial(_out_kernel, tt=tt, alpha=alpha, per_row=per_row),
        grid=(b, t // tt),
        in_specs=[tok(d), tok(conv_out.shape[2]), tok(att.shape[2]), mod_spec, const(bc), const(ba), const(woc), const(woa),
                  const(g1), const(b1), const(wr), const(br), const(cnt0)],
        out_specs=[tok(d), tok(d), tok(LANES), tok(LANES), tok(LANES), pl.BlockSpec((1, LANES), lambda i, j: (0, 0))],
        out_shape=[tok_shape(d, F32), tok_shape(d, F32), tok_shape(LANES, I32), tok_shape(LANES, F32), tok_shape(LANES, I32),
                   jax.ShapeDtypeStruct((1, LANES), F32)],
        scratch_shapes=[pltpu.VMEM((1, LANES), F32)],
        compiler_params=_cparams("arbitrary", "arbitrary"),
        name="out_proj_router",
    )(x, conv_out, att, mod, bc, ba, woc, woa, g1, b1, wr, br, cnt0)


def _expert_kernel(be_ref, cur_ref, nxt_ref, u_ref, wgu_ref, bgu_ref, wdn_ref, bdn_ref, y_ref, xbuf, sem, *, bm, d_ff):
    i = pl.program_id(0)
    n = pl.num_programs(0)
    slot = i % 2

    def row_copy(tok, s, r):
        return pltpu.make_async_copy(u_ref.at[pl.ds(tok, 1)], xbuf.at[s, pl.ds(r, 1)], sem.at[s])

    def gather(tok_ref, s):
        def body(r, carry):
            row_copy(tok_ref[0, 0, r], s, r).start()
            return carry
        lax.fori_loop(0, bm, body, 0)

    @pl.when(i == 0)
    def _():
        gather(cur_ref, 0)

    @pl.when(i + 1 < n)
    def _():
        gather(nxt_ref, 1 - slot)

    def wait(r, carry):
        row_copy(0, slot, r).wait()
        return carry

    lax.fori_loop(0, bm, wait, 0)
    x = xbuf[slot].astype(BF16)
    gu = jnp.dot(x, wgu_ref[0], preferred_element_type=F32) + bgu_ref[0]
    gl = jnp.minimum(gu[:, :d_ff], SWIGLU_LIMIT)
    lin = jnp.clip(gu[:, d_ff:], -SWIGLU_LIMIT, SWIGLU_LIMIT)
    act = gl * _sigmoid(SWIGLU_ALPHA * gl) * (lin + 1.0)
    y_ref[...] = jnp.dot(act.astype(BF16), wdn_ref[0], preferred_element_type=F32) + bdn_ref[0]


def _expert_ffn(u2, slot_tok, blk_e, wgu, bgu, wdn, bdn, *, bm):
    n_blocks = slot_tok.shape[0]
    d = u2.shape[1]
    d_ff = wdn.shape[1]
    slot_tok = slot_tok.reshape(n_blocks, 1, bm)
    smem = lambda f: pl.BlockSpec((1, 1, bm), f, memory_space=pltpu.SMEM)
    grid_spec = pltpu.PrefetchScalarGridSpec(
        num_scalar_prefetch=1, grid=(n_blocks,),
        in_specs=[smem(lambda i, be: (i, 0, 0)),
                  smem(lambda i, be: (jnp.minimum(i + 1, n_blocks - 1), 0, 0)),
                  pl.BlockSpec(memory_space=pl.ANY),
                  pl.BlockSpec((1, d, 2 * d_ff), lambda i, be: (be[i], 0, 0)),
                  pl.BlockSpec((1, 1, 2 * d_ff), lambda i, be: (be[i], 0, 0)),
                  pl.BlockSpec((1, d_ff, d), lambda i, be: (be[i], 0, 0)),
                  pl.BlockSpec((1, 1, d), lambda i, be: (be[i], 0, 0))],
        out_specs=pl.BlockSpec((bm, d), lambda i, be: (i, 0)),
        scratch_shapes=[pltpu.VMEM((2, bm, d), F32), pltpu.SemaphoreType.DMA((2,))])
    return pl.pallas_call(
        functools.partial(_expert_kernel, bm=bm, d_ff=d_ff),
        grid_spec=grid_spec,
        out_shape=jax.ShapeDtypeStruct((n_blocks * bm, d), F32),
        compiler_params=_cparams("arbitrary"),
        name="expert_ffn",
    )(blk_e, slot_tok, slot_tok, u2, wgu, bgu, wdn, bdn)


def _combine_kernel(cur_ref, nxt_ref, ys_ref, w_ref, x1_ref, mod_ref, g2_ref, b2_ref, y_ref, gbuf, sem,
                    *, tc, alpha, per_row):
    i = pl.program_id(0)
    n = pl.num_programs(0)
    slot = i % 2

    def row_copy(src, s, k, r):
        return pltpu.make_async_copy(ys_ref.at[pl.ds(src, 1)], gbuf.at[s, k, pl.ds(r, 1)], sem.at[s])

    def gather(dest_ref, s):
        def body(r, carry):
            for k in range(TOP_K):
                row_copy(dest_ref[0, 0, k * tc + r], s, k, r).start()
            return carry
        lax.fori_loop(0, tc, body, 0)

    @pl.when(i == 0)
    def _():
        gather(cur_ref, 0)

    @pl.when(i + 1 < n)
    def _():
        gather(nxt_ref, 1 - slot)

    def wait(r, carry):
        for k in range(TOP_K):
            row_copy(0, slot, k, r).wait()
        return carry

    lax.fori_loop(0, tc, wait, 0)
    w = w_ref[...]
    f = w[:, 0:1] * gbuf[slot, 0]
    for k in range(1, TOP_K):
        f = f + w[:, k:k + 1] * gbuf[slot, k]
    gate2 = mod_ref[5] if per_row else mod_ref[0, 5:6, :]
    y_ref[...] = _layer_norm(alpha * x1_ref[...] + gate2 * f, g2_ref[...], b2_ref[...])


def _combine(ys, dest_t, top_w, x1, mod, g2, b2, *, tc, alpha, per_row, tiles_per_batch):
    n, d = x1.shape
    n_tiles = n // tc
    dest_t = dest_t.reshape(n_tiles, 1, TOP_K * tc)
    smem = lambda f: pl.BlockSpec((1, 1, TOP_K * tc), f, memory_space=pltpu.SMEM)
    const = lambda a: pl.BlockSpec(a.shape, lambda i: (0,) * a.ndim)
    mod_spec = (pl.BlockSpec((6, tc, d), lambda i: (0, i, 0)) if per_row
                else pl.BlockSpec((1, 6, d), lambda i: (i // tiles_per_batch, 0, 0)))
    return pl.pallas_call(
        functools.partial(_combine_kernel, tc=tc, alpha=alpha, per_row=per_row),
        grid=(n_tiles,),
        in_specs=[smem(lambda i: (i, 0, 0)), smem(lambda i: (jnp.minimum(i + 1, n_tiles - 1), 0, 0)),
                  pl.BlockSpec(memory_space=pl.ANY),
                  pl.BlockSpec((tc, LANES), lambda i: (i, 0)), pl.BlockSpec((tc, d), lambda i: (i, 0)),
                  mod_spec, const(g2), const(b2)],
        out_specs=pl.BlockSpec((tc, d), lambda i: (i, 0)),
        out_shape=jax.ShapeDtypeStruct((n, d), F32),
        scratch_shapes=[pltpu.VMEM((2, TOP_K, tc, d), F32), pltpu.SemaphoreType.DMA((2,))],
        compiler_params=_cparams("arbitrary"),
        name="combine_ln2",
    )(dest_t, dest_t, ys, top_w, x1, mod, g2, b2)


def _moe(u2, x1, top_i, top_w, pos, counts, mod, experts, g2, b2, *, bm, tc, alpha, per_row, tiles_per_batch):
    n, d = u2.shape
    wgu, bgu, wdn, bdn = experts
    n_exp = wgu.shape[0]
    n_blocks = -(-n * TOP_K // bm) + n_exp
    counts = counts.astype(I32)
    padded = (counts + bm - 1) // bm * bm
    pad_end = jnp.cumsum(padded)
    pad_start = pad_end - padded
    dest = pad_start[top_i] + pos
    tok = jnp.broadcast_to(jnp.arange(n, dtype=I32)[:, None], (n, TOP_K))
    slot_tok = jnp.zeros((n_blocks * bm,), I32).at[dest.reshape(-1)].set(tok.reshape(-1))
    blk_e = jnp.minimum(jnp.searchsorted(pad_end, jnp.arange(n_blocks, dtype=I32) * bm, side='right'), n_exp - 1).astype(I32)
    ys = _expert_ffn(u2, slot_tok.reshape(n_blocks, bm), blk_e, wgu, bgu, wdn, bdn, bm=bm)
    dest_t = dest.reshape(n // tc, tc, TOP_K).transpose(0, 2, 1).reshape(n // tc, TOP_K * tc)
    return _combine(ys, dest_t, top_w, x1, mod, g2, b2, tc=tc, alpha=alpha, per_row=per_row, tiles_per_batch=tiles_per_batch)


def _pick_tile(n, target):
    t = min(n, target)
    while n % t:
        t -= 1
    return t


def kernel(x_prompt, x_sample, c_prompt, c_sample, cache_cmp_k, cache_cmp_v, cache_slc_k, cache_slc_v, state_win_k, state_win_v, state_conv, page_table, w_ada, b_ada, w_in, w_dw, b_dw, conv_ln_g, conv_ln_b, w_ck1, w_ck2, pe_k, w_cv1, w_cv2, pe_v, beta_conv, beta_attn, w_out, ln1_g, ln1_b, w_router, b_router, w_gate_up, b_gate_up, w_down, b_down, ln2_g, ln2_b):
    depth = w_ada.shape[0]
    assert depth == 1, "single-layer trunk"
    alpha = (2 * depth) ** 0.25
    bp, seq, d = x_prompt.shape
    bs, tq, _ = x_sample.shape
    c_conv = w_dw.shape[2]
    att_dim = N_HEADS * HEAD_DIM
    n_pool, page = cache_cmp_k.shape[1], cache_cmp_k.shape[2]
    assert state_conv.shape[2] == CONV_K - 1 and page % CMP_STRIDE == 0

    wi = w_in[0]
    cuts = [2 * c_conv, 2 * c_conv + att_dim, 2 * c_conv + att_dim + 6 * KV_DIM]
    wa = wi[:, :cuts[0]].astype(BF16)
    wq = wi[:, cuts[0]:cuts[1]].astype(BF16)
    wkv = wi[:, cuts[1]:cuts[2]].astype(BF16)
    wg = jnp.pad(wi[:, cuts[2]:], ((0, 0), (0, LANES - N_HEADS * N_GATES))).astype(BF16)
    in_w = (wa, wq, wkv, wg)
    wdw = jnp.pad(w_dw[0], ((0, CONV_HIST - CONV_K), (0, 0)))
    row = lambda a: a.reshape(1, -1)
    conv_vecs = (wdw, row(b_dw[0]), row(conv_ln_g[0]), row(conv_ln_b[0]))
    wk_c = _compress_weights(w_ck1[0], w_ck2[0], pe_k[0])
    wv_c = _compress_weights(w_cv1[0], w_cv2[0], pe_v[0])
    out_vecs = (row(beta_conv[0]), row(beta_attn[0]), row(ln1_g[0]), row(ln1_b[0]),
                jnp.pad(row(b_router[0]), ((0, 0), (0, LANES - N_EXPERTS)), constant_values=NEG_INF))
    out_w = (w_out[0][:c_conv].astype(BF16), w_out[0][c_conv:].astype(BF16),
             jnp.pad(w_router[0], ((0, 0), (0, LANES - N_EXPERTS))))
    experts = (w_gate_up[0].astype(BF16), b_gate_up[0][:, None, :], w_down[0].astype(BF16), b_down[0][:, None, :])
    g2, b2 = row(ln2_g[0]), row(ln2_b[0])

    mod = _ada(jnp.concatenate([c_prompt, c_sample], axis=0), w_ada[0], row(b_ada[0])).reshape(bp + bs, 6, d)
    mod_p = mod[:bp]
    mod_s = jnp.repeat(mod[bp:], tq, axis=0).transpose(1, 0, 2)

    tt = _pick_tile(seq, 512)
    hist0 = jnp.zeros((bp, CONV_HIST, c_conv), F32)
    (conv_p, tail_p, q_p, kc_p, vc_p, ks_p, vs_p, kw_p, vw_p, gates_p) = _in_proj(
        x_prompt, mod_p, in_w, (hist0,) + conv_vecs, tt=tt, per_row=False)
    n_chunk = seq // CMP_STRIDE
    ck_p, cv_p = _compress_prompt(kc_p.reshape(bp, n_chunk, CHUNK_ROW), vc_p.reshape(bp, n_chunk, CHUNK_ROW), wk_c, wv_c)
    att_p = _attn_prompt(q_p, gates_p, ck_p, cv_p, ks_p, vs_p, kw_p, vw_p)
    to = _pick_tile(seq, 256)
    cnt0 = jnp.zeros((1, LANES), F32)
    x1_p, u2_p, ti_p, tw_p, pos_p, cnt_p = _out_proj(x_prompt, conv_p, att_p, mod_p, out_vecs, out_w, cnt0,
                                                     tt=to, alpha=alpha, per_row=False)
    n_p = bp * seq
    bm_p = _pick_tile(n_p * TOP_K, 512)
    tc_p = _pick_tile(seq, 128)
    y_p = _moe(u2_p.reshape(n_p, d), x1_p.reshape(n_p, d), ti_p.reshape(n_p, LANES)[:, :TOP_K], tw_p.reshape(n_p, LANES),
               pos_p.reshape(n_p, LANES)[:, :TOP_K], cnt_p[0, :N_EXPERTS], mod_p, experts, g2, b2,
               bm=bm_p, tc=tc_p, alpha=alpha, per_row=False, tiles_per_batch=seq // tc_p).reshape(bp, seq, d)

    n_s = bs * tq
    (glu_s, q_s, kc_s, vc_s, ks_s, vs_s, kw_s, vw_s, gates_s) = _in_proj(
        x_sample.reshape(1, n_s, d), mod_s, in_w, None, tt=n_s, per_row=True)
    unflat = lambda a: a.reshape(bs, tq, a.shape[-1])
    glu_s, q_s, kc_s, vc_s, ks_s, vs_s, kw_s, vw_s, gates_s = map(unflat, (glu_s, q_s, kc_s, vc_s, ks_s, vs_s, kw_s, vw_s, gates_s))
    xc_s = jnp.concatenate([jnp.zeros((bs, CONV_HIST - (CONV_K - 1), c_conv), F32), state_conv[0], glu_s], axis=1)
    conv_s = _conv_small(xc_s, *conv_vecs, tt=tq)
    chunks_per_page = page // CMP_STRIDE
    ck_s, cv_s = _compress_decode(page_table, cache_cmp_k[0].reshape(n_pool, chunks_per_page, CHUNK_ROW),
                                  cache_cmp_v[0].reshape(n_pool, chunks_per_page, CHUNK_ROW), wk_c, wv_c)
    att_s = _attn_decode(page_table, q_s, gates_s, ck_s, cv_s, ks_s, vs_s, kw_s, vw_s, state_win_k[0].reshape(bs, -1, KV_DIM),
                         state_win_v[0].reshape(bs, -1, KV_DIM), cache_slc_k[0].reshape(n_pool, page, KV_DIM),
                         cache_slc_v[0].reshape(n_pool, page, KV_DIM))
    flat = lambda a: a.reshape(1, n_s, a.shape[-1])
    x1_s, u2_s, ti_s, tw_s, pos_s, cnt_s = _out_proj(flat(x_sample), flat(conv_s), flat(att_s), mod_s, out_vecs, out_w, cnt0,
                                                     tt=n_s, alpha=alpha, per_row=True)
    bm_s = max(SUBLANES, min(512, n_s * TOP_K // N_EXPERTS))
    y_s = _moe(u2_s.reshape(n_s, d), x1_s.reshape(n_s, d), ti_s.reshape(n_s, LANES)[:, :TOP_K], tw_s.reshape(n_s, LANES),
               pos_s.reshape(n_s, LANES)[:, :TOP_K], cnt_s[0, :N_EXPERTS], mod_s, experts, g2, b2,
               bm=bm_s, tc=n_s, alpha=alpha, per_row=True, tiles_per_batch=1).reshape(bs, tq, d)

    heads = lambda a: a.reshape(1, a.shape[0], a.shape[1], N_KV, HEAD_DIM)
    win_keep = min(WINDOW, seq)
    new_p = (heads(kc_p), heads(vc_p), heads(ks_p), heads(vs_p), heads(kw_p[:, seq - win_keep:]), heads(vw_p[:, seq - win_keep:]),
             tail_p[None, :, CONV_HIST - (CONV_K - 1):])
    win_buf = state_win_k.shape[2]
    keep = lambda old, new: jnp.concatenate([old[0], new.reshape(bs, tq, N_KV, HEAD_DIM)], axis=1)[None, :, -win_buf:]
    new_s = (heads(kc_s), heads(vc_s), heads(ks_s), heads(vs_s), keep(state_win_k, kw_s), keep(state_win_v, vw_s),
             jnp.concatenate([state_conv[0], glu_s], axis=1)[None, :, -(CONV_K - 1):])
    return (y_p, y_s) + new_p + new_s
```

```python
import functools

import jax
import jax.numpy as jnp
from jax import lax
from jax.experimental import pallas as pl
from jax.experimental.pallas import tpu as pltpu

F32 = jnp.float32
BF16 = jnp.bfloat16
I32 = jnp.int32

N_HEADS = 8
HEAD_DIM = 64
N_KV = 2
Q_PER_KV = N_HEADS // N_KV
KV_DIM = N_KV * HEAD_DIM
CMP_LEN = 32
CMP_STRIDE = 16
CMP_HID = 2 * HEAD_DIM
SEL_LEN = 64
SEL_RATIO = SEL_LEN // CMP_STRIDE
N_SEL = 16
WINDOW = 512
Q_BLOCK = 128
N_GATES = 3
N_EXPERTS = 32
TOP_K = 4
CONV_K = 31
SWIGLU_LIMIT = 7.0
SWIGLU_ALPHA = 1.702
LN_EPS = 1e-5
NEG_INF = -1e30
FORCE = 1e9
SLOPES = tuple(2.0 ** (-8.0 * (h + 1) / N_HEADS) for h in range(N_HEADS))

LANES = 128
SUBLANES = 8
VMEM_LIMIT_BYTES = 56 * 1024 * 1024

CONV_HIST = 32
CHUNK_ROW = CMP_STRIDE * KV_DIM
SLC_CHUNK = 512
DEC_CHUNK = 2048
DEC_Q = 8
HIGHEST = lax.Precision.HIGHEST


def _cparams(*sem):
    return pltpu.CompilerParams(dimension_semantics=sem, vmem_limit_bytes=VMEM_LIMIT_BYTES)


def _sigmoid(x):
    return 1.0 / (1.0 + jnp.exp(-x))


def _bdot(a, b):
    return jnp.dot(a.astype(BF16), b.astype(BF16), preferred_element_type=F32)


def _bdot_nt(a, b):
    return lax.dot_general(a.astype(BF16), b.astype(BF16), (((1,), (1,)), ((), ())), preferred_element_type=F32)


def _layer_norm(x, g, b):
    mu = jnp.mean(x, -1, keepdims=True)
    xc = x - mu
    var = jnp.mean(xc * xc, -1, keepdims=True)
    return xc * lax.rsqrt(var + LN_EPS) * g + b


def _gelu_tanh(x):
    return 0.5 * x * (1.0 + jnp.tanh(0.7978845608028654 * (x + 0.044715 * (x * x * x))))


def _masked_softmax(s, mask):
    s = jnp.where(mask, s, NEG_INF)
    e = jnp.where(mask, jnp.exp(s - jnp.max(s, -1, keepdims=True)), 0.0)
    return e * (1.0 / jnp.maximum(jnp.sum(e, -1, keepdims=True), 1e-30))


def _ada_kernel(c_ref, w_ref, b_ref, o_ref):
    c = c_ref[...]
    o_ref[...] = _bdot(c * _sigmoid(c), w_ref[...]) + b_ref[...]


def _ada(c_all, w_ada, b_ada):
    nb, d = c_all.shape
    n = w_ada.shape[1]
    bn = d
    return pl.pallas_call(
        _ada_kernel,
        grid=(n // bn,),
        in_specs=[pl.BlockSpec((nb, d), lambda i: (0, 0)),
                  pl.BlockSpec((d, bn), lambda i: (0, i)),
                  pl.BlockSpec((1, bn), lambda i: (0, i))],
        out_specs=pl.BlockSpec((nb, bn), lambda i: (0, i)),
        out_shape=jax.ShapeDtypeStruct((nb, n), F32),
        compiler_params=_cparams("arbitrary"),
        name="ada",
    )(c_all, w_ada, b_ada)


def _mod_rows(mod_ref, i, per_row):
    return mod_ref[i] if per_row else mod_ref[0, i:i + 1, :]


def _conv_ln_silu(xc_ref, row0, n, wdw_ref, bdw, lng, lnb):
    acc = jnp.zeros((n, xc_ref.shape[1]), F32) + bdw
    off = CONV_HIST - (CONV_K - 1)
    for k in range(CONV_K):
        acc = acc + xc_ref[pl.ds(row0 + off + k, n), :] * wdw_ref[k:k + 1, :]
    y = _layer_norm(acc, lng, lnb)
    return y * _sigmoid(y)


def _in_kernel(*refs, tt, c_conv, fuse_conv, per_row):
    if fuse_conv:
        (x_ref, mod_ref, wa_ref, wq_ref, wkv_ref, wg_ref, hist_ref, wdw_ref, bdw_ref, lng_ref, lnb_ref,
         conv_ref, tail_ref, q_ref, kc_ref, vc_ref, ks_ref, vs_ref, kw_ref, vw_ref, gates_ref, xc_ref) = refs
    else:
        (x_ref, mod_ref, wa_ref, wq_ref, wkv_ref, wg_ref,
         glu_ref, q_ref, kc_ref, vc_ref, ks_ref, vs_ref, kw_ref, vw_ref, gates_ref) = refs
    x = x_ref[0]
    shift1 = _mod_rows(mod_ref, 0, per_row)
    scale1 = _mod_rows(mod_ref, 1, per_row)
    u = (x * (1.0 + scale1) + shift1).astype(BF16)
    a = jnp.dot(u, wa_ref[...], preferred_element_type=F32)
    glu = a[:, :c_conv] * _sigmoid(a[:, c_conv:])
    q_ref[0] = jnp.dot(u, wq_ref[...], preferred_element_type=F32)
    kv = jnp.dot(u, wkv_ref[...], preferred_element_type=F32)
    for i, r in enumerate((kc_ref, vc_ref, ks_ref, vs_ref, kw_ref, vw_ref)):
        r[0] = kv[:, i * KV_DIM:(i + 1) * KV_DIM]
    gates_ref[0] = _sigmoid(jnp.dot(u, wg_ref[...], preferred_element_type=F32))
    if not fuse_conv:
        glu_ref[0] = glu
        return

    @pl.when(pl.program_id(1) == 0)
    def _():
        xc_ref[0:CONV_HIST, :] = hist_ref[0]

    xc_ref[CONV_HIST:CONV_HIST + tt, :] = glu
    rows = min(tt, 64)
    for r0 in range(0, tt, rows):
        conv_ref[0, r0:r0 + rows, :] = _conv_ln_silu(xc_ref, r0, rows, wdw_ref, bdw_ref[...], lng_ref[...], lnb_ref[...])
    tail = xc_ref[tt:tt + CONV_HIST, :]
    xc_ref[0:CONV_HIST, :] = tail
    tail_ref[0] = tail


def _in_proj(x, mod, wts, conv=None, *, tt, per_row):
    b, t, d = x.shape
    wa, wq, wkv, wg = wts
    c_conv = wa.shape[1] // 2
    fuse = conv is not None
    nt = t // tt
    const = lambda shape: pl.BlockSpec(shape, lambda i, j: (0,) * len(shape))
    mod_spec = (pl.BlockSpec((6, tt, d), lambda i, j: (0, j, 0)) if per_row
                else pl.BlockSpec((1, 6, d), lambda i, j: (i, 0, 0)))
    in_specs = [pl.BlockSpec((1, tt, d), lambda i, j: (i, j, 0)), mod_spec,
                const(wa.shape), const(wq.shape), const(wkv.shape), const(wg.shape)]
    args = [x, mod, wa, wq, wkv, wg]
    tok = lambda n: pl.BlockSpec((1, tt, n), lambda i, j: (i, j, 0))
    tok_shape = lambda n: jax.ShapeDtypeStruct((b, t, n), F32)
    out_specs, out_shape, scratch = [], [], []
    if fuse:
        hist, wdw, bdw, lng, lnb = conv
        in_specs += [pl.BlockSpec((1, CONV_HIST, c_conv), lambda i, j: (i, 0, 0)),
                     const(wdw.shape), const(bdw.shape), const(lng.shape), const(lnb.shape)]
        args += [hist, wdw, bdw, lng, lnb]
        out_specs += [tok(c_conv), pl.BlockSpec((1, CONV_HIST, c_conv), lambda i, j: (i, 0, 0))]
        out_shape += [tok_shape(c_conv), jax.ShapeDtypeStruct((b, CONV_HIST, c_conv), F32)]
        scratch = [pltpu.VMEM((CONV_HIST + tt, c_conv), F32)]
    else:
        out_specs += [tok(c_conv)]
        out_shape += [tok_shape(c_conv)]
    out_specs += [tok(wq.shape[1])] + [tok(KV_DIM)] * 6 + [tok(wg.shape[1])]
    out_shape += [tok_shape(wq.shape[1])] + [tok_shape(KV_DIM)] * 6 + [tok_shape(wg.shape[1])]
    return pl.pallas_call(
        functools.partial(_in_kernel, tt=tt, c_conv=c_conv, fuse_conv=fuse, per_row=per_row),
        grid=(b, nt), in_specs=in_specs, out_specs=out_specs, out_shape=out_shape, scratch_shapes=scratch,
        compiler_params=_cparams("arbitrary", "arbitrary"),
        name="in_proj_conv" if fuse else "in_proj",
    )(*args)


def _conv_small_kernel(xc_ref, wdw_ref, bdw_ref, lng_ref, lnb_ref, o_ref, *, tt):
    o_ref[0] = _conv_ln_silu(xc_ref.at[0], 0, tt, wdw_ref, bdw_ref[...], lng_ref[...], lnb_ref[...])


def _conv_small(xc, wdw, bdw, lng, lnb, *, tt):
    b, rows, c = xc.shape
    const = lambda shape: pl.BlockSpec(shape, lambda i: (0,) * len(shape))
    return pl.pallas_call(
        functools.partial(_conv_small_kernel, tt=tt),
        grid=(b,),
        in_specs=[pl.BlockSpec((1, rows, c), lambda i: (i, 0, 0)),
                  const(wdw.shape), const(bdw.shape), const(lng.shape), const(lnb.shape)],
        out_specs=pl.BlockSpec((1, tt, c), lambda i: (i, 0, 0)),
        out_shape=jax.ShapeDtypeStruct((b, tt, c), F32),
        compiler_params=_cparams("arbitrary"),
        name="conv_small",
    )(xc, wdw, bdw, lng, lnb)


def _compress_weights(w1, w2, pe):
    half = CMP_STRIDE * HEAD_DIM

    def expand(w):
        w = w.reshape(CMP_STRIDE, HEAD_DIM, CMP_HID)
        z = jnp.zeros_like(w)
        g0 = jnp.concatenate([w, z], axis=1).reshape(CHUNK_ROW, CMP_HID)
        g1 = jnp.concatenate([z, w], axis=1).reshape(CHUNK_ROW, CMP_HID)
        return jnp.concatenate([g0, g1], axis=1)

    w1p = jnp.concatenate([expand(w1[:half]), expand(w1[half:])], axis=1).astype(BF16)
    z2 = jnp.zeros_like(w2)
    w2p = jnp.concatenate([jnp.concatenate([w2, z2], axis=1), jnp.concatenate([z2, w2], axis=1)], axis=0).astype(BF16)
    return w1p, w1.astype(BF16), pe.reshape(1, -1), w2p


def _compress_tail(r, w1_ref, pe_ref, w2p_ref):
    n = r.shape[0]
    pe8 = jnp.broadcast_to(pe_ref[...], (SUBLANES, pe_ref.shape[1]))
    c = _bdot(pe8, w1_ref[...])[0:1]
    c2 = jnp.concatenate([c, c], axis=1)
    h = r[:, :2 * CMP_HID] + pltpu.roll(r[:, 2 * CMP_HID:], n - 1, 0) + c2
    return _bdot(_gelu_tanh(h), w2p_ref[...])


def _compress_prompt_kernel(ak_ref, av_ref, w1pk, w1k, pek, w2pk, w1pv, w1v, pev, w2pv, ck_ref, cv_ref):
    ck_ref[0] = _compress_tail(_bdot(ak_ref[0], w1pk[...]), w1k, pek, w2pk)
    cv_ref[0] = _compress_tail(_bdot(av_ref[0], w1pv[...]), w1v, pev, w2pv)


def _compress_prompt(ak, av, wk, wv):
    b, n, _ = ak.shape
    const = lambda a: pl.BlockSpec(a.shape, lambda i: (0,) * a.ndim)
    row = pl.BlockSpec((1, n, CHUNK_ROW), lambda i: (i, 0, 0))
    out = pl.BlockSpec((1, n, KV_DIM), lambda i: (i, 0, 0))
    return pl.pallas_call(
        _compress_prompt_kernel,
        grid=(b,),
        in_specs=[row, row] + [const(a) for a in wk] + [const(a) for a in wv],
        out_specs=[out, out],
        out_shape=[jax.ShapeDtypeStruct((b, n, KV_DIM), F32)] * 2,
        compiler_params=_cparams("arbitrary"),
        name="compress_prompt",
    )(ak, av, *wk, *wv)


def _compress_decode_kernel(pt_ref, pk_ref, pv_ref, w1pk, w1k, pek, w2pk, w1pv, w1v, pev, w2pv, ck_ref, cv_ref,
                            slab_k, slab_v, rbuf, sem, *, n_pages, rows_per_page):
    b = pl.program_id(0)

    def page_copies(pg):
        pid = pt_ref[b * n_pages + pg]
        dst = pl.ds(pl.multiple_of(pg * rows_per_page, rows_per_page), rows_per_page)
        return (pltpu.make_async_copy(pk_ref.at[pid], slab_k.at[dst], sem.at[0]),
                pltpu.make_async_copy(pv_ref.at[pid], slab_v.at[dst], sem.at[1]))

    def start(pg, carry):
        for cp in page_copies(pg):
            cp.start()
        return carry

    def wait(pg, carry):
        for cp in page_copies(pg):
            cp.wait()
        return carry

    lax.fori_loop(0, n_pages, start, 0)
    lax.fori_loop(0, n_pages, wait, 0)
    n = n_pages * rows_per_page
    step = min(n, 256)

    def project(slab, w1p):
        def body(i, carry):
            rows = pl.ds(pl.multiple_of(i * step, step), step)
            rbuf[rows, :] = _bdot(slab[rows, :], w1p[...])
            return carry
        lax.fori_loop(0, n // step, body, 0)
        return rbuf[...]

    ck_ref[0] = _compress_tail(project(slab_k, w1pk), w1k, pek, w2pk)
    cv_ref[0] = _compress_tail(project(slab_v, w1pv), w1v, pev, w2pv)


def _compress_decode(page_table, pool_k, pool_v, wk, wv):
    nb, n_pages = page_table.shape
    rpp = pool_k.shape[1]
    n = n_pages * rpp
    const = lambda a: pl.BlockSpec(a.shape, lambda i, pt: (0,) * a.ndim)
    any_spec = pl.BlockSpec(memory_space=pl.ANY)
    out = pl.BlockSpec((1, n, KV_DIM), lambda i, pt: (i, 0, 0))
    grid_spec = pltpu.PrefetchScalarGridSpec(
        num_scalar_prefetch=1, grid=(nb,),
        in_specs=[any_spec, any_spec] + [const(a) for a in wk] + [const(a) for a in wv],
        out_specs=[out, out],
        scratch_shapes=[pltpu.VMEM((n, CHUNK_ROW), F32), pltpu.VMEM((n, CHUNK_ROW), F32),
                        pltpu.VMEM((n, 4 * CMP_HID), F32), pltpu.SemaphoreType.DMA((2,))])
    return pl.pallas_call(
        functools.partial(_compress_decode_kernel, n_pages=n_pages, rows_per_page=rpp),
        grid_spec=grid_spec,
        out_shape=[jax.ShapeDtypeStruct((nb, n, KV_DIM), F32)] * 2,
        compiler_params=_cparams("arbitrary"),
        name="compress_decode",
    )(page_table.reshape(-1), pool_k, pool_v, *wk, *wv)


def _importance_matrix(n_cmp_rows, n_cmp, n_blk, cols):
    c = jnp.arange(n_cmp_rows)[:, None]
    b = jnp.arange(cols)[None, :]
    d = c - (SEL_RATIO * b - 1)
    return ((d >= 0) & (d <= SEL_RATIO) & (c < n_cmp) & (b < n_blk)).astype(F32)


def _expand_matrix(n_chunks, chunk, blk0):
    key = jnp.arange(n_chunks)[:, None, None] * chunk + jnp.arange(chunk)[None, None, :]
    blk = jnp.asarray(blk0)[:, None, None] + jnp.arange(LANES)[None, :, None]
    return (key // SEL_LEN == blk).astype(BF16)


def _flash_step(qh, kg, vg, bias, mask, m, l, acc):
    s = jnp.where(mask, _bdot_nt(qh, kg) + bias, NEG_INF)
    m_new = jnp.maximum(m, jnp.max(s, -1, keepdims=True))
    alpha = jnp.exp(m - m_new)
    e = jnp.where(mask, jnp.exp(s - m_new), 0.0)
    return m_new, alpha * l + jnp.sum(e, -1, keepdims=True), alpha * acc + _bdot(e, vg)


def _flash_out(l, acc):
    return acc * (1.0 / jnp.maximum(l, 1e-30))


def _attn_prompt_kernel(q_ref, g_ref, ck_ref, cv_ref, ks_ref, vs_ref, kw_ref, vw_ref, mimp_ref, esel_ref, o_ref,
                        *, seq, n_blk, n_pick, win_len):
    qb = Q_BLOCK
    qs = pl.program_id(1) * qb
    t_col = qs + lax.broadcasted_iota(I32, (qb, 1), 0)
    gates = g_ref[0]
    qh = [(q_ref[0, :, h * HEAD_DIM:(h + 1) * HEAD_DIM] * (HEAD_DIM ** -0.5)).astype(BF16) for h in range(N_HEADS)]
    nc = ck_ref.shape[1]
    cmp_end = lax.broadcasted_iota(I32, (1, nc), 1) * CMP_STRIDE + (CMP_LEN - 1)
    dist_c = t_col - cmp_end
    mask_c = dist_c >= 0
    dist_cf = dist_c.astype(F32)
    blk_pad = mimp_ref.shape[1]

    blk_i = lax.broadcasted_iota(I32, (n_blk, qb), 0)
    t_row = qs + lax.broadcasted_iota(I32, (n_blk, qb), 1)
    cur = t_row // SEL_LEN
    forced = (blk_i == 0) | (blk_i == cur) | (blk_i == cur - 1)
    future = blk_i * SEL_LEN > t_row

    win_start = pl.multiple_of(jnp.clip(qs - WINDOW, 0, seq - win_len), qb)
    wpos = win_start + lax.broadcasted_iota(I32, (1, win_len), 1)
    dist_w = t_col - wpos
    mask_w = (dist_w >= 0) & (dist_w <= WINDOW)
    dist_wf = dist_w.astype(F32)
    n_chunks = (qs + qb + SLC_CHUNK - 1) // SLC_CHUNK

    for g in range(N_KV):
        lanes = slice(g * HEAD_DIM, (g + 1) * HEAD_DIM)
        heads = range(g * Q_PER_KV, (g + 1) * Q_PER_KV)
        ckg = ck_ref[0, :, lanes].astype(BF16)
        cvg = cv_ref[0, :, lanes].astype(BF16)
        o_cmp = []
        p_sum = jnp.zeros((qb, nc), F32)
        for h in heads:
            p = _masked_softmax(_bdot_nt(qh[h], ckg) - SLOPES[h] * dist_cf, mask_c)
            p_sum = p_sum + p
            o_cmp.append(_bdot(p, cvg))
        imp = jnp.dot(p_sum, mimp_ref[...], precision=HIGHEST, preferred_element_type=F32)
        imp_t = jnp.transpose(imp)[0:n_blk, :]
        imp_t = jnp.where(future, -FORCE, jnp.where(forced, FORCE, imp_t))
        cnt = jnp.zeros((n_blk, qb), F32)
        for b2 in range(n_blk):
            row = imp_t[b2:b2 + 1, :]
            ahead = (row > imp_t) | ((row == imp_t) & (blk_i > b2))
            cnt = cnt + jnp.where(ahead, 1.0, 0.0)
        sel_t = jnp.where(cnt < n_pick, 1.0, 0.0)
        if n_blk < blk_pad:
            sel_t = jnp.concatenate([sel_t, jnp.zeros((blk_pad - n_blk, qb), F32)], axis=0)
        sel = jnp.transpose(sel_t).astype(BF16)

        def slc_step(c, carry):
            rows = pl.ds(pl.multiple_of(c * SLC_CHUNK, SLC_CHUNK), SLC_CHUNK)
            kg = ks_ref[0, rows, lanes].astype(BF16)
            vg = vs_ref[0, rows, lanes].astype(BF16)
            msel = jnp.dot(sel, esel_ref[c], preferred_element_type=F32)
            dist = t_col - (c * SLC_CHUNK + lax.broadcasted_iota(I32, (1, SLC_CHUNK), 1))
            mask = (dist >= 0) & (msel > 0.5)
            dist_f = dist.astype(F32)
            out = []
            for i, h in enumerate(heads):
                out += _flash_step(qh[h], kg, vg, -SLOPES[h] * dist_f, mask, *carry[3 * i:3 * i + 3])
            return tuple(out)

        init = (jnp.full((qb, 1), NEG_INF, F32), jnp.zeros((qb, 1), F32), jnp.zeros((qb, HEAD_DIM), F32)) * Q_PER_KV
        fin = lax.fori_loop(0, n_chunks, slc_step, init)
        rows_w = pl.ds(win_start, win_len)
        kwg = kw_ref[0, rows_w, lanes].astype(BF16)
        vwg = vw_ref[0, rows_w, lanes].astype(BF16)
        for i, h in enumerate(heads):
            o_slc = _flash_out(fin[3 * i + 1], fin[3 * i + 2])
            p = _masked_softmax(_bdot_nt(qh[h], kwg) - SLOPES[h] * dist_wf, mask_w)
            o_win = _bdot(p, vwg)
            c0 = h * N_GATES
            o_ref[0, :, h * HEAD_DIM:(h + 1) * HEAD_DIM] = (gates[:, c0:c0 + 1] * o_cmp[i] + gates[:, c0 + 1:c0 + 2] * o_slc
                                                           + gates[:, c0 + 2:c0 + 3] * o_win)


def _attn_prompt(q, gates, ck, cv, ks, vs, kw, vw):
    b, t, att = q.shape
    assert t % SLC_CHUNK == 0 and t % Q_BLOCK == 0
    n_blk = t // SEL_LEN
    n_cmp = t // CMP_STRIDE - 1
    assert n_blk <= LANES and ck.shape[1] == t // CMP_STRIDE
    n_pick = min(N_SEL, n_blk)
    win_len = min(WINDOW + Q_BLOCK, t)
    mimp = _importance_matrix(ck.shape[1], n_cmp, n_blk, LANES)
    n_ch = t // SLC_CHUNK
    esel = _expand_matrix(n_ch, SLC_CHUNK, [0] * n_ch)
    tok = lambda n: pl.BlockSpec((1, Q_BLOCK, n), lambda i, j: (i, j, 0))
    full = lambda a: pl.BlockSpec((1,) + a.shape[1:], lambda i, j: (i, 0, 0))
    const = lambda a: pl.BlockSpec(a.shape, lambda i, j: (0,) * a.ndim)
    return pl.pallas_call(
        functools.partial(_attn_prompt_kernel, seq=t, n_blk=n_blk, n_pick=n_pick, win_len=win_len),
        grid=(b, t // Q_BLOCK),
        in_specs=[tok(att), tok(gates.shape[2]), full(ck), full(cv), full(ks), full(vs), full(kw), full(vw),
                  const(mimp), const(esel)],
        out_specs=tok(att),
        out_shape=jax.ShapeDtypeStruct((b, t, att), F32),
        compiler_params=_cparams("arbitrary", "arbitrary"),
        name="attn_prompt",
    )(q, gates, ck, cv, ks, vs, kw, vw, mimp, esel)


def _attn_decode_kernel(pt_ref, q_ref, g_ref, ck_ref, cv_ref, ksn_ref, vsn_ref, kwn_ref, vwn_ref, wk_ref, wv_ref,
                        pk_ref, pv_ref, mimp_ref, esel_ref, o_ref, slab_k, slab_v, sem,
                        *, n_pages, page, past, n_blk, n_pick, tq):
    b = pl.program_id(0)

    def page_copies(pg):
        pid = pt_ref[b * n_pages + pg]
        dst = pl.ds(pl.multiple_of(pg * page, page), page)
        return (pltpu.make_async_copy(pk_ref.at[pid], slab_k.at[dst], sem.at[0]),
                pltpu.make_async_copy(pv_ref.at[pid], slab_v.at[dst], sem.at[1]))

    def start(pg, carry):
        for cp in page_copies(pg):
            cp.start()
        return carry

    def wait(pg, carry):
        for cp in page_copies(pg):
            cp.wait()
        return carry

    lax.fori_loop(0, n_pages, start, 0)

    rows = Q_PER_KV * DEC_Q
    qi = lax.broadcasted_iota(I32, (rows, 1), 0) % DEC_Q
    t_col = past + qi % tq
    hl = lax.broadcasted_iota(I32, (rows, 1), 0) // DEC_Q
    nc = ck_ref.shape[1]
    cmp_end = lax.broadcasted_iota(I32, (1, nc), 1) * CMP_STRIDE + (CMP_LEN - 1)
    dist_c = t_col - cmp_end
    mask_c = dist_c >= 0
    dist_cf = dist_c.astype(F32)
    blk_pad = mimp_ref.shape[1]
    t8 = past + lax.broadcasted_iota(I32, (DEC_Q, 1), 0) % tq
    blk_l = lax.broadcasted_iota(I32, (DEC_Q, blk_pad), 1)
    cur = t8 // SEL_LEN
    forced = (blk_l == 0) | (blk_l == cur) | (blk_l == cur - 1)
    future = blk_l * SEL_LEN > t8
    pad_blk = blk_l >= n_blk
    b_sub = lax.broadcasted_iota(I32, (blk_pad, blk_pad), 0)
    b_lane = lax.broadcasted_iota(I32, (blk_pad, blk_pad), 1)
    n_chunks = past // DEC_CHUNK
    new_rows = ksn_ref.shape[1]

    sels, slopes, o_cmps = [], [], []
    for g in range(N_KV):
        lanes = slice(g * HEAD_DIM, (g + 1) * HEAD_DIM)
        slope = jnp.zeros((rows, 1), F32)
        for i in range(Q_PER_KV):
            slope = jnp.where(hl == i, SLOPES[g * Q_PER_KV + i], slope)
        slopes.append(slope)
        qg = (q_ref[0, g] * (HEAD_DIM ** -0.5)).astype(BF16)
        p = _masked_softmax(_bdot_nt(qg, ck_ref[0, :, lanes]) - slope * dist_cf, mask_c)
        o_cmps.append(_bdot(p, cv_ref[0, :, lanes]))
        p_sum = p[0:DEC_Q]
        for i in range(1, Q_PER_KV):
            p_sum = p_sum + p[i * DEC_Q:(i + 1) * DEC_Q]
        imp = jnp.dot(p_sum, mimp_ref[...], precision=HIGHEST, preferred_element_type=F32)
        imp = jnp.where(pad_blk, -3e38, jnp.where(future, -FORCE, jnp.where(forced, FORCE, imp)))
        imp_t = jnp.transpose(jnp.concatenate([imp, jnp.zeros((LANES - DEC_Q, blk_pad), F32)], axis=0))
        sel_rows = []
        for r in range(DEC_Q):
            col = imp_t[:, r:r + 1]
            row = imp[r:r + 1, :]
            ahead = (col > row) | ((col == row) & (b_sub < b_lane))
            cnt = jnp.sum(jnp.where(ahead, 1.0, 0.0), axis=0, keepdims=True)
            sel_rows.append(jnp.where(cnt < n_pick, 1.0, 0.0))
        sel8 = jnp.concatenate(sel_rows, axis=0)
        sels.append(jnp.concatenate([sel8] * Q_PER_KV, axis=0))

    lax.fori_loop(0, n_pages, wait, 0)

    for g in range(N_KV):
        lanes = slice(g * HEAD_DIM, (g + 1) * HEAD_DIM)
        slope = slopes[g]
        qg = (q_ref[0, g] * (HEAD_DIM ** -0.5)).astype(BF16)
        sel = sels[g]
        sel_b = sel.astype(BF16)
        m = jnp.full((rows, 1), NEG_INF, F32)
        l = jnp.zeros((rows, 1), F32)
        acc = jnp.zeros((rows, HEAD_DIM), F32)
        per_mat = LANES * SEL_LEN // DEC_CHUNK
        for c in range(n_chunks):
            krows = slice(c * DEC_CHUNK, (c + 1) * DEC_CHUNK)
            lane0 = (c // per_mat) * LANES
            msel = jnp.dot(sel_b[:, lane0:lane0 + LANES], esel_ref[c % per_mat], preferred_element_type=F32)
            dist = t_col - (c * DEC_CHUNK + lax.broadcasted_iota(I32, (1, DEC_CHUNK), 1))
            mask = (dist >= 0) & (msel > 0.5)
            m, l, acc = _flash_step(qg, slab_k[krows, lanes], slab_v[krows, lanes], -slope * dist.astype(F32), mask, m, l, acc)
        dist = t_col - (past + lax.broadcasted_iota(I32, (1, new_rows), 1))
        cur_blk = past // SEL_LEN
        mask = (dist >= 0) & (sel[:, cur_blk:cur_blk + 1] > 0.5)
        m, l, acc = _flash_step(qg, ksn_ref[0, :, lanes], vsn_ref[0, :, lanes], -slope * dist.astype(F32), mask, m, l, acc)
        o_slc = _flash_out(l, acc)
        wlen = wk_ref.shape[1]
        dist = t_col - (past - wlen + lax.broadcasted_iota(I32, (1, wlen), 1))
        mask = (dist >= 0) & (dist <= WINDOW)
        m = jnp.full((rows, 1), NEG_INF, F32)
        l = jnp.zeros((rows, 1), F32)
        acc = jnp.zeros((rows, HEAD_DIM), F32)
        m, l, acc = _flash_step(qg, wk_ref[0, :, lanes], wv_ref[0, :, lanes], -slope * dist.astype(F32), mask, m, l, acc)
        dist = t_col - (past + lax.broadcasted_iota(I32, (1, new_rows), 1))
        mask = (dist >= 0) & (dist <= WINDOW)
        m, l, acc = _flash_step(qg, kwn_ref[0, :, lanes], vwn_ref[0, :, lanes], -slope * dist.astype(F32), mask, m, l, acc)
        o_win = _flash_out(l, acc)
        gt = g_ref[0, g]
        o_ref[0, g] = gt[:, 0:1] * o_cmps[g] + gt[:, 1:2] * o_slc + gt[:, 2:3] * o_win


def _attn_decode(page_table, q, gates, ck, cv, ksn, vsn, kwn, vwn, win_k, win_v, pool_k, pool_v):
    nb, tq, att = q.shape
    n_pages = page_table.shape[1]
    page = pool_k.shape[1]
    past = n_pages * page
    assert tq <= DEC_Q and past % DEC_CHUNK == 0 and past % SEL_LEN == 0 and (past + tq - 1) // SEL_LEN == past // SEL_LEN
    assert win_k.shape[1] == WINDOW and ck.shape[1] * CMP_STRIDE == past
    t_pad = -(-(past + tq) // SEL_LEN) * SEL_LEN
    n_blk = t_pad // SEL_LEN
    n_cmp = t_pad // CMP_STRIDE - 1
    n_pick = min(N_SEL, n_blk)
    blk_pad = -(-n_blk // LANES) * LANES
    mimp = _importance_matrix(ck.shape[1], n_cmp, n_blk, blk_pad)
    per_mat = LANES * SEL_LEN // DEC_CHUNK
    esel = _expand_matrix(per_mat, DEC_CHUNK, [0] * per_mat)

    def dec_rows(a, width):
        a = a.reshape(nb, tq, N_KV, Q_PER_KV, width)
        a = jnp.take(a, jnp.arange(DEC_Q) % tq, axis=1)
        return a.transpose(0, 2, 3, 1, 4).reshape(nb, N_KV, Q_PER_KV * DEC_Q, width)

    q_d = dec_rows(q, HEAD_DIM)
    g_d = dec_rows(gates[:, :, :N_HEADS * N_GATES], N_GATES)
    pad_new = lambda a: jnp.pad(a, ((0, 0), (0, DEC_Q - tq), (0, 0)))
    ksn, vsn, kwn, vwn = [pad_new(a) for a in (ksn, vsn, kwn, vwn)]
    rows = Q_PER_KV * DEC_Q
    blk = lambda a: pl.BlockSpec((1,) + a.shape[1:], lambda i, pt: (i,) + (0,) * (a.ndim - 1))
    const = lambda a: pl.BlockSpec(a.shape, lambda i, pt: (0,) * a.ndim)
    any_spec = pl.BlockSpec(memory_space=pl.ANY)
    grid_spec = pltpu.PrefetchScalarGridSpec(
        num_scalar_prefetch=1, grid=(nb,),
        in_specs=[blk(q_d), blk(g_d), blk(ck), blk(cv), blk(ksn), blk(vsn), blk(kwn), blk(vwn), blk(win_k), blk(win_v),
                  any_spec, any_spec, const(mimp), const(esel)],
        out_specs=pl.BlockSpec((1, N_KV, rows, HEAD_DIM), lambda i, pt: (i, 0, 0, 0)),
        scratch_shapes=[pltpu.VMEM((past, KV_DIM), F32), pltpu.VMEM((past, KV_DIM), F32), pltpu.SemaphoreType.DMA((2,))])
    o = pl.pallas_call(
        functools.partial(_attn_decode_kernel, n_pages=n_pages, page=page, past=past, n_blk=n_blk, n_pick=n_pick, tq=tq),
        grid_spec=grid_spec,
        out_shape=jax.ShapeDtypeStruct((nb, N_KV, rows, HEAD_DIM), F32),
        compiler_params=_cparams("arbitrary"),
        name="attn_decode",
    )(page_table.reshape(-1), q_d, g_d, ck, cv, ksn, vsn, kwn, vwn, win_k, win_v, pool_k, pool_v, mimp, esel)
    o = o.reshape(nb, N_KV, Q_PER_KV, DEC_Q, HEAD_DIM)[:, :, :, :tq]
    return o.transpose(0, 3, 1, 2, 4).reshape(nb, tq, att)


def _out_kernel(x_ref, conv_ref, att_ref, mod_ref, bc_ref, ba_ref, woc_ref, woa_ref, g1_ref, b1_ref, wr_ref, br_ref,
                cnt0_ref, x1_ref, u2_ref, idx_ref, w_ref, pos_ref, cnt_ref, carry_ref, *, tt, alpha, per_row):
    first = (pl.program_id(0) == 0) & (pl.program_id(1) == 0)

    @pl.when(first)
    def _():
        carry_ref[...] = cnt0_ref[...]

    gate1 = _mod_rows(mod_ref, 2, per_row)
    shift2 = _mod_rows(mod_ref, 3, per_row)
    scale2 = _mod_rows(mod_ref, 4, per_row)
    mix = _bdot(conv_ref[0] * bc_ref[...], woc_ref[...]) + _bdot(att_ref[0] * ba_ref[...], woa_ref[...])
    x1 = _layer_norm(alpha * x_ref[0] + gate1 * mix, g1_ref[...], b1_ref[...])
    x1_ref[0] = x1
    u2 = x1 * (1.0 + scale2) + shift2
    u2_ref[0] = u2
    logits = jnp.dot(u2, wr_ref[...], precision=HIGHEST, preferred_element_type=F32) + br_ref[...]
    lane = lax.broadcasted_iota(I32, logits.shape, 1)
    vals = logits
    top_v, top_i = [], []
    for _k in range(TOP_K):
        m = jnp.max(vals, axis=-1, keepdims=True)
        i = jnp.min(jnp.where(vals == m, lane, LANES), axis=-1, keepdims=True)
        top_v.append(m)
        top_i.append(i)
        vals = jnp.where(lane == i, -jnp.inf, vals)
    e = [jnp.exp(v - top_v[0]) for v in top_v]
    inv = 1.0 / (e[0] + e[1] + e[2] + e[3])
    onehot = [jnp.where(lane == i, 1.0, 0.0) for i in top_i]
    assigned = onehot[0] + onehot[1] + onehot[2] + onehot[3]
    r_i = lax.broadcasted_iota(I32, (tt, tt), 0)
    c_i = lax.broadcasted_iota(I32, (tt, tt), 1)
    before = jnp.where(r_i > c_i, 1.0, 0.0).astype(BF16)
    rank = carry_ref[...] + jnp.dot(before, assigned.astype(BF16), preferred_element_type=F32)
    idx_o = jnp.zeros(logits.shape, I32)
    w_o = jnp.zeros(logits.shape, F32)
    pos_o = jnp.zeros(logits.shape, F32)
    for k in range(TOP_K):
        idx_o = jnp.where(lane == k, top_i[k], idx_o)
        w_o = jnp.where(lane == k, e[k] * inv, w_o)
        pos_o = jnp.where(lane == k, jnp.sum(onehot[k] * rank, axis=-1, keepdims=True), pos_o)
    idx_ref[0] = idx_o
    w_ref[0] = w_o
    pos_ref[0] = pos_o.astype(I32)
    carry_ref[...] = carry_ref[...] + jnp.sum(assigned, axis=0, keepdims=True)
    cnt_ref[...] = carry_ref[...]


def _out_proj(x, conv_out, att, mod, vecs, wts, cnt0, *, tt, alpha, per_row):
    b, t, d = x.shape
    bc, ba, g1, b1, br = vecs
    woc, woa, wr = wts
    const = lambda a: pl.BlockSpec(a.shape, lambda i, j: (0,) * a.ndim)
    tok = lambda n: pl.BlockSpec((1, tt, n), lambda i, j: (i, j, 0))
    mod_spec = (pl.BlockSpec((6, tt, d), lambda i, j: (0, j, 0)) if per_row
                else pl.BlockSpec((1, 6, d), lambda i, j: (i, 0, 0)))
    tok_shape = lambda n, dt: jax.ShapeDtypeStruct((b, t, n), dt)
    return pl.pallas_call(
        functools.partial(_out_kernel, tt=tt, alpha=alpha, per_row=per_row),
        grid=(b, t // tt),
        in_specs=[tok(d), tok(conv_out.shape[2]), tok(att.shape[2]), mod_spec, const(bc), const(ba), const(woc), const(woa),
                  const(g1), const(b1), const(wr), const(br), const(cnt0)],
        out_specs=[tok(d), tok(d), tok(LANES), tok(LANES), tok(LANES), pl.BlockSpec((1, LANES), lambda i, j: (0, 0))],
        out_shape=[tok_shape(d, F32), tok_shape(d, F32), tok_shape(LANES, I32), tok_shape(LANES, F32), tok_shape(LANES, I32),
                   jax.ShapeDtypeStruct((1, LANES), F32)],
        scratch_shapes=[pltpu.VMEM((1, LANES), F32)],
        compiler_params=_cparams("arbitrary", "arbitrary"),
        name="out_proj_router",
    )(x, conv_out, att, mod, bc, ba, woc, woa, g1, b1, wr, br, cnt0)


def _gather_pipeline(i, n, src_ref, cur_ref, nxt_ref, bufs, sem, rows, compute):
    def row_copy(src, buf, s, r):
        return pltpu.make_async_copy(src_ref.at[pl.ds(src, 1)], buf.at[pl.ds(r, 1)], sem.at[s])

    def whole(buf, s):
        return pltpu.make_async_copy(src_ref.at[pl.ds(0, rows)], buf, sem.at[s])

    @pl.when(i == 0)
    def _():
        def body(r, carry):
            row_copy(cur_ref[0, 0, r], bufs[0], 0, r).start()
            return carry
        lax.fori_loop(0, rows, body, 0)

    def step(s):
        whole(bufs[s], s).wait()
        for r in range(rows):
            row_copy(nxt_ref[0, 0, r], bufs[1 - s], 1 - s, r).start()
        compute(bufs[s])

        @pl.when(i == n - 1)
        def _():
            whole(bufs[1 - s], 1 - s).wait()

    for s in range(2):
        pl.when(i % 2 == s)(functools.partial(step, s))


def _expert_kernel(be_ref, cur_ref, nxt_ref, u_ref, wgu_ref, bgu_ref, wdn_ref, bdn_ref, y_ref, xb0, xb1, sem, *, bm, d_ff):
    def compute(buf):
        gu = jnp.dot(buf[...].astype(BF16), wgu_ref[0], preferred_element_type=F32) + bgu_ref[0]
        gl = jnp.minimum(gu[:, :d_ff], SWIGLU_LIMIT)
        lin = jnp.clip(gu[:, d_ff:], -SWIGLU_LIMIT, SWIGLU_LIMIT)
        act = gl * _sigmoid(SWIGLU_ALPHA * gl) * (lin + 1.0)
        y_ref[...] = jnp.dot(act.astype(BF16), wdn_ref[0], preferred_element_type=F32) + bdn_ref[0]

    _gather_pipeline(pl.program_id(0), pl.num_programs(0), u_ref, cur_ref, nxt_ref, (xb0, xb1), sem, bm, compute)


def _expert_ffn(u2, slot_tok, blk_e, wgu, bgu, wdn, bdn, *, bm):
    n_blocks = slot_tok.shape[0]
    d = u2.shape[1]
    d_ff = wdn.shape[1]
    slot_tok = slot_tok.reshape(n_blocks, 1, bm)
    smem = lambda f: pl.BlockSpec((1, 1, bm), f, memory_space=pltpu.SMEM)
    grid_spec = pltpu.PrefetchScalarGridSpec(
        num_scalar_prefetch=1, grid=(n_blocks,),
        in_specs=[smem(lambda i, be: (i, 0, 0)),
                  smem(lambda i, be: (jnp.minimum(i + 1, n_blocks - 1), 0, 0)),
                  pl.BlockSpec(memory_space=pl.ANY),
                  pl.BlockSpec((1, d, 2 * d_ff), lambda i, be: (be[i], 0, 0)),
                  pl.BlockSpec((1, 1, 2 * d_ff), lambda i, be: (be[i], 0, 0)),
                  pl.BlockSpec((1, d_ff, d), lambda i, be: (be[i], 0, 0)),
                  pl.BlockSpec((1, 1, d), lambda i, be: (be[i], 0, 0))],
        out_specs=pl.BlockSpec((bm, d), lambda i, be: (i, 0)),
        scratch_shapes=[pltpu.VMEM((bm, d), F32), pltpu.VMEM((bm, d), F32), pltpu.SemaphoreType.DMA((2,))])
    return pl.pallas_call(
        functools.partial(_expert_kernel, bm=bm, d_ff=d_ff),
        grid_spec=grid_spec,
        out_shape=jax.ShapeDtypeStruct((n_blocks * bm, d), F32),
        compiler_params=_cparams("arbitrary"),
        name="expert_ffn",
    )(blk_e, slot_tok, slot_tok, u2, wgu, bgu, wdn, bdn)


def _combine_kernel(cur_ref, nxt_ref, ys_ref, w_ref, x1_ref, mod_ref, g2_ref, b2_ref, y_ref, gb0, gb1, sem,
                    *, tc, alpha, per_row):
    def compute(buf):
        w = w_ref[...]
        f = w[:, 0:1] * buf[0:tc, :]
        for k in range(1, TOP_K):
            f = f + w[:, k:k + 1] * buf[k * tc:(k + 1) * tc, :]
        gate2 = mod_ref[5] if per_row else mod_ref[0, 5:6, :]
        y_ref[...] = _layer_norm(alpha * x1_ref[...] + gate2 * f, g2_ref[...], b2_ref[...])

    _gather_pipeline(pl.program_id(0), pl.num_programs(0), ys_ref, cur_ref, nxt_ref, (gb0, gb1), sem, TOP_K * tc, compute)


def _combine(ys, dest_t, top_w, x1, mod, g2, b2, *, tc, alpha, per_row, tiles_per_batch):
    n, d = x1.shape
    n_tiles = n // tc
    dest_t = dest_t.reshape(n_tiles, 1, TOP_K * tc)
    smem = lambda f: pl.BlockSpec((1, 1, TOP_K * tc), f, memory_space=pltpu.SMEM)
    const = lambda a: pl.BlockSpec(a.shape, lambda i: (0,) * a.ndim)
    mod_spec = (pl.BlockSpec((6, tc, d), lambda i: (0, i, 0)) if per_row
                else pl.BlockSpec((1, 6, d), lambda i: (i // tiles_per_batch, 0, 0)))
    return pl.pallas_call(
        functools.partial(_combine_kernel, tc=tc, alpha=alpha, per_row=per_row),
        grid=(n_tiles,),
        in_specs=[smem(lambda i: (i, 0, 0)), smem(lambda i: (jnp.minimum(i + 1, n_tiles - 1), 0, 0)),
                  pl.BlockSpec(memory_space=pl.ANY),
                  pl.BlockSpec((tc, LANES), lambda i: (i, 0)), pl.BlockSpec((tc, d), lambda i: (i, 0)),
                  mod_spec, const(g2), const(b2)],
        out_specs=pl.BlockSpec((tc, d), lambda i: (i, 0)),
        out_shape=jax.ShapeDtypeStruct((n, d), F32),
        scratch_shapes=[pltpu.VMEM((TOP_K * tc, d), F32), pltpu.VMEM((TOP_K * tc, d), F32), pltpu.SemaphoreType.DMA((2,))],
        compiler_params=_cparams("arbitrary"),
        name="combine_ln2",
    )(dest_t, dest_t, ys, top_w, x1, mod, g2, b2)


def _moe(u2, x1, top_i, top_w, pos, counts, mod, experts, g2, b2, *, bm, tc, alpha, per_row, tiles_per_batch):
    n, d = u2.shape
    wgu, bgu, wdn, bdn = experts
    n_exp = wgu.shape[0]
    n_blocks = -(-n * TOP_K // bm) + n_exp
    counts = counts.astype(I32)
    padded = (counts + bm - 1) // bm * bm
    pad_end = jnp.cumsum(padded)
    pad_start = pad_end - padded
    dest = pad_start[top_i] + pos
    tok = jnp.broadcast_to(jnp.arange(n, dtype=I32)[:, None], (n, TOP_K))
    slot_tok = jnp.zeros((n_blocks * bm,), I32).at[dest.reshape(-1)].set(tok.reshape(-1))
    blk_start = jnp.arange(n_blocks, dtype=I32) * bm
    blk_e = jnp.minimum(jnp.sum((pad_end[None, :] <= blk_start[:, None]).astype(I32), axis=1), n_exp - 1)
    ys = _expert_ffn(u2, slot_tok.reshape(n_blocks, bm), blk_e, wgu, bgu, wdn, bdn, bm=bm)
    dest_t = dest.reshape(n // tc, tc, TOP_K).transpose(0, 2, 1).reshape(n // tc, TOP_K * tc)
    return _combine(ys, dest_t, top_w, x1, mod, g2, b2, tc=tc, alpha=alpha, per_row=per_row, tiles_per_batch=tiles_per_batch)


def _pick_tile(n, target):
    t = min(n, target)
    while n % t:
        t -= 1
    return t


def kernel(x_prompt, x_sample, c_prompt, c_sample, cache_cmp_k, cache_cmp_v, cache_slc_k, cache_slc_v, state_win_k, state_win_v, state_conv, page_table, w_ada, b_ada, w_in, w_dw, b_dw, conv_ln_g, conv_ln_b, w_ck1, w_ck2, pe_k, w_cv1, w_cv2, pe_v, beta_conv, beta_attn, w_out, ln1_g, ln1_b, w_router, b_router, w_gate_up, b_gate_up, w_down, b_down, ln2_g, ln2_b):
    depth = w_ada.shape[0]
    assert depth == 1, "single-layer trunk"
    alpha = (2 * depth) ** 0.25
    bp, seq, d = x_prompt.shape
    bs, tq, _ = x_sample.shape
    c_conv = w_dw.shape[2]
    att_dim = N_HEADS * HEAD_DIM
    n_pool, page = cache_cmp_k.shape[1], cache_cmp_k.shape[2]
    assert state_conv.shape[2] == CONV_K - 1 and page % CMP_STRIDE == 0

    wi = w_in[0]
    cuts = [2 * c_conv, 2 * c_conv + att_dim, 2 * c_conv + att_dim + 6 * KV_DIM]
    wa = wi[:, :cuts[0]].astype(BF16)
    wq = wi[:, cuts[0]:cuts[1]].astype(BF16)
    wkv = wi[:, cuts[1]:cuts[2]].astype(BF16)
    wg = jnp.pad(wi[:, cuts[2]:], ((0, 0), (0, LANES - N_HEADS * N_GATES))).astype(BF16)
    in_w = (wa, wq, wkv, wg)
    wdw = jnp.pad(w_dw[0], ((0, CONV_HIST - CONV_K), (0, 0)))
    row = lambda a: a.reshape(1, -1)
    conv_vecs = (wdw, row(b_dw[0]), row(conv_ln_g[0]), row(conv_ln_b[0]))
    wk_c = _compress_weights(w_ck1[0], w_ck2[0], pe_k[0])
    wv_c = _compress_weights(w_cv1[0], w_cv2[0], pe_v[0])
    out_vecs = (row(beta_conv[0]), row(beta_attn[0]), row(ln1_g[0]), row(ln1_b[0]),
                jnp.pad(row(b_router[0]), ((0, 0), (0, LANES - N_EXPERTS)), constant_values=NEG_INF))
    out_w = (w_out[0][:c_conv].astype(BF16), w_out[0][c_conv:].astype(BF16),
             jnp.pad(w_router[0], ((0, 0), (0, LANES - N_EXPERTS))))
    experts = (w_gate_up[0].astype(BF16), b_gate_up[0][:, None, :], w_down[0].astype(BF16), b_down[0][:, None, :])
    g2, b2 = row(ln2_g[0]), row(ln2_b[0])

    mod = _ada(jnp.concatenate([c_prompt, c_sample], axis=0), w_ada[0], row(b_ada[0])).reshape(bp + bs, 6, d)
    mod_p = mod[:bp]
    mod_s = jnp.repeat(mod[bp:], tq, axis=0).transpose(1, 0, 2)

    tt = _pick_tile(seq, 512)
    hist0 = jnp.zeros((bp, CONV_HIST, c_conv), F32)
    (conv_p, tail_p, q_p, kc_p, vc_p, ks_p, vs_p, kw_p, vw_p, gates_p) = _in_proj(
        x_prompt, mod_p, in_w, (hist0,) + conv_vecs, tt=tt, per_row=False)
    n_chunk = seq // CMP_STRIDE
    ck_p, cv_p = _compress_prompt(kc_p.reshape(bp, n_chunk, CHUNK_ROW), vc_p.reshape(bp, n_chunk, CHUNK_ROW), wk_c, wv_c)
    att_p = _attn_prompt(q_p, gates_p, ck_p, cv_p, ks_p, vs_p, kw_p, vw_p)
    to = _pick_tile(seq, 256)
    cnt0 = jnp.zeros((1, LANES), F32)
    x1_p, u2_p, ti_p, tw_p, pos_p, cnt_p = _out_proj(x_prompt, conv_p, att_p, mod_p, out_vecs, out_w, cnt0,
                                                     tt=to, alpha=alpha, per_row=False)
    n_p = bp * seq
    bm_p = _pick_tile(n_p * TOP_K, 512)
    tc_p = _pick_tile(seq, 128)
    y_p = _moe(u2_p.reshape(n_p, d), x1_p.reshape(n_p, d), ti_p.reshape(n_p, LANES)[:, :TOP_K], tw_p.reshape(n_p, LANES),
               pos_p.reshape(n_p, LANES)[:, :TOP_K], cnt_p[0, :N_EXPERTS], mod_p, experts, g2, b2,
               bm=bm_p, tc=tc_p, alpha=alpha, per_row=False, tiles_per_batch=seq // tc_p).reshape(bp, seq, d)

    n_s = bs * tq
    (glu_s, q_s, kc_s, vc_s, ks_s, vs_s, kw_s, vw_s, gates_s) = _in_proj(
        x_sample.reshape(1, n_s, d), mod_s, in_w, None, tt=n_s, per_row=True)
    unflat = lambda a: a.reshape(bs, tq, a.shape[-1])
    glu_s, q_s, kc_s, vc_s, ks_s, vs_s, kw_s, vw_s, gates_s = map(unflat, (glu_s, q_s, kc_s, vc_s, ks_s, vs_s, kw_s, vw_s, gates_s))
    xc_s = jnp.concatenate([jnp.zeros((bs, CONV_HIST - (CONV_K - 1), c_conv), F32), state_conv[0], glu_s], axis=1)
    conv_s = _conv_small(xc_s, *conv_vecs, tt=tq)
    chunks_per_page = page // CMP_STRIDE
    ck_s, cv_s = _compress_decode(page_table, cache_cmp_k[0].reshape(n_pool, chunks_per_page, CHUNK_ROW),
                                  cache_cmp_v[0].reshape(n_pool, chunks_per_page, CHUNK_ROW), wk_c, wv_c)
    att_s = _attn_decode(page_table, q_s, gates_s, ck_s, cv_s, ks_s, vs_s, kw_s, vw_s, state_win_k[0].reshape(bs, -1, KV_DIM),
                         state_win_v[0].reshape(bs, -1, KV_DIM), cache_slc_k[0].reshape(n_pool, page, KV_DIM),
                         cache_slc_v[0].reshape(n_pool, page, KV_DIM))
    flat = lambda a: a.reshape(1, n_s, a.shape[-1])
    x1_s, u2_s, ti_s, tw_s, pos_s, cnt_s = _out_proj(flat(x_sample), flat(conv_s), flat(att_s), mod_s, out_vecs, out_w, cnt0,
                                                     tt=n_s, alpha=alpha, per_row=True)
    bm_s = max(SUBLANES, min(512, n_s * TOP_K // N_EXPERTS))
    y_s = _moe(u2_s.reshape(n_s, d), x1_s.reshape(n_s, d), ti_s.reshape(n_s, LANES)[:, :TOP_K], tw_s.reshape(n_s, LANES),
               pos_s.reshape(n_s, LANES)[:, :TOP_K], cnt_s[0, :N_EXPERTS], mod_s, experts, g2, b2,
               bm=bm_s, tc=n_s, alpha=alpha, per_row=True, tiles_per_batch=1).reshape(bs, tq, d)

    heads = lambda a: a.reshape(1, a.shape[0], a.shape[1], N_KV, HEAD_DIM)
    win_keep = min(WINDOW, seq)
    new_p = (heads(kc_p), heads(vc_p), heads(ks_p), heads(vs_p), heads(kw_p[:, seq - win_keep:]), heads(vw_p[:, seq - win_keep:]),
             tail_p[None, :, CONV_HIST - (CONV_K - 1):])
    win_buf = state_win_k.shape[2]
    keep = lambda old, new: jnp.concatenate([old[0], new.reshape(bs, tq, N_KV, HEAD_DIM)], axis=1)[None, :, -win_buf:]
    new_s = (heads(kc_s), heads(vc_s), heads(ks_s), heads(vs_s), keep(state_win_k, kw_s), keep(state_win_v, vw_s),
             jnp.concatenate([state_conv[0], glu_s], axis=1)[None, :, -(CONV_K - 1):])
    return (y_p, y_s) + new_p + new_s
```

```python
import functools

import jax
import jax.numpy as jnp
from jax import lax
from jax.experimental import pallas as pl
from jax.experimental.pallas import tpu as pltpu

F32 = jnp.float32
BF16 = jnp.bfloat16
I32 = jnp.int32

N_HEADS = 8
HEAD_DIM = 64
N_KV = 2
Q_PER_KV = N_HEADS // N_KV
KV_DIM = N_KV * HEAD_DIM
CMP_LEN = 32
CMP_STRIDE = 16
CMP_HID = 2 * HEAD_DIM
SEL_LEN = 64
SEL_RATIO = SEL_LEN // CMP_STRIDE
N_SEL = 16
WINDOW = 512
Q_BLOCK = 128
N_GATES = 3
N_EXPERTS = 32
TOP_K = 4
CONV_K = 31
SWIGLU_LIMIT = 7.0
SWIGLU_ALPHA = 1.702
LN_EPS = 1e-5
NEG_INF = -1e30
FORCE = 1e9
SLOPES = tuple(2.0 ** (-8.0 * (h + 1) / N_HEADS) for h in range(N_HEADS))

LANES = 128
SUBLANES = 8
VMEM_LIMIT_BYTES = 56 * 1024 * 1024

CONV_HIST = 32
CHUNK_ROW = CMP_STRIDE * KV_DIM
SLC_CHUNK = 512
DEC_CHUNK = 2048
DEC_Q = 8
V_AUG = HEAD_DIM + 16
LOG2E = 1.4426950408889634
HIGHEST = lax.Precision.HIGHEST


def _cparams(*sem):
    return pltpu.CompilerParams(dimension_semantics=sem, vmem_limit_bytes=VMEM_LIMIT_BYTES)


def _sigmoid(x):
    return 1.0 / (1.0 + jnp.exp(-x))


def _bdot(a, b):
    return jnp.dot(a.astype(BF16), b.astype(BF16), preferred_element_type=F32)


def _bdot_nt(a, b):
    return lax.dot_general(a.astype(BF16), b.astype(BF16), (((1,), (1,)), ((), ())), preferred_element_type=F32)


def _layer_norm(x, g, b):
    mu = jnp.mean(x, -1, keepdims=True)
    xc = x - mu
    var = jnp.mean(xc * xc, -1, keepdims=True)
    return xc * lax.rsqrt(var + LN_EPS) * g + b


def _gelu_tanh(x):
    return 0.5 * x * (1.0 + jnp.tanh(0.7978845608028654 * (x + 0.044715 * (x * x * x))))


def _masked_softmax(s, mask):
    s = jnp.where(mask, s, NEG_INF)
    e = jnp.where(mask, jnp.exp(s - jnp.max(s, -1, keepdims=True)), 0.0)
    return e * (1.0 / jnp.maximum(jnp.sum(e, -1, keepdims=True), 1e-30))


def _ada_kernel(c_ref, w_ref, b_ref, o_ref):
    c = c_ref[...]
    o_ref[...] = _bdot(c * _sigmoid(c), w_ref[...]) + b_ref[...]


def _ada(c_all, w_ada, b_ada):
    nb, d = c_all.shape
    n = w_ada.shape[1]
    bn = d
    return pl.pallas_call(
        _ada_kernel,
        grid=(n // bn,),
        in_specs=[pl.BlockSpec((nb, d), lambda i: (0, 0)),
                  pl.BlockSpec((d, bn), lambda i: (0, i)),
                  pl.BlockSpec((1, bn), lambda i: (0, i))],
        out_specs=pl.BlockSpec((nb, bn), lambda i: (0, i)),
        out_shape=jax.ShapeDtypeStruct((nb, n), F32),
        compiler_params=_cparams("arbitrary"),
        name="ada",
    )(c_all, w_ada, b_ada)


def _mod_rows(mod_ref, i, per_row):
    return mod_ref[i] if per_row else mod_ref[0, i:i + 1, :]


def _conv_ln_silu(xc_ref, row0, n, wdw_ref, bdw, lng, lnb):
    acc = jnp.zeros((n, xc_ref.shape[1]), F32) + bdw
    off = CONV_HIST - (CONV_K - 1)
    for k in range(CONV_K):
        acc = acc + xc_ref[pl.ds(row0 + off + k, n), :] * wdw_ref[k:k + 1, :]
    y = _layer_norm(acc, lng, lnb)
    return y * _sigmoid(y)


def _in_kernel(*refs, tt, c_conv, fuse_conv, per_row):
    if fuse_conv:
        (x_ref, mod_ref, wa_ref, wq_ref, wkv_ref, wg_ref, hist_ref, wdw_ref, bdw_ref, lng_ref, lnb_ref,
         conv_ref, tail_ref, q_ref, kc_ref, vc_ref, *kvt_refs, gates_ref, xc_ref) = refs
    else:
        (x_ref, mod_ref, wa_ref, wq_ref, wkv_ref, wg_ref,
         glu_ref, q_ref, kc_ref, vc_ref, ks_ref, vs_ref, kw_ref, vw_ref, gates_ref) = refs
    x = x_ref[0]
    shift1 = _mod_rows(mod_ref, 0, per_row)
    scale1 = _mod_rows(mod_ref, 1, per_row)
    u = (x * (1.0 + scale1) + shift1).astype(BF16)
    a = jnp.dot(u, wa_ref[...], preferred_element_type=F32)
    glu = a[:, :c_conv] * _sigmoid(a[:, c_conv:])
    q_ref[0] = jnp.dot(u, wq_ref[...], preferred_element_type=F32)
    kv = jnp.dot(u, wkv_ref[...], preferred_element_type=F32)
    gates_ref[0] = _sigmoid(jnp.dot(u, wg_ref[...], preferred_element_type=F32))
    if not fuse_conv:
        for i, r in enumerate((kc_ref, vc_ref, ks_ref, vs_ref, kw_ref, vw_ref)):
            r[0] = kv[:, i * KV_DIM:(i + 1) * KV_DIM]
        glu_ref[0] = glu
        return
    kc_ref[0] = kv[:, 0:KV_DIM]
    vc_ref[0] = kv[:, KV_DIM:2 * KV_DIM]
    for i, r in enumerate(kvt_refs):
        r[0] = jnp.transpose(kv[:, i * KV_DIM:(i + 1) * KV_DIM])

    @pl.when(pl.program_id(1) == 0)
    def _():
        xc_ref[0:CONV_HIST, :] = hist_ref[0]

    xc_ref[CONV_HIST:CONV_HIST + tt, :] = glu
    rows = min(tt, 64)
    for r0 in range(0, tt, rows):
        conv_ref[0, r0:r0 + rows, :] = _conv_ln_silu(xc_ref, r0, rows, wdw_ref, bdw_ref[...], lng_ref[...], lnb_ref[...])
    tail = xc_ref[tt:tt + CONV_HIST, :]
    xc_ref[0:CONV_HIST, :] = tail
    tail_ref[0] = tail


def _in_proj(x, mod, wts, conv=None, *, tt, per_row):
    b, t, d = x.shape
    wa, wq, wkv, wg = wts
    c_conv = wa.shape[1] // 2
    fuse = conv is not None
    nt = t // tt
    const = lambda shape: pl.BlockSpec(shape, lambda i, j: (0,) * len(shape))
    mod_spec = (pl.BlockSpec((6, tt, d), lambda i, j: (0, j, 0)) if per_row
                else pl.BlockSpec((1, 6, d), lambda i, j: (i, 0, 0)))
    in_specs = [pl.BlockSpec((1, tt, d), lambda i, j: (i, j, 0)), mod_spec,
                const(wa.shape), const(wq.shape), const(wkv.shape), const(wg.shape)]
    args = [x, mod, wa, wq, wkv, wg]
    tok = lambda n: pl.BlockSpec((1, tt, n), lambda i, j: (i, j, 0))
    tok_shape = lambda n: jax.ShapeDtypeStruct((b, t, n), F32)
    out_specs, out_shape, scratch = [], [], []
    if fuse:
        hist, wdw, bdw, lng, lnb = conv
        in_specs += [pl.BlockSpec((1, CONV_HIST, c_conv), lambda i, j: (i, 0, 0)),
                     const(wdw.shape), const(bdw.shape), const(lng.shape), const(lnb.shape)]
        args += [hist, wdw, bdw, lng, lnb]
        out_specs += [tok(c_conv), pl.BlockSpec((1, CONV_HIST, c_conv), lambda i, j: (i, 0, 0))]
        out_shape += [tok_shape(c_conv), jax.ShapeDtypeStruct((b, CONV_HIST, c_conv), F32)]
        scratch = [pltpu.VMEM((CONV_HIST + tt, c_conv), F32)]
    else:
        out_specs += [tok(c_conv)]
        out_shape += [tok_shape(c_conv)]
    if fuse:
        kv_specs = [tok(KV_DIM)] * 2 + [pl.BlockSpec((1, KV_DIM, tt), lambda i, j: (i, 0, j))] * 6
        kv_shape = [tok_shape(KV_DIM)] * 2 + [jax.ShapeDtypeStruct((b, KV_DIM, t), F32)] * 6
    else:
        kv_specs, kv_shape = [tok(KV_DIM)] * 6, [tok_shape(KV_DIM)] * 6
    out_specs += [tok(wq.shape[1])] + kv_specs + [tok(wg.shape[1])]
    out_shape += [tok_shape(wq.shape[1])] + kv_shape + [tok_shape(wg.shape[1])]
    return pl.pallas_call(
        functools.partial(_in_kernel, tt=tt, c_conv=c_conv, fuse_conv=fuse, per_row=per_row),
        grid=(b, nt), in_specs=in_specs, out_specs=out_specs, out_shape=out_shape, scratch_shapes=scratch,
        compiler_params=_cparams("arbitrary", "arbitrary"),
        name="in_proj_conv" if fuse else "in_proj",
    )(*args)


def _conv_small_kernel(xc_ref, wdw_ref, bdw_ref, lng_ref, lnb_ref, o_ref, *, tt):
    o_ref[0] = _conv_ln_silu(xc_ref.at[0], 0, tt, wdw_ref, bdw_ref[...], lng_ref[...], lnb_ref[...])


def _conv_small(xc, wdw, bdw, lng, lnb, *, tt):
    b, rows, c = xc.shape
    const = lambda shape: pl.BlockSpec(shape, lambda i: (0,) * len(shape))
    return pl.pallas_call(
        functools.partial(_conv_small_kernel, tt=tt),
        grid=(b,),
        in_specs=[pl.BlockSpec((1, rows, c), lambda i: (i, 0, 0)),
                  const(wdw.shape), const(bdw.shape), const(lng.shape), const(lnb.shape)],
        out_specs=pl.BlockSpec((1, tt, c), lambda i: (i, 0, 0)),
        out_shape=jax.ShapeDtypeStruct((b, tt, c), F32),
        compiler_params=_cparams("arbitrary"),
        name="conv_small",
    )(xc, wdw, bdw, lng, lnb)


def _compress_weights(w1, w2, pe):
    half = CMP_STRIDE * HEAD_DIM

    def expand(w):
        w = w.reshape(CMP_STRIDE, HEAD_DIM, CMP_HID)
        z = jnp.zeros_like(w)
        g0 = jnp.concatenate([w, z], axis=1).reshape(CHUNK_ROW, CMP_HID)
        g1 = jnp.concatenate([z, w], axis=1).reshape(CHUNK_ROW, CMP_HID)
        return jnp.concatenate([g0, g1], axis=1)

    w1p = jnp.concatenate([expand(w1[:half]), expand(w1[half:])], axis=1).astype(BF16)
    z2 = jnp.zeros_like(w2)
    w2p = jnp.concatenate([jnp.concatenate([w2, z2], axis=1), jnp.concatenate([z2, w2], axis=1)], axis=0).astype(BF16)
    return w1p, w1.astype(BF16), pe.reshape(1, -1), w2p


def _compress_tail(r, w1_ref, pe_ref, w2p_ref):
    n = r.shape[0]
    pe8 = jnp.broadcast_to(pe_ref[...], (SUBLANES, pe_ref.shape[1]))
    c = _bdot(pe8, w1_ref[...])[0:1]
    c2 = jnp.concatenate([c, c], axis=1)
    h = r[:, :2 * CMP_HID] + pltpu.roll(r[:, 2 * CMP_HID:], n - 1, 0) + c2
    return _bdot(_gelu_tanh(h), w2p_ref[...])


def _compress_prompt_kernel(ak_ref, av_ref, w1pk, w1k, pek, w2pk, w1pv, w1v, pev, w2pv, ck_ref, cv_ref):
    ck_ref[0] = _compress_tail(_bdot(ak_ref[0], w1pk[...]), w1k, pek, w2pk)
    cv_ref[0] = _compress_tail(_bdot(av_ref[0], w1pv[...]), w1v, pev, w2pv)


def _compress_prompt(ak, av, wk, wv):
    b, n, _ = ak.shape
    const = lambda a: pl.BlockSpec(a.shape, lambda i: (0,) * a.ndim)
    row = pl.BlockSpec((1, n, CHUNK_ROW), lambda i: (i, 0, 0))
    out = pl.BlockSpec((1, n, KV_DIM), lambda i: (i, 0, 0))
    return pl.pallas_call(
        _compress_prompt_kernel,
        grid=(b,),
        in_specs=[row, row] + [const(a) for a in wk] + [const(a) for a in wv],
        out_specs=[out, out],
        out_shape=[jax.ShapeDtypeStruct((b, n, KV_DIM), F32)] * 2,
        compiler_params=_cparams("arbitrary"),
        name="compress_prompt",
    )(ak, av, *wk, *wv)


def _compress_decode_kernel(pt_ref, pk_ref, pv_ref, w1pk, w1k, pek, w2pk, w1pv, w1v, pev, w2pv, ck_ref, cv_ref,
                            slab_k, slab_v, rows_buf, rbuf, sem, *, n_pages, page):
    b = pl.program_id(0)

    def page_copies(pg):
        pid = pt_ref[b * n_pages + pg]
        dst = pl.ds(pl.multiple_of(pg * page, page), page)
        return (pltpu.make_async_copy(pk_ref.at[pid], slab_k.at[:, dst], sem.at[0]),
                pltpu.make_async_copy(pv_ref.at[pid], slab_v.at[:, dst], sem.at[1]))

    def start(pg, carry):
        for cp in page_copies(pg):
            cp.start()
        return carry

    def wait(pg, carry):
        for cp in page_copies(pg):
            cp.wait()
        return carry

    lax.fori_loop(0, n_pages, start, 0)
    lax.fori_loop(0, n_pages, wait, 0)
    past = n_pages * page
    n = past // CMP_STRIDE
    step = min(n, 256)
    tcols = min(past, 2048)

    def project(slab, w1p):
        def to_rows(i, carry):
            at = pl.ds(pl.multiple_of(i * tcols, tcols), tcols)
            rows_buf[at, :] = jnp.transpose(slab[:, at])
            return carry
        lax.fori_loop(0, past // tcols, to_rows, 0)

        def body(j, carry):
            base = pl.multiple_of(j * (step * CMP_STRIDE), step * CMP_STRIDE)
            acc = jnp.zeros((step, 4 * CMP_HID), F32)
            for p in range(CMP_STRIDE):
                acc = acc + _bdot(rows_buf[pl.ds(base + p, step, stride=CMP_STRIDE), :], w1p[p * KV_DIM:(p + 1) * KV_DIM, :])
            rbuf[pl.ds(pl.multiple_of(j * step, step), step), :] = acc
            return carry
        lax.fori_loop(0, n // step, body, 0)
        return rbuf[...]

    ck_ref[0] = _compress_tail(project(slab_k, w1pk), w1k, pek, w2pk)
    cv_ref[0] = _compress_tail(project(slab_v, w1pv), w1v, pev, w2pv)


def _compress_decode(page_table, pool_k, pool_v, wk, wv):
    nb, n_pages = page_table.shape
    page = pool_k.shape[2]
    past = n_pages * page
    n = past // CMP_STRIDE
    assert page % LANES == 0
    const = lambda a: pl.BlockSpec(a.shape, lambda i, pt: (0,) * a.ndim)
    any_spec = pl.BlockSpec(memory_space=pl.ANY)
    out = pl.BlockSpec((1, n, KV_DIM), lambda i, pt: (i, 0, 0))
    grid_spec = pltpu.PrefetchScalarGridSpec(
        num_scalar_prefetch=1, grid=(nb,),
        in_specs=[any_spec, any_spec] + [const(a) for a in wk] + [const(a) for a in wv],
        out_specs=[out, out],
        scratch_shapes=[pltpu.VMEM((KV_DIM, past), F32), pltpu.VMEM((KV_DIM, past), F32), pltpu.VMEM((past, KV_DIM), F32),
                        pltpu.VMEM((n, 4 * CMP_HID), F32), pltpu.SemaphoreType.DMA((2,))])
    return pl.pallas_call(
        functools.partial(_compress_decode_kernel, n_pages=n_pages, page=page),
        grid_spec=grid_spec,
        out_shape=[jax.ShapeDtypeStruct((nb, n, KV_DIM), F32)] * 2,
        compiler_params=_cparams("arbitrary"),
        name="compress_decode",
    )(page_table.reshape(-1), pool_k, pool_v, *wk, *wv)


def _importance_matrix(n_cmp_rows, n_cmp, n_blk, cols):
    c = jnp.arange(n_cmp_rows)[:, None]
    b = jnp.arange(cols)[None, :]
    d = c - (SEL_RATIO * b - 1)
    return ((d >= 0) & (d <= SEL_RATIO) & (c < n_cmp) & (b < n_blk)).astype(F32)


def _expand_matrix(n_chunks, chunk, blk0):
    key = jnp.arange(n_chunks)[:, None, None] * chunk + jnp.arange(chunk)[None, None, :]
    blk = jnp.asarray(blk0)[:, None, None] + jnp.arange(LANES)[None, :, None]
    return (key // SEL_LEN == blk).astype(BF16)


def _aug_values(vt):
    r = lax.broadcasted_iota(I32, (V_AUG - HEAD_DIM, vt.shape[1]), 0)
    return jnp.concatenate([vt, jnp.where(r == 0, 1.0, 0.0)], axis=0).astype(BF16)


def _flash_step(qh, kt, vta, bias, m, acc):
    s = jnp.dot(qh, kt, preferred_element_type=F32) + bias
    m_new = jnp.maximum(m, jnp.max(s, -1, keepdims=True))
    p = jnp.exp2(s - m_new).astype(BF16)
    pv = lax.dot_general(p, vta, (((1,), (1,)), ((), ())), preferred_element_type=F32)
    return m_new, jnp.exp2(m - m_new) * acc + pv


def _flash_init(rows):
    return jnp.full((rows, 1), NEG_INF, F32), jnp.zeros((rows, V_AUG), F32)


def _flash_out(acc):
    return acc[:, :HEAD_DIM] * (1.0 / jnp.maximum(acc[:, HEAD_DIM:HEAD_DIM + 1], 1e-30))


def _attn_prompt_kernel(q_ref, g_ref, ck_ref, cv_ref, kst_ref, vst_ref, kwt_ref, vwt_ref, mimp_ref, esel_ref, o_ref,
                        *, seq, n_blk, n_pick, win_len):
    qb = Q_BLOCK
    qs = pl.program_id(1) * qb
    t_col = qs + lax.broadcasted_iota(I32, (qb, 1), 0)
    gates = g_ref[0]
    scale = HEAD_DIM ** -0.5
    q_heads = [q_ref[0, :, h * HEAD_DIM:(h + 1) * HEAD_DIM] for h in range(N_HEADS)]
    nc = ck_ref.shape[1]
    cmp_end = lax.broadcasted_iota(I32, (1, nc), 1) * CMP_STRIDE + (CMP_LEN - 1)
    dist_c = t_col - cmp_end
    mask_c = dist_c >= 0
    dist_cf = dist_c.astype(F32)
    blk_pad = mimp_ref.shape[1]

    blk_i = lax.broadcasted_iota(I32, (n_blk, qb), 0)
    t_row = qs + lax.broadcasted_iota(I32, (n_blk, qb), 1)
    cur = t_row // SEL_LEN
    forced = (blk_i == 0) | (blk_i == cur) | (blk_i == cur - 1)
    future = blk_i * SEL_LEN > t_row

    win_start = pl.multiple_of(jnp.clip(qs - WINDOW, 0, seq - win_len), qb)
    dist_w = t_col - (win_start + lax.broadcasted_iota(I32, (1, win_len), 1))
    maskbias_w = jnp.where((dist_w >= 0) & (dist_w <= WINDOW), 0.0, NEG_INF)
    n_chunks = (qs + qb + SLC_CHUNK - 1) // SLC_CHUNK
    rows = Q_PER_KV * qb
    stack = lambda parts: jnp.concatenate(parts, axis=0)
    heads_of = lambda a: a.reshape(Q_PER_KV, qb, a.shape[1])

    def flash(qk, alibi, maskbias, off, vta, m, acc):
        s = (heads_of(qk + alibi) + maskbias[None]).reshape(rows, qk.shape[1])
        m_new = jnp.maximum(m, jnp.max(s, -1, keepdims=True) + off)
        p = jnp.exp2(s - (m_new - off)).astype(BF16)
        pv = lax.dot_general(p, vta, (((1,), (1,)), ((), ())), preferred_element_type=F32)
        return m_new, jnp.exp2(m - m_new) * acc + pv

    lanes_of = lambda g: slice(g * HEAD_DIM, (g + 1) * HEAD_DIM)
    heads_in = lambda g: range(g * Q_PER_KV, (g + 1) * Q_PER_KV)
    qg2s, slope2s, sels, o_cmps = [], [], [], []
    for g in range(N_KV):
        lanes, heads = lanes_of(g), heads_in(g)
        slope = stack([jnp.full((qb, 1), SLOPES[h], F32) for h in heads])
        qg = stack([(q_heads[h] * scale).astype(BF16) for h in heads])
        qg2s.append(stack([(q_heads[h] * (scale * LOG2E)).astype(BF16) for h in heads]))
        slope2s.append(slope * LOG2E)
        s_c = heads_of(_bdot_nt(qg, ck_ref[0, :, lanes]) - slope * stack([dist_cf] * Q_PER_KV))
        p_c = _masked_softmax(s_c, mask_c[None])
        p_sum = jnp.sum(p_c, axis=0)
        o_cmp = heads_of(_bdot(p_c.reshape(rows, nc), cv_ref[0, :, lanes]))
        imp = jnp.dot(p_sum, mimp_ref[...], precision=HIGHEST, preferred_element_type=F32)
        imp_t = jnp.transpose(imp)[0:n_blk, :]
        imp_t = jnp.where(future, -FORCE, jnp.where(forced, FORCE, imp_t))
        cnt = jnp.zeros((n_blk, qb), F32)
        for b2 in range(n_blk):
            row = imp_t[b2:b2 + 1, :]
            ahead = (row > imp_t) | ((row == imp_t) & (blk_i > b2))
            cnt = cnt + jnp.where(ahead, 1.0, 0.0)
        sel_t = jnp.where(cnt < n_pick, 1.0, 0.0)
        if n_blk < blk_pad:
            sel_t = jnp.concatenate([sel_t, jnp.zeros((blk_pad - n_blk, qb), F32)], axis=0)
        sels.append(jnp.transpose(sel_t).astype(BF16))
        o_cmps.append(o_cmp)

    key_slc = lax.broadcasted_iota(I32, (1, SLC_CHUNK), 1)
    key_win = lax.broadcasted_iota(I32, (1, win_len), 1).astype(F32)
    alibi_slc = [s2 * key_slc.astype(F32) for s2 in slope2s]

    def slc_step(c, carry):
        k0 = pl.multiple_of(c * SLC_CHUNK, SLC_CHUNK)
        causal = t_col - (k0 + key_slc) >= 0
        out = []
        for g in range(N_KV):
            kt = kst_ref[0, lanes_of(g), pl.ds(k0, SLC_CHUNK)].astype(BF16)
            vta = _aug_values(vst_ref[0, lanes_of(g), pl.ds(k0, SLC_CHUNK)])
            msel = jnp.dot(sels[g], esel_ref[c], preferred_element_type=F32)
            maskbias = jnp.where(causal & (msel > 0.5), 0.0, NEG_INF)
            qk = jnp.dot(qg2s[g], kt, preferred_element_type=F32)
            out += flash(qk, alibi_slc[g], maskbias, slope2s[g] * k0.astype(F32), vta, *carry[2 * g:2 * g + 2])
        return tuple(out)

    fin = lax.fori_loop(0, n_chunks, slc_step, _flash_init(rows) * N_KV)
    for g in range(N_KV):
        o_slc = heads_of(_flash_out(fin[2 * g + 1]))
        kwt = kwt_ref[0, lanes_of(g), pl.ds(win_start, win_len)].astype(BF16)
        vwta = _aug_values(vwt_ref[0, lanes_of(g), pl.ds(win_start, win_len)])
        qk = jnp.dot(qg2s[g], kwt, preferred_element_type=F32)
        o_win = heads_of(_flash_out(flash(qk, slope2s[g] * key_win, maskbias_w, 0.0, vwta, *_flash_init(rows))[1]))
        for i, h in enumerate(heads_in(g)):
            c0 = h * N_GATES
            o_ref[0, :, h * HEAD_DIM:(h + 1) * HEAD_DIM] = (gates[:, c0:c0 + 1] * o_cmps[g][i] + gates[:, c0 + 1:c0 + 2] * o_slc[i]
                                                           + gates[:, c0 + 2:c0 + 3] * o_win[i])


def _attn_prompt(q, gates, ck, cv, ks, vs, kw, vw):
    b, t, att = q.shape
    assert t % SLC_CHUNK == 0 and t % Q_BLOCK == 0
    n_blk = t // SEL_LEN
    n_cmp = t // CMP_STRIDE - 1
    assert n_blk <= LANES and ck.shape[1] == t // CMP_STRIDE
    n_pick = min(N_SEL, n_blk)
    win_len = min(WINDOW + Q_BLOCK, t)
    mimp = _importance_matrix(ck.shape[1], n_cmp, n_blk, LANES)
    n_ch = t // SLC_CHUNK
    esel = _expand_matrix(n_ch, SLC_CHUNK, [0] * n_ch)
    tok = lambda n: pl.BlockSpec((1, Q_BLOCK, n), lambda i, j: (i, j, 0))
    full = lambda a: pl.BlockSpec((1,) + a.shape[1:], lambda i, j: (i, 0, 0))
    const = lambda a: pl.BlockSpec(a.shape, lambda i, j: (0,) * a.ndim)
    return pl.pallas_call(
        functools.partial(_attn_prompt_kernel, seq=t, n_blk=n_blk, n_pick=n_pick, win_len=win_len),
        grid=(b, t // Q_BLOCK),
        in_specs=[tok(att), tok(gates.shape[2]), full(ck), full(cv), full(ks), full(vs), full(kw), full(vw),
                  const(mimp), const(esel)],
        out_specs=tok(att),
        out_shape=jax.ShapeDtypeStruct((b, t, att), F32),
        compiler_params=_cparams("arbitrary", "arbitrary"),
        name="attn_prompt",
    )(q, gates, ck, cv, ks, vs, kw, vw, mimp, esel)


def _attn_decode_kernel(pt_ref, q_ref, g_ref, ck_ref, cv_ref, ksn_ref, vsn_ref, kwn_ref, vwn_ref, wk_ref, wv_ref,
                        pk_ref, pv_ref, mimp_ref, esel_ref, o_ref, slab_k, slab_v, sem,
                        *, n_pages, page, past, n_blk, n_pick, tq):
    b = pl.program_id(0)

    def page_copies(pg):
        pid = pt_ref[b * n_pages + pg]
        dst = pl.ds(pl.multiple_of(pg * page, page), page)
        return (pltpu.make_async_copy(pk_ref.at[pid], slab_k.at[:, dst], sem.at[0]),
                pltpu.make_async_copy(pv_ref.at[pid], slab_v.at[:, dst], sem.at[1]))

    def start(pg, carry):
        for cp in page_copies(pg):
            cp.start()
        return carry

    def wait(pg, carry):
        for cp in page_copies(pg):
            cp.wait()
        return carry

    lax.fori_loop(0, n_pages, start, 0)

    rows = Q_PER_KV * DEC_Q
    qi = lax.broadcasted_iota(I32, (rows, 1), 0) % DEC_Q
    t_col = past + qi % tq
    hl = lax.broadcasted_iota(I32, (rows, 1), 0) // DEC_Q
    nc = ck_ref.shape[1]
    cmp_end = lax.broadcasted_iota(I32, (1, nc), 1) * CMP_STRIDE + (CMP_LEN - 1)
    dist_c = t_col - cmp_end
    mask_c = dist_c >= 0
    dist_cf = dist_c.astype(F32)
    blk_pad = mimp_ref.shape[1]
    t8 = past + lax.broadcasted_iota(I32, (DEC_Q, 1), 0) % tq
    blk_l = lax.broadcasted_iota(I32, (DEC_Q, blk_pad), 1)
    cur = t8 // SEL_LEN
    forced = (blk_l == 0) | (blk_l == cur) | (blk_l == cur - 1)
    future = blk_l * SEL_LEN > t8
    pad_blk = blk_l >= n_blk
    b_sub = lax.broadcasted_iota(I32, (blk_pad, blk_pad), 0)
    b_lane = lax.broadcasted_iota(I32, (blk_pad, blk_pad), 1)
    n_chunks = past // DEC_CHUNK
    new_rows = ksn_ref.shape[2]

    sels, slopes, o_cmps = [], [], []
    for g in range(N_KV):
        lanes = slice(g * HEAD_DIM, (g + 1) * HEAD_DIM)
        slope = jnp.zeros((rows, 1), F32)
        for i in range(Q_PER_KV):
            slope = jnp.where(hl == i, SLOPES[g * Q_PER_KV + i], slope)
        slopes.append(slope)
        qg = (q_ref[0, g] * (HEAD_DIM ** -0.5)).astype(BF16)
        p = _masked_softmax(_bdot_nt(qg, ck_ref[0, :, lanes]) - slope * dist_cf, mask_c)
        o_cmps.append(_bdot(p, cv_ref[0, :, lanes]))
        p_sum = p[0:DEC_Q]
        for i in range(1, Q_PER_KV):
            p_sum = p_sum + p[i * DEC_Q:(i + 1) * DEC_Q]
        imp = jnp.dot(p_sum, mimp_ref[...], precision=HIGHEST, preferred_element_type=F32)
        imp = jnp.where(pad_blk, -3e38, jnp.where(future, -FORCE, jnp.where(forced, FORCE, imp)))
        imp_t = jnp.transpose(jnp.concatenate([imp, jnp.zeros((LANES - DEC_Q, blk_pad), F32)], axis=0))
        sel_rows = []
        for r in range(DEC_Q):
            col = imp_t[:, r:r + 1]
            row = imp[r:r + 1, :]
            ahead = (col > row) | ((col == row) & (b_sub < b_lane))
            cnt = jnp.sum(jnp.where(ahead, 1.0, 0.0), axis=0, keepdims=True)
            sel_rows.append(jnp.where(cnt < n_pick, 1.0, 0.0))
        sel8 = jnp.concatenate(sel_rows, axis=0)
        sels.append(jnp.concatenate([sel8] * Q_PER_KV, axis=0))

    lax.fori_loop(0, n_pages, wait, 0)

    for g in range(N_KV):
        lanes = slice(g * HEAD_DIM, (g + 1) * HEAD_DIM)
        slope2 = slopes[g] * LOG2E
        qg2 = (q_ref[0, g] * (HEAD_DIM ** -0.5 * LOG2E)).astype(BF16)
        sel = sels[g]
        sel_b = sel.astype(BF16)

        def bias_of(dist, visible):
            return jnp.where(visible, 0.0, NEG_INF) - slope2 * dist.astype(F32)

        m, acc = _flash_init(rows)
        per_mat = LANES * SEL_LEN // DEC_CHUNK
        for c in range(n_chunks):
            keys = slice(c * DEC_CHUNK, (c + 1) * DEC_CHUNK)
            lane0 = (c // per_mat) * LANES
            msel = jnp.dot(sel_b[:, lane0:lane0 + LANES], esel_ref[c % per_mat], preferred_element_type=F32)
            dist = t_col - (c * DEC_CHUNK + lax.broadcasted_iota(I32, (1, DEC_CHUNK), 1))
            m, acc = _flash_step(qg2, slab_k[lanes, keys].astype(BF16), _aug_values(slab_v[lanes, keys]),
                                 bias_of(dist, (dist >= 0) & (msel > 0.5)), m, acc)
        dist_n = t_col - (past + lax.broadcasted_iota(I32, (1, new_rows), 1))
        cur_blk = past // SEL_LEN
        m, acc = _flash_step(qg2, ksn_ref[0, lanes, :].astype(BF16), _aug_values(vsn_ref[0, lanes, :]),
                             bias_of(dist_n, (dist_n >= 0) & (sel[:, cur_blk:cur_blk + 1] > 0.5)), m, acc)
        o_slc = _flash_out(acc)
        wlen = wk_ref.shape[2]
        dist = t_col - (past - wlen + lax.broadcasted_iota(I32, (1, wlen), 1))
        m, acc = _flash_step(qg2, wk_ref[0, lanes, :].astype(BF16), _aug_values(wv_ref[0, lanes, :]),
                             bias_of(dist, (dist >= 0) & (dist <= WINDOW)), *_flash_init(rows))
        m, acc = _flash_step(qg2, kwn_ref[0, lanes, :].astype(BF16), _aug_values(vwn_ref[0, lanes, :]),
                             bias_of(dist_n, (dist_n >= 0) & (dist_n <= WINDOW)), m, acc)
        o_win = _flash_out(acc)
        gt = g_ref[0, g]
        o_ref[0, g] = gt[:, 0:1] * o_cmps[g] + gt[:, 1:2] * o_slc + gt[:, 2:3] * o_win


def _attn_decode(page_table, q, gates, ck, cv, ksn, vsn, kwn, vwn, win_k, win_v, pool_k, pool_v):
    nb, tq, att = q.shape
    n_pages = page_table.shape[1]
    page = pool_k.shape[2]
    past = n_pages * page
    assert tq <= DEC_Q and past % DEC_CHUNK == 0 and past % SEL_LEN == 0 and (past + tq - 1) // SEL_LEN == past // SEL_LEN
    assert win_k.shape[2] == WINDOW and ck.shape[1] * CMP_STRIDE == past and page % LANES == 0
    t_pad = -(-(past + tq) // SEL_LEN) * SEL_LEN
    n_blk = t_pad // SEL_LEN
    n_cmp = t_pad // CMP_STRIDE - 1
    n_pick = min(N_SEL, n_blk)
    blk_pad = -(-n_blk // LANES) * LANES
    mimp = _importance_matrix(ck.shape[1], n_cmp, n_blk, blk_pad)
    per_mat = LANES * SEL_LEN // DEC_CHUNK
    esel = _expand_matrix(per_mat, DEC_CHUNK, [0] * per_mat)

    def dec_rows(a, width):
        a = a.reshape(nb, tq, N_KV, Q_PER_KV, width)
        a = jnp.take(a, jnp.arange(DEC_Q) % tq, axis=1)
        return a.transpose(0, 2, 3, 1, 4).reshape(nb, N_KV, Q_PER_KV * DEC_Q, width)

    q_d = dec_rows(q, HEAD_DIM)
    g_d = dec_rows(gates[:, :, :N_HEADS * N_GATES], N_GATES)
    pad_new = lambda a: jnp.pad(a.transpose(0, 2, 1), ((0, 0), (0, 0), (0, LANES - tq)))
    ksn, vsn, kwn, vwn = [pad_new(a) for a in (ksn, vsn, kwn, vwn)]
    rows = Q_PER_KV * DEC_Q
    blk = lambda a: pl.BlockSpec((1,) + a.shape[1:], lambda i, pt: (i,) + (0,) * (a.ndim - 1))
    const = lambda a: pl.BlockSpec(a.shape, lambda i, pt: (0,) * a.ndim)
    any_spec = pl.BlockSpec(memory_space=pl.ANY)
    grid_spec = pltpu.PrefetchScalarGridSpec(
        num_scalar_prefetch=1, grid=(nb,),
        in_specs=[blk(q_d), blk(g_d), blk(ck), blk(cv), blk(ksn), blk(vsn), blk(kwn), blk(vwn), blk(win_k), blk(win_v),
                  any_spec, any_spec, const(mimp), const(esel)],
        out_specs=pl.BlockSpec((1, N_KV, rows, HEAD_DIM), lambda i, pt: (i, 0, 0, 0)),
        scratch_shapes=[pltpu.VMEM((KV_DIM, past), F32), pltpu.VMEM((KV_DIM, past), F32), pltpu.SemaphoreType.DMA((2,))])
    o = pl.pallas_call(
        functools.partial(_attn_decode_kernel, n_pages=n_pages, page=page, past=past, n_blk=n_blk, n_pick=n_pick, tq=tq),
        grid_spec=grid_spec,
        out_shape=jax.ShapeDtypeStruct((nb, N_KV, rows, HEAD_DIM), F32),
        compiler_params=_cparams("arbitrary"),
        name="attn_decode",
    )(page_table.reshape(-1), q_d, g_d, ck, cv, ksn, vsn, kwn, vwn, win_k, win_v, pool_k, pool_v, mimp, esel)
    o = o.reshape(nb, N_KV, Q_PER_KV, DEC_Q, HEAD_DIM)[:, :, :, :tq]
    return o.transpose(0, 3, 1, 2, 4).reshape(nb, tq, att)


def _out_kernel(x_ref, conv_ref, att_ref, mod_ref, bc_ref, ba_ref, woc_ref, woa_ref, g1_ref, b1_ref, wr_ref, br_ref,
                cnt0_ref, x1_ref, u2_ref, idx_ref, w_ref, pos_ref, cnt_ref, carry_ref, *, tt, alpha, per_row):
    first = (pl.program_id(0) == 0) & (pl.program_id(1) == 0)

    @pl.when(first)
    def _():
        carry_ref[...] = cnt0_ref[...]

    gate1 = _mod_rows(mod_ref, 2, per_row)
    shift2 = _mod_rows(mod_ref, 3, per_row)
    scale2 = _mod_rows(mod_ref, 4, per_row)
    mix = _bdot(conv_ref[0] * bc_ref[...], woc_ref[...]) + _bdot(att_ref[0] * ba_ref[...], woa_ref[...])
    x1 = _layer_norm(alpha * x_ref[0] + gate1 * mix, g1_ref[...], b1_ref[...])
    x1_ref[0] = x1
    u2 = x1 * (1.0 + scale2) + shift2
    u2_ref[0] = u2
    logits = jnp.dot(u2, wr_ref[...], precision=HIGHEST, preferred_element_type=F32) + br_ref[...]
    lane = lax.broadcasted_iota(I32, logits.shape, 1)
    vals = logits
    top_v, top_i = [], []
    for _k in range(TOP_K):
        m = jnp.max(vals, axis=-1, keepdims=True)
        i = jnp.min(jnp.where(vals == m, lane, LANES), axis=-1, keepdims=True)
        top_v.append(m)
        top_i.append(i)
        vals = jnp.where(lane == i, -jnp.inf, vals)
    e = [jnp.exp(v - top_v[0]) for v in top_v]
    inv = 1.0 / (e[0] + e[1] + e[2] + e[3])
    onehot = [jnp.where(lane == i, 1.0, 0.0) for i in top_i]
    assigned = onehot[0] + onehot[1] + onehot[2] + onehot[3]
    r_i = lax.broadcasted_iota(I32, (tt, tt), 0)
    c_i = lax.broadcasted_iota(I32, (tt, tt), 1)
    before = jnp.where(r_i > c_i, 1.0, 0.0).astype(BF16)
    rank = carry_ref[...] + jnp.dot(before, assigned.astype(BF16), preferred_element_type=F32)
    idx_o = jnp.zeros(logits.shape, I32)
    w_o = jnp.zeros(logits.shape, F32)
    pos_o = jnp.zeros(logits.shape, F32)
    for k in range(TOP_K):
        idx_o = jnp.where(lane == k, top_i[k], idx_o)
        w_o = jnp.where(lane == k, e[k] * inv, w_o)
        pos_o = jnp.where(lane == k, jnp.sum(onehot[k] * rank, axis=-1, keepdims=True), pos_o)
    idx_ref[0] = idx_o
    w_ref[0] = w_o
    pos_ref[0] = pos_o.astype(I32)
    carry_ref[...] = carry_ref[...] + jnp.sum(assigned, axis=0, keepdims=True)
    cnt_ref[...] = carry_ref[...]


def _out_proj(x, conv_out, att, mod, vecs, wts, cnt0, *, tt, alpha, per_row):
    b, t, d = x.shape
    bc, ba, g1, b1, br = vecs
    woc, woa, wr = wts
    const = lambda a: pl.BlockSpec(a.shape, lambda i, j: (0,) * a.ndim)
    tok = lambda n: pl.BlockSpec((1, tt, n), lambda i, j: (i, j, 0))
    mod_spec = (pl.BlockSpec((6, tt, d), lambda i, j: (0, j, 0)) if per_row
                else pl.BlockSpec((1, 6, d), lambda i, j: (i, 0, 0)))
    tok_shape = lambda n, dt: jax.ShapeDtypeStruct((b, t, n), dt)
    return pl.pallas_call(
        functools.partial(_out_kernel, tt=tt, alpha=alpha, per_row=per_row),
        grid=(b, t // tt),
        in_specs=[tok(d), tok(conv_out.shape[2]), tok(att.shape[2]), mod_spec, const(bc), const(ba), const(woc), const(woa),
                  const(g1), const(b1), const(wr), const(br), const(cnt0)],
        out_specs=[tok(d), tok(d), tok(LANES), tok(LANES), tok(LANES), pl.BlockSpec((1, LANES), lambda i, j: (0, 0))],
        out_shape=[tok_shape(d, F32), tok_shape(d, F32), tok_shape(LANES, I32), tok_shape(LANES, F32), tok_shape(LANES, I32),
                   jax.ShapeDtypeStruct((1, LANES), F32)],
        scratch_shapes=[pltpu.VMEM((1, LANES), F32)],
        compiler_params=_cparams("arbitrary", "arbitrary"),
        name="out_proj_router",
    )(x, conv_out, att, mod, bc, ba, woc, woa, g1, b1, wr, br, cnt0)


def _gather_pipeline(i, n, src_ref, cur_ref, nxt_ref, bufs, sem, rows, compute):
    def row_copy(src, buf, s, r):
        return pltpu.make_async_copy(src_ref.at[pl.ds(src, 1)], buf.at[pl.ds(r, 1)], sem.at[s])

    def whole(buf, s):
        return pltpu.make_async_copy(src_ref.at[pl.ds(0, rows)], buf, sem.at[s])

    @pl.when(i == 0)
    def _():
        def body(r, carry):
            row_copy(cur_ref[0, 0, r], bufs[0], 0, r).start()
            return carry
        lax.fori_loop(0, rows, body, 0)

    def step(s):
        whole(bufs[s], s).wait()
        for r in range(rows):
            row_copy(nxt_ref[0, 0, r], bufs[1 - s], 1 - s, r).start()
        compute(bufs[s])

        @pl.when(i == n - 1)
        def _():
            whole(bufs[1 - s], 1 - s).wait()

    for s in range(2):
        pl.when(i % 2 == s)(functools.partial(step, s))


def _expert_kernel(be_ref, cur_ref, nxt_ref, u_ref, wgu_ref, bgu_ref, wdn_ref, bdn_ref, y_ref, xb0, xb1, sem, *, bm, d_ff):
    def compute(buf):
        gu = jnp.dot(buf[...].astype(BF16), wgu_ref[0], preferred_element_type=F32) + bgu_ref[0]
        gl = jnp.minimum(gu[:, :d_ff], SWIGLU_LIMIT)
        lin = jnp.clip(gu[:, d_ff:], -SWIGLU_LIMIT, SWIGLU_LIMIT)
        act = gl * _sigmoid(SWIGLU_ALPHA * gl) * (lin + 1.0)
        y_ref[...] = jnp.dot(act.astype(BF16), wdn_ref[0], preferred_element_type=F32) + bdn_ref[0]

    _gather_pipeline(pl.program_id(0), pl.num_programs(0), u_ref, cur_ref, nxt_ref, (xb0, xb1), sem, bm, compute)


def _expert_ffn(u2, slot_tok, blk_e, wgu, bgu, wdn, bdn, *, bm):
    n_blocks = slot_tok.shape[0]
    d = u2.shape[1]
    d_ff = wdn.shape[1]
    slot_tok = slot_tok.reshape(n_blocks, 1, bm)
    smem = lambda f: pl.BlockSpec((1, 1, bm), f, memory_space=pltpu.SMEM)
    grid_spec = pltpu.PrefetchScalarGridSpec(
        num_scalar_prefetch=1, grid=(n_blocks,),
        in_specs=[smem(lambda i, be: (i, 0, 0)),
                  smem(lambda i, be: (jnp.minimum(i + 1, n_blocks - 1), 0, 0)),
                  pl.BlockSpec(memory_space=pl.ANY),
                  pl.BlockSpec((1, d, 2 * d_ff), lambda i, be: (be[i], 0, 0)),
                  pl.BlockSpec((1, 1, 2 * d_ff), lambda i, be: (be[i], 0, 0)),
                  pl.BlockSpec((1, d_ff, d), lambda i, be: (be[i], 0, 0)),
                  pl.BlockSpec((1, 1, d), lambda i, be: (be[i], 0, 0))],
        out_specs=pl.BlockSpec((bm, d), lambda i, be: (i, 0)),
        scratch_shapes=[pltpu.VMEM((bm, d), F32), pltpu.VMEM((bm, d), F32), pltpu.SemaphoreType.DMA((2,))])
    return pl.pallas_call(
        functools.partial(_expert_kernel, bm=bm, d_ff=d_ff),
        grid_spec=grid_spec,
        out_shape=jax.ShapeDtypeStruct((n_blocks * bm, d), F32),
        compiler_params=_cparams("arbitrary"),
        name="expert_ffn",
    )(blk_e, slot_tok, slot_tok, u2, wgu, bgu, wdn, bdn)


def _combine_kernel(cur_ref, nxt_ref, ys_ref, w_ref, x1_ref, mod_ref, g2_ref, b2_ref, y_ref, gb0, gb1, sem,
                    *, tc, alpha, per_row):
    def compute(buf):
        w = w_ref[...]
        f = w[:, 0:1] * buf[0:tc, :]
        for k in range(1, TOP_K):
            f = f + w[:, k:k + 1] * buf[k * tc:(k + 1) * tc, :]
        gate2 = mod_ref[5] if per_row else mod_ref[0, 5:6, :]
        y_ref[...] = _layer_norm(alpha * x1_ref[...] + gate2 * f, g2_ref[...], b2_ref[...])

    _gather_pipeline(pl.program_id(0), pl.num_programs(0), ys_ref, cur_ref, nxt_ref, (gb0, gb1), sem, TOP_K * tc, compute)


def _combine(ys, dest_t, top_w, x1, mod, g2, b2, *, tc, alpha, per_row, tiles_per_batch):
    n, d = x1.shape
    n_tiles = n // tc
    dest_t = dest_t.reshape(n_tiles, 1, TOP_K * tc)
    smem = lambda f: pl.BlockSpec((1, 1, TOP_K * tc), f, memory_space=pltpu.SMEM)
    const = lambda a: pl.BlockSpec(a.shape, lambda i: (0,) * a.ndim)
    mod_spec = (pl.BlockSpec((6, tc, d), lambda i: (0, i, 0)) if per_row
                else pl.BlockSpec((1, 6, d), lambda i: (i // tiles_per_batch, 0, 0)))
    return pl.pallas_call(
        functools.partial(_combine_kernel, tc=tc, alpha=alpha, per_row=per_row),
        grid=(n_tiles,),
        in_specs=[smem(lambda i: (i, 0, 0)), smem(lambda i: (jnp.minimum(i + 1, n_tiles - 1), 0, 0)),
                  pl.BlockSpec(memory_space=pl.ANY),
                  pl.BlockSpec((tc, LANES), lambda i: (i, 0)), pl.BlockSpec((tc, d), lambda i: (i, 0)),
                  mod_spec, const(g2), const(b2)],
        out_specs=pl.BlockSpec((tc, d), lambda i: (i, 0)),
        out_shape=jax.ShapeDtypeStruct((n, d), F32),
        scratch_shapes=[pltpu.VMEM((TOP_K * tc, d), F32), pltpu.VMEM((TOP_K * tc, d), F32), pltpu.SemaphoreType.DMA((2,))],
        compiler_params=_cparams("arbitrary"),
        name="combine_ln2",
    )(dest_t, dest_t, ys, top_w, x1, mod, g2, b2)


def _moe(u2, x1, top_i, top_w, pos, counts, mod, experts, g2, b2, *, bm, tc, alpha, per_row, tiles_per_batch):
    n, d = u2.shape
    wgu, bgu, wdn, bdn = experts
    n_exp = wgu.shape[0]
    n_blocks = -(-n * TOP_K // bm) + n_exp
    counts = counts.astype(I32)
    padded = (counts + bm - 1) // bm * bm
    pad_end = jnp.cumsum(padded)
    pad_start = pad_end - padded
    dest = pad_start[top_i] + pos
    tok = jnp.broadcast_to(jnp.arange(n, dtype=I32)[:, None], (n, TOP_K))
    slot_tok = jnp.zeros((n_blocks * bm,), I32).at[dest.reshape(-1)].set(tok.reshape(-1))
    blk_start = jnp.arange(n_blocks, dtype=I32) * bm
    blk_e = jnp.minimum(jnp.sum((pad_end[None, :] <= blk_start[:, None]).astype(I32), axis=1), n_exp - 1)
    ys = _expert_ffn(u2, slot_tok.reshape(n_blocks, bm), blk_e, wgu, bgu, wdn, bdn, bm=bm)
    dest_t = dest.reshape(n // tc, tc, TOP_K).transpose(0, 2, 1).reshape(n // tc, TOP_K * tc)
    return _combine(ys, dest_t, top_w, x1, mod, g2, b2, tc=tc, alpha=alpha, per_row=per_row, tiles_per_batch=tiles_per_batch)


def _pick_tile(n, target):
    t = min(n, target)
    while n % t:
        t -= 1
    return t


def kernel(x_prompt, x_sample, c_prompt, c_sample, cache_cmp_k, cache_cmp_v, cache_slc_k, cache_slc_v, state_win_k, state_win_v, state_conv, page_table, w_ada, b_ada, w_in, w_dw, b_dw, conv_ln_g, conv_ln_b, w_ck1, w_ck2, pe_k, w_cv1, w_cv2, pe_v, beta_conv, beta_attn, w_out, ln1_g, ln1_b, w_router, b_router, w_gate_up, b_gate_up, w_down, b_down, ln2_g, ln2_b):
    depth = w_ada.shape[0]
    assert depth == 1, "single-layer trunk"
    alpha = (2 * depth) ** 0.25
    bp, seq, d = x_prompt.shape
    bs, tq, _ = x_sample.shape
    c_conv = w_dw.shape[2]
    att_dim = N_HEADS * HEAD_DIM
    n_pool, page = cache_cmp_k.shape[1], cache_cmp_k.shape[2]
    assert state_conv.shape[2] == CONV_K - 1 and page % CMP_STRIDE == 0

    wi = w_in[0]
    cuts = [2 * c_conv, 2 * c_conv + att_dim, 2 * c_conv + att_dim + 6 * KV_DIM]
    wa = wi[:, :cuts[0]].astype(BF16)
    wq = wi[:, cuts[0]:cuts[1]].astype(BF16)
    wkv = wi[:, cuts[1]:cuts[2]].astype(BF16)
    wg = jnp.pad(wi[:, cuts[2]:], ((0, 0), (0, LANES - N_HEADS * N_GATES))).astype(BF16)
    in_w = (wa, wq, wkv, wg)
    wdw = jnp.pad(w_dw[0], ((0, CONV_HIST - CONV_K), (0, 0)))
    row = lambda a: a.reshape(1, -1)
    conv_vecs = (wdw, row(b_dw[0]), row(conv_ln_g[0]), row(conv_ln_b[0]))
    wk_c = _compress_weights(w_ck1[0], w_ck2[0], pe_k[0])
    wv_c = _compress_weights(w_cv1[0], w_cv2[0], pe_v[0])
    out_vecs = (row(beta_conv[0]), row(beta_attn[0]), row(ln1_g[0]), row(ln1_b[0]),
                jnp.pad(row(b_router[0]), ((0, 0), (0, LANES - N_EXPERTS)), constant_values=NEG_INF))
    out_w = (w_out[0][:c_conv].astype(BF16), w_out[0][c_conv:].astype(BF16),
             jnp.pad(w_router[0], ((0, 0), (0, LANES - N_EXPERTS))))
    experts = (w_gate_up[0].astype(BF16), b_gate_up[0][:, None, :], w_down[0].astype(BF16), b_down[0][:, None, :])
    g2, b2 = row(ln2_g[0]), row(ln2_b[0])

    mod = _ada(jnp.concatenate([c_prompt, c_sample], axis=0), w_ada[0], row(b_ada[0])).reshape(bp + bs, 6, d)
    mod_p = mod[:bp]
    mod_s = jnp.repeat(mod[bp:], tq, axis=0).transpose(1, 0, 2)

    tt = _pick_tile(seq, 512)
    hist0 = jnp.zeros((bp, CONV_HIST, c_conv), F32)
    (conv_p, tail_p, q_p, kc_p, vc_p, kct_p, vct_p, kst_p, vst_p, kwt_p, vwt_p, gates_p) = _in_proj(
        x_prompt, mod_p, in_w, (hist0,) + conv_vecs, tt=tt, per_row=False)
    n_chunk = seq // CMP_STRIDE
    ck_p, cv_p = _compress_prompt(kc_p.reshape(bp, n_chunk, CHUNK_ROW), vc_p.reshape(bp, n_chunk, CHUNK_ROW), wk_c, wv_c)
    att_p = _attn_prompt(q_p, gates_p, ck_p, cv_p, kst_p, vst_p, kwt_p, vwt_p)
    to = _pick_tile(seq, 256)
    cnt0 = jnp.zeros((1, LANES), F32)
    x1_p, u2_p, ti_p, tw_p, pos_p, cnt_p = _out_proj(x_prompt, conv_p, att_p, mod_p, out_vecs, out_w, cnt0,
                                                     tt=to, alpha=alpha, per_row=False)
    n_p = bp * seq
    bm_p = _pick_tile(n_p * TOP_K, 512)
    tc_p = _pick_tile(seq, 128)
    y_p = _moe(u2_p.reshape(n_p, d), x1_p.reshape(n_p, d), ti_p.reshape(n_p, LANES)[:, :TOP_K], tw_p.reshape(n_p, LANES),
               pos_p.reshape(n_p, LANES)[:, :TOP_K], cnt_p[0, :N_EXPERTS], mod_p, experts, g2, b2,
               bm=bm_p, tc=tc_p, alpha=alpha, per_row=False, tiles_per_batch=seq // tc_p).reshape(bp, seq, d)

    n_s = bs * tq
    (glu_s, q_s, kc_s, vc_s, ks_s, vs_s, kw_s, vw_s, gates_s) = _in_proj(
        x_sample.reshape(1, n_s, d), mod_s, in_w, None, tt=n_s, per_row=True)
    unflat = lambda a: a.reshape(bs, tq, a.shape[-1])
    glu_s, q_s, kc_s, vc_s, ks_s, vs_s, kw_s, vw_s, gates_s = map(unflat, (glu_s, q_s, kc_s, vc_s, ks_s, vs_s, kw_s, vw_s, gates_s))
    xc_s = jnp.concatenate([jnp.zeros((bs, CONV_HIST - (CONV_K - 1), c_conv), F32), state_conv[0], glu_s], axis=1)
    conv_s = _conv_small(xc_s, *conv_vecs, tt=tq)
    dim_major = lambda a: a.transpose(0, 2, 3, 1).reshape(a.shape[0], KV_DIM, a.shape[1])
    ck_s, cv_s = _compress_decode(page_table, dim_major(cache_cmp_k[0]), dim_major(cache_cmp_v[0]), wk_c, wv_c)
    att_s = _attn_decode(page_table, q_s, gates_s, ck_s, cv_s, ks_s, vs_s, kw_s, vw_s, dim_major(state_win_k[0]),
                         dim_major(state_win_v[0]), dim_major(cache_slc_k[0]), dim_major(cache_slc_v[0]))
    flat = lambda a: a.reshape(1, n_s, a.shape[-1])
    x1_s, u2_s, ti_s, tw_s, pos_s, cnt_s = _out_proj(flat(x_sample), flat(conv_s), flat(att_s), mod_s, out_vecs, out_w, cnt0,
                                                     tt=n_s, alpha=alpha, per_row=True)
    bm_s = max(SUBLANES, min(512, n_s * TOP_K // N_EXPERTS))
    y_s = _moe(u2_s.reshape(n_s, d), x1_s.reshape(n_s, d), ti_s.reshape(n_s, LANES)[:, :TOP_K], tw_s.reshape(n_s, LANES),
               pos_s.reshape(n_s, LANES)[:, :TOP_K], cnt_s[0, :N_EXPERTS], mod_s, experts, g2, b2,
               bm=bm_s, tc=n_s, alpha=alpha, per_row=True, tiles_per_batch=1).reshape(bs, tq, d)

    heads = lambda a: a.reshape(1, a.shape[0], a.shape[1], N_KV, HEAD_DIM)
    win_keep = min(WINDOW, seq)
    heads_t = lambda a: a.reshape(1, a.shape[0], N_KV, HEAD_DIM, a.shape[2]).transpose(0, 1, 4, 2, 3)
    new_p = (heads_t(kct_p), heads_t(vct_p), heads_t(kst_p), heads_t(vst_p), heads_t(kwt_p[:, :, seq - win_keep:]),
             heads_t(vwt_p[:, :, seq - win_keep:]), tail_p[None, :, CONV_HIST - (CONV_K - 1):])
    win_buf = state_win_k.shape[2]
    keep = lambda old, new: jnp.concatenate([old[0], new.reshape(bs, tq, N_KV, HEAD_DIM)], axis=1)[None, :, -win_buf:]
    new_s = (heads(kc_s), heads(vc_s), heads(ks_s), heads(vs_s), keep(state_win_k, kw_s), keep(state_win_v, vw_s),
             jnp.concatenate([state_conv[0], glu_s], axis=1)[None, :, -(CONV_K - 1):])
    return (y_p, y_s) + new_p + new_s
```

```python
import functools

import jax
import jax.numpy as jnp
from jax import lax
from jax.experimental import pallas as pl
from jax.experimental.pallas import tpu as pltpu

F32 = jnp.float32
BF16 = jnp.bfloat16
I32 = jnp.int32

N_HEADS = 8
HEAD_DIM = 64
N_KV = 2
Q_PER_KV = N_HEADS // N_KV
KV_DIM = N_KV * HEAD_DIM
CMP_LEN = 32
CMP_STRIDE = 16
CMP_HID = 2 * HEAD_DIM
SEL_LEN = 64
SEL_RATIO = SEL_LEN // CMP_STRIDE
N_SEL = 16
WINDOW = 512
Q_BLOCK = 128
N_GATES = 3
N_EXPERTS = 32
TOP_K = 4
CONV_K = 31
SWIGLU_LIMIT = 7.0
SWIGLU_ALPHA = 1.702
LN_EPS = 1e-5
NEG_INF = -1e30
FORCE = 1e9
SLOPES = tuple(2.0 ** (-8.0 * (h + 1) / N_HEADS) for h in range(N_HEADS))

LANES = 128
SUBLANES = 8
VMEM_LIMIT_BYTES = 56 * 1024 * 1024

CONV_HIST = 32
CHUNK_ROW = CMP_STRIDE * KV_DIM
SLC_CHUNK = 512
DEC_CHUNK = 2048
DEC_Q = 8
V_AUG = HEAD_DIM + 16
LOG2E = 1.4426950408889634
HIGHEST = lax.Precision.HIGHEST


def _cparams(*sem):
    return pltpu.CompilerParams(dimension_semantics=sem, vmem_limit_bytes=VMEM_LIMIT_BYTES)


def _sigmoid(x):
    return 1.0 / (1.0 + jnp.exp(-x))


def _bdot(a, b):
    return jnp.dot(a.astype(BF16), b.astype(BF16), preferred_element_type=F32)


def _bdot_nt(a, b):
    return lax.dot_general(a.astype(BF16), b.astype(BF16), (((1,), (1,)), ((), ())), preferred_element_type=F32)


def _layer_norm(x, g, b):
    mu = jnp.mean(x, -1, keepdims=True)
    xc = x - mu
    var = jnp.mean(xc * xc, -1, keepdims=True)
    return xc * lax.rsqrt(var + LN_EPS) * g + b


def _gelu_tanh(x):
    return 0.5 * x * (1.0 + jnp.tanh(0.7978845608028654 * (x + 0.044715 * (x * x * x))))


def _masked_softmax(s, mask):
    s = jnp.where(mask, s, NEG_INF)
    e = jnp.where(mask, jnp.exp(s - jnp.max(s, -1, keepdims=True)), 0.0)
    return e * (1.0 / jnp.maximum(jnp.sum(e, -1, keepdims=True), 1e-30))


def _ada_kernel(c_ref, w_ref, b_ref, o_ref):
    c = c_ref[...]
    o_ref[...] = _bdot(c * _sigmoid(c), w_ref[...]) + b_ref[...]


def _ada(c_all, w_ada, b_ada):
    nb, d = c_all.shape
    n = w_ada.shape[1]
    bn = d
    return pl.pallas_call(
        _ada_kernel,
        grid=(n // bn,),
        in_specs=[pl.BlockSpec((nb, d), lambda i: (0, 0)),
                  pl.BlockSpec((d, bn), lambda i: (0, i)),
                  pl.BlockSpec((1, bn), lambda i: (0, i))],
        out_specs=pl.BlockSpec((nb, bn), lambda i: (0, i)),
        out_shape=jax.ShapeDtypeStruct((nb, n), F32),
        compiler_params=_cparams("arbitrary"),
        name="ada",
    )(c_all, w_ada, b_ada)


def _mod_rows(mod_ref, i, per_row):
    return mod_ref[i] if per_row else mod_ref[0, i:i + 1, :]


def _conv_ln_silu(xc_ref, row0, n, wdw_ref, bdw, lng, lnb):
    acc = jnp.zeros((n, xc_ref.shape[1]), F32) + bdw
    off = CONV_HIST - (CONV_K - 1)
    for k in range(CONV_K):
        acc = acc + xc_ref[pl.ds(row0 + off + k, n), :] * wdw_ref[k:k + 1, :]
    y = _layer_norm(acc, lng, lnb)
    return y * _sigmoid(y)


def _in_kernel(*refs, tt, c_conv, fuse_conv, per_row):
    if fuse_conv:
        (x_ref, mod_ref, wa_ref, wq_ref, wkv_ref, wg_ref, hist_ref, wdw_ref, bdw_ref, lng_ref, lnb_ref,
         conv_ref, tail_ref, q_ref, kc_ref, vc_ref, *kvt_refs, gates_ref, xc_ref) = refs
    else:
        (x_ref, mod_ref, wa_ref, wq_ref, wkv_ref, wg_ref,
         glu_ref, q_ref, kc_ref, vc_ref, ks_ref, vs_ref, kw_ref, vw_ref, gates_ref) = refs
    x = x_ref[0]
    shift1 = _mod_rows(mod_ref, 0, per_row)
    scale1 = _mod_rows(mod_ref, 1, per_row)
    u = (x * (1.0 + scale1) + shift1).astype(BF16)
    a = jnp.dot(u, wa_ref[...], preferred_element_type=F32)
    glu = a[:, :c_conv] * _sigmoid(a[:, c_conv:])
    q_ref[0] = jnp.dot(u, wq_ref[...], preferred_element_type=F32)
    kv = jnp.dot(u, wkv_ref[...], preferred_element_type=F32)
    gates_ref[0] = _sigmoid(jnp.dot(u, wg_ref[...], preferred_element_type=F32))
    if not fuse_conv:
        for i, r in enumerate((kc_ref, vc_ref, ks_ref, vs_ref, kw_ref, vw_ref)):
            r[0] = kv[:, i * KV_DIM:(i + 1) * KV_DIM]
        glu_ref[0] = glu
        return
    kc_ref[0] = kv[:, 0:KV_DIM]
    vc_ref[0] = kv[:, KV_DIM:2 * KV_DIM]
    for i, r in enumerate(kvt_refs):
        r[0] = jnp.transpose(kv[:, i * KV_DIM:(i + 1) * KV_DIM])

    @pl.when(pl.program_id(1) == 0)
    def _():
        xc_ref[0:CONV_HIST, :] = hist_ref[0]

    xc_ref[CONV_HIST:CONV_HIST + tt, :] = glu
    rows = min(tt, 64)
    for r0 in range(0, tt, rows):
        conv_ref[0, r0:r0 + rows, :] = _conv_ln_silu(xc_ref, r0, rows, wdw_ref, bdw_ref[...], lng_ref[...], lnb_ref[...])
    tail = xc_ref[tt:tt + CONV_HIST, :]
    xc_ref[0:CONV_HIST, :] = tail
    tail_ref[0] = tail


def _in_proj(x, mod, wts, conv=None, *, tt, per_row):
    b, t, d = x.shape
    wa, wq, wkv, wg = wts
    c_conv = wa.shape[1] // 2
    fuse = conv is not None
    nt = t // tt
    const = lambda shape: pl.BlockSpec(shape, lambda i, j: (0,) * len(shape))
    mod_spec = (pl.BlockSpec((6, tt, d), lambda i, j: (0, j, 0)) if per_row
                else pl.BlockSpec((1, 6, d), lambda i, j: (i, 0, 0)))
    in_specs = [pl.BlockSpec((1, tt, d), lambda i, j: (i, j, 0)), mod_spec,
                const(wa.shape), const(wq.shape), const(wkv.shape), const(wg.shape)]
    args = [x, mod, wa, wq, wkv, wg]
    tok = lambda n: pl.BlockSpec((1, tt, n), lambda i, j: (i, j, 0))
    tok_shape = lambda n: jax.ShapeDtypeStruct((b, t, n), F32)
    out_specs, out_shape, scratch = [], [], []
    if fuse:
        hist, wdw, bdw, lng, lnb = conv
        in_specs += [pl.BlockSpec((1, CONV_HIST, c_conv), lambda i, j: (i, 0, 0)),
                     const(wdw.shape), const(bdw.shape), const(lng.shape), const(lnb.shape)]
        args += [hist, wdw, bdw, lng, lnb]
        out_specs += [tok(c_conv), pl.BlockSpec((1, CONV_HIST, c_conv), lambda i, j: (i, 0, 0))]
        out_shape += [tok_shape(c_conv), jax.ShapeDtypeStruct((b, CONV_HIST, c_conv), F32)]
        scratch = [pltpu.VMEM((CONV_HIST + tt, c_conv), F32)]
    else:
        out_specs += [tok(c_conv)]
        out_shape += [tok_shape(c_conv)]
    if fuse:
        kv_specs = [tok(KV_DIM)] * 2 + [pl.BlockSpec((1, KV_DIM, tt), lambda i, j: (i, 0, j))] * 6
        kv_shape = [tok_shape(KV_DIM)] * 2 + [jax.ShapeDtypeStruct((b, KV_DIM, t), F32)] * 6
    else:
        kv_specs, kv_shape = [tok(KV_DIM)] * 6, [tok_shape(KV_DIM)] * 6
    out_specs += [tok(wq.shape[1])] + kv_specs + [tok(wg.shape[1])]
    out_shape += [tok_shape(wq.shape[1])] + kv_shape + [tok_shape(wg.shape[1])]
    return pl.pallas_call(
        functools.partial(_in_kernel, tt=tt, c_conv=c_conv, fuse_conv=fuse, per_row=per_row),
        grid=(b, nt), in_specs=in_specs, out_specs=out_specs, out_shape=out_shape, scratch_shapes=scratch,
        compiler_params=_cparams("arbitrary", "arbitrary"),
        name="in_proj_conv" if fuse else "in_proj",
    )(*args)


def _conv_small_kernel(xc_ref, wdw_ref, bdw_ref, lng_ref, lnb_ref, o_ref, *, tt):
    o_ref[0] = _conv_ln_silu(xc_ref.at[0], 0, tt, wdw_ref, bdw_ref[...], lng_ref[...], lnb_ref[...])


def _conv_small(xc, wdw, bdw, lng, lnb, *, tt):
    b, rows, c = xc.shape
    const = lambda shape: pl.BlockSpec(shape, lambda i: (0,) * len(shape))
    return pl.pallas_call(
        functools.partial(_conv_small_kernel, tt=tt),
        grid=(b,),
        in_specs=[pl.BlockSpec((1, rows, c), lambda i: (i, 0, 0)),
                  const(wdw.shape), const(bdw.shape), const(lng.shape), const(lnb.shape)],
        out_specs=pl.BlockSpec((1, tt, c), lambda i: (i, 0, 0)),
        out_shape=jax.ShapeDtypeStruct((b, tt, c), F32),
        compiler_params=_cparams("arbitrary"),
        name="conv_small",
    )(xc, wdw, bdw, lng, lnb)


def _compress_weights(w1, w2, pe):
    half = CMP_STRIDE * HEAD_DIM

    def expand(w):
        w = w.reshape(CMP_STRIDE, HEAD_DIM, CMP_HID)
        z = jnp.zeros_like(w)
        g0 = jnp.concatenate([w, z], axis=1).reshape(CHUNK_ROW, CMP_HID)
        g1 = jnp.concatenate([z, w], axis=1).reshape(CHUNK_ROW, CMP_HID)
        return jnp.concatenate([g0, g1], axis=1)

    w1p = jnp.concatenate([expand(w1[:half]), expand(w1[half:])], axis=1).astype(BF16)
    z2 = jnp.zeros_like(w2)
    w2p = jnp.concatenate([jnp.concatenate([w2, z2], axis=1), jnp.concatenate([z2, w2], axis=1)], axis=0).astype(BF16)
    return w1p, w1.astype(BF16), pe.reshape(1, -1), w2p


def _compress_tail(r, w1_ref, pe_ref, w2p_ref):
    n = r.shape[0]
    pe8 = jnp.broadcast_to(pe_ref[...], (SUBLANES, pe_ref.shape[1]))
    c = _bdot(pe8, w1_ref[...])[0:1]
    c2 = jnp.concatenate([c, c], axis=1)
    h = r[:, :2 * CMP_HID] + pltpu.roll(r[:, 2 * CMP_HID:], n - 1, 0) + c2
    return _bdot(_gelu_tanh(h), w2p_ref[...])


def _compress_prompt_kernel(ak_ref, av_ref, w1pk, w1k, pek, w2pk, w1pv, w1v, pev, w2pv, ck_ref, cv_ref):
    ck_ref[0] = _compress_tail(_bdot(ak_ref[0], w1pk[...]), w1k, pek, w2pk)
    cv_ref[0] = _compress_tail(_bdot(av_ref[0], w1pv[...]), w1v, pev, w2pv)


def _compress_prompt(ak, av, wk, wv):
    b, n, _ = ak.shape
    const = lambda a: pl.BlockSpec(a.shape, lambda i: (0,) * a.ndim)
    row = pl.BlockSpec((1, n, CHUNK_ROW), lambda i: (i, 0, 0))
    out = pl.BlockSpec((1, n, KV_DIM), lambda i: (i, 0, 0))
    return pl.pallas_call(
        _compress_prompt_kernel,
        grid=(b,),
        in_specs=[row, row] + [const(a) for a in wk] + [const(a) for a in wv],
        out_specs=[out, out],
        out_shape=[jax.ShapeDtypeStruct((b, n, KV_DIM), F32)] * 2,
        compiler_params=_cparams("arbitrary"),
        name="compress_prompt",
    )(ak, av, *wk, *wv)


def _compress_decode_kernel(pt_ref, pk_ref, pv_ref, w1pk, w1k, pek, w2pk, w1pv, w1v, pev, w2pv, ck_ref, cv_ref,
                            slab_k, slab_v, rows_buf, rbuf, sem, *, n_pages, page):
    b = pl.program_id(0)

    def page_copies(pg):
        pid = pt_ref[b * n_pages + pg]
        dst = pl.ds(pl.multiple_of(pg * page, page), page)
        return (pltpu.make_async_copy(pk_ref.at[pid], slab_k.at[:, dst], sem.at[0]),
                pltpu.make_async_copy(pv_ref.at[pid], slab_v.at[:, dst], sem.at[1]))

    def start(pg, carry):
        for cp in page_copies(pg):
            cp.start()
        return carry

    def wait(pg, carry):
        for cp in page_copies(pg):
            cp.wait()
        return carry

    lax.fori_loop(0, n_pages, start, 0)
    lax.fori_loop(0, n_pages, wait, 0)
    past = n_pages * page
    n = past // CMP_STRIDE
    step = min(n, 256)
    tcols = min(past, 2048)

    def project(slab, w1p):
        def to_rows(i, carry):
            at = pl.ds(pl.multiple_of(i * tcols, tcols), tcols)
            rows_buf[at, :] = jnp.transpose(slab[:, at])
            return carry
        lax.fori_loop(0, past // tcols, to_rows, 0)

        def body(j, carry):
            base = pl.multiple_of(j * (step * CMP_STRIDE), step * CMP_STRIDE)
            acc = jnp.zeros((step, 4 * CMP_HID), F32)
            for p in range(CMP_STRIDE):
                acc = acc + _bdot(rows_buf[pl.ds(base + p, step, stride=CMP_STRIDE), :], w1p[p * KV_DIM:(p + 1) * KV_DIM, :])
            rbuf[pl.ds(pl.multiple_of(j * step, step), step), :] = acc
            return carry
        lax.fori_loop(0, n // step, body, 0)
        return rbuf[...]

    ck_ref[0] = _compress_tail(project(slab_k, w1pk), w1k, pek, w2pk)
    cv_ref[0] = _compress_tail(project(slab_v, w1pv), w1v, pev, w2pv)


def _compress_decode(page_table, pool_k, pool_v, wk, wv):
    nb, n_pages = page_table.shape
    page = pool_k.shape[2]
    past = n_pages * page
    n = past // CMP_STRIDE
    assert page % LANES == 0
    const = lambda a: pl.BlockSpec(a.shape, lambda i, pt: (0,) * a.ndim)
    any_spec = pl.BlockSpec(memory_space=pl.ANY)
    out = pl.BlockSpec((1, n, KV_DIM), lambda i, pt: (i, 0, 0))
    grid_spec = pltpu.PrefetchScalarGridSpec(
        num_scalar_prefetch=1, grid=(nb,),
        in_specs=[any_spec, any_spec] + [const(a) for a in wk] + [const(a) for a in wv],
        out_specs=[out, out],
        scratch_shapes=[pltpu.VMEM((KV_DIM, past), F32), pltpu.VMEM((KV_DIM, past), F32), pltpu.VMEM((past, KV_DIM), F32),
                        pltpu.VMEM((n, 4 * CMP_HID), F32), pltpu.SemaphoreType.DMA((2,))])
    return pl.pallas_call(
        functools.partial(_compress_decode_kernel, n_pages=n_pages, page=page),
        grid_spec=grid_spec,
        out_shape=[jax.ShapeDtypeStruct((nb, n, KV_DIM), F32)] * 2,
        compiler_params=_cparams("arbitrary"),
        name="compress_decode",
    )(page_table.reshape(-1), pool_k, pool_v, *wk, *wv)


def _importance_matrix(n_cmp_rows, n_cmp, n_blk, cols):
    c = jnp.arange(n_cmp_rows)[:, None]
    b = jnp.arange(cols)[None, :]
    d = c - (SEL_RATIO * b - 1)
    return ((d >= 0) & (d <= SEL_RATIO) & (c < n_cmp) & (b < n_blk)).astype(F32)


def _expand_matrix(n_chunks, chunk, blk0):
    key = jnp.arange(n_chunks)[:, None, None] * chunk + jnp.arange(chunk)[None, None, :]
    blk = jnp.asarray(blk0)[:, None, None] + jnp.arange(LANES)[None, :, None]
    return (key // SEL_LEN == blk).astype(BF16)


def _aug_values(vt):
    r = lax.broadcasted_iota(I32, (V_AUG - HEAD_DIM, vt.shape[1]), 0)
    return jnp.concatenate([vt, jnp.where(r == 0, 1.0, 0.0)], axis=0).astype(BF16)


def _flash_step(qh, kt, vta, bias, m, acc):
    s = jnp.dot(qh, kt, preferred_element_type=F32) + bias
    m_new = jnp.maximum(m, jnp.max(s, -1, keepdims=True))
    p = jnp.exp2(s - m_new).astype(BF16)
    pv = lax.dot_general(p, vta, (((1,), (1,)), ((), ())), preferred_element_type=F32)
    return m_new, jnp.exp2(m - m_new) * acc + pv


def _flash_init(rows):
    return jnp.full((rows, 1), NEG_INF, F32), jnp.zeros((rows, V_AUG), F32)


def _flash_out(acc):
    return acc[:, :HEAD_DIM] * (1.0 / jnp.maximum(acc[:, HEAD_DIM:HEAD_DIM + 1], 1e-30))


def _attn_prompt_kernel(q_ref, g_ref, ck_ref, cv_ref, kst_ref, vst_ref, kwt_ref, vwt_ref, mimp_ref, esel_ref, o_ref,
                        *, seq, n_blk, n_pick, win_len):
    qb = Q_BLOCK
    qs = pl.program_id(1) * qb
    t_col = qs + lax.broadcasted_iota(I32, (qb, 1), 0)
    gates = g_ref[0]
    scale = HEAD_DIM ** -0.5
    q_heads = [q_ref[0, :, h * HEAD_DIM:(h + 1) * HEAD_DIM] for h in range(N_HEADS)]
    nc = ck_ref.shape[1]
    cmp_end = lax.broadcasted_iota(I32, (1, nc), 1) * CMP_STRIDE + (CMP_LEN - 1)
    dist_c = t_col - cmp_end
    mask_c = dist_c >= 0
    dist_cf = dist_c.astype(F32)
    blk_pad = mimp_ref.shape[1]

    blk_i = lax.broadcasted_iota(I32, (n_blk, qb), 0)
    t_row = qs + lax.broadcasted_iota(I32, (n_blk, qb), 1)
    cur = t_row // SEL_LEN
    forced = (blk_i == 0) | (blk_i == cur) | (blk_i == cur - 1)
    future = blk_i * SEL_LEN > t_row

    win_start = pl.multiple_of(jnp.clip(qs - WINDOW, 0, seq - win_len), qb)
    dist_w = t_col - (win_start + lax.broadcasted_iota(I32, (1, win_len), 1))
    maskbias_w = jnp.where((dist_w >= 0) & (dist_w <= WINDOW), 0.0, NEG_INF)
    n_chunks = (qs + qb + SLC_CHUNK - 1) // SLC_CHUNK
    rows = Q_PER_KV * qb
    stack = lambda parts: jnp.concatenate(parts, axis=0)
    heads_of = lambda a: a.reshape(Q_PER_KV, qb, a.shape[1])

    def flash(qk, alibi, maskbias, off, vta, m, acc):
        s = (heads_of(qk + alibi) + maskbias[None]).reshape(rows, qk.shape[1])
        m_new = jnp.maximum(m, jnp.max(s, -1, keepdims=True) + off)
        p = jnp.exp2(s - (m_new - off)).astype(BF16)
        pv = lax.dot_general(p, vta, (((1,), (1,)), ((), ())), preferred_element_type=F32)
        return m_new, jnp.exp2(m - m_new) * acc + pv

    lanes_of = lambda g: slice(g * HEAD_DIM, (g + 1) * HEAD_DIM)
    heads_in = lambda g: range(g * Q_PER_KV, (g + 1) * Q_PER_KV)
    qg2s, slope2s, sels, o_cmps = [], [], [], []
    for g in range(N_KV):
        lanes, heads = lanes_of(g), heads_in(g)
        slope = stack([jnp.full((qb, 1), SLOPES[h], F32) for h in heads])
        qg = stack([(q_heads[h] * scale).astype(BF16) for h in heads])
        qg2s.append(stack([(q_heads[h] * (scale * LOG2E)).astype(BF16) for h in heads]))
        slope2s.append(slope * LOG2E)
        s_c = heads_of(_bdot_nt(qg, ck_ref[0, :, lanes]) - slope * stack([dist_cf] * Q_PER_KV))
        p_c = _masked_softmax(s_c, mask_c[None])
        p_sum = jnp.sum(p_c, axis=0)
        o_cmp = heads_of(_bdot(p_c.reshape(rows, nc), cv_ref[0, :, lanes]))
        imp = jnp.dot(p_sum, mimp_ref[...], precision=HIGHEST, preferred_element_type=F32)
        imp_t = jnp.transpose(imp)[0:n_blk, :]
        imp_t = jnp.where(future, -FORCE, jnp.where(forced, FORCE, imp_t))
        cnt = jnp.zeros((n_blk, qb), F32)
        for b2 in range(n_blk):
            row = imp_t[b2:b2 + 1, :]
            ahead = (row > imp_t) | ((row == imp_t) & (blk_i > b2))
            cnt = cnt + jnp.where(ahead, 1.0, 0.0)
        sel_t = jnp.where(cnt < n_pick, 1.0, 0.0)
        if n_blk < blk_pad:
            sel_t = jnp.concatenate([sel_t, jnp.zeros((blk_pad - n_blk, qb), F32)], axis=0)
        sels.append(jnp.transpose(sel_t).astype(BF16))
        o_cmps.append(o_cmp)

    key_slc = lax.broadcasted_iota(I32, (1, SLC_CHUNK), 1)
    key_win = lax.broadcasted_iota(I32, (1, win_len), 1).astype(F32)
    alibi_slc = [s2 * key_slc.astype(F32) for s2 in slope2s]

    def slc_step(c, carry):
        k0 = pl.multiple_of(c * SLC_CHUNK, SLC_CHUNK)
        causal = t_col - (k0 + key_slc) >= 0
        out = []
        for g in range(N_KV):
            kt = kst_ref[0, lanes_of(g), pl.ds(k0, SLC_CHUNK)].astype(BF16)
            vta = _aug_values(vst_ref[0, lanes_of(g), pl.ds(k0, SLC_CHUNK)])
            msel = jnp.dot(sels[g], esel_ref[c], preferred_element_type=F32)
            maskbias = jnp.where(causal & (msel > 0.5), 0.0, NEG_INF)
            qk = jnp.dot(qg2s[g], kt, preferred_element_type=F32)
            out += flash(qk, alibi_slc[g], maskbias, slope2s[g] * k0.astype(F32), vta, *carry[2 * g:2 * g + 2])
        return tuple(out)

    fin = lax.fori_loop(0, n_chunks, slc_step, _flash_init(rows) * N_KV)
    for g in range(N_KV):
        o_slc = heads_of(_flash_out(fin[2 * g + 1]))
        kwt = kwt_ref[0, lanes_of(g), pl.ds(win_start, win_len)].astype(BF16)
        vwta = _aug_values(vwt_ref[0, lanes_of(g), pl.ds(win_start, win_len)])
        qk = jnp.dot(qg2s[g], kwt, preferred_element_type=F32)
        o_win = heads_of(_flash_out(flash(qk, slope2s[g] * key_win, maskbias_w, 0.0, vwta, *_flash_init(rows))[1]))
        for i, h in enumerate(heads_in(g)):
            c0 = h * N_GATES
            o_ref[0, :, h * HEAD_DIM:(h + 1) * HEAD_DIM] = (gates[:, c0:c0 + 1] * o_cmps[g][i] + gates[:, c0 + 1:c0 + 2] * o_slc[i]
                                                           + gates[:, c0 + 2:c0 + 3] * o_win[i])


def _attn_prompt(q, gates, ck, cv, ks, vs, kw, vw):
    b, t, att = q.shape
    assert t % SLC_CHUNK == 0 and t % Q_BLOCK == 0
    n_blk = t // SEL_LEN
    n_cmp = t // CMP_STRIDE - 1
    assert n_blk <= LANES and ck.shape[1] == t // CMP_STRIDE
    n_pick = min(N_SEL, n_blk)
    win_len = min(WINDOW + Q_BLOCK, t)
    mimp = _importance_matrix(ck.shape[1], n_cmp, n_blk, LANES)
    n_ch = t // SLC_CHUNK
    esel = _expand_matrix(n_ch, SLC_CHUNK, [0] * n_ch)
    tok = lambda n: pl.BlockSpec((1, Q_BLOCK, n), lambda i, j: (i, j, 0))
    full = lambda a: pl.BlockSpec((1,) + a.shape[1:], lambda i, j: (i, 0, 0))
    const = lambda a: pl.BlockSpec(a.shape, lambda i, j: (0,) * a.ndim)
    return pl.pallas_call(
        functools.partial(_attn_prompt_kernel, seq=t, n_blk=n_blk, n_pick=n_pick, win_len=win_len),
        grid=(b, t // Q_BLOCK),
        in_specs=[tok(att), tok(gates.shape[2]), full(ck), full(cv), full(ks), full(vs), full(kw), full(vw),
                  const(mimp), const(esel)],
        out_specs=tok(att),
        out_shape=jax.ShapeDtypeStruct((b, t, att), F32),
        compiler_params=_cparams("arbitrary", "arbitrary"),
        name="attn_prompt",
    )(q, gates, ck, cv, ks, vs, kw, vw, mimp, esel)


def _attn_decode_kernel(pt_ref, q_ref, g_ref, ck_ref, cv_ref, ksn_ref, vsn_ref, kwn_ref, vwn_ref, wk_ref, wv_ref,
                        pk_ref, pv_ref, mimp_ref, esel_ref, o_ref, slab_k, slab_v, sem,
                        *, n_pages, page, past, n_blk, n_pick, tq):
    b = pl.program_id(0)

    def page_copies(pg):
        pid = pt_ref[b * n_pages + pg]
        dst = pl.ds(pl.multiple_of(pg * page, page), page)
        return (pltpu.make_async_copy(pk_ref.at[pid], slab_k.at[:, dst], sem.at[0]),
                pltpu.make_async_copy(pv_ref.at[pid], slab_v.at[:, dst], sem.at[1]))

    def start(pg, carry):
        for cp in page_copies(pg):
            cp.start()
        return carry

    def wait(pg, carry):
        for cp in page_copies(pg):
            cp.wait()
        return carry

    lax.fori_loop(0, n_pages, start, 0)

    rows = Q_PER_KV * DEC_Q
    qi = lax.broadcasted_iota(I32, (rows, 1), 0) % DEC_Q
    t_col = past + qi % tq
    hl = lax.broadcasted_iota(I32, (rows, 1), 0) // DEC_Q
    nc = ck_ref.shape[1]
    cmp_end = lax.broadcasted_iota(I32, (1, nc), 1) * CMP_STRIDE + (CMP_LEN - 1)
    dist_c = t_col - cmp_end
    mask_c = dist_c >= 0
    dist_cf = dist_c.astype(F32)
    blk_pad = mimp_ref.shape[1]
    t8 = past + lax.broadcasted_iota(I32, (DEC_Q, 1), 0) % tq
    blk_l = lax.broadcasted_iota(I32, (DEC_Q, blk_pad), 1)
    cur = t8 // SEL_LEN
    forced = (blk_l == 0) | (blk_l == cur) | (blk_l == cur - 1)
    future = blk_l * SEL_LEN > t8
    pad_blk = blk_l >= n_blk
    b_sub = lax.broadcasted_iota(I32, (blk_pad, blk_pad), 0)
    b_lane = lax.broadcasted_iota(I32, (blk_pad, blk_pad), 1)
    n_chunks = past // DEC_CHUNK
    new_rows = ksn_ref.shape[2]

    sels, slopes, o_cmps = [], [], []
    for g in range(N_KV):
        lanes = slice(g * HEAD_DIM, (g + 1) * HEAD_DIM)
        slope = jnp.zeros((rows, 1), F32)
        for i in range(Q_PER_KV):
            slope = jnp.where(hl == i, SLOPES[g * Q_PER_KV + i], slope)
        slopes.append(slope)
        qg = (q_ref[0, g] * (HEAD_DIM ** -0.5)).astype(BF16)
        p = _masked_softmax(_bdot_nt(qg, ck_ref[0, :, lanes]) - slope * dist_cf, mask_c)
        o_cmps.append(_bdot(p, cv_ref[0, :, lanes]))
        p_sum = p[0:DEC_Q]
        for i in range(1, Q_PER_KV):
            p_sum = p_sum + p[i * DEC_Q:(i + 1) * DEC_Q]
        imp = jnp.dot(p_sum, mimp_ref[...], precision=HIGHEST, preferred_element_type=F32)
        imp = jnp.where(pad_blk, -3e38, jnp.where(future, -FORCE, jnp.where(forced, FORCE, imp)))
        imp_t = jnp.transpose(jnp.concatenate([imp, jnp.zeros((LANES - DEC_Q, blk_pad), F32)], axis=0))
        sel_rows = []
        for r in range(DEC_Q):
            col = imp_t[:, r:r + 1]
            row = imp[r:r + 1, :]
            ahead = (col > row) | ((col == row) & (b_sub < b_lane))
            cnt = jnp.sum(jnp.where(ahead, 1.0, 0.0), axis=0, keepdims=True)
            sel_rows.append(jnp.where(cnt < n_pick, 1.0, 0.0))
        sel8 = jnp.concatenate(sel_rows, axis=0)
        sels.append(jnp.concatenate([sel8] * Q_PER_KV, axis=0))

    lax.fori_loop(0, n_pages, wait, 0)

    for g in range(N_KV):
        lanes = slice(g * HEAD_DIM, (g + 1) * HEAD_DIM)
        slope2 = slopes[g] * LOG2E
        qg2 = (q_ref[0, g] * (HEAD_DIM ** -0.5 * LOG2E)).astype(BF16)
        sel = sels[g]
        sel_b = sel.astype(BF16)

        def bias_of(dist, visible):
            return jnp.where(visible, 0.0, NEG_INF) - slope2 * dist.astype(F32)

        m, acc = _flash_init(rows)
        per_mat = LANES * SEL_LEN // DEC_CHUNK
        for c in range(n_chunks):
            keys = slice(c * DEC_CHUNK, (c + 1) * DEC_CHUNK)
            lane0 = (c // per_mat) * LANES
            msel = jnp.dot(sel_b[:, lane0:lane0 + LANES], esel_ref[c % per_mat], preferred_element_type=F32)
            dist = t_col - (c * DEC_CHUNK + lax.broadcasted_iota(I32, (1, DEC_CHUNK), 1))
            m, acc = _flash_step(qg2, slab_k[lanes, keys].astype(BF16), _aug_values(slab_v[lanes, keys]),
                                 bias_of(dist, (dist >= 0) & (msel > 0.5)), m, acc)
        dist_n = t_col - (past + lax.broadcasted_iota(I32, (1, new_rows), 1))
        cur_blk = past // SEL_LEN
        m, acc = _flash_step(qg2, ksn_ref[0, lanes, :].astype(BF16), _aug_values(vsn_ref[0, lanes, :]),
                             bias_of(dist_n, (dist_n >= 0) & (sel[:, cur_blk:cur_blk + 1] > 0.5)), m, acc)
        o_slc = _flash_out(acc)
        wlen = wk_ref.shape[2]
        dist = t_col - (past - wlen + lax.broadcasted_iota(I32, (1, wlen), 1))
        m, acc = _flash_step(qg2, wk_ref[0, lanes, :].astype(BF16), _aug_values(wv_ref[0, lanes, :]),
                             bias_of(dist, (dist >= 0) & (dist <= WINDOW)), *_flash_init(rows))
        m, acc = _flash_step(qg2, kwn_ref[0, lanes, :].astype(BF16), _aug_values(vwn_ref[0, lanes, :]),
                             bias_of(dist_n, (dist_n >= 0) & (dist_n <= WINDOW)), m, acc)
        o_win = _flash_out(acc)
        gt = g_ref[0, g]
        o_ref[0, g] = gt[:, 0:1] * o_cmps[g] + gt[:, 1:2] * o_slc + gt[:, 2:3] * o_win


def _attn_decode(page_table, q, gates, ck, cv, ksn, vsn, kwn, vwn, win_k, win_v, pool_k, pool_v):
    nb, tq, att = q.shape
    n_pages = page_table.shape[1]
    page = pool_k.shape[2]
    past = n_pages * page
    assert tq <= DEC_Q and past % DEC_CHUNK == 0 and past % SEL_LEN == 0 and (past + tq - 1) // SEL_LEN == past // SEL_LEN
    assert win_k.shape[2] == WINDOW and ck.shape[1] * CMP_STRIDE == past and page % LANES == 0
    t_pad = -(-(past + tq) // SEL_LEN) * SEL_LEN
    n_blk = t_pad // SEL_LEN
    n_cmp = t_pad // CMP_STRIDE - 1
    n_pick = min(N_SEL, n_blk)
    blk_pad = -(-n_blk // LANES) * LANES
    mimp = _importance_matrix(ck.shape[1], n_cmp, n_blk, blk_pad)
    per_mat = LANES * SEL_LEN // DEC_CHUNK
    esel = _expand_matrix(per_mat, DEC_CHUNK, [0] * per_mat)

    def dec_rows(a, width):
        a = a.reshape(nb, tq, N_KV, Q_PER_KV, width)
        a = jnp.take(a, jnp.arange(DEC_Q) % tq, axis=1)
        return a.transpose(0, 2, 3, 1, 4).reshape(nb, N_KV, Q_PER_KV * DEC_Q, width)

    q_d = dec_rows(q, HEAD_DIM)
    g_d = dec_rows(gates[:, :, :N_HEADS * N_GATES], N_GATES)
    pad_new = lambda a: jnp.pad(a.transpose(0, 2, 1), ((0, 0), (0, 0), (0, LANES - tq)))
    ksn, vsn, kwn, vwn = [pad_new(a) for a in (ksn, vsn, kwn, vwn)]
    rows = Q_PER_KV * DEC_Q
    blk = lambda a: pl.BlockSpec((1,) + a.shape[1:], lambda i, pt: (i,) + (0,) * (a.ndim - 1))
    const = lambda a: pl.BlockSpec(a.shape, lambda i, pt: (0,) * a.ndim)
    any_spec = pl.BlockSpec(memory_space=pl.ANY)
    grid_spec = pltpu.PrefetchScalarGridSpec(
        num_scalar_prefetch=1, grid=(nb,),
        in_specs=[blk(q_d), blk(g_d), blk(ck), blk(cv), blk(ksn), blk(vsn), blk(kwn), blk(vwn), blk(win_k), blk(win_v),
                  any_spec, any_spec, const(mimp), const(esel)],
        out_specs=pl.BlockSpec((1, N_KV, rows, HEAD_DIM), lambda i, pt: (i, 0, 0, 0)),
        scratch_shapes=[pltpu.VMEM((KV_DIM, past), F32), pltpu.VMEM((KV_DIM, past), F32), pltpu.SemaphoreType.DMA((2,))])
    o = pl.pallas_call(
        functools.partial(_attn_decode_kernel, n_pages=n_pages, page=page, past=past, n_blk=n_blk, n_pick=n_pick, tq=tq),
        grid_spec=grid_spec,
        out_shape=jax.ShapeDtypeStruct((nb, N_KV, rows, HEAD_DIM), F32),
        compiler_params=_cparams("arbitrary"),
        name="attn_decode",
    )(page_table.reshape(-1), q_d, g_d, ck, cv, ksn, vsn, kwn, vwn, win_k, win_v, pool_k, pool_v, mimp, esel)
    o = o.reshape(nb, N_KV, Q_PER_KV, DEC_Q, HEAD_DIM)[:, :, :, :tq]
    return o.transpose(0, 3, 1, 2, 4).reshape(nb, tq, att)


def _out_kernel(x_ref, conv_ref, att_ref, mod_ref, bc_ref, ba_ref, woc_ref, woa_ref, g1_ref, b1_ref, wr_ref, br_ref,
                cnt0_ref, x1_ref, u2_ref, idx_ref, w_ref, pos_ref, cnt_ref, carry_ref, *, tt, alpha, per_row):
    first = (pl.program_id(0) == 0) & (pl.program_id(1) == 0)

    @pl.when(first)
    def _():
        carry_ref[...] = cnt0_ref[...]

    gate1 = _mod_rows(mod_ref, 2, per_row)
    shift2 = _mod_rows(mod_ref, 3, per_row)
    scale2 = _mod_rows(mod_ref, 4, per_row)
    mix = _bdot(conv_ref[0] * bc_ref[...], woc_ref[...]) + _bdot(att_ref[0] * ba_ref[...], woa_ref[...])
    x1 = _layer_norm(alpha * x_ref[0] + gate1 * mix, g1_ref[...], b1_ref[...])
    x1_ref[0] = x1
    u2 = x1 * (1.0 + scale2) + shift2
    _store_tiled(u2_ref.at[0], 0, u2)
    logits = jnp.dot(u2, wr_ref[...], precision=HIGHEST, preferred_element_type=F32) + br_ref[...]
    lane = lax.broadcasted_iota(I32, logits.shape, 1)
    vals = logits
    top_v, top_i = [], []
    for _k in range(TOP_K):
        m = jnp.max(vals, axis=-1, keepdims=True)
        i = jnp.min(jnp.where(vals == m, lane, LANES), axis=-1, keepdims=True)
        top_v.append(m)
        top_i.append(i)
        vals = jnp.where(lane == i, -jnp.inf, vals)
    e = [jnp.exp(v - top_v[0]) for v in top_v]
    inv = 1.0 / (e[0] + e[1] + e[2] + e[3])
    onehot = [jnp.where(lane == i, 1.0, 0.0) for i in top_i]
    assigned = onehot[0] + onehot[1] + onehot[2] + onehot[3]
    r_i = lax.broadcasted_iota(I32, (tt, tt), 0)
    c_i = lax.broadcasted_iota(I32, (tt, tt), 1)
    before = jnp.where(r_i > c_i, 1.0, 0.0).astype(BF16)
    rank = carry_ref[...] + jnp.dot(before, assigned.astype(BF16), preferred_element_type=F32)
    idx_o = jnp.zeros(logits.shape, I32)
    w_o = jnp.zeros(logits.shape, F32)
    pos_o = jnp.zeros(logits.shape, F32)
    for k in range(TOP_K):
        idx_o = jnp.where(lane == k, top_i[k], idx_o)
        w_o = jnp.where(lane == k, e[k] * inv, w_o)
        pos_o = jnp.where(lane == k, jnp.sum(onehot[k] * rank, axis=-1, keepdims=True), pos_o)
    idx_ref[0] = idx_o
    w_ref[0] = w_o
    pos_ref[0] = pos_o.astype(I32)
    carry_ref[...] = carry_ref[...] + jnp.sum(assigned, axis=0, keepdims=True)
    cnt_ref[...] = carry_ref[...]


def _out_proj(x, conv_out, att, mod, vecs, wts, cnt0, *, tt, alpha, per_row):
    b, t, d = x.shape
    bc, ba, g1, b1, br = vecs
    woc, woa, wr = wts
    const = lambda a: pl.BlockSpec(a.shape, lambda i, j: (0,) * a.ndim)
    tok = lambda n: pl.BlockSpec((1, tt, n), lambda i, j: (i, j, 0))
    mod_spec = (pl.BlockSpec((6, tt, d), lambda i, j: (0, j, 0)) if per_row
                else pl.BlockSpec((1, 6, d), lambda i, j: (i, 0, 0)))
    tok_shape = lambda n, dt: jax.ShapeDtypeStruct((b, t, n), dt)
    return pl.pallas_call(
        functools.partial(_out_kernel, tt=tt, alpha=alpha, per_row=per_row),
        grid=(b, t // tt),
        in_specs=[tok(d), tok(conv_out.shape[2]), tok(att.shape[2]), mod_spec, const(bc), const(ba), const(woc), const(woa),
                  const(g1), const(b1), const(wr), const(br), const(cnt0)],
        out_specs=[tok(d), pl.BlockSpec((1, tt * SUBLANES, LANES), lambda i, j: (i, j, 0)), tok(LANES), tok(LANES), tok(LANES),
                   pl.BlockSpec((1, LANES), lambda i, j: (0, 0))],
        out_shape=[tok_shape(d, F32), jax.ShapeDtypeStruct((b, t * SUBLANES, LANES), F32), tok_shape(LANES, I32),
                   tok_shape(LANES, F32), tok_shape(LANES, I32),
                   jax.ShapeDtypeStruct((1, LANES), F32)],
        scratch_shapes=[pltpu.VMEM((1, LANES), F32)],
        compiler_params=_cparams("arbitrary", "arbitrary"),
        name="out_proj_router",
    )(x, conv_out, att, mod, bc, ba, woc, woa, g1, b1, wr, br, cnt0)


def _tile_rows(x):
    return [x[:, j * LANES:(j + 1) * LANES] for j in range(x.shape[1] // LANES)]


def _store_tiled(ref, at, x):
    rows = x.shape[0]
    for j, piece in enumerate(_tile_rows(x)):
        ref[pl.ds(at + j, rows, stride=SUBLANES), :] = piece


def _load_tiled(ref, at, rows):
    return jnp.concatenate([ref[pl.ds(at + j, rows, stride=SUBLANES), :] for j in range(SUBLANES)], axis=1)


def _gather_pipeline(i, n, src_ref, cur_ref, nxt_ref, bufs, sem, rows, compute):
    def row_copy(src, buf, s, r):
        at = r * SUBLANES if isinstance(r, int) else pl.multiple_of(r * SUBLANES, SUBLANES)
        return pltpu.make_async_copy(src_ref.at[pl.ds(pl.multiple_of(src, SUBLANES), SUBLANES)],
                                     buf.at[pl.ds(at, SUBLANES)], sem.at[s])

    def whole(buf, s):
        return pltpu.make_async_copy(src_ref.at[pl.ds(0, rows * SUBLANES)], buf, sem.at[s])

    @pl.when(i == 0)
    def _():
        def body(r, carry):
            row_copy(cur_ref[0, 0, r], bufs[0], 0, r).start()
            return carry
        lax.fori_loop(0, rows, body, 0)

    def step(s):
        whole(bufs[s], s).wait()
        for r in range(rows):
            row_copy(nxt_ref[0, 0, r], bufs[1 - s], 1 - s, r).start()
        compute(bufs[s])

        @pl.when(i == n - 1)
        def _():
            whole(bufs[1 - s], 1 - s).wait()

    for s in range(2):
        pl.when(i % 2 == s)(functools.partial(step, s))


def _expert_kernel(be_ref, cur_ref, nxt_ref, u_ref, wgu_ref, bgu_ref, wdn_ref, bdn_ref, y_ref, xb0, xb1, sem, *, bm, d_ff):
    def compute(buf):
        gu = jnp.dot(_load_tiled(buf, 0, bm).astype(BF16), wgu_ref[0], preferred_element_type=F32) + bgu_ref[0]
        gl = jnp.minimum(gu[:, :d_ff], SWIGLU_LIMIT)
        lin = jnp.clip(gu[:, d_ff:], -SWIGLU_LIMIT, SWIGLU_LIMIT)
        act = gl * _sigmoid(SWIGLU_ALPHA * gl) * (lin + 1.0)
        _store_tiled(y_ref, 0, jnp.dot(act.astype(BF16), wdn_ref[0], preferred_element_type=F32) + bdn_ref[0])

    _gather_pipeline(pl.program_id(0), pl.num_programs(0), u_ref, cur_ref, nxt_ref, (xb0, xb1), sem, bm, compute)


def _expert_ffn(u2, slot_tok, blk_e, wgu, bgu, wdn, bdn, *, bm):
    n_blocks = slot_tok.shape[0]
    d = wgu.shape[1]
    assert d == SUBLANES * LANES
    d_ff = wdn.shape[1]
    slot_tok = slot_tok.reshape(n_blocks, 1, bm)
    smem = lambda f: pl.BlockSpec((1, 1, bm), f, memory_space=pltpu.SMEM)
    grid_spec = pltpu.PrefetchScalarGridSpec(
        num_scalar_prefetch=1, grid=(n_blocks,),
        in_specs=[smem(lambda i, be: (i, 0, 0)),
                  smem(lambda i, be: (jnp.minimum(i + 1, n_blocks - 1), 0, 0)),
                  pl.BlockSpec(memory_space=pl.ANY),
                  pl.BlockSpec((1, d, 2 * d_ff), lambda i, be: (be[i], 0, 0)),
                  pl.BlockSpec((1, 1, 2 * d_ff), lambda i, be: (be[i], 0, 0)),
                  pl.BlockSpec((1, d_ff, d), lambda i, be: (be[i], 0, 0)),
                  pl.BlockSpec((1, 1, d), lambda i, be: (be[i], 0, 0))],
        out_specs=pl.BlockSpec((bm * SUBLANES, LANES), lambda i, be: (i, 0)),
        scratch_shapes=[pltpu.VMEM((bm * SUBLANES, LANES), F32), pltpu.VMEM((bm * SUBLANES, LANES), F32),
                        pltpu.SemaphoreType.DMA((2,))])
    return pl.pallas_call(
        functools.partial(_expert_kernel, bm=bm, d_ff=d_ff),
        grid_spec=grid_spec,
        out_shape=jax.ShapeDtypeStruct((n_blocks * bm * SUBLANES, LANES), F32),
        compiler_params=_cparams("arbitrary"),
        name="expert_ffn",
    )(blk_e, slot_tok, slot_tok, u2, wgu, bgu, wdn, bdn)


def _combine_kernel(cur_ref, nxt_ref, ys_ref, w_ref, x1_ref, mod_ref, g2_ref, b2_ref, y_ref, gb0, gb1, sem,
                    *, tc, alpha, per_row):
    def compute(buf):
        w = w_ref[...]
        f = w[:, 0:1] * _load_tiled(buf, 0, tc)
        for k in range(1, TOP_K):
            f = f + w[:, k:k + 1] * _load_tiled(buf, k * tc * SUBLANES, tc)
        gate2 = mod_ref[5] if per_row else mod_ref[0, 5:6, :]
        y_ref[...] = _layer_norm(alpha * x1_ref[...] + gate2 * f, g2_ref[...], b2_ref[...])

    _gather_pipeline(pl.program_id(0), pl.num_programs(0), ys_ref, cur_ref, nxt_ref, (gb0, gb1), sem, TOP_K * tc, compute)


def _combine(ys, dest_t, top_w, x1, mod, g2, b2, *, tc, alpha, per_row, tiles_per_batch):
    n, d = x1.shape
    n_tiles = n // tc
    dest_t = dest_t.reshape(n_tiles, 1, TOP_K * tc)
    smem = lambda f: pl.BlockSpec((1, 1, TOP_K * tc), f, memory_space=pltpu.SMEM)
    const = lambda a: pl.BlockSpec(a.shape, lambda i: (0,) * a.ndim)
    mod_spec = (pl.BlockSpec((6, tc, d), lambda i: (0, i, 0)) if per_row
                else pl.BlockSpec((1, 6, d), lambda i: (i // tiles_per_batch, 0, 0)))
    return pl.pallas_call(
        functools.partial(_combine_kernel, tc=tc, alpha=alpha, per_row=per_row),
        grid=(n_tiles,),
        in_specs=[smem(lambda i: (i, 0, 0)), smem(lambda i: (jnp.minimum(i + 1, n_tiles - 1), 0, 0)),
                  pl.BlockSpec(memory_space=pl.ANY),
                  pl.BlockSpec((tc, LANES), lambda i: (i, 0)), pl.BlockSpec((tc, d), lambda i: (i, 0)),
                  mod_spec, const(g2), const(b2)],
        out_specs=pl.BlockSpec((tc, d), lambda i: (i, 0)),
        out_shape=jax.ShapeDtypeStruct((n, d), F32),
        scratch_shapes=[pltpu.VMEM((TOP_K * tc * SUBLANES, LANES), F32), pltpu.VMEM((TOP_K * tc * SUBLANES, LANES), F32),
                        pltpu.SemaphoreType.DMA((2,))],
        compiler_params=_cparams("arbitrary"),
        name="combine_ln2",
    )(dest_t, dest_t, ys, top_w, x1, mod, g2, b2)


def _moe(u2, x1, top_i, top_w, pos, counts, mod, experts, g2, b2, *, bm, tc, alpha, per_row, tiles_per_batch):
    n, d = x1.shape
    wgu, bgu, wdn, bdn = experts
    n_exp = wgu.shape[0]
    n_blocks = -(-n * TOP_K // bm) + n_exp
    counts = counts.astype(I32)
    padded = (counts + bm - 1) // bm * bm
    pad_end = jnp.cumsum(padded)
    pad_start = pad_end - padded
    dest = pad_start[top_i] + pos
    tok = jnp.broadcast_to(jnp.arange(n, dtype=I32)[:, None], (n, TOP_K))
    slot_tok = jnp.zeros((n_blocks * bm,), I32).at[dest.reshape(-1)].set(tok.reshape(-1))
    blk_start = jnp.arange(n_blocks, dtype=I32) * bm
    blk_e = jnp.minimum(jnp.sum((pad_end[None, :] <= blk_start[:, None]).astype(I32), axis=1), n_exp - 1)
    ys = _expert_ffn(u2, slot_tok.reshape(n_blocks, bm) * SUBLANES, blk_e, wgu, bgu, wdn, bdn, bm=bm)
    dest_t = dest.reshape(n // tc, tc, TOP_K).transpose(0, 2, 1).reshape(n // tc, TOP_K * tc) * SUBLANES
    return _combine(ys, dest_t, top_w, x1, mod, g2, b2, tc=tc, alpha=alpha, per_row=per_row, tiles_per_batch=tiles_per_batch)


def _pick_tile(n, target):
    t = min(n, target)
    while n % t:
        t -= 1
    return t


def kernel(x_prompt, x_sample, c_prompt, c_sample, cache_cmp_k, cache_cmp_v, cache_slc_k, cache_slc_v, state_win_k, state_win_v, state_conv, page_table, w_ada, b_ada, w_in, w_dw, b_dw, conv_ln_g, conv_ln_b, w_ck1, w_ck2, pe_k, w_cv1, w_cv2, pe_v, beta_conv, beta_attn, w_out, ln1_g, ln1_b, w_router, b_router, w_gate_up, b_gate_up, w_down, b_down, ln2_g, ln2_b):
    depth = w_ada.shape[0]
    assert depth == 1, "single-layer trunk"
    alpha = (2 * depth) ** 0.25
    bp, seq, d = x_prompt.shape
    bs, tq, _ = x_sample.shape
    c_conv = w_dw.shape[2]
    att_dim = N_HEADS * HEAD_DIM
    n_pool, page = cache_cmp_k.shape[1], cache_cmp_k.shape[2]
    assert state_conv.shape[2] == CONV_K - 1 and page % CMP_STRIDE == 0

    wi = w_in[0]
    cuts = [2 * c_conv, 2 * c_conv + att_dim, 2 * c_conv + att_dim + 6 * KV_DIM]
    wa = wi[:, :cuts[0]].astype(BF16)
    wq = wi[:, cuts[0]:cuts[1]].astype(BF16)
    wkv = wi[:, cuts[1]:cuts[2]].astype(BF16)
    wg = jnp.pad(wi[:, cuts[2]:], ((0, 0), (0, LANES - N_HEADS * N_GATES))).astype(BF16)
    in_w = (wa, wq, wkv, wg)
    wdw = jnp.pad(w_dw[0], ((0, CONV_HIST - CONV_K), (0, 0)))
    row = lambda a: a.reshape(1, -1)
    conv_vecs = (wdw, row(b_dw[0]), row(conv_ln_g[0]), row(conv_ln_b[0]))
    wk_c = _compress_weights(w_ck1[0], w_ck2[0], pe_k[0])
    wv_c = _compress_weights(w_cv1[0], w_cv2[0], pe_v[0])
    out_vecs = (row(beta_conv[0]), row(beta_attn[0]), row(ln1_g[0]), row(ln1_b[0]),
                jnp.pad(row(b_router[0]), ((0, 0), (0, LANES - N_EXPERTS)), constant_values=NEG_INF))
    out_w = (w_out[0][:c_conv].astype(BF16), w_out[0][c_conv:].astype(BF16),
             jnp.pad(w_router[0], ((0, 0), (0, LANES - N_EXPERTS))))
    experts = (w_gate_up[0].astype(BF16), b_gate_up[0][:, None, :], w_down[0].astype(BF16), b_down[0][:, None, :])
    g2, b2 = row(ln2_g[0]), row(ln2_b[0])

    mod = _ada(jnp.concatenate([c_prompt, c_sample], axis=0), w_ada[0], row(b_ada[0])).reshape(bp + bs, 6, d)
    mod_p = mod[:bp]
    mod_s = jnp.repeat(mod[bp:], tq, axis=0).transpose(1, 0, 2)

    tt = _pick_tile(seq, 512)
    hist0 = jnp.zeros((bp, CONV_HIST, c_conv), F32)
    (conv_p, tail_p, q_p, kc_p, vc_p, kct_p, vct_p, kst_p, vst_p, kwt_p, vwt_p, gates_p) = _in_proj(
        x_prompt, mod_p, in_w, (hist0,) + conv_vecs, tt=tt, per_row=False)
    n_chunk = seq // CMP_STRIDE
    ck_p, cv_p = _compress_prompt(kc_p.reshape(bp, n_chunk, CHUNK_ROW), vc_p.reshape(bp, n_chunk, CHUNK_ROW), wk_c, wv_c)
    att_p = _attn_prompt(q_p, gates_p, ck_p, cv_p, kst_p, vst_p, kwt_p, vwt_p)
    to = _pick_tile(seq, 256)
    cnt0 = jnp.zeros((1, LANES), F32)
    x1_p, u2_p, ti_p, tw_p, pos_p, cnt_p = _out_proj(x_prompt, conv_p, att_p, mod_p, out_vecs, out_w, cnt0,
                                                     tt=to, alpha=alpha, per_row=False)
    n_p = bp * seq
    bm_p = _pick_tile(n_p * TOP_K, 512)
    tc_p = _pick_tile(seq, 128)
    y_p = _moe(u2_p.reshape(n_p * SUBLANES, LANES), x1_p.reshape(n_p, d), ti_p.reshape(n_p, LANES)[:, :TOP_K], tw_p.reshape(n_p, LANES),
               pos_p.reshape(n_p, LANES)[:, :TOP_K], cnt_p[0, :N_EXPERTS], mod_p, experts, g2, b2,
               bm=bm_p, tc=tc_p, alpha=alpha, per_row=False, tiles_per_batch=seq // tc_p).reshape(bp, seq, d)

    n_s = bs * tq
    (glu_s, q_s, kc_s, vc_s, ks_s, vs_s, kw_s, vw_s, gates_s) = _in_proj(
        x_sample.reshape(1, n_s, d), mod_s, in_w, None, tt=n_s, per_row=True)
    unflat = lambda a: a.reshape(bs, tq, a.shape[-1])
    glu_s, q_s, kc_s, vc_s, ks_s, vs_s, kw_s, vw_s, gates_s = map(unflat, (glu_s, q_s, kc_s, vc_s, ks_s, vs_s, kw_s, vw_s, gates_s))
    xc_s = jnp.concatenate([jnp.zeros((bs, CONV_HIST - (CONV_K - 1), c_conv), F32), state_conv[0], glu_s], axis=1)
    conv_s = _conv_small(xc_s, *conv_vecs, tt=tq)
    dim_major = lambda a: a.transpose(0, 2, 3, 1).reshape(a.shape[0], KV_DIM, a.shape[1])
    ck_s, cv_s = _compress_decode(page_table, dim_major(cache_cmp_k[0]), dim_major(cache_cmp_v[0]), wk_c, wv_c)
    att_s = _attn_decode(page_table, q_s, gates_s, ck_s, cv_s, ks_s, vs_s, kw_s, vw_s, dim_major(state_win_k[0]),
                         dim_major(state_win_v[0]), dim_major(cache_slc_k[0]), dim_major(cache_slc_v[0]))
    flat = lambda a: a.reshape(1, n_s, a.shape[-1])
    x1_s, u2_s, ti_s, tw_s, pos_s, cnt_s = _out_proj(flat(x_sample), flat(conv_s), flat(att_s), mod_s, out_vecs, out_w, cnt0,
                                                     tt=n_s, alpha=alpha, per_row=True)
    bm_s = max(SUBLANES, min(512, n_s * TOP_K // N_EXPERTS))
    y_s = _moe(u2_s.reshape(n_s * SUBLANES, LANES), x1_s.reshape(n_s, d), ti_s.reshape(n_s, LANES)[:, :TOP_K], tw_s.reshape(n_s, LANES),
               pos_s.reshape(n_s, LANES)[:, :TOP_K], cnt_s[0, :N_EXPERTS], mod_s, experts, g2, b2,
               bm=bm_s, tc=n_s, alpha=alpha, per_row=True, tiles_per_batch=1).reshape(bs, tq, d)

    heads = lambda a: a.reshape(1, a.shape[0], a.shape[1], N_KV, HEAD_DIM)
    win_keep = min(WINDOW, seq)
    heads_t = lambda a: a.reshape(1, a.shape[0], N_KV, HEAD_DIM, a.shape[2]).transpose(0, 1, 4, 2, 3)
    new_p = (heads_t(kct_p), heads_t(vct_p), heads_t(kst_p), heads_t(vst_p), heads_t(kwt_p[:, :, seq - win_keep:]),
             heads_t(vwt_p[:, :, seq - win_keep:]), tail_p[None, :, CONV_HIST - (CONV_K - 1):])
    win_buf = state_win_k.shape[2]
    keep = lambda old, new: jnp.concatenate([old[0], new.reshape(bs, tq, N_KV, HEAD_DIM)], axis=1)[None, :, -win_buf:]
    new_s = (heads(kc_s), heads(vc_s), heads(ks_s), heads(vs_s), keep(state_win_k, kw_s), keep(state_win_v, vw_s),
             jnp.concatenate([state_conv[0], glu_s], axis=1)[None, :, -(CONV_K - 1):])
    return (y_p, y_s) + new_p + new_s
```

```python
import functools

import jax
import jax.numpy as jnp
from jax import lax
from jax.experimental import pallas as pl
from jax.experimental.pallas import tpu as pltpu

F32 = jnp.float32
BF16 = jnp.bfloat16
I32 = jnp.int32

N_HEADS = 8
HEAD_DIM = 64
N_KV = 2
Q_PER_KV = N_HEADS // N_KV
KV_DIM = N_KV * HEAD_DIM
CMP_LEN = 32
CMP_STRIDE = 16
CMP_HID = 2 * HEAD_DIM
SEL_LEN = 64
SEL_RATIO = SEL_LEN // CMP_STRIDE
N_SEL = 16
WINDOW = 512
Q_BLOCK = 128
N_GATES = 3
N_EXPERTS = 32
TOP_K = 4
CONV_K = 31
SWIGLU_LIMIT = 7.0
SWIGLU_ALPHA = 1.702
LN_EPS = 1e-5
NEG_INF = -1e30
FORCE = 1e9
SLOPES = tuple(2.0 ** (-8.0 * (h + 1) / N_HEADS) for h in range(N_HEADS))

LANES = 128
SUBLANES = 8
VMEM_LIMIT_BYTES = 56 * 1024 * 1024

CONV_HIST = 32
CHUNK_ROW = CMP_STRIDE * KV_DIM
SLC_CHUNK = 512
DEC_CHUNK = 2048
DEC_Q = 8
V_AUG = HEAD_DIM + 16
LOG2E = 1.4426950408889634
HIGHEST = lax.Precision.HIGHEST


def _cparams(*sem):
    return pltpu.CompilerParams(dimension_semantics=sem, vmem_limit_bytes=VMEM_LIMIT_BYTES)


def _sigmoid(x):
    return 1.0 / (1.0 + jnp.exp(-x))


def _bdot(a, b):
    return jnp.dot(a.astype(BF16), b.astype(BF16), preferred_element_type=F32)


def _bdot_nt(a, b):
    return lax.dot_general(a.astype(BF16), b.astype(BF16), (((1,), (1,)), ((), ())), preferred_element_type=F32)


def _layer_norm(x, g, b):
    mu = jnp.mean(x, -1, keepdims=True)
    xc = x - mu
    var = jnp.mean(xc * xc, -1, keepdims=True)
    return xc * lax.rsqrt(var + LN_EPS) * g + b


def _gelu_tanh(x):
    return 0.5 * x * (1.0 + jnp.tanh(0.7978845608028654 * (x + 0.044715 * (x * x * x))))


def _masked_softmax(s, mask):
    s = jnp.where(mask, s, NEG_INF)
    e = jnp.where(mask, jnp.exp(s - jnp.max(s, -1, keepdims=True)), 0.0)
    return e * (1.0 / jnp.maximum(jnp.sum(e, -1, keepdims=True), 1e-30))


def _ada_kernel(c_ref, w_ref, b_ref, o_ref):
    c = c_ref[...]
    o_ref[...] = _bdot(c * _sigmoid(c), w_ref[...]) + b_ref[...]


def _ada(c_all, w_ada, b_ada):
    nb, d = c_all.shape
    n = w_ada.shape[1]
    bn = d
    return pl.pallas_call(
        _ada_kernel,
        grid=(n // bn,),
        in_specs=[pl.BlockSpec((nb, d), lambda i: (0, 0)),
                  pl.BlockSpec((d, bn), lambda i: (0, i)),
                  pl.BlockSpec((1, bn), lambda i: (0, i))],
        out_specs=pl.BlockSpec((nb, bn), lambda i: (0, i)),
        out_shape=jax.ShapeDtypeStruct((nb, n), F32),
        compiler_params=_cparams("arbitrary"),
        name="ada",
    )(c_all, w_ada, b_ada)


def _mod_rows(mod_ref, i, per_row):
    return mod_ref[i] if per_row else mod_ref[0, i:i + 1, :]


def _conv_ln_silu(xc_ref, row0, n, wdw_ref, bdw, lng, lnb):
    acc = jnp.zeros((n, xc_ref.shape[1]), F32) + bdw
    off = CONV_HIST - (CONV_K - 1)
    for k in range(CONV_K):
        acc = acc + xc_ref[pl.ds(row0 + off + k, n), :] * wdw_ref[k:k + 1, :]
    y = _layer_norm(acc, lng, lnb)
    return y * _sigmoid(y)


def _in_kernel(*refs, tt, c_conv, fuse_conv, per_row):
    if fuse_conv:
        (x_ref, mod_ref, wa_ref, wq_ref, wkv_ref, wg_ref, hist_ref, wdw_ref, bdw_ref, lng_ref, lnb_ref,
         conv_ref, tail_ref, q_ref, kc_ref, vc_ref, *kvt_refs, gates_ref, xc_ref) = refs
    else:
        (x_ref, mod_ref, wa_ref, wq_ref, wkv_ref, wg_ref,
         glu_ref, q_ref, kc_ref, vc_ref, ks_ref, vs_ref, kw_ref, vw_ref, gates_ref) = refs
    x = x_ref[0]
    shift1 = _mod_rows(mod_ref, 0, per_row)
    scale1 = _mod_rows(mod_ref, 1, per_row)
    u = (x * (1.0 + scale1) + shift1).astype(BF16)
    a = jnp.dot(u, wa_ref[...], preferred_element_type=F32)
    glu = a[:, :c_conv] * _sigmoid(a[:, c_conv:])
    q_ref[0] = jnp.dot(u, wq_ref[...], preferred_element_type=F32)
    kv = jnp.dot(u, wkv_ref[...], preferred_element_type=F32)
    gates_ref[0] = _sigmoid(jnp.dot(u, wg_ref[...], preferred_element_type=F32))
    if not fuse_conv:
        for i, r in enumerate((kc_ref, vc_ref, ks_ref, vs_ref, kw_ref, vw_ref)):
            r[0] = kv[:, i * KV_DIM:(i + 1) * KV_DIM]
        glu_ref[0] = glu
        return
    kc_ref[0] = kv[:, 0:KV_DIM]
    vc_ref[0] = kv[:, KV_DIM:2 * KV_DIM]
    for i, r in enumerate(kvt_refs):
        r[0] = jnp.transpose(kv[:, i * KV_DIM:(i + 1) * KV_DIM])

    @pl.when(pl.program_id(1) == 0)
    def _():
        xc_ref[0:CONV_HIST, :] = hist_ref[0]

    xc_ref[CONV_HIST:CONV_HIST + tt, :] = glu
    rows = min(tt, 64)
    for r0 in range(0, tt, rows):
        conv_ref[0, r0:r0 + rows, :] = _conv_ln_silu(xc_ref, r0, rows, wdw_ref, bdw_ref[...], lng_ref[...], lnb_ref[...])
    tail = xc_ref[tt:tt + CONV_HIST, :]
    xc_ref[0:CONV_HIST, :] = tail
    tail_ref[0] = tail


def _in_proj(x, mod, wts, conv=None, *, tt, per_row):
    b, t, d = x.shape
    wa, wq, wkv, wg = wts
    c_conv = wa.shape[1] // 2
    fuse = conv is not None
    nt = t // tt
    const = lambda shape: pl.BlockSpec(shape, lambda i, j: (0,) * len(shape))
    mod_spec = (pl.BlockSpec((6, tt, d), lambda i, j: (0, j, 0)) if per_row
                else pl.BlockSpec((1, 6, d), lambda i, j: (i, 0, 0)))
    in_specs = [pl.BlockSpec((1, tt, d), lambda i, j: (i, j, 0)), mod_spec,
                const(wa.shape), const(wq.shape), const(wkv.shape), const(wg.shape)]
    args = [x, mod, wa, wq, wkv, wg]
    tok = lambda n: pl.BlockSpec((1, tt, n), lambda i, j: (i, j, 0))
    tok_shape = lambda n: jax.ShapeDtypeStruct((b, t, n), F32)
    out_specs, out_shape, scratch = [], [], []
    if fuse:
        hist, wdw, bdw, lng, lnb = conv
        in_specs += [pl.BlockSpec((1, CONV_HIST, c_conv), lambda i, j: (i, 0, 0)),
                     const(wdw.shape), const(bdw.shape), const(lng.shape), const(lnb.shape)]
        args += [hist, wdw, bdw, lng, lnb]
        out_specs += [tok(c_conv), pl.BlockSpec((1, CONV_HIST, c_conv), lambda i, j: (i, 0, 0))]
        out_shape += [tok_shape(c_conv), jax.ShapeDtypeStruct((b, CONV_HIST, c_conv), F32)]
        scratch = [pltpu.VMEM((CONV_HIST + tt, c_conv), F32)]
    else:
        out_specs += [tok(c_conv)]
        out_shape += [tok_shape(c_conv)]
    if fuse:
        kv_specs = [tok(KV_DIM)] * 2 + [pl.BlockSpec((1, KV_DIM, tt), lambda i, j: (i, 0, j))] * 6
        kv_shape = [tok_shape(KV_DIM)] * 2 + [jax.ShapeDtypeStruct((b, KV_DIM, t), F32)] * 6
    else:
        kv_specs, kv_shape = [tok(KV_DIM)] * 6, [tok_shape(KV_DIM)] * 6
    out_specs += [tok(wq.shape[1])] + kv_specs + [tok(wg.shape[1])]
    out_shape += [tok_shape(wq.shape[1])] + kv_shape + [tok_shape(wg.shape[1])]
    return pl.pallas_call(
        functools.partial(_in_kernel, tt=tt, c_conv=c_conv, fuse_conv=fuse, per_row=per_row),
        grid=(b, nt), in_specs=in_specs, out_specs=out_specs, out_shape=out_shape, scratch_shapes=scratch,
        compiler_params=_cparams("arbitrary", "arbitrary"),
        name="in_proj_conv" if fuse else "in_proj",
    )(*args)


def _conv_small_kernel(xc_ref, wdw_ref, bdw_ref, lng_ref, lnb_ref, o_ref, *, tt):
    o_ref[0] = _conv_ln_silu(xc_ref.at[0], 0, tt, wdw_ref, bdw_ref[...], lng_ref[...], lnb_ref[...])


def _conv_small(xc, wdw, bdw, lng, lnb, *, tt):
    b, rows, c = xc.shape
    const = lambda shape: pl.BlockSpec(shape, lambda i: (0,) * len(shape))
    return pl.pallas_call(
        functools.partial(_conv_small_kernel, tt=tt),
        grid=(b,),
        in_specs=[pl.BlockSpec((1, rows, c), lambda i: (i, 0, 0)),
                  const(wdw.shape), const(bdw.shape), const(lng.shape), const(lnb.shape)],
        out_specs=pl.BlockSpec((1, tt, c), lambda i: (i, 0, 0)),
        out_shape=jax.ShapeDtypeStruct((b, tt, c), F32),
        compiler_params=_cparams("arbitrary"),
        name="conv_small",
    )(xc, wdw, bdw, lng, lnb)


def _compress_weights(w1, w2, pe):
    half = CMP_STRIDE * HEAD_DIM

    def expand(w):
        w = w.reshape(CMP_STRIDE, HEAD_DIM, CMP_HID)
        z = jnp.zeros_like(w)
        g0 = jnp.concatenate([w, z], axis=1).reshape(CHUNK_ROW, CMP_HID)
        g1 = jnp.concatenate([z, w], axis=1).reshape(CHUNK_ROW, CMP_HID)
        return jnp.concatenate([g0, g1], axis=1)

    w1p = jnp.concatenate([expand(w1[:half]), expand(w1[half:])], axis=1).astype(BF16)
    z2 = jnp.zeros_like(w2)
    w2p = jnp.concatenate([jnp.concatenate([w2, z2], axis=1), jnp.concatenate([z2, w2], axis=1)], axis=0).astype(BF16)
    return w1p, w1.astype(BF16), pe.reshape(1, -1), w2p


def _compress_tail(r, w1_ref, pe_ref, w2p_ref):
    n = r.shape[0]
    pe8 = jnp.broadcast_to(pe_ref[...], (SUBLANES, pe_ref.shape[1]))
    c = _bdot(pe8, w1_ref[...])[0:1]
    c2 = jnp.concatenate([c, c], axis=1)
    h = r[:, :2 * CMP_HID] + pltpu.roll(r[:, 2 * CMP_HID:], n - 1, 0) + c2
    return _bdot(_gelu_tanh(h), w2p_ref[...])


def _compress_prompt_kernel(ak_ref, av_ref, w1pk, w1k, pek, w2pk, w1pv, w1v, pev, w2pv, ck_ref, cv_ref):
    ck_ref[0] = _compress_tail(_bdot(ak_ref[0], w1pk[...]), w1k, pek, w2pk)
    cv_ref[0] = _compress_tail(_bdot(av_ref[0], w1pv[...]), w1v, pev, w2pv)


def _compress_prompt(ak, av, wk, wv):
    b, n, _ = ak.shape
    const = lambda a: pl.BlockSpec(a.shape, lambda i: (0,) * a.ndim)
    row = pl.BlockSpec((1, n, CHUNK_ROW), lambda i: (i, 0, 0))
    out = pl.BlockSpec((1, n, KV_DIM), lambda i: (i, 0, 0))
    return pl.pallas_call(
        _compress_prompt_kernel,
        grid=(b,),
        in_specs=[row, row] + [const(a) for a in wk] + [const(a) for a in wv],
        out_specs=[out, out],
        out_shape=[jax.ShapeDtypeStruct((b, n, KV_DIM), F32)] * 2,
        compiler_params=_cparams("arbitrary"),
        name="compress_prompt",
    )(ak, av, *wk, *wv)


def _compress_decode_kernel(pt_ref, pk_ref, pv_ref, w1pk, w1k, pek, w2pk, w1pv, w1v, pev, w2pv, ck_ref, cv_ref,
                            slab_k, slab_v, rows_buf, rbuf, sem, *, n_pages, page):
    j = pl.program_id(0)
    n_jobs = pl.num_programs(0)
    past = n_pages * page
    n = past // CMP_STRIDE
    step = min(n, 256)
    tcols = min(past, 2048)

    def fetch(job, pool_ref, slab, s, wait):
        def body(pg, carry):
            pid = pt_ref[(job // 2) * n_pages + pg]
            cp = pltpu.make_async_copy(pool_ref.at[pid], slab.at[:, pl.ds(pl.multiple_of(pg * page, page), page)], sem.at[s])
            cp.wait() if wait else cp.start()
            return carry
        lax.fori_loop(0, n_pages, body, 0)

    def project(slab, w1p):
        def to_rows(i, carry):
            at = pl.ds(pl.multiple_of(i * tcols, tcols), tcols)
            rows_buf[at, :] = jnp.transpose(slab[:, at])
            return carry
        lax.fori_loop(0, past // tcols, to_rows, 0)

        def body(i, carry):
            base = pl.multiple_of(i * (step * CMP_STRIDE), step * CMP_STRIDE)
            a = jnp.concatenate([rows_buf[pl.ds(base + p, step, stride=CMP_STRIDE), :] for p in range(CMP_STRIDE)], axis=1)
            rbuf[pl.ds(pl.multiple_of(i * step, step), step), :] = _bdot(a, w1p[...])
            return carry
        lax.fori_loop(0, n // step, body, 0)
        return rbuf[...]

    @pl.when(j == 0)
    def _():
        fetch(j, pk_ref, slab_k, 0, False)

    @pl.when(j % 2 == 0)
    def _():
        fetch(j + 1, pv_ref, slab_v, 1, False)
        fetch(j, pk_ref, slab_k, 0, True)
        ck_ref[0] = _compress_tail(project(slab_k, w1pk), w1k, pek, w2pk)

    @pl.when(j % 2 == 1)
    def _():
        @pl.when(j + 1 < n_jobs)
        def _():
            fetch(j + 1, pk_ref, slab_k, 0, False)
        fetch(j, pv_ref, slab_v, 1, True)
        cv_ref[0] = _compress_tail(project(slab_v, w1pv), w1v, pev, w2pv)


def _compress_decode(page_table, pool_k, pool_v, wk, wv):
    nb, n_pages = page_table.shape
    page = pool_k.shape[2]
    past = n_pages * page
    n = past // CMP_STRIDE
    assert page % LANES == 0
    const = lambda a: pl.BlockSpec(a.shape, lambda i, pt: (0,) * a.ndim)
    any_spec = pl.BlockSpec(memory_space=pl.ANY)
    out = pl.BlockSpec((1, n, KV_DIM), lambda i, pt: (i // 2, 0, 0))
    grid_spec = pltpu.PrefetchScalarGridSpec(
        num_scalar_prefetch=1, grid=(2 * nb,),
        in_specs=[any_spec, any_spec] + [const(a) for a in wk] + [const(a) for a in wv],
        out_specs=[out, out],
        scratch_shapes=[pltpu.VMEM((KV_DIM, past), F32), pltpu.VMEM((KV_DIM, past), F32), pltpu.VMEM((past, KV_DIM), F32),
                        pltpu.VMEM((n, 4 * CMP_HID), F32), pltpu.SemaphoreType.DMA((2,))])
    return pl.pallas_call(
        functools.partial(_compress_decode_kernel, n_pages=n_pages, page=page),
        grid_spec=grid_spec,
        out_shape=[jax.ShapeDtypeStruct((nb, n, KV_DIM), F32)] * 2,
        compiler_params=_cparams("arbitrary"),
        name="compress_decode",
    )(page_table.reshape(-1), pool_k, pool_v, *wk, *wv)


def _importance_matrix(n_cmp_rows, n_cmp, n_blk, cols):
    c = jnp.arange(n_cmp_rows)[:, None]
    b = jnp.arange(cols)[None, :]
    d = c - (SEL_RATIO * b - 1)
    return ((d >= 0) & (d <= SEL_RATIO) & (c < n_cmp) & (b < n_blk)).astype(F32)


def _expand_matrix(n_chunks, chunk, blk0):
    key = jnp.arange(n_chunks)[:, None, None] * chunk + jnp.arange(chunk)[None, None, :]
    blk = jnp.asarray(blk0)[:, None, None] + jnp.arange(LANES)[None, :, None]
    return (key // SEL_LEN == blk).astype(BF16)


def _aug_values(vt):
    r = lax.broadcasted_iota(I32, (V_AUG - HEAD_DIM, vt.shape[1]), 0)
    return jnp.concatenate([vt, jnp.where(r == 0, 1.0, 0.0)], axis=0).astype(BF16)


def _flash_step(qh, kt, vta, bias, m, acc):
    s = jnp.dot(qh, kt, preferred_element_type=F32) + bias
    m_new = jnp.maximum(m, jnp.max(s, -1, keepdims=True))
    p = jnp.exp2(s - m_new).astype(BF16)
    pv = lax.dot_general(p, vta, (((1,), (1,)), ((), ())), preferred_element_type=F32)
    return m_new, jnp.exp2(m - m_new) * acc + pv


def _flash_init(rows):
    return jnp.full((rows, 1), NEG_INF, F32), jnp.zeros((rows, V_AUG), F32)


def _flash_out(acc):
    return acc[:, :HEAD_DIM] * (1.0 / jnp.maximum(acc[:, HEAD_DIM:HEAD_DIM + 1], 1e-30))


def _attn_prompt_kernel(q_ref, g_ref, ck_ref, cv_ref, kst_ref, vst_ref, kwt_ref, vwt_ref, mimp_ref, esel_ref, o_ref,
                        *, seq, n_blk, n_pick, win_len):
    qb = Q_BLOCK
    qs = pl.program_id(1) * qb
    t_col = qs + lax.broadcasted_iota(I32, (qb, 1), 0)
    gates = g_ref[0]
    scale = HEAD_DIM ** -0.5
    q_heads = [q_ref[0, :, h * HEAD_DIM:(h + 1) * HEAD_DIM] for h in range(N_HEADS)]
    nc = ck_ref.shape[1]
    cmp_end = lax.broadcasted_iota(I32, (1, nc), 1) * CMP_STRIDE + (CMP_LEN - 1)
    dist_c = t_col - cmp_end
    mask_c = dist_c >= 0
    dist_cf = dist_c.astype(F32)
    blk_pad = mimp_ref.shape[1]

    blk_i = lax.broadcasted_iota(I32, (n_blk, qb), 0)
    t_row = qs + lax.broadcasted_iota(I32, (n_blk, qb), 1)
    cur = t_row // SEL_LEN
    forced = (blk_i == 0) | (blk_i == cur) | (blk_i == cur - 1)
    future = blk_i * SEL_LEN > t_row

    win_start = pl.multiple_of(jnp.clip(qs - WINDOW, 0, seq - win_len), qb)
    dist_w = t_col - (win_start + lax.broadcasted_iota(I32, (1, win_len), 1))
    maskbias_w = jnp.where((dist_w >= 0) & (dist_w <= WINDOW), 0.0, NEG_INF)
    n_chunks = (qs + qb + SLC_CHUNK - 1) // SLC_CHUNK
    rows = Q_PER_KV * qb
    stack = lambda parts: jnp.concatenate(parts, axis=0)
    heads_of = lambda a: a.reshape(Q_PER_KV, qb, a.shape[1])

    def flash(qk, alibi, maskbias, off, vta, m, acc):
        s = (heads_of(qk + alibi) + maskbias[None]).reshape(rows, qk.shape[1])
        m_new = jnp.maximum(m, jnp.max(s, -1, keepdims=True) + off)
        p = jnp.exp2(s - (m_new - off)).astype(BF16)
        pv = lax.dot_general(p, vta, (((1,), (1,)), ((), ())), preferred_element_type=F32)
        return m_new, jnp.exp2(m - m_new) * acc + pv

    lanes_of = lambda g: slice(g * HEAD_DIM, (g + 1) * HEAD_DIM)
    heads_in = lambda g: range(g * Q_PER_KV, (g + 1) * Q_PER_KV)
    qg2s, slope2s, sels, o_cmps = [], [], [], []
    for g in range(N_KV):
        lanes, heads = lanes_of(g), heads_in(g)
        slope = stack([jnp.full((qb, 1), SLOPES[h], F32) for h in heads])
        qg = stack([(q_heads[h] * scale).astype(BF16) for h in heads])
        qg2s.append(stack([(q_heads[h] * (scale * LOG2E)).astype(BF16) for h in heads]))
        slope2s.append(slope * LOG2E)
        s_c = heads_of(_bdot_nt(qg, ck_ref[0, :, lanes]) - slope * stack([dist_cf] * Q_PER_KV))
        p_c = _masked_softmax(s_c, mask_c[None])
        p_sum = jnp.sum(p_c, axis=0)
        o_cmp = heads_of(_bdot(p_c.reshape(rows, nc), cv_ref[0, :, lanes]))
        imp = jnp.dot(p_sum, mimp_ref[...], precision=HIGHEST, preferred_element_type=F32)
        imp_t = jnp.transpose(imp)[0:n_blk, :]
        imp_t = jnp.where(future, -FORCE, jnp.where(forced, FORCE, imp_t))
        cnt = jnp.zeros((n_blk, qb), F32)
        for b2 in range(n_blk):
            row = imp_t[b2:b2 + 1, :]
            ahead = (row > imp_t) | ((row == imp_t) & (blk_i > b2))
            cnt = cnt + jnp.where(ahead, 1.0, 0.0)
        sel_t = jnp.where(cnt < n_pick, 1.0, 0.0)
        if n_blk < blk_pad:
            sel_t = jnp.concatenate([sel_t, jnp.zeros((blk_pad - n_blk, qb), F32)], axis=0)
        sels.append(jnp.transpose(sel_t).astype(BF16))
        o_cmps.append(o_cmp)

    key_slc = lax.broadcasted_iota(I32, (1, SLC_CHUNK), 1)
    key_win = lax.broadcasted_iota(I32, (1, win_len), 1).astype(F32)
    alibi_slc = [s2 * key_slc.astype(F32) for s2 in slope2s]

    def slc_step(c, carry):
        k0 = pl.multiple_of(c * SLC_CHUNK, SLC_CHUNK)
        causal = t_col - (k0 + key_slc) >= 0
        out = []
        for g in range(N_KV):
            kt = kst_ref[0, lanes_of(g), pl.ds(k0, SLC_CHUNK)].astype(BF16)
            vta = _aug_values(vst_ref[0, lanes_of(g), pl.ds(k0, SLC_CHUNK)])
            msel = jnp.dot(sels[g], esel_ref[c], preferred_element_type=F32)
            maskbias = jnp.where(causal & (msel > 0.5), 0.0, NEG_INF)
            qk = jnp.dot(qg2s[g], kt, preferred_element_type=F32)
            out += flash(qk, alibi_slc[g], maskbias, slope2s[g] * k0.astype(F32), vta, *carry[2 * g:2 * g + 2])
        return tuple(out)

    fin = lax.fori_loop(0, n_chunks, slc_step, _flash_init(rows) * N_KV)
    for g in range(N_KV):
        o_slc = heads_of(_flash_out(fin[2 * g + 1]))
        kwt = kwt_ref[0, lanes_of(g), pl.ds(win_start, win_len)].astype(BF16)
        vwta = _aug_values(vwt_ref[0, lanes_of(g), pl.ds(win_start, win_len)])
        qk = jnp.dot(qg2s[g], kwt, preferred_element_type=F32)
        o_win = heads_of(_flash_out(flash(qk, slope2s[g] * key_win, maskbias_w, 0.0, vwta, *_flash_init(rows))[1]))
        for i, h in enumerate(heads_in(g)):
            c0 = h * N_GATES
            o_ref[0, :, h * HEAD_DIM:(h + 1) * HEAD_DIM] = (gates[:, c0:c0 + 1] * o_cmps[g][i] + gates[:, c0 + 1:c0 + 2] * o_slc[i]
                                                           + gates[:, c0 + 2:c0 + 3] * o_win[i])


def _attn_prompt(q, gates, ck, cv, ks, vs, kw, vw):
    b, t, att = q.shape
    assert t % SLC_CHUNK == 0 and t % Q_BLOCK == 0
    n_blk = t // SEL_LEN
    n_cmp = t // CMP_STRIDE - 1
    assert n_blk <= LANES and ck.shape[1] == t // CMP_STRIDE
    n_pick = min(N_SEL, n_blk)
    win_len = min(WINDOW + Q_BLOCK, t)
    mimp = _importance_matrix(ck.shape[1], n_cmp, n_blk, LANES)
    n_ch = t // SLC_CHUNK
    esel = _expand_matrix(n_ch, SLC_CHUNK, [0] * n_ch)
    tok = lambda n: pl.BlockSpec((1, Q_BLOCK, n), lambda i, j: (i, j, 0))
    full = lambda a: pl.BlockSpec((1,) + a.shape[1:], lambda i, j: (i, 0, 0))
    const = lambda a: pl.BlockSpec(a.shape, lambda i, j: (0,) * a.ndim)
    return pl.pallas_call(
        functools.partial(_attn_prompt_kernel, seq=t, n_blk=n_blk, n_pick=n_pick, win_len=win_len),
        grid=(b, t // Q_BLOCK),
        in_specs=[tok(att), tok(gates.shape[2]), full(ck), full(cv), full(ks), full(vs), full(kw), full(vw),
                  const(mimp), const(esel)],
        out_specs=tok(att),
        out_shape=jax.ShapeDtypeStruct((b, t, att), F32),
        compiler_params=_cparams("arbitrary", "arbitrary"),
        name="attn_prompt",
    )(q, gates, ck, cv, ks, vs, kw, vw, mimp, esel)


def _attn_decode_kernel(pt_ref, q_ref, g_ref, ck_ref, cv_ref, ksn_ref, vsn_ref, kwn_ref, vwn_ref, wk_ref, wv_ref,
                        pk_ref, pv_ref, mimp_ref, esel_ref, o_ref, slab_k, slab_v, sem,
                        *, n_pages, page, past, n_blk, n_pick, tq):
    b = pl.program_id(0)
    slot = b % 2

    def fetch(bb, s, wait):
        def body(pg, carry):
            pid = pt_ref[bb * n_pages + pg]
            dst = pl.ds(pl.multiple_of(pg * page, page), page)
            for pool, slab, i in ((pk_ref, slab_k, 0), (pv_ref, slab_v, 1)):
                cp = pltpu.make_async_copy(pool.at[pid], slab.at[s, :, dst], sem.at[s, i])
                cp.wait() if wait else cp.start()
            return carry
        lax.fori_loop(0, n_pages, body, 0)

    @pl.when(b == 0)
    def _():
        fetch(b, slot, False)

    @pl.when(b + 1 < pl.num_programs(0))
    def _():
        fetch(b + 1, 1 - slot, False)

    rows = Q_PER_KV * DEC_Q
    qi = lax.broadcasted_iota(I32, (rows, 1), 0) % DEC_Q
    t_col = past + qi % tq
    hl = lax.broadcasted_iota(I32, (rows, 1), 0) // DEC_Q
    nc = ck_ref.shape[1]
    cmp_end = lax.broadcasted_iota(I32, (1, nc), 1) * CMP_STRIDE + (CMP_LEN - 1)
    dist_c = t_col - cmp_end
    mask_c = dist_c >= 0
    dist_cf = dist_c.astype(F32)
    blk_pad = mimp_ref.shape[1]
    t8 = past + lax.broadcasted_iota(I32, (DEC_Q, 1), 0) % tq
    blk_l = lax.broadcasted_iota(I32, (DEC_Q, blk_pad), 1)
    cur = t8 // SEL_LEN
    forced = (blk_l == 0) | (blk_l == cur) | (blk_l == cur - 1)
    future = blk_l * SEL_LEN > t8
    pad_blk = blk_l >= n_blk
    b_sub = lax.broadcasted_iota(I32, (blk_pad, blk_pad), 0)
    b_lane = lax.broadcasted_iota(I32, (blk_pad, blk_pad), 1)
    n_chunks = past // DEC_CHUNK
    new_rows = ksn_ref.shape[2]

    sels, slopes, o_cmps = [], [], []
    for g in range(N_KV):
        lanes = slice(g * HEAD_DIM, (g + 1) * HEAD_DIM)
        slope = jnp.zeros((rows, 1), F32)
        for i in range(Q_PER_KV):
            slope = jnp.where(hl == i, SLOPES[g * Q_PER_KV + i], slope)
        slopes.append(slope)
        qg = (q_ref[0, g] * (HEAD_DIM ** -0.5)).astype(BF16)
        p = _masked_softmax(_bdot_nt(qg, ck_ref[0, :, lanes]) - slope * dist_cf, mask_c)
        o_cmps.append(_bdot(p, cv_ref[0, :, lanes]))
        p_sum = p[0:DEC_Q]
        for i in range(1, Q_PER_KV):
            p_sum = p_sum + p[i * DEC_Q:(i + 1) * DEC_Q]
        imp = jnp.dot(p_sum, mimp_ref[...], precision=HIGHEST, preferred_element_type=F32)
        imp = jnp.where(pad_blk, -3e38, jnp.where(future, -FORCE, jnp.where(forced, FORCE, imp)))
        imp_t = jnp.transpose(jnp.concatenate([imp, jnp.zeros((LANES - DEC_Q, blk_pad), F32)], axis=0))
        sel_rows = []
        for r in range(DEC_Q):
            col = imp_t[:, r:r + 1]
            row = imp[r:r + 1, :]
            ahead = (col > row) | ((col == row) & (b_sub < b_lane))
            cnt = jnp.sum(jnp.where(ahead, 1.0, 0.0), axis=0, keepdims=True)
            sel_rows.append(jnp.where(cnt < n_pick, 1.0, 0.0))
        sel8 = jnp.concatenate(sel_rows, axis=0)
        sels.append(jnp.concatenate([sel8] * Q_PER_KV, axis=0))

    fetch(b, slot, True)

    for g in range(N_KV):
        lanes = slice(g * HEAD_DIM, (g + 1) * HEAD_DIM)
        slope2 = slopes[g] * LOG2E
        qg2 = (q_ref[0, g] * (HEAD_DIM ** -0.5 * LOG2E)).astype(BF16)
        sel = sels[g]
        sel_b = sel.astype(BF16)

        def bias_of(dist, visible):
            return jnp.where(visible, 0.0, NEG_INF) - slope2 * dist.astype(F32)

        m, acc = _flash_init(rows)
        per_mat = LANES * SEL_LEN // DEC_CHUNK
        for c in range(n_chunks):
            keys = slice(c * DEC_CHUNK, (c + 1) * DEC_CHUNK)
            lane0 = (c // per_mat) * LANES
            msel = jnp.dot(sel_b[:, lane0:lane0 + LANES], esel_ref[c % per_mat], preferred_element_type=F32)
            dist = t_col - (c * DEC_CHUNK + lax.broadcasted_iota(I32, (1, DEC_CHUNK), 1))
            m, acc = _flash_step(qg2, slab_k[slot, lanes, keys].astype(BF16), _aug_values(slab_v[slot, lanes, keys]),
                                 bias_of(dist, (dist >= 0) & (msel > 0.5)), m, acc)
        dist_n = t_col - (past + lax.broadcasted_iota(I32, (1, new_rows), 1))
        cur_blk = past // SEL_LEN
        m, acc = _flash_step(qg2, ksn_ref[0, lanes, :].astype(BF16), _aug_values(vsn_ref[0, lanes, :]),
                             bias_of(dist_n, (dist_n >= 0) & (sel[:, cur_blk:cur_blk + 1] > 0.5)), m, acc)
        o_slc = _flash_out(acc)
        wlen = wk_ref.shape[2]
        dist = t_col - (past - wlen + lax.broadcasted_iota(I32, (1, wlen), 1))
        m, acc = _flash_step(qg2, wk_ref[0, lanes, :].astype(BF16), _aug_values(wv_ref[0, lanes, :]),
                             bias_of(dist, (dist >= 0) & (dist <= WINDOW)), *_flash_init(rows))
        m, acc = _flash_step(qg2, kwn_ref[0, lanes, :].astype(BF16), _aug_values(vwn_ref[0, lanes, :]),
                             bias_of(dist_n, (dist_n >= 0) & (dist_n <= WINDOW)), m, acc)
        o_win = _flash_out(acc)
        gt = g_ref[0, g]
        o_ref[0, g] = gt[:, 0:1] * o_cmps[g] + gt[:, 1:2] * o_slc + gt[:, 2:3] * o_win


def _attn_decode(page_table, q, gates, ck, cv, ksn, vsn, kwn, vwn, win_k, win_v, pool_k, pool_v):
    nb, tq, att = q.shape
    n_pages = page_table.shape[1]
    page = pool_k.shape[2]
    past = n_pages * page
    assert tq <= DEC_Q and past % DEC_CHUNK == 0 and past % SEL_LEN == 0 and (past + tq - 1) // SEL_LEN == past // SEL_LEN
    assert win_k.shape[2] == WINDOW and ck.shape[1] * CMP_STRIDE == past and page % LANES == 0
    t_pad = -(-(past + tq) // SEL_LEN) * SEL_LEN
    n_blk = t_pad // SEL_LEN
    n_cmp = t_pad // CMP_STRIDE - 1
    n_pick = min(N_SEL, n_blk)
    blk_pad = -(-n_blk // LANES) * LANES
    mimp = _importance_matrix(ck.shape[1], n_cmp, n_blk, blk_pad)
    per_mat = LANES * SEL_LEN // DEC_CHUNK
    esel = _expand_matrix(per_mat, DEC_CHUNK, [0] * per_mat)

    def dec_rows(a, width):
        a = a.reshape(nb, tq, N_KV, Q_PER_KV, width)
        a = jnp.take(a, jnp.arange(DEC_Q) % tq, axis=1)
        return a.transpose(0, 2, 3, 1, 4).reshape(nb, N_KV, Q_PER_KV * DEC_Q, width)

    q_d = dec_rows(q, HEAD_DIM)
    g_d = dec_rows(gates[:, :, :N_HEADS * N_GATES], N_GATES)
    pad_new = lambda a: jnp.pad(a.transpose(0, 2, 1), ((0, 0), (0, 0), (0, LANES - tq)))
    ksn, vsn, kwn, vwn = [pad_new(a) for a in (ksn, vsn, kwn, vwn)]
    rows = Q_PER_KV * DEC_Q
    blk = lambda a: pl.BlockSpec((1,) + a.shape[1:], lambda i, pt: (i,) + (0,) * (a.ndim - 1))
    const = lambda a: pl.BlockSpec(a.shape, lambda i, pt: (0,) * a.ndim)
    any_spec = pl.BlockSpec(memory_space=pl.ANY)
    grid_spec = pltpu.PrefetchScalarGridSpec(
        num_scalar_prefetch=1, grid=(nb,),
        in_specs=[blk(q_d), blk(g_d), blk(ck), blk(cv), blk(ksn), blk(vsn), blk(kwn), blk(vwn), blk(win_k), blk(win_v),
                  any_spec, any_spec, const(mimp), const(esel)],
        out_specs=pl.BlockSpec((1, N_KV, rows, HEAD_DIM), lambda i, pt: (i, 0, 0, 0)),
        scratch_shapes=[pltpu.VMEM((2, KV_DIM, past), F32), pltpu.VMEM((2, KV_DIM, past), F32),
                        pltpu.SemaphoreType.DMA((2, 2))])
    o = pl.pallas_call(
        functools.partial(_attn_decode_kernel, n_pages=n_pages, page=page, past=past, n_blk=n_blk, n_pick=n_pick, tq=tq),
        grid_spec=grid_spec,
        out_shape=jax.ShapeDtypeStruct((nb, N_KV, rows, HEAD_DIM), F32),
        compiler_params=_cparams("arbitrary"),
        name="attn_decode",
    )(page_table.reshape(-1), q_d, g_d, ck, cv, ksn, vsn, kwn, vwn, win_k, win_v, pool_k, pool_v, mimp, esel)
    o = o.reshape(nb, N_KV, Q_PER_KV, DEC_Q, HEAD_DIM)[:, :, :, :tq]
    return o.transpose(0, 3, 1, 2, 4).reshape(nb, tq, att)


def _out_kernel(x_ref, conv_ref, att_ref, mod_ref, bc_ref, ba_ref, woc_ref, woa_ref, g1_ref, b1_ref, wr_ref, br_ref,
                cnt0_ref, x1_ref, u2_ref, idx_ref, w_ref, pos_ref, cnt_ref, carry_ref, *, tt, alpha, per_row):
    first = (pl.program_id(0) == 0) & (pl.program_id(1) == 0)

    @pl.when(first)
    def _():
        carry_ref[...] = cnt0_ref[...]

    gate1 = _mod_rows(mod_ref, 2, per_row)
    shift2 = _mod_rows(mod_ref, 3, per_row)
    scale2 = _mod_rows(mod_ref, 4, per_row)
    mix = _bdot(conv_ref[0] * bc_ref[...], woc_ref[...]) + _bdot(att_ref[0] * ba_ref[...], woa_ref[...])
    x1 = _layer_norm(alpha * x_ref[0] + gate1 * mix, g1_ref[...], b1_ref[...])
    x1_ref[0] = x1
    u2 = x1 * (1.0 + scale2) + shift2
    _store_tiled(u2_ref.at[0], 0, u2)
    logits = jnp.dot(u2, wr_ref[...], precision=HIGHEST, preferred_element_type=F32) + br_ref[...]
    lane = lax.broadcasted_iota(I32, logits.shape, 1)
    vals = logits
    top_v, top_i = [], []
    for _k in range(TOP_K):
        m = jnp.max(vals, axis=-1, keepdims=True)
        i = jnp.min(jnp.where(vals == m, lane, LANES), axis=-1, keepdims=True)
        top_v.append(m)
        top_i.append(i)
        vals = jnp.where(lane == i, -jnp.inf, vals)
    e = [jnp.exp(v - top_v[0]) for v in top_v]
    inv = 1.0 / (e[0] + e[1] + e[2] + e[3])
    onehot = [jnp.where(lane == i, 1.0, 0.0) for i in top_i]
    assigned = onehot[0] + onehot[1] + onehot[2] + onehot[3]
    r_i = lax.broadcasted_iota(I32, (tt, tt), 0)
    c_i = lax.broadcasted_iota(I32, (tt, tt), 1)
    before = jnp.where(r_i > c_i, 1.0, 0.0).astype(BF16)
    rank = carry_ref[...] + jnp.dot(before, assigned.astype(BF16), preferred_element_type=F32)
    idx_o = jnp.zeros(logits.shape, I32)
    w_o = jnp.zeros(logits.shape, F32)
    pos_o = jnp.zeros(logits.shape, F32)
    for k in range(TOP_K):
        idx_o = jnp.where(lane == k, top_i[k], idx_o)
        w_o = jnp.where(lane == k, e[k] * inv, w_o)
        pos_o = jnp.where(lane == k, jnp.sum(onehot[k] * rank, axis=-1, keepdims=True), pos_o)
    idx_ref[0] = idx_o
    w_ref[0] = w_o
    pos_ref[0] = pos_o.astype(I32)
    carry_ref[...] = carry_ref[...] + jnp.sum(assigned, axis=0, keepdims=True)
    cnt_ref[...] = carry_ref[...]


def _out_proj(x, conv_out, att, mod, vecs, wts, cnt0, *, tt, alpha, per_row):
    b, t, d = x.shape
    bc, ba, g1, b1, br = vecs
    woc, woa, wr = wts
    const = lambda a: pl.BlockSpec(a.shape, lambda i, j: (0,) * a.ndim)
    tok = lambda n: pl.BlockSpec((1, tt, n), lambda i, j: (i, j, 0))
    mod_spec = (pl.BlockSpec((6, tt, d), lambda i, j: (0, j, 0)) if per_row
                else pl.BlockSpec((1, 6, d), lambda i, j: (i, 0, 0)))
    tok_shape = lambda n, dt: jax.ShapeDtypeStruct((b, t, n), dt)
    return pl.pallas_call(
        functools.partial(_out_kernel, tt=tt, alpha=alpha, per_row=per_row),
        grid=(b, t // tt),
        in_specs=[tok(d), tok(conv_out.shape[2]), tok(att.shape[2]), mod_spec, const(bc), const(ba), const(woc), const(woa),
                  const(g1), const(b1), const(wr), const(br), const(cnt0)],
        out_specs=[tok(d), pl.BlockSpec((1, tt * SUBLANES, LANES), lambda i, j: (i, j, 0)), tok(LANES), tok(LANES), tok(LANES),
                   pl.BlockSpec((1, LANES), lambda i, j: (0, 0))],
        out_shape=[tok_shape(d, F32), jax.ShapeDtypeStruct((b, t * SUBLANES, LANES), F32), tok_shape(LANES, I32),
                   tok_shape(LANES, F32), tok_shape(LANES, I32),
                   jax.ShapeDtypeStruct((1, LANES), F32)],
        scratch_shapes=[pltpu.VMEM((1, LANES), F32)],
        compiler_params=_cparams("arbitrary", "arbitrary"),
        name="out_proj_router",
    )(x, conv_out, att, mod, bc, ba, woc, woa, g1, b1, wr, br, cnt0)


def _tile_rows(x):
    return [x[:, j * LANES:(j + 1) * LANES] for j in range(x.shape[1] // LANES)]


def _store_tiled(ref, at, x):
    rows = x.shape[0]
    for j, piece in enumerate(_tile_rows(x)):
        ref[pl.ds(at + j, rows, stride=SUBLANES), :] = piece


def _load_tiled(ref, at, rows):
    return jnp.concatenate([ref[pl.ds(at + j, rows, stride=SUBLANES), :] for j in range(SUBLANES)], axis=1)


def _gather_pipeline(i, n, src_ref, cur_ref, nxt_ref, bufs, sem, rows, compute):
    def row_copy(src, buf, s, r):
        at = r * SUBLANES if isinstance(r, int) else pl.multiple_of(r * SUBLANES, SUBLANES)
        return pltpu.make_async_copy(src_ref.at[pl.ds(pl.multiple_of(src, SUBLANES), SUBLANES)],
                                     buf.at[pl.ds(at, SUBLANES)], sem.at[s])

    def whole(buf, s):
        return pltpu.make_async_copy(src_ref.at[pl.ds(0, rows * SUBLANES)], buf, sem.at[s])

    @pl.when(i == 0)
    def _():
        def body(r, carry):
            row_copy(cur_ref[0, 0, r], bufs[0], 0, r).start()
            return carry
        lax.fori_loop(0, rows, body, 0)

    def step(s):
        whole(bufs[s], s).wait()
        for r in range(rows):
            row_copy(nxt_ref[0, 0, r], bufs[1 - s], 1 - s, r).start()
        compute(bufs[s])

        @pl.when(i == n - 1)
        def _():
            whole(bufs[1 - s], 1 - s).wait()

    for s in range(2):
        pl.when(i % 2 == s)(functools.partial(step, s))


def _expert_kernel(be_ref, cur_ref, nxt_ref, u_ref, wgu_ref, bgu_ref, wdn_ref, bdn_ref, y_ref, xb0, xb1, sem, *, bm, d_ff):
    def compute(buf):
        gu = jnp.dot(_load_tiled(buf, 0, bm).astype(BF16), wgu_ref[0], preferred_element_type=F32) + bgu_ref[0]
        gl = jnp.minimum(gu[:, :d_ff], SWIGLU_LIMIT)
        lin = jnp.clip(gu[:, d_ff:], -SWIGLU_LIMIT, SWIGLU_LIMIT)
        act = gl * _sigmoid(SWIGLU_ALPHA * gl) * (lin + 1.0)
        _store_tiled(y_ref, 0, jnp.dot(act.astype(BF16), wdn_ref[0], preferred_element_type=F32) + bdn_ref[0])

    _gather_pipeline(pl.program_id(0), pl.num_programs(0), u_ref, cur_ref, nxt_ref, (xb0, xb1), sem, bm, compute)


def _expert_ffn(u2, slot_tok, blk_e, wgu, bgu, wdn, bdn, *, bm):
    n_blocks = slot_tok.shape[0]
    d = wgu.shape[1]
    assert d == SUBLANES * LANES
    d_ff = wdn.shape[1]
    slot_tok = slot_tok.reshape(n_blocks, 1, bm)
    smem = lambda f: pl.BlockSpec((1, 1, bm), f, memory_space=pltpu.SMEM)
    grid_spec = pltpu.PrefetchScalarGridSpec(
        num_scalar_prefetch=1, grid=(n_blocks,),
        in_specs=[smem(lambda i, be: (i, 0, 0)),
                  smem(lambda i, be: (jnp.minimum(i + 1, n_blocks - 1), 0, 0)),
                  pl.BlockSpec(memory_space=pl.ANY),
                  pl.BlockSpec((1, d, 2 * d_ff), lambda i, be: (be[i], 0, 0)),
                  pl.BlockSpec((1, 1, 2 * d_ff), lambda i, be: (be[i], 0, 0)),
                  pl.BlockSpec((1, d_ff, d), lambda i, be: (be[i], 0, 0)),
                  pl.BlockSpec((1, 1, d), lambda i, be: (be[i], 0, 0))],
        out_specs=pl.BlockSpec((bm * SUBLANES, LANES), lambda i, be: (i, 0)),
        scratch_shapes=[pltpu.VMEM((bm * SUBLANES, LANES), F32), pltpu.VMEM((bm * SUBLANES, LANES), F32),
                        pltpu.SemaphoreType.DMA((2,))])
    return pl.pallas_call(
        functools.partial(_expert_kernel, bm=bm, d_ff=d_ff),
        grid_spec=grid_spec,
        out_shape=jax.ShapeDtypeStruct((n_blocks * bm * SUBLANES, LANES), F32),
        compiler_params=_cparams("arbitrary"),
        name="expert_ffn",
    )(blk_e, slot_tok, slot_tok, u2, wgu, bgu, wdn, bdn)


def _combine_kernel(cur_ref, nxt_ref, ys_ref, w_ref, x1_ref, mod_ref, g2_ref, b2_ref, y_ref, gb0, gb1, sem,
                    *, tc, alpha, per_row):
    def compute(buf):
        w = w_ref[...]
        f = w[:, 0:1] * _load_tiled(buf, 0, tc)
        for k in range(1, TOP_K):
            f = f + w[:, k:k + 1] * _load_tiled(buf, k * tc * SUBLANES, tc)
        gate2 = mod_ref[5] if per_row else mod_ref[0, 5:6, :]
        y_ref[...] = _layer_norm(alpha * x1_ref[...] + gate2 * f, g2_ref[...], b2_ref[...])

    _gather_pipeline(pl.program_id(0), pl.num_programs(0), ys_ref, cur_ref, nxt_ref, (gb0, gb1), sem, TOP_K * tc, compute)


def _combine(ys, dest_t, top_w, x1, mod, g2, b2, *, tc, alpha, per_row, tiles_per_batch):
    n, d = x1.shape
    n_tiles = n // tc
    dest_t = dest_t.reshape(n_tiles, 1, TOP_K * tc)
    smem = lambda f: pl.BlockSpec((1, 1, TOP_K * tc), f, memory_space=pltpu.SMEM)
    const = lambda a: pl.BlockSpec(a.shape, lambda i: (0,) * a.ndim)
    mod_spec = (pl.BlockSpec((6, tc, d), lambda i: (0, i, 0)) if per_row
                else pl.BlockSpec((1, 6, d), lambda i: (i // tiles_per_batch, 0, 0)))
    return pl.pallas_call(
        functools.partial(_combine_kernel, tc=tc, alpha=alpha, per_row=per_row),
        grid=(n_tiles,),
        in_specs=[smem(lambda i: (i, 0, 0)), smem(lambda i: (jnp.minimum(i + 1, n_tiles - 1), 0, 0)),
                  pl.BlockSpec(memory_space=pl.ANY),
                  pl.BlockSpec((tc, LANES), lambda i: (i, 0)), pl.BlockSpec((tc, d), lambda i: (i, 0)),
                  mod_spec, const(g2), const(b2)],
        out_specs=pl.BlockSpec((tc, d), lambda i: (i, 0)),
        out_shape=jax.ShapeDtypeStruct((n, d), F32),
        scratch_shapes=[pltpu.VMEM((TOP_K * tc * SUBLANES, LANES), F32), pltpu.VMEM((TOP_K * tc * SUBLANES, LANES), F32),
                        pltpu.SemaphoreType.DMA((2,))],
        compiler_params=_cparams("arbitrary"),
        name="combine_ln2",
    )(dest_t, dest_t, ys, top_w, x1, mod, g2, b2)


def _moe(u2, x1, top_i, top_w, pos, counts, mod, experts, g2, b2, *, bm, tc, alpha, per_row, tiles_per_batch):
    n, d = x1.shape
    wgu, bgu, wdn, bdn = experts
    n_exp = wgu.shape[0]
    n_blocks = -(-n * TOP_K // bm) + n_exp
    counts = counts.astype(I32)
    padded = (counts + bm - 1) // bm * bm
    pad_end = jnp.cumsum(padded)
    pad_start = pad_end - padded
    dest = pad_start[top_i] + pos
    tok = jnp.broadcast_to(jnp.arange(n, dtype=I32)[:, None], (n, TOP_K))
    slot_tok = jnp.zeros((n_blocks * bm,), I32).at[dest.reshape(-1)].set(tok.reshape(-1), unique_indices=True,
                                                                        mode="promise_in_bounds")
    blk_start = jnp.arange(n_blocks, dtype=I32) * bm
    blk_e = jnp.minimum(jnp.sum((pad_end[None, :] <= blk_start[:, None]).astype(I32), axis=1), n_exp - 1)
    ys = _expert_ffn(u2, slot_tok.reshape(n_blocks, bm) * SUBLANES, blk_e, wgu, bgu, wdn, bdn, bm=bm)
    dest_t = dest.reshape(n // tc, tc, TOP_K).transpose(0, 2, 1).reshape(n // tc, TOP_K * tc) * SUBLANES
    return _combine(ys, dest_t, top_w, x1, mod, g2, b2, tc=tc, alpha=alpha, per_row=per_row, tiles_per_batch=tiles_per_batch)


def _pick_tile(n, target):
    t = min(n, target)
    while n % t:
        t -= 1
    return t


def kernel(x_prompt, x_sample, c_prompt, c_sample, cache_cmp_k, cache_cmp_v, cache_slc_k, cache_slc_v, state_win_k, state_win_v, state_conv, page_table, w_ada, b_ada, w_in, w_dw, b_dw, conv_ln_g, conv_ln_b, w_ck1, w_ck2, pe_k, w_cv1, w_cv2, pe_v, beta_conv, beta_attn, w_out, ln1_g, ln1_b, w_router, b_router, w_gate_up, b_gate_up, w_down, b_down, ln2_g, ln2_b):
    depth = w_ada.shape[0]
    assert depth == 1, "single-layer trunk"
    alpha = (2 * depth) ** 0.25
    bp, seq, d = x_prompt.shape
    bs, tq, _ = x_sample.shape
    c_conv = w_dw.shape[2]
    att_dim = N_HEADS * HEAD_DIM
    n_pool, page = cache_cmp_k.shape[1], cache_cmp_k.shape[2]
    assert state_conv.shape[2] == CONV_K - 1 and page % CMP_STRIDE == 0

    wi = w_in[0]
    cuts = [2 * c_conv, 2 * c_conv + att_dim, 2 * c_conv + att_dim + 6 * KV_DIM]
    wa = wi[:, :cuts[0]].astype(BF16)
    wq = wi[:, cuts[0]:cuts[1]].astype(BF16)
    wkv = wi[:, cuts[1]:cuts[2]].astype(BF16)
    wg = jnp.pad(wi[:, cuts[2]:], ((0, 0), (0, LANES - N_HEADS * N_GATES))).astype(BF16)
    in_w = (wa, wq, wkv, wg)
    wdw = jnp.pad(w_dw[0], ((0, CONV_HIST - CONV_K), (0, 0)))
    row = lambda a: a.reshape(1, -1)
    conv_vecs = (wdw, row(b_dw[0]), row(conv_ln_g[0]), row(conv_ln_b[0]))
    wk_c = _compress_weights(w_ck1[0], w_ck2[0], pe_k[0])
    wv_c = _compress_weights(w_cv1[0], w_cv2[0], pe_v[0])
    out_vecs = (row(beta_conv[0]), row(beta_attn[0]), row(ln1_g[0]), row(ln1_b[0]),
                jnp.pad(row(b_router[0]), ((0, 0), (0, LANES - N_EXPERTS)), constant_values=NEG_INF))
    out_w = (w_out[0][:c_conv].astype(BF16), w_out[0][c_conv:].astype(BF16),
             jnp.pad(w_router[0], ((0, 0), (0, LANES - N_EXPERTS))))
    experts = (w_gate_up[0].astype(BF16), b_gate_up[0][:, None, :], w_down[0].astype(BF16), b_down[0][:, None, :])
    g2, b2 = row(ln2_g[0]), row(ln2_b[0])

    mod = _ada(jnp.concatenate([c_prompt, c_sample], axis=0), w_ada[0], row(b_ada[0])).reshape(bp + bs, 6, d)
    mod_p = mod[:bp]
    mod_s = jnp.repeat(mod[bp:], tq, axis=0).transpose(1, 0, 2)

    tt = _pick_tile(seq, 512)
    hist0 = jnp.zeros((bp, CONV_HIST, c_conv), F32)
    (conv_p, tail_p, q_p, kc_p, vc_p, kct_p, vct_p, kst_p, vst_p, kwt_p, vwt_p, gates_p) = _in_proj(
        x_prompt, mod_p, in_w, (hist0,) + conv_vecs, tt=tt, per_row=False)
    n_chunk = seq // CMP_STRIDE
    ck_p, cv_p = _compress_prompt(kc_p.reshape(bp, n_chunk, CHUNK_ROW), vc_p.reshape(bp, n_chunk, CHUNK_ROW), wk_c, wv_c)
    att_p = _attn_prompt(q_p, gates_p, ck_p, cv_p, kst_p, vst_p, kwt_p, vwt_p)
    to = _pick_tile(seq, 256)
    cnt0 = jnp.zeros((1, LANES), F32)
    x1_p, u2_p, ti_p, tw_p, pos_p, cnt_p = _out_proj(x_prompt, conv_p, att_p, mod_p, out_vecs, out_w, cnt0,
                                                     tt=to, alpha=alpha, per_row=False)
    n_p = bp * seq
    bm_p = _pick_tile(n_p * TOP_K, 512)
    tc_p = _pick_tile(seq, 128)
    y_p = _moe(u2_p.reshape(n_p * SUBLANES, LANES), x1_p.reshape(n_p, d), ti_p.reshape(n_p, LANES)[:, :TOP_K], tw_p.reshape(n_p, LANES),
               pos_p.reshape(n_p, LANES)[:, :TOP_K], cnt_p[0, :N_EXPERTS], mod_p, experts, g2, b2,
               bm=bm_p, tc=tc_p, alpha=alpha, per_row=False, tiles_per_batch=seq // tc_p).reshape(bp, seq, d)

    n_s = bs * tq
    (glu_s, q_s, kc_s, vc_s, ks_s, vs_s, kw_s, vw_s, gates_s) = _in_proj(
        x_sample.reshape(1, n_s, d), mod_s, in_w, None, tt=n_s, per_row=True)
    unflat = lambda a: a.reshape(bs, tq, a.shape[-1])
    glu_s, q_s, kc_s, vc_s, ks_s, vs_s, kw_s, vw_s, gates_s = map(unflat, (glu_s, q_s, kc_s, vc_s, ks_s, vs_s, kw_s, vw_s, gates_s))
    xc_s = jnp.concatenate([jnp.zeros((bs, CONV_HIST - (CONV_K - 1), c_conv), F32), state_conv[0], glu_s], axis=1)
    conv_s = _conv_small(xc_s, *conv_vecs, tt=tq)
    dim_major = lambda a: a.transpose(0, 2, 3, 1).reshape(a.shape[0], KV_DIM, a.shape[1])
    ck_s, cv_s = _compress_decode(page_table, dim_major(cache_cmp_k[0]), dim_major(cache_cmp_v[0]), wk_c, wv_c)
    att_s = _attn_decode(page_table, q_s, gates_s, ck_s, cv_s, ks_s, vs_s, kw_s, vw_s, dim_major(state_win_k[0]),
                         dim_major(state_win_v[0]), dim_major(cache_slc_k[0]), dim_major(cache_slc_v[0]))
    flat = lambda a: a.reshape(1, n_s, a.shape[-1])
    x1_s, u2_s, ti_s, tw_s, pos_s, cnt_s = _out_proj(flat(x_sample), flat(conv_s), flat(att_s), mod_s, out_vecs, out_w, cnt0,
                                                     tt=n_s, alpha=alpha, per_row=True)
    bm_s = max(SUBLANES, min(512, n_s * TOP_K // N_EXPERTS))
    y_s = _moe(u2_s.reshape(n_s * SUBLANES, LANES), x1_s.reshape(n_s, d), ti_s.reshape(n_s, LANES)[:, :TOP_K], tw_s.reshape(n_s, LANES),
               pos_s.reshape(n_s, LANES)[:, :TOP_K], cnt_s[0, :N_EXPERTS], mod_s, experts, g2, b2,
               bm=bm_s, tc=n_s, alpha=alpha, per_row=True, tiles_per_batch=1).reshape(bs, tq, d)

    heads = lambda a: a.reshape(1, a.shape[0], a.shape[1], N_KV, HEAD_DIM)
    win_keep = min(WINDOW, seq)
    heads_t = lambda a: a.reshape(1, a.shape[0], N_KV, HEAD_DIM, a.shape[2]).transpose(0, 1, 4, 2, 3)
    new_p = (heads_t(kct_p), heads_t(vct_p), heads_t(kst_p), heads_t(vst_p), heads_t(kwt_p[:, :, seq - win_keep:]),
             heads_t(vwt_p[:, :, seq - win_keep:]), tail_p[None, :, CONV_HIST - (CONV_K - 1):])
    win_buf = state_win_k.shape[2]
    keep = lambda old, new: jnp.concatenate([old[0], new.reshape(bs, tq, N_KV, HEAD_DIM)], axis=1)[None, :, -win_buf:]
    new_s = (heads(kc_s), heads(vc_s), heads(ks_s), heads(vs_s), keep(state_win_k, kw_s), keep(state_win_v, vw_s),
             jnp.concatenate([state_conv[0], glu_s], axis=1)[None, :, -(CONV_K - 1):])
    return (y_p, y_s) + new_p + new_s
```

```python
import functools

import jax
import jax.numpy as jnp
from jax import lax
from jax.experimental import pallas as pl
from jax.experimental.pallas import tpu as pltpu

F32 = jnp.float32
BF16 = jnp.bfloat16
I32 = jnp.int32

N_HEADS = 8
HEAD_DIM = 64
N_KV = 2
Q_PER_KV = N_HEADS // N_KV
KV_DIM = N_KV * HEAD_DIM
CMP_LEN = 32
CMP_STRIDE = 16
CMP_HID = 2 * HEAD_DIM
SEL_LEN = 64
SEL_RATIO = SEL_LEN // CMP_STRIDE
N_SEL = 16
WINDOW = 512
Q_BLOCK = 128
N_GATES = 3
N_EXPERTS = 32
TOP_K = 4
CONV_K = 31
SWIGLU_LIMIT = 7.0
SWIGLU_ALPHA = 1.702
LN_EPS = 1e-5
NEG_INF = -1e30
FORCE = 1e9
SLOPES = tuple(2.0 ** (-8.0 * (h + 1) / N_HEADS) for h in range(N_HEADS))

LANES = 128
SUBLANES = 8
VMEM_LIMIT_BYTES = 56 * 1024 * 1024

CONV_HIST = 32
CHUNK_ROW = CMP_STRIDE * KV_DIM
SLC_CHUNK = 512
DEC_CHUNK = 2048
DEC_Q = 8
V_AUG = HEAD_DIM + 16
LOG2E = 1.4426950408889634
HIGHEST = lax.Precision.HIGHEST


def _cparams(*sem):
    return pltpu.CompilerParams(dimension_semantics=sem, vmem_limit_bytes=VMEM_LIMIT_BYTES)


def _sigmoid(x):
    return 1.0 / (1.0 + jnp.exp(-x))


def _bdot(a, b):
    return jnp.dot(a.astype(BF16), b.astype(BF16), preferred_element_type=F32)


def _bdot_nt(a, b):
    return lax.dot_general(a.astype(BF16), b.astype(BF16), (((1,), (1,)), ((), ())), preferred_element_type=F32)


def _layer_norm(x, g, b):
    mu = jnp.mean(x, -1, keepdims=True)
    xc = x - mu
    var = jnp.mean(xc * xc, -1, keepdims=True)
    return xc * lax.rsqrt(var + LN_EPS) * g + b


def _gelu_tanh(x):
    return 0.5 * x * (1.0 + jnp.tanh(0.7978845608028654 * (x + 0.044715 * (x * x * x))))


def _masked_softmax(s, mask):
    s = jnp.where(mask, s, NEG_INF)
    e = jnp.where(mask, jnp.exp(s - jnp.max(s, -1, keepdims=True)), 0.0)
    return e * (1.0 / jnp.maximum(jnp.sum(e, -1, keepdims=True), 1e-30))


def _ada_kernel(c_ref, w_ref, b_ref, o_ref):
    c = c_ref[...]
    o_ref[...] = _bdot(c * _sigmoid(c), w_ref[...]) + b_ref[...]


def _ada(c_all, w_ada, b_ada):
    nb, d = c_all.shape
    n = w_ada.shape[1]
    bn = d
    return pl.pallas_call(
        _ada_kernel,
        grid=(n // bn,),
        in_specs=[pl.BlockSpec((nb, d), lambda i: (0, 0)),
                  pl.BlockSpec((d, bn), lambda i: (0, i)),
                  pl.BlockSpec((1, bn), lambda i: (0, i))],
        out_specs=pl.BlockSpec((nb, bn), lambda i: (0, i)),
        out_shape=jax.ShapeDtypeStruct((nb, n), F32),
        compiler_params=_cparams("arbitrary"),
        name="ada",
    )(c_all, w_ada, b_ada)


def _mod_rows(mod_ref, i, per_row):
    return mod_ref[i] if per_row else mod_ref[0, i:i + 1, :]


def _conv_ln_silu(xc_ref, row0, n, wdw_ref, bdw, lng, lnb, shifted_ref=None):
    acc = jnp.zeros((n, xc_ref.shape[1]), F32) + bdw
    off = CONV_HIST - (CONV_K - 1)
    for k in range(CONV_K):
        at = row0 + off + k
        if shifted_ref is None or at % SUBLANES == 0:
            x = xc_ref[pl.ds(at, n), :]
        else:
            x = shifted_ref[at % SUBLANES - 1, pl.ds(at - at % SUBLANES, n), :]
        acc = acc + x * wdw_ref[k:k + 1, :]
    y = _layer_norm(acc, lng, lnb)
    return y * _sigmoid(y)


def _in_kernel(*refs, tt, c_conv, fuse_conv, per_row):
    if fuse_conv:
        (x_ref, mod_ref, wa_ref, wq_ref, wkv_ref, wg_ref, hist_ref, wdw_ref, bdw_ref, lng_ref, lnb_ref,
         conv_ref, tail_ref, q_ref, kc_ref, vc_ref, *kvt_refs, gates_ref, xc_ref, xs_ref) = refs
    else:
        (x_ref, mod_ref, wa_ref, wq_ref, wkv_ref, wg_ref,
         glu_ref, q_ref, kc_ref, vc_ref, ks_ref, vs_ref, kw_ref, vw_ref, gates_ref) = refs
    x = x_ref[0]
    shift1 = _mod_rows(mod_ref, 0, per_row)
    scale1 = _mod_rows(mod_ref, 1, per_row)
    u = (x * (1.0 + scale1) + shift1).astype(BF16)
    a = jnp.dot(u, wa_ref[...], preferred_element_type=F32)
    glu = a[:, :c_conv] * _sigmoid(a[:, c_conv:])
    q_ref[0] = jnp.dot(u, wq_ref[...], preferred_element_type=F32)
    kv = jnp.dot(u, wkv_ref[...], preferred_element_type=F32)
    gates_ref[0] = _sigmoid(jnp.dot(u, wg_ref[...], preferred_element_type=F32))
    if not fuse_conv:
        for i, r in enumerate((kc_ref, vc_ref, ks_ref, vs_ref, kw_ref, vw_ref)):
            r[0] = kv[:, i * KV_DIM:(i + 1) * KV_DIM]
        glu_ref[0] = glu
        return
    kc_ref[0] = kv[:, 0:KV_DIM]
    vc_ref[0] = kv[:, KV_DIM:2 * KV_DIM]
    for i, r in enumerate(kvt_refs):
        r[0] = jnp.transpose(kv[:, i * KV_DIM:(i + 1) * KV_DIM])

    @pl.when(pl.program_id(1) == 0)
    def _():
        xc_ref[0:CONV_HIST, :] = hist_ref[0]

    xc_ref[CONV_HIST:CONV_HIST + tt, :] = glu
    rows = min(tt, 64)
    assert rows % SUBLANES == 0
    keep = CONV_HIST + tt - SUBLANES
    for b in range(1, SUBLANES):
        xs_ref[b - 1, 0:keep, :] = xc_ref[b:b + keep, :]
    for r0 in range(0, tt, rows):
        conv_ref[0, r0:r0 + rows, :] = _conv_ln_silu(xc_ref, r0, rows, wdw_ref, bdw_ref[...], lng_ref[...], lnb_ref[...], xs_ref)
    tail = xc_ref[tt:tt + CONV_HIST, :]
    xc_ref[0:CONV_HIST, :] = tail
    tail_ref[0] = tail


def _in_proj(x, mod, wts, conv=None, *, tt, per_row):
    b, t, d = x.shape
    wa, wq, wkv, wg = wts
    c_conv = wa.shape[1] // 2
    fuse = conv is not None
    nt = t // tt
    const = lambda shape: pl.BlockSpec(shape, lambda i, j: (0,) * len(shape))
    mod_spec = (pl.BlockSpec((6, tt, d), lambda i, j: (0, j, 0)) if per_row
                else pl.BlockSpec((1, 6, d), lambda i, j: (i, 0, 0)))
    in_specs = [pl.BlockSpec((1, tt, d), lambda i, j: (i, j, 0)), mod_spec,
                const(wa.shape), const(wq.shape), const(wkv.shape), const(wg.shape)]
    args = [x, mod, wa, wq, wkv, wg]
    tok = lambda n: pl.BlockSpec((1, tt, n), lambda i, j: (i, j, 0))
    tok_shape = lambda n: jax.ShapeDtypeStruct((b, t, n), F32)
    out_specs, out_shape, scratch = [], [], []
    if fuse:
        hist, wdw, bdw, lng, lnb = conv
        in_specs += [pl.BlockSpec((1, CONV_HIST, c_conv), lambda i, j: (i, 0, 0)),
                     const(wdw.shape), const(bdw.shape), const(lng.shape), const(lnb.shape)]
        args += [hist, wdw, bdw, lng, lnb]
        out_specs += [tok(c_conv), pl.BlockSpec((1, CONV_HIST, c_conv), lambda i, j: (i, 0, 0))]
        out_shape += [tok_shape(c_conv), jax.ShapeDtypeStruct((b, CONV_HIST, c_conv), F32)]
        scratch = [pltpu.VMEM((CONV_HIST + tt, c_conv), F32), pltpu.VMEM((SUBLANES - 1, CONV_HIST + tt, c_conv), F32)]
    else:
        out_specs += [tok(c_conv)]
        out_shape += [tok_shape(c_conv)]
    if fuse:
        kv_specs = [tok(KV_DIM)] * 2 + [pl.BlockSpec((1, KV_DIM, tt), lambda i, j: (i, 0, j))] * 6
        kv_shape = [tok_shape(KV_DIM)] * 2 + [jax.ShapeDtypeStruct((b, KV_DIM, t), F32)] * 6
    else:
        kv_specs, kv_shape = [tok(KV_DIM)] * 6, [tok_shape(KV_DIM)] * 6
    out_specs += [tok(wq.shape[1])] + kv_specs + [tok(wg.shape[1])]
    out_shape += [tok_shape(wq.shape[1])] + kv_shape + [tok_shape(wg.shape[1])]
    return pl.pallas_call(
        functools.partial(_in_kernel, tt=tt, c_conv=c_conv, fuse_conv=fuse, per_row=per_row),
        grid=(b, nt), in_specs=in_specs, out_specs=out_specs, out_shape=out_shape, scratch_shapes=scratch,
        compiler_params=_cparams("arbitrary", "arbitrary"),
        name="in_proj_conv" if fuse else "in_proj",
    )(*args)


def _conv_small_kernel(xc_ref, wdw_ref, bdw_ref, lng_ref, lnb_ref, o_ref, *, tt):
    o_ref[0] = _conv_ln_silu(xc_ref.at[0], 0, tt, wdw_ref, bdw_ref[...], lng_ref[...], lnb_ref[...])


def _conv_small(xc, wdw, bdw, lng, lnb, *, tt):
    b, rows, c = xc.shape
    const = lambda shape: pl.BlockSpec(shape, lambda i: (0,) * len(shape))
    return pl.pallas_call(
        functools.partial(_conv_small_kernel, tt=tt),
        grid=(b,),
        in_specs=[pl.BlockSpec((1, rows, c), lambda i: (i, 0, 0)),
                  const(wdw.shape), const(bdw.shape), const(lng.shape), const(lnb.shape)],
        out_specs=pl.BlockSpec((1, tt, c), lambda i: (i, 0, 0)),
        out_shape=jax.ShapeDtypeStruct((b, tt, c), F32),
        compiler_params=_cparams("arbitrary"),
        name="conv_small",
    )(xc, wdw, bdw, lng, lnb)


def _compress_weights(w1, w2, pe):
    half = CMP_STRIDE * HEAD_DIM

    def expand(w):
        w = w.reshape(CMP_STRIDE, HEAD_DIM, CMP_HID)
        z = jnp.zeros_like(w)
        g0 = jnp.concatenate([w, z], axis=1).reshape(CHUNK_ROW, CMP_HID)
        g1 = jnp.concatenate([z, w], axis=1).reshape(CHUNK_ROW, CMP_HID)
        return jnp.concatenate([g0, g1], axis=1)

    w1p = jnp.concatenate([expand(w1[:half]), expand(w1[half:])], axis=1).astype(BF16)
    z2 = jnp.zeros_like(w2)
    w2p = jnp.concatenate([jnp.concatenate([w2, z2], axis=1), jnp.concatenate([z2, w2], axis=1)], axis=0).astype(BF16)
    return w1p, w1.astype(BF16), pe.reshape(1, -1), w2p


def _compress_tail(r, w1_ref, pe_ref, w2p_ref):
    n = r.shape[0]
    pe8 = jnp.broadcast_to(pe_ref[...], (SUBLANES, pe_ref.shape[1]))
    c = _bdot(pe8, w1_ref[...])[0:1]
    c2 = jnp.concatenate([c, c], axis=1)
    h = r[:, :2 * CMP_HID] + pltpu.roll(r[:, 2 * CMP_HID:], n - 1, 0) + c2
    return _bdot(_gelu_tanh(h), w2p_ref[...])


def _compress_prompt_kernel(ak_ref, av_ref, w1pk, w1k, pek, w2pk, w1pv, w1v, pev, w2pv, ck_ref, cv_ref):
    ck_ref[0] = _compress_tail(_bdot(ak_ref[0], w1pk[...]), w1k, pek, w2pk)
    cv_ref[0] = _compress_tail(_bdot(av_ref[0], w1pv[...]), w1v, pev, w2pv)


def _compress_prompt(ak, av, wk, wv):
    b, n, _ = ak.shape
    const = lambda a: pl.BlockSpec(a.shape, lambda i: (0,) * a.ndim)
    row = pl.BlockSpec((1, n, CHUNK_ROW), lambda i: (i, 0, 0))
    out = pl.BlockSpec((1, n, KV_DIM), lambda i: (i, 0, 0))
    return pl.pallas_call(
        _compress_prompt_kernel,
        grid=(b,),
        in_specs=[row, row] + [const(a) for a in wk] + [const(a) for a in wv],
        out_specs=[out, out],
        out_shape=[jax.ShapeDtypeStruct((b, n, KV_DIM), F32)] * 2,
        compiler_params=_cparams("arbitrary"),
        name="compress_prompt",
    )(ak, av, *wk, *wv)


def _compress_decode_kernel(pt_ref, pk_ref, pv_ref, w1pk, w1k, pek, w2pk, w1pv, w1v, pev, w2pv, ck_ref, cv_ref,
                            slab_k, slab_v, rows_buf, rbuf, sem, *, n_pages, page):
    j = pl.program_id(0)
    n_jobs = pl.num_programs(0)
    past = n_pages * page
    n = past // CMP_STRIDE
    step = min(n, 256)
    tcols = min(past, 2048)

    def fetch(job, pool_ref, slab, s, wait):
        def body(pg, carry):
            pid = pt_ref[(job // 2) * n_pages + pg]
            cp = pltpu.make_async_copy(pool_ref.at[pid], slab.at[:, pl.ds(pl.multiple_of(pg * page, page), page)], sem.at[s])
            cp.wait() if wait else cp.start()
            return carry
        lax.fori_loop(0, n_pages, body, 0)

    def project(slab, w1p):
        def to_rows(i, carry):
            at = pl.ds(pl.multiple_of(i * tcols, tcols), tcols)
            rows_buf[at, :] = jnp.transpose(slab[:, at])
            return carry
        lax.fori_loop(0, past // tcols, to_rows, 0)

        def body(i, carry):
            base = pl.multiple_of(i * (step * CMP_STRIDE), step * CMP_STRIDE)
            a = jnp.concatenate([rows_buf[pl.ds(base + p, step, stride=CMP_STRIDE), :] for p in range(CMP_STRIDE)], axis=1)
            rbuf[pl.ds(pl.multiple_of(i * step, step), step), :] = _bdot(a, w1p[...])
            return carry
        lax.fori_loop(0, n // step, body, 0)
        return rbuf[...]

    @pl.when(j == 0)
    def _():
        fetch(j, pk_ref, slab_k, 0, False)

    @pl.when(j % 2 == 0)
    def _():
        fetch(j + 1, pv_ref, slab_v, 1, False)
        fetch(j, pk_ref, slab_k, 0, True)
        ck_ref[0] = _compress_tail(project(slab_k, w1pk), w1k, pek, w2pk)

    @pl.when(j % 2 == 1)
    def _():
        @pl.when(j + 1 < n_jobs)
        def _():
            fetch(j + 1, pk_ref, slab_k, 0, False)
        fetch(j, pv_ref, slab_v, 1, True)
        cv_ref[0] = _compress_tail(project(slab_v, w1pv), w1v, pev, w2pv)


def _compress_decode(page_table, pool_k, pool_v, wk, wv):
    nb, n_pages = page_table.shape
    page = pool_k.shape[2]
    past = n_pages * page
    n = past // CMP_STRIDE
    assert page % LANES == 0
    const = lambda a: pl.BlockSpec(a.shape, lambda i, pt: (0,) * a.ndim)
    any_spec = pl.BlockSpec(memory_space=pl.ANY)
    out = pl.BlockSpec((1, n, KV_DIM), lambda i, pt: (i // 2, 0, 0))
    grid_spec = pltpu.PrefetchScalarGridSpec(
        num_scalar_prefetch=1, grid=(2 * nb,),
        in_specs=[any_spec, any_spec] + [const(a) for a in wk] + [const(a) for a in wv],
        out_specs=[out, out],
        scratch_shapes=[pltpu.VMEM((KV_DIM, past), F32), pltpu.VMEM((KV_DIM, past), F32), pltpu.VMEM((past, KV_DIM), F32),
                        pltpu.VMEM((n, 4 * CMP_HID), F32), pltpu.SemaphoreType.DMA((2,))])
    return pl.pallas_call(
        functools.partial(_compress_decode_kernel, n_pages=n_pages, page=page),
        grid_spec=grid_spec,
        out_shape=[jax.ShapeDtypeStruct((nb, n, KV_DIM), F32)] * 2,
        compiler_params=_cparams("arbitrary"),
        name="compress_decode",
    )(page_table.reshape(-1), pool_k, pool_v, *wk, *wv)


def _importance_matrix(n_cmp_rows, n_cmp, n_blk, cols):
    c = jnp.arange(n_cmp_rows)[:, None]
    b = jnp.arange(cols)[None, :]
    d = c - (SEL_RATIO * b - 1)
    return ((d >= 0) & (d <= SEL_RATIO) & (c < n_cmp) & (b < n_blk)).astype(F32)


def _expand_matrix(n_chunks, chunk, blk0):
    key = jnp.arange(n_chunks)[:, None, None] * chunk + jnp.arange(chunk)[None, None, :]
    blk = jnp.asarray(blk0)[:, None, None] + jnp.arange(LANES)[None, :, None]
    return (key // SEL_LEN == blk).astype(BF16)


def _aug_values(vt):
    r = lax.broadcasted_iota(I32, (V_AUG - HEAD_DIM, vt.shape[1]), 0)
    return jnp.concatenate([vt, jnp.where(r == 0, 1.0, 0.0)], axis=0).astype(BF16)


def _flash_step(qh, kt, vta, bias, m, acc):
    s = jnp.dot(qh, kt, preferred_element_type=F32) + bias
    m_new = jnp.maximum(m, jnp.max(s, -1, keepdims=True))
    p = jnp.exp2(s - m_new).astype(BF16)
    pv = lax.dot_general(p, vta, (((1,), (1,)), ((), ())), preferred_element_type=F32)
    return m_new, jnp.exp2(m - m_new) * acc + pv


def _flash_init(rows):
    return jnp.full((rows, 1), NEG_INF, F32), jnp.zeros((rows, V_AUG), F32)


def _flash_out(acc):
    return acc[:, :HEAD_DIM] * (1.0 / jnp.maximum(acc[:, HEAD_DIM:HEAD_DIM + 1], 1e-30))


def _attn_prompt_kernel(q_ref, g_ref, ck_ref, cv_ref, kst_ref, vst_ref, kwt_ref, vwt_ref, mimp_ref, esel_ref, o_ref,
                        *, seq, n_blk, n_pick, win_len):
    qb = Q_BLOCK
    qs = pl.program_id(1) * qb
    t_col = qs + lax.broadcasted_iota(I32, (qb, 1), 0)
    gates = g_ref[0]
    scale = HEAD_DIM ** -0.5
    q_heads = [q_ref[0, :, h * HEAD_DIM:(h + 1) * HEAD_DIM] for h in range(N_HEADS)]
    nc = ck_ref.shape[1]
    cmp_end = lax.broadcasted_iota(I32, (1, nc), 1) * CMP_STRIDE + (CMP_LEN - 1)
    dist_c = t_col - cmp_end
    mask_c = dist_c >= 0
    dist_cf = dist_c.astype(F32)
    blk_pad = mimp_ref.shape[1]

    blk_i = lax.broadcasted_iota(I32, (n_blk, qb), 0)
    t_row = qs + lax.broadcasted_iota(I32, (n_blk, qb), 1)
    cur = t_row // SEL_LEN
    forced = (blk_i == 0) | (blk_i == cur) | (blk_i == cur - 1)
    future = blk_i * SEL_LEN > t_row

    win_start = pl.multiple_of(jnp.clip(qs - WINDOW, 0, seq - win_len), qb)
    dist_w = t_col - (win_start + lax.broadcasted_iota(I32, (1, win_len), 1))
    maskbias_w = jnp.where((dist_w >= 0) & (dist_w <= WINDOW), 0.0, NEG_INF)
    n_chunks = (qs + qb + SLC_CHUNK - 1) // SLC_CHUNK
    rows = Q_PER_KV * qb
    stack = lambda parts: jnp.concatenate(parts, axis=0)
    heads_of = lambda a: a.reshape(Q_PER_KV, qb, a.shape[1])

    def flash(qk, alibi, maskbias, off, vta, m, acc):
        s = (heads_of(qk + alibi) + maskbias[None]).reshape(rows, qk.shape[1])
        m_new = jnp.maximum(m, jnp.max(s, -1, keepdims=True) + off)
        p = jnp.exp2(s - (m_new - off)).astype(BF16)
        pv = lax.dot_general(p, vta, (((1,), (1,)), ((), ())), preferred_element_type=F32)
        return m_new, jnp.exp2(m - m_new) * acc + pv

    lanes_of = lambda g: slice(g * HEAD_DIM, (g + 1) * HEAD_DIM)
    heads_in = lambda g: range(g * Q_PER_KV, (g + 1) * Q_PER_KV)
    qg2s, slope2s, sels, o_cmps = [], [], [], []
    for g in range(N_KV):
        lanes, heads = lanes_of(g), heads_in(g)
        slope = stack([jnp.full((qb, 1), SLOPES[h], F32) for h in heads])
        qg = stack([(q_heads[h] * scale).astype(BF16) for h in heads])
        qg2s.append(stack([(q_heads[h] * (scale * LOG2E)).astype(BF16) for h in heads]))
        slope2s.append(slope * LOG2E)
        s_c = heads_of(_bdot_nt(qg, ck_ref[0, :, lanes]) - slope * stack([dist_cf] * Q_PER_KV))
        p_c = _masked_softmax(s_c, mask_c[None])
        p_sum = jnp.sum(p_c, axis=0)
        o_cmp = heads_of(_bdot(p_c.reshape(rows, nc), cv_ref[0, :, lanes]))
        imp = jnp.dot(p_sum, mimp_ref[...], precision=HIGHEST, preferred_element_type=F32)
        imp_t = jnp.transpose(imp)[0:n_blk, :]
        imp_t = jnp.where(future, -FORCE, jnp.where(forced, FORCE, imp_t))
        cnt = jnp.zeros((n_blk, qb), F32)
        for b2 in range(n_blk):
            row = imp_t[b2:b2 + 1, :]
            ahead = (row > imp_t) | ((row == imp_t) & (blk_i > b2))
            cnt = cnt + jnp.where(ahead, 1.0, 0.0)
        sel_t = jnp.where(cnt < n_pick, 1.0, 0.0)
        if n_blk < blk_pad:
            sel_t = jnp.concatenate([sel_t, jnp.zeros((blk_pad - n_blk, qb), F32)], axis=0)
        sels.append(jnp.transpose(sel_t).astype(BF16))
        o_cmps.append(o_cmp)

    key_slc = lax.broadcasted_iota(I32, (1, SLC_CHUNK), 1)
    key_win = lax.broadcasted_iota(I32, (1, win_len), 1).astype(F32)
    alibi_slc = [s2 * key_slc.astype(F32) for s2 in slope2s]

    def slc_step(c, carry):
        k0 = pl.multiple_of(c * SLC_CHUNK, SLC_CHUNK)
        causal = t_col - (k0 + key_slc) >= 0
        out = []
        for g in range(N_KV):
            kt = kst_ref[0, lanes_of(g), pl.ds(k0, SLC_CHUNK)].astype(BF16)
            vta = _aug_values(vst_ref[0, lanes_of(g), pl.ds(k0, SLC_CHUNK)])
            msel = jnp.dot(sels[g], esel_ref[c], preferred_element_type=F32)
            maskbias = jnp.where(causal & (msel > 0.5), 0.0, NEG_INF)
            qk = jnp.dot(qg2s[g], kt, preferred_element_type=F32)
            out += flash(qk, alibi_slc[g], maskbias, slope2s[g] * k0.astype(F32), vta, *carry[2 * g:2 * g + 2])
        return tuple(out)

    fin = lax.fori_loop(0, n_chunks, slc_step, _flash_init(rows) * N_KV)
    for g in range(N_KV):
        o_slc = heads_of(_flash_out(fin[2 * g + 1]))
        kwt = kwt_ref[0, lanes_of(g), pl.ds(win_start, win_len)].astype(BF16)
        vwta = _aug_values(vwt_ref[0, lanes_of(g), pl.ds(win_start, win_len)])
        qk = jnp.dot(qg2s[g], kwt, preferred_element_type=F32)
        o_win = heads_of(_flash_out(flash(qk, slope2s[g] * key_win, maskbias_w, 0.0, vwta, *_flash_init(rows))[1]))
        for i, h in enumerate(heads_in(g)):
            c0 = h * N_GATES
            o_ref[0, :, h * HEAD_DIM:(h + 1) * HEAD_DIM] = (gates[:, c0:c0 + 1] * o_cmps[g][i] + gates[:, c0 + 1:c0 + 2] * o_slc[i]
                                                           + gates[:, c0 + 2:c0 + 3] * o_win[i])


def _attn_prompt(q, gates, ck, cv, ks, vs, kw, vw):
    b, t, att = q.shape
    assert t % SLC_CHUNK == 0 and t % Q_BLOCK == 0
    n_blk = t // SEL_LEN
    n_cmp = t // CMP_STRIDE - 1
    assert n_blk <= LANES and ck.shape[1] == t // CMP_STRIDE
    n_pick = min(N_SEL, n_blk)
    win_len = min(WINDOW + Q_BLOCK, t)
    mimp = _importance_matrix(ck.shape[1], n_cmp, n_blk, LANES)
    n_ch = t // SLC_CHUNK
    esel = _expand_matrix(n_ch, SLC_CHUNK, [0] * n_ch)
    tok = lambda n: pl.BlockSpec((1, Q_BLOCK, n), lambda i, j: (i, j, 0))
    full = lambda a: pl.BlockSpec((1,) + a.shape[1:], lambda i, j: (i, 0, 0))
    const = lambda a: pl.BlockSpec(a.shape, lambda i, j: (0,) * a.ndim)
    return pl.pallas_call(
        functools.partial(_attn_prompt_kernel, seq=t, n_blk=n_blk, n_pick=n_pick, win_len=win_len),
        grid=(b, t // Q_BLOCK),
        in_specs=[tok(att), tok(gates.shape[2]), full(ck), full(cv), full(ks), full(vs), full(kw), full(vw),
                  const(mimp), const(esel)],
        out_specs=tok(att),
        out_shape=jax.ShapeDtypeStruct((b, t, att), F32),
        compiler_params=_cparams("arbitrary", "arbitrary"),
        name="attn_prompt",
    )(q, gates, ck, cv, ks, vs, kw, vw, mimp, esel)


def _attn_decode_kernel(pt_ref, q_ref, g_ref, ck_ref, cv_ref, ksn_ref, vsn_ref, kwn_ref, vwn_ref, wk_ref, wv_ref,
                        pk_ref, pv_ref, mimp_ref, esel_ref, o_ref, slab_k, slab_v, sem,
                        *, n_pages, page, past, n_blk, n_pick, tq):
    b = pl.program_id(0)
    slot = b % 2

    def fetch(bb, s, wait):
        def body(pg, carry):
            pid = pt_ref[bb * n_pages + pg]
            dst = pl.ds(pl.multiple_of(pg * page, page), page)
            for pool, slab, i in ((pk_ref, slab_k, 0), (pv_ref, slab_v, 1)):
                cp = pltpu.make_async_copy(pool.at[pid], slab.at[s, :, dst], sem.at[s, i])
                cp.wait() if wait else cp.start()
            return carry
        lax.fori_loop(0, n_pages, body, 0)

    @pl.when(b == 0)
    def _():
        fetch(b, slot, False)

    @pl.when(b + 1 < pl.num_programs(0))
    def _():
        fetch(b + 1, 1 - slot, False)

    rows = Q_PER_KV * DEC_Q
    qi = lax.broadcasted_iota(I32, (rows, 1), 0) % DEC_Q
    t_col = past + qi % tq
    hl = lax.broadcasted_iota(I32, (rows, 1), 0) // DEC_Q
    nc = ck_ref.shape[1]
    cmp_end = lax.broadcasted_iota(I32, (1, nc), 1) * CMP_STRIDE + (CMP_LEN - 1)
    dist_c = t_col - cmp_end
    mask_c = dist_c >= 0
    dist_cf = dist_c.astype(F32)
    blk_pad = mimp_ref.shape[1]
    t8 = past + lax.broadcasted_iota(I32, (DEC_Q, 1), 0) % tq
    blk_l = lax.broadcasted_iota(I32, (DEC_Q, blk_pad), 1)
    cur = t8 // SEL_LEN
    forced = (blk_l == 0) | (blk_l == cur) | (blk_l == cur - 1)
    future = blk_l * SEL_LEN > t8
    pad_blk = blk_l >= n_blk
    b_sub = lax.broadcasted_iota(I32, (blk_pad, blk_pad), 0)
    b_lane = lax.broadcasted_iota(I32, (blk_pad, blk_pad), 1)
    n_chunks = past // DEC_CHUNK
    new_rows = ksn_ref.shape[2]

    sels, slopes, o_cmps = [], [], []
    for g in range(N_KV):
        lanes = slice(g * HEAD_DIM, (g + 1) * HEAD_DIM)
        slope = jnp.zeros((rows, 1), F32)
        for i in range(Q_PER_KV):
            slope = jnp.where(hl == i, SLOPES[g * Q_PER_KV + i], slope)
        slopes.append(slope)
        qg = (q_ref[0, g] * (HEAD_DIM ** -0.5)).astype(BF16)
        p = _masked_softmax(_bdot_nt(qg, ck_ref[0, :, lanes]) - slope * dist_cf, mask_c)
        o_cmps.append(_bdot(p, cv_ref[0, :, lanes]))
        p_sum = p[0:DEC_Q]
        for i in range(1, Q_PER_KV):
            p_sum = p_sum + p[i * DEC_Q:(i + 1) * DEC_Q]
        imp = jnp.dot(p_sum, mimp_ref[...], precision=HIGHEST, preferred_element_type=F32)
        imp = jnp.where(pad_blk, -3e38, jnp.where(future, -FORCE, jnp.where(forced, FORCE, imp)))
        imp_t = jnp.transpose(jnp.concatenate([imp, jnp.zeros((LANES - DEC_Q, blk_pad), F32)], axis=0))
        sel_rows = []
        for r in range(DEC_Q):
            col = imp_t[:, r:r + 1]
            row = imp[r:r + 1, :]
            ahead = (col > row) | ((col == row) & (b_sub < b_lane))
            cnt = jnp.sum(jnp.where(ahead, 1.0, 0.0), axis=0, keepdims=True)
            sel_rows.append(jnp.where(cnt < n_pick, 1.0, 0.0))
        sel8 = jnp.concatenate(sel_rows, axis=0)
        sels.append(jnp.concatenate([sel8] * Q_PER_KV, axis=0))

    fetch(b, slot, True)

    for g in range(N_KV):
        lanes = slice(g * HEAD_DIM, (g + 1) * HEAD_DIM)
        slope2 = slopes[g] * LOG2E
        qg2 = (q_ref[0, g] * (HEAD_DIM ** -0.5 * LOG2E)).astype(BF16)
        sel = sels[g]
        sel_b = sel.astype(BF16)

        def bias_of(dist, visible):
            return jnp.where(visible, 0.0, NEG_INF) - slope2 * dist.astype(F32)

        m, acc = _flash_init(rows)
        per_mat = LANES * SEL_LEN // DEC_CHUNK
        for c in range(n_chunks):
            keys = slice(c * DEC_CHUNK, (c + 1) * DEC_CHUNK)
            lane0 = (c // per_mat) * LANES
            msel = jnp.dot(sel_b[:, lane0:lane0 + LANES], esel_ref[c % per_mat], preferred_element_type=F32)
            dist = t_col - (c * DEC_CHUNK + lax.broadcasted_iota(I32, (1, DEC_CHUNK), 1))
            m, acc = _flash_step(qg2, slab_k[slot, lanes, keys].astype(BF16), _aug_values(slab_v[slot, lanes, keys]),
                                 bias_of(dist, (dist >= 0) & (msel > 0.5)), m, acc)
        dist_n = t_col - (past + lax.broadcasted_iota(I32, (1, new_rows), 1))
        cur_blk = past // SEL_LEN
        m, acc = _flash_step(qg2, ksn_ref[0, lanes, :].astype(BF16), _aug_values(vsn_ref[0, lanes, :]),
                             bias_of(dist_n, (dist_n >= 0) & (sel[:, cur_blk:cur_blk + 1] > 0.5)), m, acc)
        o_slc = _flash_out(acc)
        wlen = wk_ref.shape[2]
        dist = t_col - (past - wlen + lax.broadcasted_iota(I32, (1, wlen), 1))
        m, acc = _flash_step(qg2, wk_ref[0, lanes, :].astype(BF16), _aug_values(wv_ref[0, lanes, :]),
                             bias_of(dist, (dist >= 0) & (dist <= WINDOW)), *_flash_init(rows))
        m, acc = _flash_step(qg2, kwn_ref[0, lanes, :].astype(BF16), _aug_values(vwn_ref[0, lanes, :]),
                             bias_of(dist_n, (dist_n >= 0) & (dist_n <= WINDOW)), m, acc)
        o_win = _flash_out(acc)
        gt = g_ref[0, g]
        o_ref[0, g] = gt[:, 0:1] * o_cmps[g] + gt[:, 1:2] * o_slc + gt[:, 2:3] * o_win


def _attn_decode(page_table, q, gates, ck, cv, ksn, vsn, kwn, vwn, win_k, win_v, pool_k, pool_v):
    nb, tq, att = q.shape
    n_pages = page_table.shape[1]
    page = pool_k.shape[2]
    past = n_pages * page
    assert tq <= DEC_Q and past % DEC_CHUNK == 0 and past % SEL_LEN == 0 and (past + tq - 1) // SEL_LEN == past // SEL_LEN
    assert win_k.shape[2] == WINDOW and ck.shape[1] * CMP_STRIDE == past and page % LANES == 0
    t_pad = -(-(past + tq) // SEL_LEN) * SEL_LEN
    n_blk = t_pad // SEL_LEN
    n_cmp = t_pad // CMP_STRIDE - 1
    n_pick = min(N_SEL, n_blk)
    blk_pad = -(-n_blk // LANES) * LANES
    mimp = _importance_matrix(ck.shape[1], n_cmp, n_blk, blk_pad)
    per_mat = LANES * SEL_LEN // DEC_CHUNK
    esel = _expand_matrix(per_mat, DEC_CHUNK, [0] * per_mat)

    def dec_rows(a, width):
        a = a.reshape(nb, tq, N_KV, Q_PER_KV, width)
        a = jnp.take(a, jnp.arange(DEC_Q) % tq, axis=1)
        return a.transpose(0, 2, 3, 1, 4).reshape(nb, N_KV, Q_PER_KV * DEC_Q, width)

    q_d = dec_rows(q, HEAD_DIM)
    g_d = dec_rows(gates[:, :, :N_HEADS * N_GATES], N_GATES)
    pad_new = lambda a: jnp.pad(a.transpose(0, 2, 1), ((0, 0), (0, 0), (0, LANES - tq)))
    ksn, vsn, kwn, vwn = [pad_new(a) for a in (ksn, vsn, kwn, vwn)]
    rows = Q_PER_KV * DEC_Q
    blk = lambda a: pl.BlockSpec((1,) + a.shape[1:], lambda i, pt: (i,) + (0,) * (a.ndim - 1))
    const = lambda a: pl.BlockSpec(a.shape, lambda i, pt: (0,) * a.ndim)
    any_spec = pl.BlockSpec(memory_space=pl.ANY)
    grid_spec = pltpu.PrefetchScalarGridSpec(
        num_scalar_prefetch=1, grid=(nb,),
        in_specs=[blk(q_d), blk(g_d), blk(ck), blk(cv), blk(ksn), blk(vsn), blk(kwn), blk(vwn), blk(win_k), blk(win_v),
                  any_spec, any_spec, const(mimp), const(esel)],
        out_specs=pl.BlockSpec((1, N_KV, rows, HEAD_DIM), lambda i, pt: (i, 0, 0, 0)),
        scratch_shapes=[pltpu.VMEM((2, KV_DIM, past), F32), pltpu.VMEM((2, KV_DIM, past), F32),
                        pltpu.SemaphoreType.DMA((2, 2))])
    o = pl.pallas_call(
        functools.partial(_attn_decode_kernel, n_pages=n_pages, page=page, past=past, n_blk=n_blk, n_pick=n_pick, tq=tq),
        grid_spec=grid_spec,
        out_shape=jax.ShapeDtypeStruct((nb, N_KV, rows, HEAD_DIM), F32),
        compiler_params=_cparams("arbitrary"),
        name="attn_decode",
    )(page_table.reshape(-1), q_d, g_d, ck, cv, ksn, vsn, kwn, vwn, win_k, win_v, pool_k, pool_v, mimp, esel)
    o = o.reshape(nb, N_KV, Q_PER_KV, DEC_Q, HEAD_DIM)[:, :, :, :tq]
    return o.transpose(0, 3, 1, 2, 4).reshape(nb, tq, att)


def _out_kernel(x_ref, conv_ref, att_ref, mod_ref, bc_ref, ba_ref, woc_ref, woa_ref, g1_ref, b1_ref, wr_ref, br_ref,
                cnt0_ref, x1_ref, u2_ref, idx_ref, w_ref, pos_ref, cnt_ref, carry_ref, *, tt, alpha, per_row):
    first = (pl.program_id(0) == 0) & (pl.program_id(1) == 0)

    @pl.when(first)
    def _():
        carry_ref[...] = cnt0_ref[...]

    gate1 = _mod_rows(mod_ref, 2, per_row)
    shift2 = _mod_rows(mod_ref, 3, per_row)
    scale2 = _mod_rows(mod_ref, 4, per_row)
    mix = _bdot(conv_ref[0] * bc_ref[...], woc_ref[...]) + _bdot(att_ref[0] * ba_ref[...], woa_ref[...])
    x1 = _layer_norm(alpha * x_ref[0] + gate1 * mix, g1_ref[...], b1_ref[...])
    x1_ref[0] = x1
    u2 = x1 * (1.0 + scale2) + shift2
    _store_tiled(u2_ref.at[0], 0, u2)
    logits = jnp.dot(u2, wr_ref[...], precision=HIGHEST, preferred_element_type=F32) + br_ref[...]
    lane = lax.broadcasted_iota(I32, logits.shape, 1)
    vals = logits
    top_v, top_i = [], []
    for _k in range(TOP_K):
        m = jnp.max(vals, axis=-1, keepdims=True)
        i = jnp.min(jnp.where(vals == m, lane, LANES), axis=-1, keepdims=True)
        top_v.append(m)
        top_i.append(i)
        vals = jnp.where(lane == i, -jnp.inf, vals)
    e = [jnp.exp(v - top_v[0]) for v in top_v]
    inv = 1.0 / (e[0] + e[1] + e[2] + e[3])
    onehot = [jnp.where(lane == i, 1.0, 0.0) for i in top_i]
    assigned = onehot[0] + onehot[1] + onehot[2] + onehot[3]
    r_i = lax.broadcasted_iota(I32, (tt, tt), 0)
    c_i = lax.broadcasted_iota(I32, (tt, tt), 1)
    before = jnp.where(r_i > c_i, 1.0, 0.0).astype(BF16)
    rank = carry_ref[...] + jnp.dot(before, assigned.astype(BF16), preferred_element_type=F32)
    idx_o = jnp.zeros(logits.shape, I32)
    w_o = jnp.zeros(logits.shape, F32)
    pos_o = jnp.zeros(logits.shape, F32)
    for k in range(TOP_K):
        idx_o = jnp.where(lane == k, top_i[k], idx_o)
        w_o = jnp.where(lane == k, e[k] * inv, w_o)
        pos_o = jnp.where(lane == k, jnp.sum(onehot[k] * rank, axis=-1, keepdims=True), pos_o)
    idx_ref[0] = idx_o
    w_ref[0] = w_o
    pos_ref[0] = pos_o.astype(I32)
    carry_ref[...] = carry_ref[...] + jnp.sum(assigned, axis=0, keepdims=True)
    cnt_ref[...] = carry_ref[...]


def _out_proj(x, conv_out, att, mod, vecs, wts, cnt0, *, tt, alpha, per_row):
    b, t, d = x.shape
    bc, ba, g1, b1, br = vecs
    woc, woa, wr = wts
    const = lambda a: pl.BlockSpec(a.shape, lambda i, j: (0,) * a.ndim)
    tok = lambda n: pl.BlockSpec((1, tt, n), lambda i, j: (i, j, 0))
    mod_spec = (pl.BlockSpec((6, tt, d), lambda i, j: (0, j, 0)) if per_row
                else pl.BlockSpec((1, 6, d), lambda i, j: (i, 0, 0)))
    tok_shape = lambda n, dt: jax.ShapeDtypeStruct((b, t, n), dt)
    return pl.pallas_call(
        functools.partial(_out_kernel, tt=tt, alpha=alpha, per_row=per_row),
        grid=(b, t // tt),
        in_specs=[tok(d), tok(conv_out.shape[2]), tok(att.shape[2]), mod_spec, const(bc), const(ba), const(woc), const(woa),
                  const(g1), const(b1), const(wr), const(br), const(cnt0)],
        out_specs=[tok(d), pl.BlockSpec((1, tt * SUBLANES, LANES), lambda i, j: (i, j, 0)), tok(LANES), tok(LANES), tok(LANES),
                   pl.BlockSpec((1, LANES), lambda i, j: (0, 0))],
        out_shape=[tok_shape(d, F32), jax.ShapeDtypeStruct((b, t * SUBLANES, LANES), F32), tok_shape(LANES, I32),
                   tok_shape(LANES, F32), tok_shape(LANES, I32),
                   jax.ShapeDtypeStruct((1, LANES), F32)],
        scratch_shapes=[pltpu.VMEM((1, LANES), F32)],
        compiler_params=_cparams("arbitrary", "arbitrary"),
        name="out_proj_router",
    )(x, conv_out, att, mod, bc, ba, woc, woa, g1, b1, wr, br, cnt0)


def _tile_rows(x):
    return [x[:, j * LANES:(j + 1) * LANES] for j in range(x.shape[1] // LANES)]


def _store_tiled(ref, at, x):
    rows = x.shape[0]
    for j, piece in enumerate(_tile_rows(x)):
        ref[pl.ds(at + j, rows, stride=SUBLANES), :] = piece


def _load_tiled(ref, at, rows):
    return jnp.concatenate([ref[pl.ds(at + j, rows, stride=SUBLANES), :] for j in range(SUBLANES)], axis=1)


def _gather_pipeline(i, n, src_ref, cur_ref, nxt_ref, bufs, sem, rows, compute):
    def row_copy(src, buf, s, r):
        at = r * SUBLANES if isinstance(r, int) else pl.multiple_of(r * SUBLANES, SUBLANES)
        return pltpu.make_async_copy(src_ref.at[pl.ds(pl.multiple_of(src, SUBLANES), SUBLANES)],
                                     buf.at[pl.ds(at, SUBLANES)], sem.at[s])

    def whole(buf, s):
        return pltpu.make_async_copy(src_ref.at[pl.ds(0, rows * SUBLANES)], buf, sem.at[s])

    @pl.when(i == 0)
    def _():
        def body(r, carry):
            row_copy(cur_ref[0, 0, r], bufs[0], 0, r).start()
            return carry
        lax.fori_loop(0, rows, body, 0)

    def step(s):
        whole(bufs[s], s).wait()
        for r in range(rows):
            row_copy(nxt_ref[0, 0, r], bufs[1 - s], 1 - s, r).start(priority=r % 2)
        compute(bufs[s])

        @pl.when(i == n - 1)
        def _():
            whole(bufs[1 - s], 1 - s).wait()

    for s in range(2):
        pl.when(i % 2 == s)(functools.partial(step, s))


def _expert_kernel(be_ref, cur_ref, nxt_ref, u_ref, wgu_ref, bgu_ref, wdn_ref, bdn_ref, y_ref, xb0, xb1, sem, *, bm, d_ff):
    def compute(buf):
        gu = jnp.dot(_load_tiled(buf, 0, bm).astype(BF16), wgu_ref[0], preferred_element_type=F32) + bgu_ref[0]
        gl = jnp.minimum(gu[:, :d_ff], SWIGLU_LIMIT)
        lin = jnp.clip(gu[:, d_ff:], -SWIGLU_LIMIT, SWIGLU_LIMIT)
        act = gl * _sigmoid(SWIGLU_ALPHA * gl) * (lin + 1.0)
        _store_tiled(y_ref, 0, jnp.dot(act.astype(BF16), wdn_ref[0], preferred_element_type=F32) + bdn_ref[0])

    _gather_pipeline(pl.program_id(0), pl.num_programs(0), u_ref, cur_ref, nxt_ref, (xb0, xb1), sem, bm, compute)


def _expert_ffn(u2, slot_tok, blk_e, wgu, bgu, wdn, bdn, *, bm):
    n_blocks = slot_tok.shape[0]
    d = wgu.shape[1]
    assert d == SUBLANES * LANES
    d_ff = wdn.shape[1]
    slot_tok = slot_tok.reshape(n_blocks, 1, bm)
    smem = lambda f: pl.BlockSpec((1, 1, bm), f, memory_space=pltpu.SMEM)
    grid_spec = pltpu.PrefetchScalarGridSpec(
        num_scalar_prefetch=1, grid=(n_blocks,),
        in_specs=[smem(lambda i, be: (i, 0, 0)),
                  smem(lambda i, be: (jnp.minimum(i + 1, n_blocks - 1), 0, 0)),
                  pl.BlockSpec(memory_space=pl.ANY),
                  pl.BlockSpec((1, d, 2 * d_ff), lambda i, be: (be[i], 0, 0)),
                  pl.BlockSpec((1, 1, 2 * d_ff), lambda i, be: (be[i], 0, 0)),
                  pl.BlockSpec((1, d_ff, d), lambda i, be: (be[i], 0, 0)),
                  pl.BlockSpec((1, 1, d), lambda i, be: (be[i], 0, 0))],
        out_specs=pl.BlockSpec((bm * SUBLANES, LANES), lambda i, be: (i, 0)),
        scratch_shapes=[pltpu.VMEM((bm * SUBLANES, LANES), F32), pltpu.VMEM((bm * SUBLANES, LANES), F32),
                        pltpu.SemaphoreType.DMA((2,))])
    return pl.pallas_call(
        functools.partial(_expert_kernel, bm=bm, d_ff=d_ff),
        grid_spec=grid_spec,
        out_shape=jax.ShapeDtypeStruct((n_blocks * bm * SUBLANES, LANES), F32),
        compiler_params=_cparams("arbitrary"),
        name="expert_ffn",
    )(blk_e, slot_tok, slot_tok, u2, wgu, bgu, wdn, bdn)


def _combine_kernel(cur_ref, nxt_ref, ys_ref, w_ref, x1_ref, mod_ref, g2_ref, b2_ref, y_ref, gb0, gb1, sem,
                    *, tc, alpha, per_row):
    def compute(buf):
        w = w_ref[...]
        f = w[:, 0:1] * _load_tiled(buf, 0, tc)
        for k in range(1, TOP_K):
            f = f + w[:, k:k + 1] * _load_tiled(buf, k * tc * SUBLANES, tc)
        gate2 = mod_ref[5] if per_row else mod_ref[0, 5:6, :]
        y_ref[...] = _layer_norm(alpha * x1_ref[...] + gate2 * f, g2_ref[...], b2_ref[...])

    _gather_pipeline(pl.program_id(0), pl.num_programs(0), ys_ref, cur_ref, nxt_ref, (gb0, gb1), sem, TOP_K * tc, compute)


def _combine(ys, dest_t, top_w, x1, mod, g2, b2, *, tc, alpha, per_row, tiles_per_batch):
    n, d = x1.shape
    n_tiles = n // tc
    dest_t = dest_t.reshape(n_tiles, 1, TOP_K * tc)
    smem = lambda f: pl.BlockSpec((1, 1, TOP_K * tc), f, memory_space=pltpu.SMEM)
    const = lambda a: pl.BlockSpec(a.shape, lambda i: (0,) * a.ndim)
    mod_spec = (pl.BlockSpec((6, tc, d), lambda i: (0, i, 0)) if per_row
                else pl.BlockSpec((1, 6, d), lambda i: (i // tiles_per_batch, 0, 0)))
    return pl.pallas_call(
        functools.partial(_combine_kernel, tc=tc, alpha=alpha, per_row=per_row),
        grid=(n_tiles,),
        in_specs=[smem(lambda i: (i, 0, 0)), smem(lambda i: (jnp.minimum(i + 1, n_tiles - 1), 0, 0)),
                  pl.BlockSpec(memory_space=pl.ANY),
                  pl.BlockSpec((tc, LANES), lambda i: (i, 0)), pl.BlockSpec((tc, d), lambda i: (i, 0)),
                  mod_spec, const(g2), const(b2)],
        out_specs=pl.BlockSpec((tc, d), lambda i: (i, 0)),
        out_shape=jax.ShapeDtypeStruct((n, d), F32),
        scratch_shapes=[pltpu.VMEM((TOP_K * tc * SUBLANES, LANES), F32), pltpu.VMEM((TOP_K * tc * SUBLANES, LANES), F32),
                        pltpu.SemaphoreType.DMA((2,))],
        compiler_params=_cparams("arbitrary"),
        name="combine_ln2",
    )(dest_t, dest_t, ys, top_w, x1, mod, g2, b2)


def _moe(u2, x1, top_i, top_w, pos, counts, mod, experts, g2, b2, *, bm, tc, alpha, per_row, tiles_per_batch):
    n, d = x1.shape
    wgu, bgu, wdn, bdn = experts
    n_exp = wgu.shape[0]
    n_blocks = -(-n * TOP_K // bm) + n_exp
    counts = counts.astype(I32)
    padded = (counts + bm - 1) // bm * bm
    pad_end = jnp.cumsum(padded)
    pad_start = pad_end - padded
    dest = pad_start[top_i] + pos
    tok = jnp.broadcast_to(jnp.arange(n, dtype=I32)[:, None], (n, TOP_K))
    slot_tok = jnp.zeros((n_blocks * bm,), I32).at[dest.reshape(-1)].set(tok.reshape(-1), unique_indices=True,
                                                                        mode="promise_in_bounds")
    blk_start = jnp.arange(n_blocks, dtype=I32) * bm
    blk_e = jnp.minimum(jnp.sum((pad_end[None, :] <= blk_start[:, None]).astype(I32), axis=1), n_exp - 1)
    ys = _expert_ffn(u2, slot_tok.reshape(n_blocks, bm) * SUBLANES, blk_e, wgu, bgu, wdn, bdn, bm=bm)
    dest_t = dest.reshape(n // tc, tc, TOP_K).transpose(0, 2, 1).reshape(n // tc, TOP_K * tc) * SUBLANES
    return _combine(ys, dest_t, top_w, x1, mod, g2, b2, tc=tc, alpha=alpha, per_row=per_row, tiles_per_batch=tiles_per_batch)


def _pick_tile(n, target):
    t = min(n, target)
    while n % t:
        t -= 1
    return t


def kernel(x_prompt, x_sample, c_prompt, c_sample, cache_cmp_k, cache_cmp_v, cache_slc_k, cache_slc_v, state_win_k, state_win_v, state_conv, page_table, w_ada, b_ada, w_in, w_dw, b_dw, conv_ln_g, conv_ln_b, w_ck1, w_ck2, pe_k, w_cv1, w_cv2, pe_v, beta_conv, beta_attn, w_out, ln1_g, ln1_b, w_router, b_router, w_gate_up, b_gate_up, w_down, b_down, ln2_g, ln2_b):
    depth = w_ada.shape[0]
    assert depth == 1, "single-layer trunk"
    alpha = (2 * depth) ** 0.25
    bp, seq, d = x_prompt.shape
    bs, tq, _ = x_sample.shape
    c_conv = w_dw.shape[2]
    att_dim = N_HEADS * HEAD_DIM
    n_pool, page = cache_cmp_k.shape[1], cache_cmp_k.shape[2]
    assert state_conv.shape[2] == CONV_K - 1 and page % CMP_STRIDE == 0

    wi = w_in[0]
    cuts = [2 * c_conv, 2 * c_conv + att_dim, 2 * c_conv + att_dim + 6 * KV_DIM]
    wa = wi[:, :cuts[0]].astype(BF16)
    wq = wi[:, cuts[0]:cuts[1]].astype(BF16)
    wkv = wi[:, cuts[1]:cuts[2]].astype(BF16)
    wg = jnp.pad(wi[:, cuts[2]:], ((0, 0), (0, LANES - N_HEADS * N_GATES))).astype(BF16)
    in_w = (wa, wq, wkv, wg)
    wdw = jnp.pad(w_dw[0], ((0, CONV_HIST - CONV_K), (0, 0)))
    row = lambda a: a.reshape(1, -1)
    conv_vecs = (wdw, row(b_dw[0]), row(conv_ln_g[0]), row(conv_ln_b[0]))
    wk_c = _compress_weights(w_ck1[0], w_ck2[0], pe_k[0])
    wv_c = _compress_weights(w_cv1[0], w_cv2[0], pe_v[0])
    out_vecs = (row(beta_conv[0]), row(beta_attn[0]), row(ln1_g[0]), row(ln1_b[0]),
                jnp.pad(row(b_router[0]), ((0, 0), (0, LANES - N_EXPERTS)), constant_values=NEG_INF))
    out_w = (w_out[0][:c_conv].astype(BF16), w_out[0][c_conv:].astype(BF16),
             jnp.pad(w_router[0], ((0, 0), (0, LANES - N_EXPERTS))))
    experts = (w_gate_up[0].astype(BF16), b_gate_up[0][:, None, :], w_down[0].astype(BF16), b_down[0][:, None, :])
    g2, b2 = row(ln2_g[0]), row(ln2_b[0])

    mod = _ada(jnp.concatenate([c_prompt, c_sample], axis=0), w_ada[0], row(b_ada[0])).reshape(bp + bs, 6, d)
    mod_p = mod[:bp]
    mod_s = jnp.repeat(mod[bp:], tq, axis=0).transpose(1, 0, 2)

    tt = _pick_tile(seq, 512)
    hist0 = jnp.zeros((bp, CONV_HIST, c_conv), F32)
    (conv_p, tail_p, q_p, kc_p, vc_p, kct_p, vct_p, kst_p, vst_p, kwt_p, vwt_p, gates_p) = _in_proj(
        x_prompt, mod_p, in_w, (hist0,) + conv_vecs, tt=tt, per_row=False)
    n_chunk = seq // CMP_STRIDE
    ck_p, cv_p = _compress_prompt(kc_p.reshape(bp, n_chunk, CHUNK_ROW), vc_p.reshape(bp, n_chunk, CHUNK_ROW), wk_c, wv_c)
    att_p = _attn_prompt(q_p, gates_p, ck_p, cv_p, kst_p, vst_p, kwt_p, vwt_p)
    to = _pick_tile(seq, 256)
    cnt0 = jnp.zeros((1, LANES), F32)
    x1_p, u2_p, ti_p, tw_p, pos_p, cnt_p = _out_proj(x_prompt, conv_p, att_p, mod_p, out_vecs, out_w, cnt0,
                                                     tt=to, alpha=alpha, per_row=False)
    n_p = bp * seq
    bm_p = _pick_tile(n_p * TOP_K, 512)
    tc_p = _pick_tile(seq, 128)
    y_p = _moe(u2_p.reshape(n_p * SUBLANES, LANES), x1_p.reshape(n_p, d), ti_p.reshape(n_p, LANES)[:, :TOP_K], tw_p.reshape(n_p, LANES),
               pos_p.reshape(n_p, LANES)[:, :TOP_K], cnt_p[0, :N_EXPERTS], mod_p, experts, g2, b2,
               bm=bm_p, tc=tc_p, alpha=alpha, per_row=False, tiles_per_batch=seq // tc_p).reshape(bp, seq, d)

    n_s = bs * tq
    (glu_s, q_s, kc_s, vc_s, ks_s, vs_s, kw_s, vw_s, gates_s) = _in_proj(
        x_sample.reshape(1, n_s, d), mod_s, in_w, None, tt=n_s, per_row=True)
    unflat = lambda a: a.reshape(bs, tq, a.shape[-1])
    glu_s, q_s, kc_s, vc_s, ks_s, vs_s, kw_s, vw_s, gates_s = map(unflat, (glu_s, q_s, kc_s, vc_s, ks_s, vs_s, kw_s, vw_s, gates_s))
    xc_s = jnp.concatenate([jnp.zeros((bs, CONV_HIST - (CONV_K - 1), c_conv), F32), state_conv[0], glu_s], axis=1)
    conv_s = _conv_small(xc_s, *conv_vecs, tt=tq)
    dim_major = lambda a: a.transpose(0, 2, 3, 1).reshape(a.shape[0], KV_DIM, a.shape[1])
    ck_s, cv_s = _compress_decode(page_table, dim_major(cache_cmp_k[0]), dim_major(cache_cmp_v[0]), wk_c, wv_c)
    att_s = _attn_decode(page_table, q_s, gates_s, ck_s, cv_s, ks_s, vs_s, kw_s, vw_s, dim_major(state_win_k[0]),
                         dim_major(state_win_v[0]), dim_major(cache_slc_k[0]), dim_major(cache_slc_v[0]))
    flat = lambda a: a.reshape(1, n_s, a.shape[-1])
    x1_s, u2_s, ti_s, tw_s, pos_s, cnt_s = _out_proj(flat(x_sample), flat(conv_s), flat(att_s), mod_s, out_vecs, out_w, cnt0,
                                                     tt=n_s, alpha=alpha, per_row=True)
    bm_s = max(SUBLANES, min(512, n_s * TOP_K // N_EXPERTS))
    y_s = _moe(u2_s.reshape(n_s * SUBLANES, LANES), x1_s.reshape(n_s, d), ti_s.reshape(n_s, LANES)[:, :TOP_K], tw_s.reshape(n_s, LANES),
               pos_s.reshape(n_s, LANES)[:, :TOP_K], cnt_s[0, :N_EXPERTS], mod_s, experts, g2, b2,
               bm=bm_s, tc=n_s, alpha=alpha, per_row=True, tiles_per_batch=1).reshape(bs, tq, d)

    heads = lambda a: a.reshape(1, a.shape[0], a.shape[1], N_KV, HEAD_DIM)
    win_keep = min(WINDOW, seq)
    heads_t = lambda a: a.reshape(1, a.shape[0], N_KV, HEAD_DIM, a.shape[2]).transpose(0, 1, 4, 2, 3)
    new_p = (heads_t(kct_p), heads_t(vct_p), heads_t(kst_p), heads_t(vst_p), heads_t(kwt_p[:, :, seq - win_keep:]),
             heads_t(vwt_p[:, :, seq - win_keep:]), tail_p[None, :, CONV_HIST - (CONV_K - 1):])
    win_buf = state_win_k.shape[2]
    keep = lambda old, new: jnp.concatenate([old[0], new.reshape(bs, tq, N_KV, HEAD_DIM)], axis=1)[None, :, -win_buf:]
    new_s = (heads(kc_s), heads(vc_s), heads(ks_s), heads(vs_s), keep(state_win_k, kw_s), keep(state_win_v, vw_s),
             jnp.concatenate([state_conv[0], glu_s], axis=1)[None, :, -(CONV_K - 1):])
    return (y_p, y_s) + new_p + new_s
```

```python
import functools

import jax
import jax.numpy as jnp
from jax import lax
from jax.experimental import pallas as pl
from jax.experimental.pallas import tpu as pltpu

F32 = jnp.float32
BF16 = jnp.bfloat16
I32 = jnp.int32

N_HEADS = 8
HEAD_DIM = 64
N_KV = 2
Q_PER_KV = N_HEADS // N_KV
KV_DIM = N_KV * HEAD_DIM
CMP_LEN = 32
CMP_STRIDE = 16
CMP_HID = 2 * HEAD_DIM
SEL_LEN = 64
SEL_RATIO = SEL_LEN // CMP_STRIDE
N_SEL = 16
WINDOW = 512
Q_BLOCK = 128
N_GATES = 3
N_EXPERTS = 32
TOP_K = 4
CONV_K = 31
SWIGLU_LIMIT = 7.0
SWIGLU_ALPHA = 1.702
LN_EPS = 1e-5
NEG_INF = -1e30
FORCE = 1e9
SLOPES = tuple(2.0 ** (-8.0 * (h + 1) / N_HEADS) for h in range(N_HEADS))

LANES = 128
SUBLANES = 8
VMEM_LIMIT_BYTES = 56 * 1024 * 1024

CONV_HIST = 32
CHUNK_ROW = CMP_STRIDE * KV_DIM
SLC_CHUNK = 512
DEC_CHUNK = 2048
DEC_Q = 8
V_AUG = HEAD_DIM + 16
LOG2E = 1.4426950408889634
HIGHEST = lax.Precision.HIGHEST


def _cparams(*sem):
    return pltpu.CompilerParams(dimension_semantics=sem, vmem_limit_bytes=VMEM_LIMIT_BYTES)


def _sigmoid(x):
    return 1.0 / (1.0 + jnp.exp(-x))


def _bdot(a, b):
    return jnp.dot(a.astype(BF16), b.astype(BF16), preferred_element_type=F32)


def _bdot_nt(a, b):
    return lax.dot_general(a.astype(BF16), b.astype(BF16), (((1,), (1,)), ((), ())), preferred_element_type=F32)


def _layer_norm(x, g, b):
    mu = jnp.mean(x, -1, keepdims=True)
    xc = x - mu
    var = jnp.mean(xc * xc, -1, keepdims=True)
    return xc * lax.rsqrt(var + LN_EPS) * g + b


def _gelu_tanh(x):
    return 0.5 * x * (1.0 + jnp.tanh(0.7978845608028654 * (x + 0.044715 * (x * x * x))))


def _masked_softmax(s, mask):
    s = jnp.where(mask, s, NEG_INF)
    e = jnp.where(mask, jnp.exp(s - jnp.max(s, -1, keepdims=True)), 0.0)
    return e * (1.0 / jnp.maximum(jnp.sum(e, -1, keepdims=True), 1e-30))


def _ada_kernel(c_ref, w_ref, b_ref, o_ref):
    c = c_ref[...]
    o_ref[...] = _bdot(c * _sigmoid(c), w_ref[...]) + b_ref[...]


def _ada(c_all, w_ada, b_ada):
    nb, d = c_all.shape
    n = w_ada.shape[1]
    bn = d
    return pl.pallas_call(
        _ada_kernel,
        grid=(n // bn,),
        in_specs=[pl.BlockSpec((nb, d), lambda i: (0, 0)),
                  pl.BlockSpec((d, bn), lambda i: (0, i)),
                  pl.BlockSpec((1, bn), lambda i: (0, i))],
        out_specs=pl.BlockSpec((nb, bn), lambda i: (0, i)),
        out_shape=jax.ShapeDtypeStruct((nb, n), F32),
        compiler_params=_cparams("arbitrary"),
        name="ada",
    )(c_all, w_ada, b_ada)


def _mod_rows(mod_ref, i, per_row):
    return mod_ref[i] if per_row else mod_ref[0, i:i + 1, :]


def _conv_ln_silu(xc_ref, row0, n, wdw_ref, bdw, lng, lnb, shifted_ref=None):
    acc = jnp.zeros((n, xc_ref.shape[1]), F32) + bdw
    off = CONV_HIST - (CONV_K - 1)
    for k in range(CONV_K):
        at = row0 + off + k
        if shifted_ref is None or at % SUBLANES == 0:
            x = xc_ref[pl.ds(at, n), :]
        else:
            x = shifted_ref[at % SUBLANES - 1, pl.ds(at - at % SUBLANES, n), :]
        acc = acc + x * wdw_ref[k:k + 1, :]
    y = _layer_norm(acc, lng, lnb)
    return y * _sigmoid(y)


def _in_kernel(*refs, tt, c_conv, fuse_conv, per_row):
    if fuse_conv:
        (x_ref, mod_ref, wa_ref, wq_ref, wkv_ref, wg_ref, hist_ref, wdw_ref, bdw_ref, lng_ref, lnb_ref,
         conv_ref, tail_ref, q_ref, kc_ref, vc_ref, *kvt_refs, gates_ref, xc_ref, xs_ref) = refs
    else:
        (x_ref, mod_ref, wa_ref, wq_ref, wkv_ref, wg_ref,
         glu_ref, q_ref, kc_ref, vc_ref, ks_ref, vs_ref, kw_ref, vw_ref, gates_ref) = refs
    x = x_ref[0]
    shift1 = _mod_rows(mod_ref, 0, per_row)
    scale1 = _mod_rows(mod_ref, 1, per_row)
    u = (x * (1.0 + scale1) + shift1).astype(BF16)
    a = jnp.dot(u, wa_ref[...], preferred_element_type=F32)
    glu = a[:, :c_conv] * _sigmoid(a[:, c_conv:])
    q_ref[0] = jnp.dot(u, wq_ref[...], preferred_element_type=F32)
    kv = jnp.dot(u, wkv_ref[...], preferred_element_type=F32)
    gates_ref[0] = _sigmoid(jnp.dot(u, wg_ref[...], preferred_element_type=F32))
    if not fuse_conv:
        for i, r in enumerate((kc_ref, vc_ref, ks_ref, vs_ref, kw_ref, vw_ref)):
            r[0] = kv[:, i * KV_DIM:(i + 1) * KV_DIM]
        glu_ref[0] = glu
        return
    kc_ref[0] = kv[:, 0:KV_DIM]
    vc_ref[0] = kv[:, KV_DIM:2 * KV_DIM]
    for i, r in enumerate(kvt_refs):
        r[0] = jnp.transpose(kv[:, i * KV_DIM:(i + 1) * KV_DIM])

    @pl.when(pl.program_id(1) == 0)
    def _():
        xc_ref[0:CONV_HIST, :] = hist_ref[0]

    xc_ref[CONV_HIST:CONV_HIST + tt, :] = glu
    rows = min(tt, 64)
    assert rows % SUBLANES == 0
    keep = CONV_HIST + tt - SUBLANES
    for b in range(1, SUBLANES):
        xs_ref[b - 1, 0:keep, :] = xc_ref[b:b + keep, :]
    for r0 in range(0, tt, rows):
        conv_ref[0, r0:r0 + rows, :] = _conv_ln_silu(xc_ref, r0, rows, wdw_ref, bdw_ref[...], lng_ref[...], lnb_ref[...], xs_ref)
    tail = xc_ref[tt:tt + CONV_HIST, :]
    xc_ref[0:CONV_HIST, :] = tail
    tail_ref[0] = tail


def _in_proj(x, mod, wts, conv=None, *, tt, per_row):
    b, t, d = x.shape
    wa, wq, wkv, wg = wts
    c_conv = wa.shape[1] // 2
    fuse = conv is not None
    nt = t // tt
    const = lambda shape: pl.BlockSpec(shape, lambda i, j: (0,) * len(shape))
    mod_spec = (pl.BlockSpec((6, tt, d), lambda i, j: (0, j, 0)) if per_row
                else pl.BlockSpec((1, 6, d), lambda i, j: (i, 0, 0)))
    in_specs = [pl.BlockSpec((1, tt, d), lambda i, j: (i, j, 0)), mod_spec,
                const(wa.shape), const(wq.shape), const(wkv.shape), const(wg.shape)]
    args = [x, mod, wa, wq, wkv, wg]
    tok = lambda n: pl.BlockSpec((1, tt, n), lambda i, j: (i, j, 0))
    tok_shape = lambda n: jax.ShapeDtypeStruct((b, t, n), F32)
    out_specs, out_shape, scratch = [], [], []
    if fuse:
        hist, wdw, bdw, lng, lnb = conv
        in_specs += [pl.BlockSpec((1, CONV_HIST, c_conv), lambda i, j: (i, 0, 0)),
                     const(wdw.shape), const(bdw.shape), const(lng.shape), const(lnb.shape)]
        args += [hist, wdw, bdw, lng, lnb]
        out_specs += [tok(c_conv), pl.BlockSpec((1, CONV_HIST, c_conv), lambda i, j: (i, 0, 0))]
        out_shape += [tok_shape(c_conv), jax.ShapeDtypeStruct((b, CONV_HIST, c_conv), F32)]
        scratch = [pltpu.VMEM((CONV_HIST + tt, c_conv), F32), pltpu.VMEM((SUBLANES - 1, CONV_HIST + tt, c_conv), F32)]
    else:
        out_specs += [tok(c_conv)]
        out_shape += [tok_shape(c_conv)]
    if fuse:
        kv_specs = [tok(KV_DIM)] * 2 + [pl.BlockSpec((1, KV_DIM, tt), lambda i, j: (i, 0, j))] * 6
        kv_shape = [tok_shape(KV_DIM)] * 2 + [jax.ShapeDtypeStruct((b, KV_DIM, t), F32)] * 6
    else:
        kv_specs, kv_shape = [tok(KV_DIM)] * 6, [tok_shape(KV_DIM)] * 6
    out_specs += [tok(wq.shape[1])] + kv_specs + [tok(wg.shape[1])]
    out_shape += [tok_shape(wq.shape[1])] + kv_shape + [tok_shape(wg.shape[1])]
    return pl.pallas_call(
        functools.partial(_in_kernel, tt=tt, c_conv=c_conv, fuse_conv=fuse, per_row=per_row),
        grid=(b, nt), in_specs=in_specs, out_specs=out_specs, out_shape=out_shape, scratch_shapes=scratch,
        compiler_params=_cparams("arbitrary", "arbitrary"),
        name="in_proj_conv" if fuse else "in_proj",
    )(*args)


def _conv_small_kernel(xc_ref, wdw_ref, bdw_ref, lng_ref, lnb_ref, o_ref, *, tt):
    o_ref[0] = _conv_ln_silu(xc_ref.at[0], 0, tt, wdw_ref, bdw_ref[...], lng_ref[...], lnb_ref[...])


def _conv_small(xc, wdw, bdw, lng, lnb, *, tt):
    b, rows, c = xc.shape
    const = lambda shape: pl.BlockSpec(shape, lambda i: (0,) * len(shape))
    return pl.pallas_call(
        functools.partial(_conv_small_kernel, tt=tt),
        grid=(b,),
        in_specs=[pl.BlockSpec((1, rows, c), lambda i: (i, 0, 0)),
                  const(wdw.shape), const(bdw.shape), const(lng.shape), const(lnb.shape)],
        out_specs=pl.BlockSpec((1, tt, c), lambda i: (i, 0, 0)),
        out_shape=jax.ShapeDtypeStruct((b, tt, c), F32),
        compiler_params=_cparams("arbitrary"),
        name="conv_small",
    )(xc, wdw, bdw, lng, lnb)


def _compress_weights(w1, w2, pe):
    half = CMP_STRIDE * HEAD_DIM

    def expand(w):
        w = w.reshape(CMP_STRIDE, HEAD_DIM, CMP_HID)
        z = jnp.zeros_like(w)
        g0 = jnp.concatenate([w, z], axis=1).reshape(CHUNK_ROW, CMP_HID)
        g1 = jnp.concatenate([z, w], axis=1).reshape(CHUNK_ROW, CMP_HID)
        return jnp.concatenate([g0, g1], axis=1)

    w1p = jnp.concatenate([expand(w1[:half]), expand(w1[half:])], axis=1).astype(BF16)
    z2 = jnp.zeros_like(w2)
    w2p = jnp.concatenate([jnp.concatenate([w2, z2], axis=1), jnp.concatenate([z2, w2], axis=1)], axis=0).astype(BF16)
    return w1p, w1.astype(BF16), pe.reshape(1, -1), w2p


def _compress_tail(r, w1_ref, pe_ref, w2p_ref):
    n = r.shape[0]
    pe8 = jnp.broadcast_to(pe_ref[...], (SUBLANES, pe_ref.shape[1]))
    c = _bdot(pe8, w1_ref[...])[0:1]
    c2 = jnp.concatenate([c, c], axis=1)
    h = r[:, :2 * CMP_HID] + pltpu.roll(r[:, 2 * CMP_HID:], n - 1, 0) + c2
    return _bdot(_gelu_tanh(h), w2p_ref[...])


def _compress_prompt_kernel(ak_ref, av_ref, w1pk, w1k, pek, w2pk, w1pv, w1v, pev, w2pv, ck_ref, cv_ref):
    ck_ref[0] = _compress_tail(_bdot(ak_ref[0], w1pk[...]), w1k, pek, w2pk)
    cv_ref[0] = _compress_tail(_bdot(av_ref[0], w1pv[...]), w1v, pev, w2pv)


def _compress_prompt(ak, av, wk, wv):
    b, n, _ = ak.shape
    const = lambda a: pl.BlockSpec(a.shape, lambda i: (0,) * a.ndim)
    row = pl.BlockSpec((1, n, CHUNK_ROW), lambda i: (i, 0, 0))
    out = pl.BlockSpec((1, n, KV_DIM), lambda i: (i, 0, 0))
    return pl.pallas_call(
        _compress_prompt_kernel,
        grid=(b,),
        in_specs=[row, row] + [const(a) for a in wk] + [const(a) for a in wv],
        out_specs=[out, out],
        out_shape=[jax.ShapeDtypeStruct((b, n, KV_DIM), F32)] * 2,
        compiler_params=_cparams("arbitrary"),
        name="compress_prompt",
    )(ak, av, *wk, *wv)


def _compress_decode_kernel(pt_ref, pk_ref, pv_ref, w1pk, w1k, pek, w2pk, w1pv, w1v, pev, w2pv, ck_ref, cv_ref,
                            slab_k, slab_v, rows_buf, rbuf, sem, *, n_pages, page):
    j = pl.program_id(0)
    n_jobs = pl.num_programs(0)
    past = n_pages * page
    n = past // CMP_STRIDE
    step = min(n, 256)
    tcols = min(past, 2048)

    def fetch(job, pool_ref, slab, s, wait):
        def body(pg, carry):
            pid = pt_ref[(job // 2) * n_pages + pg]
            cp = pltpu.make_async_copy(pool_ref.at[pid], slab.at[:, pl.ds(pl.multiple_of(pg * page, page), page)], sem.at[s])
            cp.wait() if wait else cp.start()
            return carry
        lax.fori_loop(0, n_pages, body, 0)

    def project(slab, w1p):
        def to_rows(i, carry):
            at = pl.ds(pl.multiple_of(i * tcols, tcols), tcols)
            rows_buf[at, :] = jnp.transpose(slab[:, at])
            return carry
        lax.fori_loop(0, past // tcols, to_rows, 0)

        def body(i, carry):
            base = pl.multiple_of(i * (step * CMP_STRIDE), step * CMP_STRIDE)
            a = jnp.concatenate([rows_buf[pl.ds(base + p, step, stride=CMP_STRIDE), :] for p in range(CMP_STRIDE)], axis=1)
            rbuf[pl.ds(pl.multiple_of(i * step, step), step), :] = _bdot(a, w1p[...])
            return carry
        lax.fori_loop(0, n // step, body, 0)
        return rbuf[...]

    @pl.when(j == 0)
    def _():
        fetch(j, pk_ref, slab_k, 0, False)

    @pl.when(j % 2 == 0)
    def _():
        fetch(j + 1, pv_ref, slab_v, 1, False)
        fetch(j, pk_ref, slab_k, 0, True)
        ck_ref[0] = _compress_tail(project(slab_k, w1pk), w1k, pek, w2pk)

    @pl.when(j % 2 == 1)
    def _():
        @pl.when(j + 1 < n_jobs)
        def _():
            fetch(j + 1, pk_ref, slab_k, 0, False)
        fetch(j, pv_ref, slab_v, 1, True)
        cv_ref[0] = _compress_tail(project(slab_v, w1pv), w1v, pev, w2pv)


def _compress_decode(page_table, pool_k, pool_v, wk, wv):
    nb, n_pages = page_table.shape
    page = pool_k.shape[2]
    past = n_pages * page
    n = past // CMP_STRIDE
    assert page % LANES == 0
    const = lambda a: pl.BlockSpec(a.shape, lambda i, pt: (0,) * a.ndim)
    any_spec = pl.BlockSpec(memory_space=pl.ANY)
    out = pl.BlockSpec((1, n, KV_DIM), lambda i, pt: (i // 2, 0, 0))
    grid_spec = pltpu.PrefetchScalarGridSpec(
        num_scalar_prefetch=1, grid=(2 * nb,),
        in_specs=[any_spec, any_spec] + [const(a) for a in wk] + [const(a) for a in wv],
        out_specs=[out, out],
        scratch_shapes=[pltpu.VMEM((KV_DIM, past), F32), pltpu.VMEM((KV_DIM, past), F32), pltpu.VMEM((past, KV_DIM), F32),
                        pltpu.VMEM((n, 4 * CMP_HID), F32), pltpu.SemaphoreType.DMA((2,))])
    return pl.pallas_call(
        functools.partial(_compress_decode_kernel, n_pages=n_pages, page=page),
        grid_spec=grid_spec,
        out_shape=[jax.ShapeDtypeStruct((nb, n, KV_DIM), F32)] * 2,
        compiler_params=_cparams("arbitrary"),
        name="compress_decode",
    )(page_table.reshape(-1), pool_k, pool_v, *wk, *wv)


def _importance_matrix(n_cmp_rows, n_cmp, n_blk, cols):
    c = jnp.arange(n_cmp_rows)[:, None]
    b = jnp.arange(cols)[None, :]
    d = c - (SEL_RATIO * b - 1)
    return ((d >= 0) & (d <= SEL_RATIO) & (c < n_cmp) & (b < n_blk)).astype(F32)


def _expand_matrix(n_chunks, chunk, blk0):
    key = jnp.arange(n_chunks)[:, None, None] * chunk + jnp.arange(chunk)[None, None, :]
    blk = jnp.asarray(blk0)[:, None, None] + jnp.arange(LANES)[None, :, None]
    return (key // SEL_LEN == blk).astype(BF16)


def _aug_values(vt):
    r = lax.broadcasted_iota(I32, (V_AUG - HEAD_DIM, vt.shape[1]), 0)
    return jnp.concatenate([vt, jnp.where(r == 0, 1.0, 0.0)], axis=0).astype(BF16)


def _flash_step(qh, kt, vta, bias, m, acc):
    s = jnp.dot(qh, kt, preferred_element_type=F32) + bias
    m_new = jnp.maximum(m, jnp.max(s, -1, keepdims=True))
    p = jnp.exp2(s - m_new).astype(BF16)
    pv = lax.dot_general(p, vta, (((1,), (1,)), ((), ())), preferred_element_type=F32)
    return m_new, jnp.exp2(m - m_new) * acc + pv


def _flash_init(rows):
    return jnp.full((rows, 1), NEG_INF, F32), jnp.zeros((rows, V_AUG), F32)


def _flash_out(acc):
    return acc[:, :HEAD_DIM] * (1.0 / jnp.maximum(acc[:, HEAD_DIM:HEAD_DIM + 1], 1e-30))


def _attn_prompt_kernel(q_ref, g_ref, ck_ref, cv_ref, kst_ref, vst_ref, kwt_ref, vwt_ref, mimp_ref, esel_ref, o_ref,
                        *, seq, n_blk, n_pick, win_len):
    qb = Q_BLOCK
    qs = pl.program_id(1) * qb
    t_col = qs + lax.broadcasted_iota(I32, (qb, 1), 0)
    gates = g_ref[0]
    scale = HEAD_DIM ** -0.5
    q_heads = [q_ref[0, :, h * HEAD_DIM:(h + 1) * HEAD_DIM] for h in range(N_HEADS)]
    nc = ck_ref.shape[1]
    cmp_end = lax.broadcasted_iota(I32, (1, nc), 1) * CMP_STRIDE + (CMP_LEN - 1)
    dist_c = t_col - cmp_end
    mask_c = dist_c >= 0
    dist_cf = dist_c.astype(F32)
    blk_pad = mimp_ref.shape[1]

    blk_i = lax.broadcasted_iota(I32, (n_blk, qb), 0)
    t_row = qs + lax.broadcasted_iota(I32, (n_blk, qb), 1)
    cur = t_row // SEL_LEN
    forced = (blk_i == 0) | (blk_i == cur) | (blk_i == cur - 1)
    future = blk_i * SEL_LEN > t_row

    win_start = pl.multiple_of(jnp.clip(qs - WINDOW, 0, seq - win_len), qb)
    dist_w = t_col - (win_start + lax.broadcasted_iota(I32, (1, win_len), 1))
    maskbias_w = jnp.where((dist_w >= 0) & (dist_w <= WINDOW), 0.0, NEG_INF)
    n_chunks = (qs + qb + SLC_CHUNK - 1) // SLC_CHUNK
    rows = Q_PER_KV * qb
    stack = lambda parts: jnp.concatenate(parts, axis=0)
    heads_of = lambda a: a.reshape(Q_PER_KV, qb, a.shape[1])

    def flash(qk, alibi, maskbias, off, vta, m, acc):
        s = (heads_of(qk + alibi) + maskbias[None]).reshape(rows, qk.shape[1])
        m_new = jnp.maximum(m, jnp.max(s, -1, keepdims=True) + off)
        p = jnp.exp2(s - (m_new - off)).astype(BF16)
        pv = lax.dot_general(p, vta, (((1,), (1,)), ((), ())), preferred_element_type=F32)
        return m_new, jnp.exp2(m - m_new) * acc + pv

    lanes_of = lambda g: slice(g * HEAD_DIM, (g + 1) * HEAD_DIM)
    heads_in = lambda g: range(g * Q_PER_KV, (g + 1) * Q_PER_KV)
    qg2s, slope2s, sels, o_cmps = [], [], [], []
    for g in range(N_KV):
        lanes, heads = lanes_of(g), heads_in(g)
        slope = stack([jnp.full((qb, 1), SLOPES[h], F32) for h in heads])
        qg = stack([(q_heads[h] * scale).astype(BF16) for h in heads])
        qg2s.append(stack([(q_heads[h] * (scale * LOG2E)).astype(BF16) for h in heads]))
        slope2s.append(slope * LOG2E)
        s_c = heads_of(_bdot_nt(qg, ck_ref[0, :, lanes]) - slope * stack([dist_cf] * Q_PER_KV))
        p_c = _masked_softmax(s_c, mask_c[None])
        p_sum = jnp.sum(p_c, axis=0)
        o_cmp = heads_of(_bdot(p_c.reshape(rows, nc), cv_ref[0, :, lanes]))
        imp = jnp.dot(p_sum, mimp_ref[...], precision=HIGHEST, preferred_element_type=F32)
        imp_t = jnp.transpose(imp)[0:n_blk, :]
        imp_t = jnp.where(future, -FORCE, jnp.where(forced, FORCE, imp_t))
        cnt = jnp.zeros((n_blk, qb), F32)
        for b2 in range(n_blk):
            row = imp_t[b2:b2 + 1, :]
            ahead = (row > imp_t) | ((row == imp_t) & (blk_i > b2))
            cnt = cnt + jnp.where(ahead, 1.0, 0.0)
        sel_t = jnp.where(cnt < n_pick, 1.0, 0.0)
        if n_blk < blk_pad:
            sel_t = jnp.concatenate([sel_t, jnp.zeros((blk_pad - n_blk, qb), F32)], axis=0)
        sels.append(jnp.transpose(sel_t).astype(BF16))
        o_cmps.append(o_cmp)

    key_slc = lax.broadcasted_iota(I32, (1, SLC_CHUNK), 1)
    key_win = lax.broadcasted_iota(I32, (1, win_len), 1).astype(F32)
    alibi_slc = [s2 * key_slc.astype(F32) for s2 in slope2s]

    def slc_step(c, carry):
        k0 = pl.multiple_of(c * SLC_CHUNK, SLC_CHUNK)
        causal = t_col - (k0 + key_slc) >= 0
        out = []
        for g in range(N_KV):
            kt = kst_ref[0, lanes_of(g), pl.ds(k0, SLC_CHUNK)].astype(BF16)
            vta = _aug_values(vst_ref[0, lanes_of(g), pl.ds(k0, SLC_CHUNK)])
            msel = jnp.dot(sels[g], esel_ref[c], preferred_element_type=F32)
            maskbias = jnp.where(causal & (msel > 0.5), 0.0, NEG_INF)
            qk = jnp.dot(qg2s[g], kt, preferred_element_type=F32)
            out += flash(qk, alibi_slc[g], maskbias, slope2s[g] * k0.astype(F32), vta, *carry[2 * g:2 * g + 2])
        return tuple(out)

    fin = lax.fori_loop(0, n_chunks, slc_step, _flash_init(rows) * N_KV)
    for g in range(N_KV):
        o_slc = heads_of(_flash_out(fin[2 * g + 1]))
        kwt = kwt_ref[0, lanes_of(g), pl.ds(win_start, win_len)].astype(BF16)
        vwta = _aug_values(vwt_ref[0, lanes_of(g), pl.ds(win_start, win_len)])
        qk = jnp.dot(qg2s[g], kwt, preferred_element_type=F32)
        o_win = heads_of(_flash_out(flash(qk, slope2s[g] * key_win, maskbias_w, 0.0, vwta, *_flash_init(rows))[1]))
        for i, h in enumerate(heads_in(g)):
            c0 = h * N_GATES
            o_ref[0, :, h * HEAD_DIM:(h + 1) * HEAD_DIM] = (gates[:, c0:c0 + 1] * o_cmps[g][i] + gates[:, c0 + 1:c0 + 2] * o_slc[i]
                                                           + gates[:, c0 + 2:c0 + 3] * o_win[i])


def _attn_prompt(q, gates, ck, cv, ks, vs, kw, vw):
    b, t, att = q.shape
    assert t % SLC_CHUNK == 0 and t % Q_BLOCK == 0
    n_blk = t // SEL_LEN
    n_cmp = t // CMP_STRIDE - 1
    assert n_blk <= LANES and ck.shape[1] == t // CMP_STRIDE
    n_pick = min(N_SEL, n_blk)
    win_len = min(WINDOW + Q_BLOCK, t)
    mimp = _importance_matrix(ck.shape[1], n_cmp, n_blk, LANES)
    n_ch = t // SLC_CHUNK
    esel = _expand_matrix(n_ch, SLC_CHUNK, [0] * n_ch)
    tok = lambda n: pl.BlockSpec((1, Q_BLOCK, n), lambda i, j: (i, j, 0))
    full = lambda a: pl.BlockSpec((1,) + a.shape[1:], lambda i, j: (i, 0, 0))
    const = lambda a: pl.BlockSpec(a.shape, lambda i, j: (0,) * a.ndim)
    return pl.pallas_call(
        functools.partial(_attn_prompt_kernel, seq=t, n_blk=n_blk, n_pick=n_pick, win_len=win_len),
        grid=(b, t // Q_BLOCK),
        in_specs=[tok(att), tok(gates.shape[2]), full(ck), full(cv), full(ks), full(vs), full(kw), full(vw),
                  const(mimp), const(esel)],
        out_specs=tok(att),
        out_shape=jax.ShapeDtypeStruct((b, t, att), F32),
        compiler_params=_cparams("arbitrary", "arbitrary"),
        name="attn_prompt",
    )(q, gates, ck, cv, ks, vs, kw, vw, mimp, esel)


def _attn_decode_kernel(pt_ref, q_ref, g_ref, ck_ref, cv_ref, ksn_ref, vsn_ref, kwn_ref, vwn_ref, wk_ref, wv_ref,
                        pk_ref, pv_ref, mimp_ref, esel_ref, o_ref, slab_k, slab_v, sem,
                        *, n_pages, page, past, n_blk, n_pick, tq):
    b = pl.program_id(0)
    slot = b % 2

    def fetch(bb, s, wait):
        def body(pg, carry):
            pid = pt_ref[bb * n_pages + pg]
            dst = pl.ds(pl.multiple_of(pg * page, page), page)
            for pool, slab, i in ((pk_ref, slab_k, 0), (pv_ref, slab_v, 1)):
                cp = pltpu.make_async_copy(pool.at[pid], slab.at[s, :, dst], sem.at[s, i])
                cp.wait() if wait else cp.start()
            return carry
        lax.fori_loop(0, n_pages, body, 0)

    @pl.when(b == 0)
    def _():
        fetch(b, slot, False)

    @pl.when(b + 1 < pl.num_programs(0))
    def _():
        fetch(b + 1, 1 - slot, False)

    rows = Q_PER_KV * DEC_Q
    qi = lax.broadcasted_iota(I32, (rows, 1), 0) % DEC_Q
    t_col = past + qi % tq
    hl = lax.broadcasted_iota(I32, (rows, 1), 0) // DEC_Q
    nc = ck_ref.shape[1]
    cmp_end = lax.broadcasted_iota(I32, (1, nc), 1) * CMP_STRIDE + (CMP_LEN - 1)
    dist_c = t_col - cmp_end
    mask_c = dist_c >= 0
    dist_cf = dist_c.astype(F32)
    blk_pad = mimp_ref.shape[1]
    t8 = past + lax.broadcasted_iota(I32, (DEC_Q, 1), 0) % tq
    blk_l = lax.broadcasted_iota(I32, (DEC_Q, blk_pad), 1)
    cur = t8 // SEL_LEN
    forced = (blk_l == 0) | (blk_l == cur) | (blk_l == cur - 1)
    future = blk_l * SEL_LEN > t8
    pad_blk = blk_l >= n_blk
    b_sub = lax.broadcasted_iota(I32, (blk_pad, blk_pad), 0)
    b_lane = lax.broadcasted_iota(I32, (blk_pad, blk_pad), 1)
    n_chunks = past // DEC_CHUNK
    new_rows = ksn_ref.shape[2]

    sels, slopes, o_cmps = [], [], []
    for g in range(N_KV):
        lanes = slice(g * HEAD_DIM, (g + 1) * HEAD_DIM)
        slope = jnp.zeros((rows, 1), F32)
        for i in range(Q_PER_KV):
            slope = jnp.where(hl == i, SLOPES[g * Q_PER_KV + i], slope)
        slopes.append(slope)
        qg = (q_ref[0, g] * (HEAD_DIM ** -0.5)).astype(BF16)
        p = _masked_softmax(_bdot_nt(qg, ck_ref[0, :, lanes]) - slope * dist_cf, mask_c)
        o_cmps.append(_bdot(p, cv_ref[0, :, lanes]))
        p_sum = p[0:DEC_Q]
        for i in range(1, Q_PER_KV):
            p_sum = p_sum + p[i * DEC_Q:(i + 1) * DEC_Q]
        imp = jnp.dot(p_sum, mimp_ref[...], precision=HIGHEST, preferred_element_type=F32)
        imp = jnp.where(pad_blk, -3e38, jnp.where(future, -FORCE, jnp.where(forced, FORCE, imp)))
        imp_t = jnp.transpose(jnp.concatenate([imp, jnp.zeros((LANES - DEC_Q, blk_pad), F32)], axis=0))
        sel_rows = []
        for r in range(DEC_Q):
            col = imp_t[:, r:r + 1]
            row = imp[r:r + 1, :]
            ahead = (col > row) | ((col == row) & (b_sub < b_lane))
            cnt = jnp.sum(jnp.where(ahead, 1.0, 0.0), axis=0, keepdims=True)
            sel_rows.append(jnp.where(cnt < n_pick, 1.0, 0.0))
        sel8 = jnp.concatenate(sel_rows, axis=0)
        sels.append(jnp.concatenate([sel8] * Q_PER_KV, axis=0))

    fetch(b, slot, True)

    for g in range(N_KV):
        lanes = slice(g * HEAD_DIM, (g + 1) * HEAD_DIM)
        slope2 = slopes[g] * LOG2E
        qg2 = (q_ref[0, g] * (HEAD_DIM ** -0.5 * LOG2E)).astype(BF16)
        sel = sels[g]
        sel_b = sel.astype(BF16)

        def bias_of(dist, visible):
            return jnp.where(visible, 0.0, NEG_INF) - slope2 * dist.astype(F32)

        m, acc = _flash_init(rows)
        per_mat = LANES * SEL_LEN // DEC_CHUNK
        for c in range(n_chunks):
            keys = slice(c * DEC_CHUNK, (c + 1) * DEC_CHUNK)
            lane0 = (c // per_mat) * LANES
            msel = jnp.dot(sel_b[:, lane0:lane0 + LANES], esel_ref[c % per_mat], preferred_element_type=F32)
            dist = t_col - (c * DEC_CHUNK + lax.broadcasted_iota(I32, (1, DEC_CHUNK), 1))
            m, acc = _flash_step(qg2, slab_k[slot, lanes, keys].astype(BF16), _aug_values(slab_v[slot, lanes, keys]),
                                 bias_of(dist, (dist >= 0) & (msel > 0.5)), m, acc)
        dist_n = t_col - (past + lax.broadcasted_iota(I32, (1, new_rows), 1))
        cur_blk = past // SEL_LEN
        m, acc = _flash_step(qg2, ksn_ref[0, lanes, :].astype(BF16), _aug_values(vsn_ref[0, lanes, :]),
                             bias_of(dist_n, (dist_n >= 0) & (sel[:, cur_blk:cur_blk + 1] > 0.5)), m, acc)
        o_slc = _flash_out(acc)
        wlen = wk_ref.shape[2]
        dist = t_col - (past - wlen + lax.broadcasted_iota(I32, (1, wlen), 1))
        m, acc = _flash_step(qg2, wk_ref[0, lanes, :].astype(BF16), _aug_values(wv_ref[0, lanes, :]),
                             bias_of(dist, (dist >= 0) & (dist <= WINDOW)), *_flash_init(rows))
        m, acc = _flash_step(qg2, kwn_ref[0, lanes, :].astype(BF16), _aug_values(vwn_ref[0, lanes, :]),
                             bias_of(dist_n, (dist_n >= 0) & (dist_n <= WINDOW)), m, acc)
        o_win = _flash_out(acc)
        gt = g_ref[0, g]
        o_ref[0, g] = gt[:, 0:1] * o_cmps[g] + gt[:, 1:2] * o_slc + gt[:, 2:3] * o_win


def _attn_decode(page_table, q, gates, ck, cv, ksn, vsn, kwn, vwn, win_k, win_v, pool_k, pool_v):
    nb, tq, att = q.shape
    n_pages = page_table.shape[1]
    page = pool_k.shape[2]
    past = n_pages * page
    assert tq <= DEC_Q and past % DEC_CHUNK == 0 and past % SEL_LEN == 0 and (past + tq - 1) // SEL_LEN == past // SEL_LEN
    assert win_k.shape[2] == WINDOW and ck.shape[1] * CMP_STRIDE == past and page % LANES == 0
    t_pad = -(-(past + tq) // SEL_LEN) * SEL_LEN
    n_blk = t_pad // SEL_LEN
    n_cmp = t_pad // CMP_STRIDE - 1
    n_pick = min(N_SEL, n_blk)
    blk_pad = -(-n_blk // LANES) * LANES
    mimp = _importance_matrix(ck.shape[1], n_cmp, n_blk, blk_pad)
    per_mat = LANES * SEL_LEN // DEC_CHUNK
    esel = _expand_matrix(per_mat, DEC_CHUNK, [0] * per_mat)

    def dec_rows(a, width):
        a = a.reshape(nb, tq, N_KV, Q_PER_KV, width)
        a = jnp.take(a, jnp.arange(DEC_Q) % tq, axis=1)
        return a.transpose(0, 2, 3, 1, 4).reshape(nb, N_KV, Q_PER_KV * DEC_Q, width)

    q_d = dec_rows(q, HEAD_DIM)
    g_d = dec_rows(gates[:, :, :N_HEADS * N_GATES], N_GATES)
    pad_new = lambda a: jnp.pad(a.transpose(0, 2, 1), ((0, 0), (0, 0), (0, LANES - tq)))
    ksn, vsn, kwn, vwn = [pad_new(a) for a in (ksn, vsn, kwn, vwn)]
    rows = Q_PER_KV * DEC_Q
    blk = lambda a: pl.BlockSpec((1,) + a.shape[1:], lambda i, pt: (i,) + (0,) * (a.ndim - 1))
    const = lambda a: pl.BlockSpec(a.shape, lambda i, pt: (0,) * a.ndim)
    any_spec = pl.BlockSpec(memory_space=pl.ANY)
    grid_spec = pltpu.PrefetchScalarGridSpec(
        num_scalar_prefetch=1, grid=(nb,),
        in_specs=[blk(q_d), blk(g_d), blk(ck), blk(cv), blk(ksn), blk(vsn), blk(kwn), blk(vwn), blk(win_k), blk(win_v),
                  any_spec, any_spec, const(mimp), const(esel)],
        out_specs=pl.BlockSpec((1, N_KV, rows, HEAD_DIM), lambda i, pt: (i, 0, 0, 0)),
        scratch_shapes=[pltpu.VMEM((2, KV_DIM, past), F32), pltpu.VMEM((2, KV_DIM, past), F32),
                        pltpu.SemaphoreType.DMA((2, 2))])
    o = pl.pallas_call(
        functools.partial(_attn_decode_kernel, n_pages=n_pages, page=page, past=past, n_blk=n_blk, n_pick=n_pick, tq=tq),
        grid_spec=grid_spec,
        out_shape=jax.ShapeDtypeStruct((nb, N_KV, rows, HEAD_DIM), F32),
        compiler_params=_cparams("arbitrary"),
        name="attn_decode",
    )(page_table.reshape(-1), q_d, g_d, ck, cv, ksn, vsn, kwn, vwn, win_k, win_v, pool_k, pool_v, mimp, esel)
    o = o.reshape(nb, N_KV, Q_PER_KV, DEC_Q, HEAD_DIM)[:, :, :, :tq]
    return o.transpose(0, 3, 1, 2, 4).reshape(nb, tq, att)


def _out_kernel(x_ref, conv_ref, att_ref, mod_ref, bc_ref, ba_ref, woc_ref, woa_ref, g1_ref, b1_ref, wr_ref, br_ref,
                cnt0_ref, x1_ref, u2_ref, idx_ref, w_ref, pos_ref, cnt_ref, carry_ref, *, tt, alpha, per_row):
    first = (pl.program_id(0) == 0) & (pl.program_id(1) == 0)

    @pl.when(first)
    def _():
        carry_ref[...] = cnt0_ref[...]

    gate1 = _mod_rows(mod_ref, 2, per_row)
    shift2 = _mod_rows(mod_ref, 3, per_row)
    scale2 = _mod_rows(mod_ref, 4, per_row)
    mix = _bdot(conv_ref[0] * bc_ref[...], woc_ref[...]) + _bdot(att_ref[0] * ba_ref[...], woa_ref[...])
    x1 = _layer_norm(alpha * x_ref[0] + gate1 * mix, g1_ref[...], b1_ref[...])
    x1_ref[0] = x1
    u2 = x1 * (1.0 + scale2) + shift2
    _store_tiled(u2_ref, 0, u2)
    logits = jnp.dot(u2, wr_ref[...], precision=HIGHEST, preferred_element_type=F32) + br_ref[...]
    lane = lax.broadcasted_iota(I32, logits.shape, 1)
    vals = logits
    top_v, top_i = [], []
    for _k in range(TOP_K):
        m = jnp.max(vals, axis=-1, keepdims=True)
        i = jnp.min(jnp.where(vals == m, lane, LANES), axis=-1, keepdims=True)
        top_v.append(m)
        top_i.append(i)
        vals = jnp.where(lane == i, -jnp.inf, vals)
    e = [jnp.exp(v - top_v[0]) for v in top_v]
    inv = 1.0 / (e[0] + e[1] + e[2] + e[3])
    onehot = [jnp.where(lane == i, 1.0, 0.0) for i in top_i]
    assigned = onehot[0] + onehot[1] + onehot[2] + onehot[3]
    r_i = lax.broadcasted_iota(I32, (tt, tt), 0)
    c_i = lax.broadcasted_iota(I32, (tt, tt), 1)
    before = jnp.where(r_i > c_i, 1.0, 0.0).astype(BF16)
    rank = carry_ref[...] + jnp.dot(before, assigned.astype(BF16), preferred_element_type=F32)
    idx_o = jnp.zeros(logits.shape, I32)
    w_o = jnp.zeros(logits.shape, F32)
    pos_o = jnp.zeros(logits.shape, F32)
    for k in range(TOP_K):
        idx_o = jnp.where(lane == k, top_i[k], idx_o)
        w_o = jnp.where(lane == k, e[k] * inv, w_o)
        pos_o = jnp.where(lane == k, jnp.sum(onehot[k] * rank, axis=-1, keepdims=True), pos_o)
    idx_ref[0] = idx_o
    w_ref[0] = w_o
    pos_ref[0] = pos_o.astype(I32)
    carry_ref[...] = carry_ref[...] + jnp.sum(assigned, axis=0, keepdims=True)
    cnt_ref[...] = carry_ref[...]


def _out_proj(x, conv_out, att, mod, vecs, wts, cnt0, *, tt, alpha, per_row):
    b, t, d = x.shape
    bc, ba, g1, b1, br = vecs
    woc, woa, wr = wts
    nt = t // tt
    const = lambda a: pl.BlockSpec(a.shape, lambda i, j: (0,) * a.ndim)
    tok = lambda n: pl.BlockSpec((1, tt, n), lambda i, j: (i, j, 0))
    mod_spec = (pl.BlockSpec((6, tt, d), lambda i, j: (0, j, 0)) if per_row
                else pl.BlockSpec((1, 6, d), lambda i, j: (i, 0, 0)))
    tok_shape = lambda n, dt: jax.ShapeDtypeStruct((b, t, n), dt)
    return pl.pallas_call(
        functools.partial(_out_kernel, tt=tt, alpha=alpha, per_row=per_row),
        grid=(b, nt),
        in_specs=[tok(d), tok(conv_out.shape[2]), tok(att.shape[2]), mod_spec, const(bc), const(ba), const(woc), const(woa),
                  const(g1), const(b1), const(wr), const(br), const(cnt0)],
        out_specs=[tok(d), pl.BlockSpec((tt * SUBLANES, LANES), lambda i, j: (i * nt + j, 0)),
                   tok(LANES), tok(LANES), tok(LANES), pl.BlockSpec((1, LANES), lambda i, j: (0, 0))],
        out_shape=[tok_shape(d, F32), jax.ShapeDtypeStruct((b * t * SUBLANES, LANES), F32), tok_shape(LANES, I32),
                   tok_shape(LANES, F32), tok_shape(LANES, I32),
                   jax.ShapeDtypeStruct((1, LANES), F32)],
        scratch_shapes=[pltpu.VMEM((1, LANES), F32)],
        compiler_params=_cparams("arbitrary", "arbitrary"),
        name="out_proj_router",
    )(x, conv_out, att, mod, bc, ba, woc, woa, g1, b1, wr, br, cnt0)


def _tile_rows(x):
    return [x[:, j * LANES:(j + 1) * LANES] for j in range(x.shape[1] // LANES)]


def _store_tiled(ref, at, x):
    rows = x.shape[0]
    for j, piece in enumerate(_tile_rows(x)):
        ref[pl.ds(at + j, rows, stride=SUBLANES), :] = piece


def _load_tiled(ref, at, rows):
    return jnp.concatenate([ref[pl.ds(at + j, rows, stride=SUBLANES), :] for j in range(SUBLANES)], axis=1)


def _gather_pipeline(i, n, src_ref, cur_ref, nxt_ref, bufs, sem, rows, compute):
    def row_copy(src, buf, s, r):
        at = r * SUBLANES if isinstance(r, int) else pl.multiple_of(r * SUBLANES, SUBLANES)
        return pltpu.make_async_copy(src_ref.at[pl.ds(pl.multiple_of(src, SUBLANES), SUBLANES)],
                                     buf.at[pl.ds(at, SUBLANES)], sem.at[s])

    def whole(buf, s):
        return pltpu.make_async_copy(src_ref.at[pl.ds(0, rows * SUBLANES)], buf, sem.at[s])

    @pl.when(i == 0)
    def _():
        def body(r, carry):
            row_copy(cur_ref[0, 0, r], bufs[0], 0, r).start()
            return carry
        lax.fori_loop(0, rows, body, 0)

    def step(s):
        whole(bufs[s], s).wait()
        for r in range(rows):
            row_copy(nxt_ref[0, 0, r], bufs[1 - s], 1 - s, r).start(priority=r % 2)
        compute(bufs[s])

        @pl.when(i == n - 1)
        def _():
            whole(bufs[1 - s], 1 - s).wait()

    for s in range(2):
        pl.when(i % 2 == s)(functools.partial(step, s))


def _expert_kernel(be_ref, cur_ref, nxt_ref, u_ref, wgu_ref, bgu_ref, wdn_ref, bdn_ref, y_ref, xb0, xb1, sem, *, bm, d_ff):
    def compute(buf):
        gu = jnp.dot(_load_tiled(buf, 0, bm).astype(BF16), wgu_ref[0].astype(BF16), preferred_element_type=F32) + bgu_ref[0]
        gl = jnp.minimum(gu[:, :d_ff], SWIGLU_LIMIT)
        lin = jnp.clip(gu[:, d_ff:], -SWIGLU_LIMIT, SWIGLU_LIMIT)
        act = gl * _sigmoid(SWIGLU_ALPHA * gl) * (lin + 1.0)
        _store_tiled(y_ref, 0, jnp.dot(act.astype(BF16), wdn_ref[0].astype(BF16), preferred_element_type=F32) + bdn_ref[0])

    _gather_pipeline(pl.program_id(0), pl.num_programs(0), u_ref, cur_ref, nxt_ref, (xb0, xb1), sem, bm, compute)


def _expert_ffn(u2, slot_tok, blk_e, wgu, bgu, wdn, bdn, *, bm):
    n_blocks = slot_tok.shape[0]
    d = wgu.shape[1]
    assert d == SUBLANES * LANES
    d_ff = wdn.shape[1]
    slot_tok = slot_tok.reshape(n_blocks, 1, bm)
    smem = lambda f: pl.BlockSpec((1, 1, bm), f, memory_space=pltpu.SMEM)
    grid_spec = pltpu.PrefetchScalarGridSpec(
        num_scalar_prefetch=1, grid=(n_blocks,),
        in_specs=[smem(lambda i, be: (i, 0, 0)),
                  smem(lambda i, be: (jnp.minimum(i + 1, n_blocks - 1), 0, 0)),
                  pl.BlockSpec(memory_space=pl.ANY),
                  pl.BlockSpec((1, d, 2 * d_ff), lambda i, be: (be[i], 0, 0)),
                  pl.BlockSpec((1, 1, 2 * d_ff), lambda i, be: (be[i], 0, 0)),
                  pl.BlockSpec((1, d_ff, d), lambda i, be: (be[i], 0, 0)),
                  pl.BlockSpec((1, 1, d), lambda i, be: (be[i], 0, 0))],
        out_specs=pl.BlockSpec((bm * SUBLANES, LANES), lambda i, be: (i, 0)),
        scratch_shapes=[pltpu.VMEM((bm * SUBLANES, LANES), F32), pltpu.VMEM((bm * SUBLANES, LANES), F32),
                        pltpu.SemaphoreType.DMA((2,))])
    return pl.pallas_call(
        functools.partial(_expert_kernel, bm=bm, d_ff=d_ff),
        grid_spec=grid_spec,
        out_shape=jax.ShapeDtypeStruct((n_blocks * bm * SUBLANES, LANES), F32),
        compiler_params=_cparams("arbitrary"),
        name="expert_ffn",
    )(blk_e, slot_tok, slot_tok, u2, wgu, bgu, wdn, bdn)


def _combine_kernel(cur_ref, nxt_ref, ys_ref, w_ref, x1_ref, mod_ref, g2_ref, b2_ref, y_ref, gb0, gb1, sem,
                    *, tc, alpha, per_row):
    def compute(buf):
        w = w_ref[...]
        f = w[:, 0:1] * _load_tiled(buf, 0, tc)
        for k in range(1, TOP_K):
            f = f + w[:, k:k + 1] * _load_tiled(buf, k * tc * SUBLANES, tc)
        gate2 = mod_ref[5] if per_row else mod_ref[0, 5:6, :]
        y_ref[...] = _layer_norm(alpha * x1_ref[...] + gate2 * f, g2_ref[...], b2_ref[...])

    _gather_pipeline(pl.program_id(0), pl.num_programs(0), ys_ref, cur_ref, nxt_ref, (gb0, gb1), sem, TOP_K * tc, compute)


def _combine(ys, dest_t, top_w, x1, mod, g2, b2, *, tc, alpha, per_row, tiles_per_batch):
    n, d = x1.shape
    n_tiles = n // tc
    dest_t = dest_t.reshape(n_tiles, 1, TOP_K * tc)
    smem = lambda f: pl.BlockSpec((1, 1, TOP_K * tc), f, memory_space=pltpu.SMEM)
    const = lambda a: pl.BlockSpec(a.shape, lambda i: (0,) * a.ndim)
    mod_spec = (pl.BlockSpec((6, tc, d), lambda i: (0, i, 0)) if per_row
                else pl.BlockSpec((1, 6, d), lambda i: (i // tiles_per_batch, 0, 0)))
    return pl.pallas_call(
        functools.partial(_combine_kernel, tc=tc, alpha=alpha, per_row=per_row),
        grid=(n_tiles,),
        in_specs=[smem(lambda i: (i, 0, 0)), smem(lambda i: (jnp.minimum(i + 1, n_tiles - 1), 0, 0)),
                  pl.BlockSpec(memory_space=pl.ANY),
                  pl.BlockSpec((tc, LANES), lambda i: (i, 0)), pl.BlockSpec((tc, d), lambda i: (i, 0)),
                  mod_spec, const(g2), const(b2)],
        out_specs=pl.BlockSpec((tc, d), lambda i: (i, 0)),
        out_shape=jax.ShapeDtypeStruct((n, d), F32),
        scratch_shapes=[pltpu.VMEM((TOP_K * tc * SUBLANES, LANES), F32), pltpu.VMEM((TOP_K * tc * SUBLANES, LANES), F32),
                        pltpu.SemaphoreType.DMA((2,))],
        compiler_params=_cparams("arbitrary"),
        name="combine_ln2",
    )(dest_t, dest_t, ys, top_w, x1, mod, g2, b2)


def _moe_experts(u2, top_i, pos, counts, experts, *, bm):
    n = top_i.shape[0]
    wgu, bgu, wdn, bdn = experts
    n_exp = wgu.shape[0]
    n_blocks = -(-n * TOP_K // bm) + n_exp
    counts = counts.astype(I32)
    padded = (counts + bm - 1) // bm * bm
    pad_end = jnp.cumsum(padded)
    pad_start = pad_end - padded
    dest = pad_start[top_i] + pos
    tok_row = jnp.broadcast_to((jnp.arange(n, dtype=I32) * SUBLANES)[:, None], (n, TOP_K))
    slot_tok = jnp.zeros((n_blocks * bm,), I32).at[dest.reshape(-1)].set(tok_row.reshape(-1), unique_indices=True,
                                                                        mode="promise_in_bounds")
    blk_start = jnp.arange(n_blocks, dtype=I32) * bm
    blk_e = jnp.minimum(jnp.sum((pad_end[None, :] <= blk_start[:, None]).astype(I32), axis=1), n_exp - 1)
    ys = _expert_ffn(u2, slot_tok.reshape(n_blocks, bm), blk_e, wgu, bgu, wdn, bdn, bm=bm)
    return ys, dest * SUBLANES


def _moe_combine(ys, dest, top_w, x1, mod, g2, b2, *, tc, alpha, per_row, tiles_per_batch):
    n = x1.shape[0]
    dest_t = dest.reshape(n // tc, tc, TOP_K).transpose(0, 2, 1).reshape(n // tc, TOP_K * tc)
    return _combine(ys, dest_t, top_w, x1, mod, g2, b2, tc=tc, alpha=alpha, per_row=per_row, tiles_per_batch=tiles_per_batch)


def _pick_tile(n, target):
    t = min(n, target)
    while n % t:
        t -= 1
    return t


def kernel(x_prompt, x_sample, c_prompt, c_sample, cache_cmp_k, cache_cmp_v, cache_slc_k, cache_slc_v, state_win_k, state_win_v, state_conv, page_table, w_ada, b_ada, w_in, w_dw, b_dw, conv_ln_g, conv_ln_b, w_ck1, w_ck2, pe_k, w_cv1, w_cv2, pe_v, beta_conv, beta_attn, w_out, ln1_g, ln1_b, w_router, b_router, w_gate_up, b_gate_up, w_down, b_down, ln2_g, ln2_b):
    depth = w_ada.shape[0]
    assert depth == 1, "single-layer trunk"
    alpha = (2 * depth) ** 0.25
    bp, seq, d = x_prompt.shape
    bs, tq, _ = x_sample.shape
    c_conv = w_dw.shape[2]
    att_dim = N_HEADS * HEAD_DIM
    n_pool, page = cache_cmp_k.shape[1], cache_cmp_k.shape[2]
    assert state_conv.shape[2] == CONV_K - 1 and page % CMP_STRIDE == 0

    wi = w_in[0]
    cuts = [2 * c_conv, 2 * c_conv + att_dim, 2 * c_conv + att_dim + 6 * KV_DIM]
    wa = wi[:, :cuts[0]].astype(BF16)
    wq = wi[:, cuts[0]:cuts[1]].astype(BF16)
    wkv = wi[:, cuts[1]:cuts[2]].astype(BF16)
    wg = jnp.pad(wi[:, cuts[2]:], ((0, 0), (0, LANES - N_HEADS * N_GATES))).astype(BF16)
    in_w = (wa, wq, wkv, wg)
    wdw = jnp.pad(w_dw[0], ((0, CONV_HIST - CONV_K), (0, 0)))
    row = lambda a: a.reshape(1, -1)
    conv_vecs = (wdw, row(b_dw[0]), row(conv_ln_g[0]), row(conv_ln_b[0]))
    wk_c = _compress_weights(w_ck1[0], w_ck2[0], pe_k[0])
    wv_c = _compress_weights(w_cv1[0], w_cv2[0], pe_v[0])
    out_vecs = (row(beta_conv[0]), row(beta_attn[0]), row(ln1_g[0]), row(ln1_b[0]),
                jnp.pad(row(b_router[0]), ((0, 0), (0, LANES - N_EXPERTS)), constant_values=NEG_INF))
    out_w = (w_out[0][:c_conv].astype(BF16), w_out[0][c_conv:].astype(BF16),
             jnp.pad(w_router[0], ((0, 0), (0, LANES - N_EXPERTS))))
    experts = (w_gate_up[0], b_gate_up[0][:, None, :], w_down[0], b_down[0][:, None, :])
    g2, b2 = row(ln2_g[0]), row(ln2_b[0])

    mod = _ada(jnp.concatenate([c_prompt, c_sample], axis=0), w_ada[0], row(b_ada[0])).reshape(bp + bs, 6, d)
    mod_p = mod[:bp]
    mod_s = jnp.repeat(mod[bp:], tq, axis=0).transpose(1, 0, 2)

    tt = _pick_tile(seq, 512)
    hist0 = jnp.zeros((bp, CONV_HIST, c_conv), F32)
    (conv_p, tail_p, q_p, kc_p, vc_p, kct_p, vct_p, kst_p, vst_p, kwt_p, vwt_p, gates_p) = _in_proj(
        x_prompt, mod_p, in_w, (hist0,) + conv_vecs, tt=tt, per_row=False)
    n_chunk = seq // CMP_STRIDE
    ck_p, cv_p = _compress_prompt(kc_p.reshape(bp, n_chunk, CHUNK_ROW), vc_p.reshape(bp, n_chunk, CHUNK_ROW), wk_c, wv_c)
    att_p = _attn_prompt(q_p, gates_p, ck_p, cv_p, kst_p, vst_p, kwt_p, vwt_p)
    to = _pick_tile(seq, 256)
    cnt0 = jnp.zeros((1, LANES), F32)
    n_p, n_s = bp * seq, bs * tq
    x1_p, u2_p, ti_p, tw_p, pos_p, cnt_p = _out_proj(x_prompt, conv_p, att_p, mod_p, out_vecs, out_w, cnt0,
                                                     tt=to, alpha=alpha, per_row=False)
    picks = lambda a: a.reshape(-1, LANES)[:, :TOP_K]
    ys_p, dest_p = _moe_experts(u2_p, picks(ti_p), picks(pos_p), cnt_p[0, :N_EXPERTS], experts, bm=_pick_tile(n_p * TOP_K, 512))
    tc_p = _pick_tile(seq, 128)
    y_p = _moe_combine(ys_p, dest_p, tw_p.reshape(n_p, LANES), x1_p.reshape(n_p, d), mod_p, g2, b2,
                       tc=tc_p, alpha=alpha, per_row=False, tiles_per_batch=seq // tc_p).reshape(bp, seq, d)

    (glu_s, q_s, kc_s, vc_s, ks_s, vs_s, kw_s, vw_s, gates_s) = _in_proj(
        x_sample.reshape(1, n_s, d), mod_s, in_w, None, tt=n_s, per_row=True)
    unflat = lambda a: a.reshape(bs, tq, a.shape[-1])
    glu_s, q_s, kc_s, vc_s, ks_s, vs_s, kw_s, vw_s, gates_s = map(unflat, (glu_s, q_s, kc_s, vc_s, ks_s, vs_s, kw_s, vw_s, gates_s))
    xc_s = jnp.concatenate([jnp.zeros((bs, CONV_HIST - (CONV_K - 1), c_conv), F32), state_conv[0], glu_s], axis=1)
    conv_s = _conv_small(xc_s, *conv_vecs, tt=tq)
    dim_major = lambda a: a.transpose(0, 2, 3, 1).reshape(a.shape[0], KV_DIM, a.shape[1])
    ck_s, cv_s = _compress_decode(page_table, dim_major(cache_cmp_k[0]), dim_major(cache_cmp_v[0]), wk_c, wv_c)
    att_s = _attn_decode(page_table, q_s, gates_s, ck_s, cv_s, ks_s, vs_s, kw_s, vw_s, dim_major(state_win_k[0]),
                         dim_major(state_win_v[0]), dim_major(cache_slc_k[0]), dim_major(cache_slc_v[0]))
    flat = lambda a: a.reshape(1, n_s, a.shape[-1])
    x1_s, u2_s, ti_s, tw_s, pos_s, cnt_s = _out_proj(flat(x_sample), flat(conv_s), flat(att_s), mod_s, out_vecs, out_w, cnt0,
                                                     tt=n_s, alpha=alpha, per_row=True)
    bm_s = max(2 * SUBLANES, min(512, n_s * TOP_K // N_EXPERTS // SUBLANES * SUBLANES))
    ys_s, dest_s = _moe_experts(u2_s, picks(ti_s), picks(pos_s), cnt_s[0, :N_EXPERTS], experts, bm=bm_s)
    y_s = _moe_combine(ys_s, dest_s, tw_s.reshape(n_s, LANES), x1_s.reshape(n_s, d), mod_s, g2, b2,
                       tc=n_s, alpha=alpha, per_row=True, tiles_per_batch=1).reshape(bs, tq, d)

    heads = lambda a: a.reshape(1, a.shape[0], a.shape[1], N_KV, HEAD_DIM)
    win_keep = min(WINDOW, seq)
    heads_t = lambda a: a.reshape(1, a.shape[0], N_KV, HEAD_DIM, a.shape[2]).transpose(0, 1, 4, 2, 3)
    new_p = (heads_t(kct_p), heads_t(vct_p), heads_t(kst_p), heads_t(vst_p), heads_t(kwt_p[:, :, seq - win_keep:]),
             heads_t(vwt_p[:, :, seq - win_keep:]), tail_p[None, :, CONV_HIST - (CONV_K - 1):])
    win_buf = state_win_k.shape[2]
    keep = lambda old, new: jnp.concatenate([old[0], new.reshape(bs, tq, N_KV, HEAD_DIM)], axis=1)[None, :, -win_buf:]
    new_s = (heads(kc_s), heads(vc_s), heads(ks_s), heads(vs_s), keep(state_win_k, kw_s), keep(state_win_v, vw_s),
             jnp.concatenate([state_conv[0], glu_s], axis=1)[None, :, -(CONV_K - 1):])
    return (y_p, y_s) + new_p + new_s
```

```python
import functools

import jax
import jax.numpy as jnp
from jax import lax
from jax.experimental import pallas as pl
from jax.experimental.pallas import tpu as pltpu

F32 = jnp.float32
BF16 = jnp.bfloat16
I32 = jnp.int32

N_HEADS = 8
HEAD_DIM = 64
N_KV = 2
Q_PER_KV = N_HEADS // N_KV
KV_DIM = N_KV * HEAD_DIM
CMP_LEN = 32
CMP_STRIDE = 16
CMP_HID = 2 * HEAD_DIM
SEL_LEN = 64
SEL_RATIO = SEL_LEN // CMP_STRIDE
N_SEL = 16
WINDOW = 512
Q_BLOCK = 128
N_GATES = 3
N_EXPERTS = 32
TOP_K = 4
CONV_K = 31
SWIGLU_LIMIT = 7.0
SWIGLU_ALPHA = 1.702
LN_EPS = 1e-5
NEG_INF = -1e30
FORCE = 1e9
SLOPES = tuple(2.0 ** (-8.0 * (h + 1) / N_HEADS) for h in range(N_HEADS))

LANES = 128
SUBLANES = 8
VMEM_LIMIT_BYTES = 56 * 1024 * 1024

CONV_HIST = 32
CHUNK_ROW = CMP_STRIDE * KV_DIM
SLC_CHUNK = 512
DEC_CHUNK = 2048
DEC_Q = 8
V_AUG = HEAD_DIM + 16
LOG2E = 1.4426950408889634
HIGHEST = lax.Precision.HIGHEST


def _cparams(*sem):
    return pltpu.CompilerParams(dimension_semantics=sem, vmem_limit_bytes=VMEM_LIMIT_BYTES)


def _sigmoid(x):
    return 1.0 / (1.0 + jnp.exp(-x))


def _bdot(a, b):
    return jnp.dot(a.astype(BF16), b.astype(BF16), preferred_element_type=F32)


def _bdot_nt(a, b):
    return lax.dot_general(a.astype(BF16), b.astype(BF16), (((1,), (1,)), ((), ())), preferred_element_type=F32)


def _layer_norm(x, g, b):
    mu = jnp.mean(x, -1, keepdims=True)
    xc = x - mu
    var = jnp.mean(xc * xc, -1, keepdims=True)
    return xc * lax.rsqrt(var + LN_EPS) * g + b


def _gelu_tanh(x):
    return 0.5 * x * (1.0 + jnp.tanh(0.7978845608028654 * (x + 0.044715 * (x * x * x))))


def _masked_softmax(s, mask):
    s = jnp.where(mask, s, NEG_INF)
    e = jnp.where(mask, jnp.exp(s - jnp.max(s, -1, keepdims=True)), 0.0)
    return e * (1.0 / jnp.maximum(jnp.sum(e, -1, keepdims=True), 1e-30))


def _ada_kernel(c_ref, w_ref, b_ref, o_ref):
    c = c_ref[...]
    o_ref[...] = _bdot(c * _sigmoid(c), w_ref[...]) + b_ref[...]


def _ada(c_all, w_ada, b_ada):
    nb, d = c_all.shape
    n = w_ada.shape[1]
    bn = d
    return pl.pallas_call(
        _ada_kernel,
        grid=(n // bn,),
        in_specs=[pl.BlockSpec((nb, d), lambda i: (0, 0)),
                  pl.BlockSpec((d, bn), lambda i: (0, i)),
                  pl.BlockSpec((1, bn), lambda i: (0, i))],
        out_specs=pl.BlockSpec((nb, bn), lambda i: (0, i)),
        out_shape=jax.ShapeDtypeStruct((nb, n), F32),
        compiler_params=_cparams("arbitrary"),
        name="ada",
    )(c_all, w_ada, b_ada)


def _mod_rows(mod_ref, i, per_row):
    return mod_ref[i] if per_row else mod_ref[0, i:i + 1, :]


def _conv_ln_silu(xc_ref, row0, n, wdw_ref, bdw, lng, lnb, shifted_ref=None):
    acc = jnp.zeros((n, xc_ref.shape[1]), F32) + bdw
    off = CONV_HIST - (CONV_K - 1)
    for k in range(CONV_K):
        at = row0 + off + k
        if shifted_ref is None or at % SUBLANES == 0:
            x = xc_ref[pl.ds(at, n), :]
        else:
            x = shifted_ref[at % SUBLANES - 1, pl.ds(at - at % SUBLANES, n), :]
        acc = acc + x * wdw_ref[k:k + 1, :]
    y = _layer_norm(acc, lng, lnb)
    return y * _sigmoid(y)


def _in_kernel(*refs, tt, c_conv, fuse_conv, per_row):
    if fuse_conv:
        (x_ref, mod_ref, wa_ref, wq_ref, wkv_ref, wg_ref, hist_ref, wdw_ref, bdw_ref, lng_ref, lnb_ref,
         conv_ref, tail_ref, q_ref, kc_ref, vc_ref, *kvt_refs, gates_ref, xc_ref, xs_ref) = refs
    else:
        (x_ref, mod_ref, wa_ref, wq_ref, wkv_ref, wg_ref,
         glu_ref, q_ref, kc_ref, vc_ref, ks_ref, vs_ref, kw_ref, vw_ref, gates_ref) = refs
    x = x_ref[0]
    shift1 = _mod_rows(mod_ref, 0, per_row)
    scale1 = _mod_rows(mod_ref, 1, per_row)
    u = (x * (1.0 + scale1) + shift1).astype(BF16)
    a = jnp.dot(u, wa_ref[...], preferred_element_type=F32)
    glu = a[:, :c_conv] * _sigmoid(a[:, c_conv:])
    q_ref[0] = jnp.dot(u, wq_ref[...], preferred_element_type=F32)
    kv = jnp.dot(u, wkv_ref[...], preferred_element_type=F32)
    gates_ref[0] = _sigmoid(jnp.dot(u, wg_ref[...], preferred_element_type=F32))
    if not fuse_conv:
        for i, r in enumerate((kc_ref, vc_ref, ks_ref, vs_ref, kw_ref, vw_ref)):
            r[0] = kv[:, i * KV_DIM:(i + 1) * KV_DIM]
        glu_ref[0] = glu
        return
    kc_ref[0] = kv[:, 0:KV_DIM]
    vc_ref[0] = kv[:, KV_DIM:2 * KV_DIM]
    for i, r in enumerate(kvt_refs):
        r[0] = jnp.transpose(kv[:, i * KV_DIM:(i + 1) * KV_DIM])

    @pl.when(pl.program_id(1) == 0)
    def _():
        xc_ref[0:CONV_HIST, :] = hist_ref[0]

    xc_ref[CONV_HIST:CONV_HIST + tt, :] = glu
    rows = min(tt, 64)
    assert rows % SUBLANES == 0
    keep = CONV_HIST + tt - SUBLANES
    for b in range(1, SUBLANES):
        xs_ref[b - 1, 0:keep, :] = xc_ref[b:b + keep, :]
    for r0 in range(0, tt, rows):
        conv_ref[0, r0:r0 + rows, :] = _conv_ln_silu(xc_ref, r0, rows, wdw_ref, bdw_ref[...], lng_ref[...], lnb_ref[...], xs_ref)
    tail = xc_ref[tt:tt + CONV_HIST, :]
    xc_ref[0:CONV_HIST, :] = tail
    tail_ref[0] = tail


def _in_proj(x, mod, wts, conv=None, *, tt, per_row):
    b, t, d = x.shape
    wa, wq, wkv, wg = wts
    c_conv = wa.shape[1] // 2
    fuse = conv is not None
    nt = t // tt
    const = lambda shape: pl.BlockSpec(shape, lambda i, j: (0,) * len(shape))
    mod_spec = (pl.BlockSpec((6, tt, d), lambda i, j: (0, j, 0)) if per_row
                else pl.BlockSpec((1, 6, d), lambda i, j: (i, 0, 0)))
    in_specs = [pl.BlockSpec((1, tt, d), lambda i, j: (i, j, 0)), mod_spec,
                const(wa.shape), const(wq.shape), const(wkv.shape), const(wg.shape)]
    args = [x, mod, wa, wq, wkv, wg]
    tok = lambda n: pl.BlockSpec((1, tt, n), lambda i, j: (i, j, 0))
    tok_shape = lambda n: jax.ShapeDtypeStruct((b, t, n), F32)
    out_specs, out_shape, scratch = [], [], []
    if fuse:
        hist, wdw, bdw, lng, lnb = conv
        in_specs += [pl.BlockSpec((1, CONV_HIST, c_conv), lambda i, j: (i, 0, 0)),
                     const(wdw.shape), const(bdw.shape), const(lng.shape), const(lnb.shape)]
        args += [hist, wdw, bdw, lng, lnb]
        out_specs += [tok(c_conv), pl.BlockSpec((1, CONV_HIST, c_conv), lambda i, j: (i, 0, 0))]
        out_shape += [tok_shape(c_conv), jax.ShapeDtypeStruct((b, CONV_HIST, c_conv), F32)]
        scratch = [pltpu.VMEM((CONV_HIST + tt, c_conv), F32), pltpu.VMEM((SUBLANES - 1, CONV_HIST + tt, c_conv), F32)]
    else:
        out_specs += [tok(c_conv)]
        out_shape += [tok_shape(c_conv)]
    if fuse:
        kv_specs = [tok(KV_DIM)] * 2 + [pl.BlockSpec((1, KV_DIM, tt), lambda i, j: (i, 0, j))] * 6
        kv_shape = [tok_shape(KV_DIM)] * 2 + [jax.ShapeDtypeStruct((b, KV_DIM, t), F32)] * 6
    else:
        kv_specs, kv_shape = [tok(KV_DIM)] * 6, [tok_shape(KV_DIM)] * 6
    out_specs += [tok(wq.shape[1])] + kv_specs + [tok(wg.shape[1])]
    out_shape += [tok_shape(wq.shape[1])] + kv_shape + [tok_shape(wg.shape[1])]
    return pl.pallas_call(
        functools.partial(_in_kernel, tt=tt, c_conv=c_conv, fuse_conv=fuse, per_row=per_row),
        grid=(b, nt), in_specs=in_specs, out_specs=out_specs, out_shape=out_shape, scratch_shapes=scratch,
        compiler_params=_cparams("arbitrary", "arbitrary"),
        name="in_proj_conv" if fuse else "in_proj",
    )(*args)


def _conv_small_kernel(xc_ref, wdw_ref, bdw_ref, lng_ref, lnb_ref, o_ref, *, tt):
    o_ref[0] = _conv_ln_silu(xc_ref.at[0], 0, tt, wdw_ref, bdw_ref[...], lng_ref[...], lnb_ref[...])


def _conv_small(xc, wdw, bdw, lng, lnb, *, tt):
    b, rows, c = xc.shape
    const = lambda shape: pl.BlockSpec(shape, lambda i: (0,) * len(shape))
    return pl.pallas_call(
        functools.partial(_conv_small_kernel, tt=tt),
        grid=(b,),
        in_specs=[pl.BlockSpec((1, rows, c), lambda i: (i, 0, 0)),
                  const(wdw.shape), const(bdw.shape), const(lng.shape), const(lnb.shape)],
        out_specs=pl.BlockSpec((1, tt, c), lambda i: (i, 0, 0)),
        out_shape=jax.ShapeDtypeStruct((b, tt, c), F32),
        compiler_params=_cparams("arbitrary"),
        name="conv_small",
    )(xc, wdw, bdw, lng, lnb)


def _compress_weights(w1, w2, pe):
    half = CMP_STRIDE * HEAD_DIM

    def expand(w):
        w = w.reshape(CMP_STRIDE, HEAD_DIM, CMP_HID)
        z = jnp.zeros_like(w)
        g0 = jnp.concatenate([w, z], axis=1).reshape(CHUNK_ROW, CMP_HID)
        g1 = jnp.concatenate([z, w], axis=1).reshape(CHUNK_ROW, CMP_HID)
        return jnp.concatenate([g0, g1], axis=1)

    w1p = jnp.concatenate([expand(w1[:half]), expand(w1[half:])], axis=1).astype(BF16)
    z2 = jnp.zeros_like(w2)
    w2p = jnp.concatenate([jnp.concatenate([w2, z2], axis=1), jnp.concatenate([z2, w2], axis=1)], axis=0).astype(BF16)
    return w1p, w1.astype(BF16), pe.reshape(1, -1), w2p


def _compress_tail(r, w1_ref, pe_ref, w2p_ref):
    n = r.shape[0]
    pe8 = jnp.broadcast_to(pe_ref[...], (SUBLANES, pe_ref.shape[1]))
    c = _bdot(pe8, w1_ref[...])[0:1]
    c2 = jnp.concatenate([c, c], axis=1)
    h = r[:, :2 * CMP_HID] + pltpu.roll(r[:, 2 * CMP_HID:], n - 1, 0) + c2
    return _bdot(_gelu_tanh(h), w2p_ref[...])


def _compress_prompt_kernel(ak_ref, av_ref, w1pk, w1k, pek, w2pk, w1pv, w1v, pev, w2pv, ck_ref, cv_ref):
    ck_ref[0] = _compress_tail(_bdot(ak_ref[0], w1pk[...]), w1k, pek, w2pk)
    cv_ref[0] = _compress_tail(_bdot(av_ref[0], w1pv[...]), w1v, pev, w2pv)


def _compress_prompt(ak, av, wk, wv):
    b, n, _ = ak.shape
    const = lambda a: pl.BlockSpec(a.shape, lambda i: (0,) * a.ndim)
    row = pl.BlockSpec((1, n, CHUNK_ROW), lambda i: (i, 0, 0))
    out = pl.BlockSpec((1, n, KV_DIM), lambda i: (i, 0, 0))
    return pl.pallas_call(
        _compress_prompt_kernel,
        grid=(b,),
        in_specs=[row, row] + [const(a) for a in wk] + [const(a) for a in wv],
        out_specs=[out, out],
        out_shape=[jax.ShapeDtypeStruct((b, n, KV_DIM), F32)] * 2,
        compiler_params=_cparams("arbitrary"),
        name="compress_prompt",
    )(ak, av, *wk, *wv)


def _compress_decode_kernel(pt_ref, pk_ref, pv_ref, w1pk, w1k, pek, w2pk, w1pv, w1v, pev, w2pv, ck_ref, cv_ref,
                            slab_k, slab_v, rows_buf, rbuf, sem, *, n_pages, page):
    j = pl.program_id(0)
    n_jobs = pl.num_programs(0)
    past = n_pages * page
    n = past // CMP_STRIDE
    step = min(n, 256)
    tcols = min(past, 2048)

    def fetch(job, pool_ref, slab, s, wait):
        def body(pg, carry):
            pid = pt_ref[(job // 2) * n_pages + pg]
            cp = pltpu.make_async_copy(pool_ref.at[pid], slab.at[:, pl.ds(pl.multiple_of(pg * page, page), page)], sem.at[s])
            cp.wait() if wait else cp.start()
            return carry
        lax.fori_loop(0, n_pages, body, 0)

    def project(slab, w1p):
        def to_rows(i, carry):
            at = pl.ds(pl.multiple_of(i * tcols, tcols), tcols)
            rows_buf[at, :] = jnp.transpose(slab[:, at])
            return carry
        lax.fori_loop(0, past // tcols, to_rows, 0)

        def body(i, carry):
            base = pl.multiple_of(i * (step * CMP_STRIDE), step * CMP_STRIDE)
            a = jnp.concatenate([rows_buf[pl.ds(base + p, step, stride=CMP_STRIDE), :] for p in range(CMP_STRIDE)], axis=1)
            rbuf[pl.ds(pl.multiple_of(i * step, step), step), :] = _bdot(a, w1p[...])
            return carry
        lax.fori_loop(0, n // step, body, 0)
        return rbuf[...]

    @pl.when(j == 0)
    def _():
        fetch(j, pk_ref, slab_k, 0, False)

    @pl.when(j % 2 == 0)
    def _():
        fetch(j + 1, pv_ref, slab_v, 1, False)
        fetch(j, pk_ref, slab_k, 0, True)
        ck_ref[0] = _compress_tail(project(slab_k, w1pk), w1k, pek, w2pk)

    @pl.when(j % 2 == 1)
    def _():
        @pl.when(j + 1 < n_jobs)
        def _():
            fetch(j + 1, pk_ref, slab_k, 0, False)
        fetch(j, pv_ref, slab_v, 1, True)
        cv_ref[0] = _compress_tail(project(slab_v, w1pv), w1v, pev, w2pv)


def _compress_decode(page_table, pool_k, pool_v, wk, wv):
    nb, n_pages = page_table.shape
    page = pool_k.shape[2]
    past = n_pages * page
    n = past // CMP_STRIDE
    assert page % LANES == 0
    const = lambda a: pl.BlockSpec(a.shape, lambda i, pt: (0,) * a.ndim)
    any_spec = pl.BlockSpec(memory_space=pl.ANY)
    out = pl.BlockSpec((1, n, KV_DIM), lambda i, pt: (i // 2, 0, 0))
    grid_spec = pltpu.PrefetchScalarGridSpec(
        num_scalar_prefetch=1, grid=(2 * nb,),
        in_specs=[any_spec, any_spec] + [const(a) for a in wk] + [const(a) for a in wv],
        out_specs=[out, out],
        scratch_shapes=[pltpu.VMEM((KV_DIM, past), F32), pltpu.VMEM((KV_DIM, past), F32), pltpu.VMEM((past, KV_DIM), F32),
                        pltpu.VMEM((n, 4 * CMP_HID), F32), pltpu.SemaphoreType.DMA((2,))])
    return pl.pallas_call(
        functools.partial(_compress_decode_kernel, n_pages=n_pages, page=page),
        grid_spec=grid_spec,
        out_shape=[jax.ShapeDtypeStruct((nb, n, KV_DIM), F32)] * 2,
        compiler_params=_cparams("arbitrary"),
        name="compress_decode",
    )(page_table.reshape(-1), pool_k, pool_v, *wk, *wv)


def _importance_matrix(n_cmp_rows, n_cmp, n_blk, cols):
    c = jnp.arange(n_cmp_rows)[:, None]
    b = jnp.arange(cols)[None, :]
    d = c - (SEL_RATIO * b - 1)
    return ((d >= 0) & (d <= SEL_RATIO) & (c < n_cmp) & (b < n_blk)).astype(F32)


def _expand_matrix(n_chunks, chunk, blk0):
    key = jnp.arange(n_chunks)[:, None, None] * chunk + jnp.arange(chunk)[None, None, :]
    blk = jnp.asarray(blk0)[:, None, None] + jnp.arange(LANES)[None, :, None]
    return (key // SEL_LEN == blk).astype(BF16)


def _aug_values(vt):
    r = lax.broadcasted_iota(I32, (V_AUG - HEAD_DIM, vt.shape[1]), 0)
    return jnp.concatenate([vt, jnp.where(r == 0, 1.0, 0.0)], axis=0).astype(BF16)


def _flash_step(qh, kt, vta, bias, m, acc):
    s = jnp.dot(qh, kt, preferred_element_type=F32) + bias
    m_new = jnp.maximum(m, jnp.max(s, -1, keepdims=True))
    p = jnp.exp2(s - m_new).astype(BF16)
    pv = lax.dot_general(p, vta, (((1,), (1,)), ((), ())), preferred_element_type=F32)
    return m_new, jnp.exp2(m - m_new) * acc + pv


def _flash_init(rows):
    return jnp.full((rows, 1), NEG_INF, F32), jnp.zeros((rows, V_AUG), F32)


def _flash_out(acc):
    return acc[:, :HEAD_DIM] * (1.0 / jnp.maximum(acc[:, HEAD_DIM:HEAD_DIM + 1], 1e-30))


def _attn_prompt_kernel(q_ref, g_ref, ck_ref, cv_ref, kst_ref, vst_ref, kwt_ref, vwt_ref, mimp_ref, esel_ref, o_ref,
                        *, seq, n_blk, n_pick, win_len):
    qb = Q_BLOCK
    qs = pl.program_id(1) * qb
    t_col = qs + lax.broadcasted_iota(I32, (qb, 1), 0)
    gates = g_ref[0]
    scale = HEAD_DIM ** -0.5
    q_heads = [q_ref[0, :, h * HEAD_DIM:(h + 1) * HEAD_DIM] for h in range(N_HEADS)]
    nc = ck_ref.shape[1]
    cmp_end = lax.broadcasted_iota(I32, (1, nc), 1) * CMP_STRIDE + (CMP_LEN - 1)
    dist_c = t_col - cmp_end
    mask_c = dist_c >= 0
    dist_cf = dist_c.astype(F32)
    blk_pad = mimp_ref.shape[1]

    blk_i = lax.broadcasted_iota(I32, (n_blk, qb), 0)
    t_row = qs + lax.broadcasted_iota(I32, (n_blk, qb), 1)
    cur = t_row // SEL_LEN
    forced = (blk_i == 0) | (blk_i == cur) | (blk_i == cur - 1)
    future = blk_i * SEL_LEN > t_row

    win_start = pl.multiple_of(jnp.clip(qs - WINDOW, 0, seq - win_len), qb)
    dist_w = t_col - (win_start + lax.broadcasted_iota(I32, (1, win_len), 1))
    maskbias_w = jnp.where((dist_w >= 0) & (dist_w <= WINDOW), 0.0, NEG_INF)
    n_chunks = (qs + qb + SLC_CHUNK - 1) // SLC_CHUNK
    rows = Q_PER_KV * qb
    stack = lambda parts: jnp.concatenate(parts, axis=0)
    heads_of = lambda a: a.reshape(Q_PER_KV, qb, a.shape[1])

    def flash(qk, alibi, maskbias, off, vta, m, acc):
        s = (heads_of(qk + alibi) + maskbias[None]).reshape(rows, qk.shape[1])
        m_new = jnp.maximum(m, jnp.max(s, -1, keepdims=True) + off)
        p = jnp.exp2(s - (m_new - off)).astype(BF16)
        pv = lax.dot_general(p, vta, (((1,), (1,)), ((), ())), preferred_element_type=F32)
        return m_new, jnp.exp2(m - m_new) * acc + pv

    lanes_of = lambda g: slice(g * HEAD_DIM, (g + 1) * HEAD_DIM)
    heads_in = lambda g: range(g * Q_PER_KV, (g + 1) * Q_PER_KV)
    qg2s, slope2s, sels, o_cmps = [], [], [], []
    for g in range(N_KV):
        lanes, heads = lanes_of(g), heads_in(g)
        slope = stack([jnp.full((qb, 1), SLOPES[h], F32) for h in heads])
        qg = stack([(q_heads[h] * scale).astype(BF16) for h in heads])
        qg2s.append(stack([(q_heads[h] * (scale * LOG2E)).astype(BF16) for h in heads]))
        slope2s.append(slope * LOG2E)
        s_c = heads_of(_bdot_nt(qg, ck_ref[0, :, lanes]) - slope * stack([dist_cf] * Q_PER_KV))
        p_c = _masked_softmax(s_c, mask_c[None])
        p_sum = jnp.sum(p_c, axis=0)
        o_cmp = heads_of(_bdot(p_c.reshape(rows, nc), cv_ref[0, :, lanes]))
        imp = jnp.dot(p_sum, mimp_ref[...], precision=HIGHEST, preferred_element_type=F32)
        imp_t = jnp.transpose(imp)[0:n_blk, :]
        imp_t = jnp.where(future, -FORCE, jnp.where(forced, FORCE, imp_t))
        cnt = jnp.zeros((n_blk, qb), F32)
        for b2 in range(n_blk):
            row = imp_t[b2:b2 + 1, :]
            ahead = (row > imp_t) | ((row == imp_t) & (blk_i > b2))
            cnt = cnt + jnp.where(ahead, 1.0, 0.0)
        sel_t = jnp.where(cnt < n_pick, 1.0, 0.0)
        if n_blk < blk_pad:
            sel_t = jnp.concatenate([sel_t, jnp.zeros((blk_pad - n_blk, qb), F32)], axis=0)
        sels.append(jnp.transpose(sel_t).astype(BF16))
        o_cmps.append(o_cmp)

    key_slc = lax.broadcasted_iota(I32, (1, SLC_CHUNK), 1)
    key_win = lax.broadcasted_iota(I32, (1, win_len), 1).astype(F32)
    alibi_slc = [s2 * key_slc.astype(F32) for s2 in slope2s]

    def slc_step(c, carry):
        k0 = pl.multiple_of(c * SLC_CHUNK, SLC_CHUNK)
        causal = t_col - (k0 + key_slc) >= 0
        out = []
        for g in range(N_KV):
            kt = kst_ref[0, lanes_of(g), pl.ds(k0, SLC_CHUNK)].astype(BF16)
            vta = _aug_values(vst_ref[0, lanes_of(g), pl.ds(k0, SLC_CHUNK)])
            msel = jnp.dot(sels[g], esel_ref[c], preferred_element_type=F32)
            maskbias = jnp.where(causal & (msel > 0.5), 0.0, NEG_INF)
            qk = jnp.dot(qg2s[g], kt, preferred_element_type=F32)
            out += flash(qk, alibi_slc[g], maskbias, slope2s[g] * k0.astype(F32), vta, *carry[2 * g:2 * g + 2])
        return tuple(out)

    fin = lax.fori_loop(0, n_chunks, slc_step, _flash_init(rows) * N_KV)
    for g in range(N_KV):
        o_slc = heads_of(_flash_out(fin[2 * g + 1]))
        kwt = kwt_ref[0, lanes_of(g), pl.ds(win_start, win_len)].astype(BF16)
        vwta = _aug_values(vwt_ref[0, lanes_of(g), pl.ds(win_start, win_len)])
        qk = jnp.dot(qg2s[g], kwt, preferred_element_type=F32)
        o_win = heads_of(_flash_out(flash(qk, slope2s[g] * key_win, maskbias_w, 0.0, vwta, *_flash_init(rows))[1]))
        for i, h in enumerate(heads_in(g)):
            c0 = h * N_GATES
            o_ref[0, :, h * HEAD_DIM:(h + 1) * HEAD_DIM] = (gates[:, c0:c0 + 1] * o_cmps[g][i] + gates[:, c0 + 1:c0 + 2] * o_slc[i]
                                                           + gates[:, c0 + 2:c0 + 3] * o_win[i])


def _attn_prompt(q, gates, ck, cv, ks, vs, kw, vw):
    b, t, att = q.shape
    assert t % SLC_CHUNK == 0 and t % Q_BLOCK == 0
    n_blk = t // SEL_LEN
    n_cmp = t // CMP_STRIDE - 1
    assert n_blk <= LANES and ck.shape[1] == t // CMP_STRIDE
    n_pick = min(N_SEL, n_blk)
    win_len = min(WINDOW + Q_BLOCK, t)
    mimp = _importance_matrix(ck.shape[1], n_cmp, n_blk, LANES)
    n_ch = t // SLC_CHUNK
    esel = _expand_matrix(n_ch, SLC_CHUNK, [0] * n_ch)
    tok = lambda n: pl.BlockSpec((1, Q_BLOCK, n), lambda i, j: (i, j, 0))
    full = lambda a: pl.BlockSpec((1,) + a.shape[1:], lambda i, j: (i, 0, 0))
    const = lambda a: pl.BlockSpec(a.shape, lambda i, j: (0,) * a.ndim)
    return pl.pallas_call(
        functools.partial(_attn_prompt_kernel, seq=t, n_blk=n_blk, n_pick=n_pick, win_len=win_len),
        grid=(b, t // Q_BLOCK),
        in_specs=[tok(att), tok(gates.shape[2]), full(ck), full(cv), full(ks), full(vs), full(kw), full(vw),
                  const(mimp), const(esel)],
        out_specs=tok(att),
        out_shape=jax.ShapeDtypeStruct((b, t, att), F32),
        compiler_params=_cparams("arbitrary", "arbitrary"),
        name="attn_prompt",
    )(q, gates, ck, cv, ks, vs, kw, vw, mimp, esel)


def _attn_decode_kernel(pt_ref, q_ref, g_ref, ck_ref, cv_ref, ksn_ref, vsn_ref, kwn_ref, vwn_ref, wk_ref, wv_ref,
                        pk_ref, pv_ref, mimp_ref, esel_ref, o_ref, slab_k, slab_v, sem,
                        *, n_pages, page, past, n_blk, n_pick, tq):
    b = pl.program_id(0)
    slot = b % 2

    def fetch(bb, s, wait):
        def body(pg, carry):
            pid = pt_ref[bb * n_pages + pg]
            dst = pl.ds(pl.multiple_of(pg * page, page), page)
            for pool, slab, i in ((pk_ref, slab_k, 0), (pv_ref, slab_v, 1)):
                cp = pltpu.make_async_copy(pool.at[pid], slab.at[s, :, dst], sem.at[s, i])
                cp.wait() if wait else cp.start()
            return carry
        lax.fori_loop(0, n_pages, body, 0)

    @pl.when(b == 0)
    def _():
        fetch(b, slot, False)

    @pl.when(b + 1 < pl.num_programs(0))
    def _():
        fetch(b + 1, 1 - slot, False)

    rows = Q_PER_KV * DEC_Q
    qi = lax.broadcasted_iota(I32, (rows, 1), 0) % DEC_Q
    t_col = past + qi % tq
    hl = lax.broadcasted_iota(I32, (rows, 1), 0) // DEC_Q
    nc = ck_ref.shape[1]
    cmp_end = lax.broadcasted_iota(I32, (1, nc), 1) * CMP_STRIDE + (CMP_LEN - 1)
    dist_c = t_col - cmp_end
    mask_c = dist_c >= 0
    dist_cf = dist_c.astype(F32)
    blk_pad = mimp_ref.shape[1]
    t8 = past + lax.broadcasted_iota(I32, (DEC_Q, 1), 0) % tq
    blk_l = lax.broadcasted_iota(I32, (DEC_Q, blk_pad), 1)
    cur = t8 // SEL_LEN
    forced = (blk_l == 0) | (blk_l == cur) | (blk_l == cur - 1)
    future = blk_l * SEL_LEN > t8
    pad_blk = blk_l >= n_blk
    b_sub = lax.broadcasted_iota(I32, (blk_pad, blk_pad), 0)
    b_lane = lax.broadcasted_iota(I32, (blk_pad, blk_pad), 1)
    n_chunks = past // DEC_CHUNK
    new_rows = ksn_ref.shape[2]

    sels, slopes, o_cmps = [], [], []
    for g in range(N_KV):
        lanes = slice(g * HEAD_DIM, (g + 1) * HEAD_DIM)
        slope = jnp.zeros((rows, 1), F32)
        for i in range(Q_PER_KV):
            slope = jnp.where(hl == i, SLOPES[g * Q_PER_KV + i], slope)
        slopes.append(slope)
        qg = (q_ref[0, g] * (HEAD_DIM ** -0.5)).astype(BF16)
        p = _masked_softmax(_bdot_nt(qg, ck_ref[0, :, lanes]) - slope * dist_cf, mask_c)
        o_cmps.append(_bdot(p, cv_ref[0, :, lanes]))
        p_sum = p[0:DEC_Q]
        for i in range(1, Q_PER_KV):
            p_sum = p_sum + p[i * DEC_Q:(i + 1) * DEC_Q]
        imp = jnp.dot(p_sum, mimp_ref[...], precision=HIGHEST, preferred_element_type=F32)
        imp = jnp.where(pad_blk, -3e38, jnp.where(future, -FORCE, jnp.where(forced, FORCE, imp)))
        imp_t = jnp.transpose(jnp.concatenate([imp, jnp.zeros((LANES - DEC_Q, blk_pad), F32)], axis=0))
        sel_rows = []
        for r in range(DEC_Q):
            col = imp_t[:, r:r + 1]
            row = imp[r:r + 1, :]
            ahead = (col > row) | ((col == row) & (b_sub < b_lane))
            cnt = jnp.sum(jnp.where(ahead, 1.0, 0.0), axis=0, keepdims=True)
            sel_rows.append(jnp.where(cnt < n_pick, 1.0, 0.0))
        sel8 = jnp.concatenate(sel_rows, axis=0)
        sels.append(jnp.concatenate([sel8] * Q_PER_KV, axis=0))

    fetch(b, slot, True)

    for g in range(N_KV):
        lanes = slice(g * HEAD_DIM, (g + 1) * HEAD_DIM)
        slope2 = slopes[g] * LOG2E
        qg2 = (q_ref[0, g] * (HEAD_DIM ** -0.5 * LOG2E)).astype(BF16)
        sel = sels[g]
        sel_b = sel.astype(BF16)

        def bias_of(dist, visible):
            return jnp.where(visible, 0.0, NEG_INF) - slope2 * dist.astype(F32)

        m, acc = _flash_init(rows)
        per_mat = LANES * SEL_LEN // DEC_CHUNK
        for c in range(n_chunks):
            keys = slice(c * DEC_CHUNK, (c + 1) * DEC_CHUNK)
            lane0 = (c // per_mat) * LANES
            msel = jnp.dot(sel_b[:, lane0:lane0 + LANES], esel_ref[c % per_mat], preferred_element_type=F32)
            dist = t_col - (c * DEC_CHUNK + lax.broadcasted_iota(I32, (1, DEC_CHUNK), 1))
            m, acc = _flash_step(qg2, slab_k[slot, lanes, keys].astype(BF16), _aug_values(slab_v[slot, lanes, keys]),
                                 bias_of(dist, (dist >= 0) & (msel > 0.5)), m, acc)
        dist_n = t_col - (past + lax.broadcasted_iota(I32, (1, new_rows), 1))
        cur_blk = past // SEL_LEN
        m, acc = _flash_step(qg2, ksn_ref[0, lanes, :].astype(BF16), _aug_values(vsn_ref[0, lanes, :]),
                             bias_of(dist_n, (dist_n >= 0) & (sel[:, cur_blk:cur_blk + 1] > 0.5)), m, acc)
        o_slc = _flash_out(acc)
        wlen = wk_ref.shape[2]
        dist = t_col - (past - wlen + lax.broadcasted_iota(I32, (1, wlen), 1))
        m, acc = _flash_step(qg2, wk_ref[0, lanes, :].astype(BF16), _aug_values(wv_ref[0, lanes, :]),
                             bias_of(dist, (dist >= 0) & (dist <= WINDOW)), *_flash_init(rows))
        m, acc = _flash_step(qg2, kwn_ref[0, lanes, :].astype(BF16), _aug_values(vwn_ref[0, lanes, :]),
                             bias_of(dist_n, (dist_n >= 0) & (dist_n <= WINDOW)), m, acc)
        o_win = _flash_out(acc)
        gt = g_ref[0, g]
        o_ref[0, g] = gt[:, 0:1] * o_cmps[g] + gt[:, 1:2] * o_slc + gt[:, 2:3] * o_win


def _attn_decode(page_table, q, gates, ck, cv, ksn, vsn, kwn, vwn, win_k, win_v, pool_k, pool_v):
    nb, tq, att = q.shape
    n_pages = page_table.shape[1]
    page = pool_k.shape[2]
    past = n_pages * page
    assert tq <= DEC_Q and past % DEC_CHUNK == 0 and past % SEL_LEN == 0 and (past + tq - 1) // SEL_LEN == past // SEL_LEN
    assert win_k.shape[2] == WINDOW and ck.shape[1] * CMP_STRIDE == past and page % LANES == 0
    t_pad = -(-(past + tq) // SEL_LEN) * SEL_LEN
    n_blk = t_pad // SEL_LEN
    n_cmp = t_pad // CMP_STRIDE - 1
    n_pick = min(N_SEL, n_blk)
    blk_pad = -(-n_blk // LANES) * LANES
    mimp = _importance_matrix(ck.shape[1], n_cmp, n_blk, blk_pad)
    per_mat = LANES * SEL_LEN // DEC_CHUNK
    esel = _expand_matrix(per_mat, DEC_CHUNK, [0] * per_mat)

    def dec_rows(a, width):
        a = a.reshape(nb, tq, N_KV, Q_PER_KV, width)
        a = jnp.take(a, jnp.arange(DEC_Q) % tq, axis=1)
        return a.transpose(0, 2, 3, 1, 4).reshape(nb, N_KV, Q_PER_KV * DEC_Q, width)

    q_d = dec_rows(q, HEAD_DIM)
    g_d = dec_rows(gates[:, :, :N_HEADS * N_GATES], N_GATES)
    pad_new = lambda a: jnp.pad(a.transpose(0, 2, 1), ((0, 0), (0, 0), (0, LANES - tq)))
    ksn, vsn, kwn, vwn = [pad_new(a) for a in (ksn, vsn, kwn, vwn)]
    rows = Q_PER_KV * DEC_Q
    blk = lambda a: pl.BlockSpec((1,) + a.shape[1:], lambda i, pt: (i,) + (0,) * (a.ndim - 1))
    const = lambda a: pl.BlockSpec(a.shape, lambda i, pt: (0,) * a.ndim)
    any_spec = pl.BlockSpec(memory_space=pl.ANY)
    grid_spec = pltpu.PrefetchScalarGridSpec(
        num_scalar_prefetch=1, grid=(nb,),
        in_specs=[blk(q_d), blk(g_d), blk(ck), blk(cv), blk(ksn), blk(vsn), blk(kwn), blk(vwn), blk(win_k), blk(win_v),
                  any_spec, any_spec, const(mimp), const(esel)],
        out_specs=pl.BlockSpec((1, N_KV, rows, HEAD_DIM), lambda i, pt: (i, 0, 0, 0)),
        scratch_shapes=[pltpu.VMEM((2, KV_DIM, past), F32), pltpu.VMEM((2, KV_DIM, past), F32),
                        pltpu.SemaphoreType.DMA((2, 2))])
    o = pl.pallas_call(
        functools.partial(_attn_decode_kernel, n_pages=n_pages, page=page, past=past, n_blk=n_blk, n_pick=n_pick, tq=tq),
        grid_spec=grid_spec,
        out_shape=jax.ShapeDtypeStruct((nb, N_KV, rows, HEAD_DIM), F32),
        compiler_params=_cparams("arbitrary"),
        name="attn_decode",
    )(page_table.reshape(-1), q_d, g_d, ck, cv, ksn, vsn, kwn, vwn, win_k, win_v, pool_k, pool_v, mimp, esel)
    o = o.reshape(nb, N_KV, Q_PER_KV, DEC_Q, HEAD_DIM)[:, :, :, :tq]
    return o.transpose(0, 3, 1, 2, 4).reshape(nb, tq, att)


def _out_kernel(x_ref, conv_ref, att_ref, mod_ref, bc_ref, ba_ref, woc_ref, woa_ref, g1_ref, b1_ref, wr_ref, br_ref,
                cnt0_ref, x1_ref, u2_ref, idx_ref, w_ref, pos_ref, cnt_ref, carry_ref, *, tt, alpha, per_row):
    first = (pl.program_id(0) == 0) & (pl.program_id(1) == 0)

    @pl.when(first)
    def _():
        carry_ref[...] = cnt0_ref[...]

    gate1 = _mod_rows(mod_ref, 2, per_row)
    shift2 = _mod_rows(mod_ref, 3, per_row)
    scale2 = _mod_rows(mod_ref, 4, per_row)
    mix = _bdot(conv_ref[0] * bc_ref[...], woc_ref[...]) + _bdot(att_ref[0] * ba_ref[...], woa_ref[...])
    x1 = _layer_norm(alpha * x_ref[0] + gate1 * mix, g1_ref[...], b1_ref[...])
    x1_ref[0] = x1
    u2 = x1 * (1.0 + scale2) + shift2
    _store_tiled(u2_ref, 0, u2)
    logits = jnp.dot(u2, wr_ref[...], precision=HIGHEST, preferred_element_type=F32) + br_ref[...]
    lane = lax.broadcasted_iota(I32, logits.shape, 1)
    vals = logits
    top_v, top_i = [], []
    for _k in range(TOP_K):
        m = jnp.max(vals, axis=-1, keepdims=True)
        i = jnp.min(jnp.where(vals == m, lane, LANES), axis=-1, keepdims=True)
        top_v.append(m)
        top_i.append(i)
        vals = jnp.where(lane == i, -jnp.inf, vals)
    e = [jnp.exp(v - top_v[0]) for v in top_v]
    inv = 1.0 / (e[0] + e[1] + e[2] + e[3])
    onehot = [jnp.where(lane == i, 1.0, 0.0) for i in top_i]
    assigned = onehot[0] + onehot[1] + onehot[2] + onehot[3]
    r_i = lax.broadcasted_iota(I32, (tt, tt), 0)
    c_i = lax.broadcasted_iota(I32, (tt, tt), 1)
    before = jnp.where(r_i > c_i, 1.0, 0.0).astype(BF16)
    rank = carry_ref[...] + jnp.dot(before, assigned.astype(BF16), preferred_element_type=F32)
    idx_o = jnp.zeros(logits.shape, I32)
    w_o = jnp.zeros(logits.shape, F32)
    pos_o = jnp.zeros(logits.shape, F32)
    for k in range(TOP_K):
        idx_o = jnp.where(lane == k, top_i[k], idx_o)
        w_o = jnp.where(lane == k, e[k] * inv, w_o)
        pos_o = jnp.where(lane == k, jnp.sum(onehot[k] * rank, axis=-1, keepdims=True), pos_o)
    idx_ref[0] = idx_o
    w_ref[0] = w_o
    pos_ref[0] = pos_o.astype(I32)
    carry_ref[...] = carry_ref[...] + jnp.sum(assigned, axis=0, keepdims=True)
    cnt_ref[...] = carry_ref[...]


def _out_proj(x, conv_out, att, mod, vecs, wts, cnt0, *, tt, alpha, per_row):
    b, t, d = x.shape
    bc, ba, g1, b1, br = vecs
    woc, woa, wr = wts
    nt = t // tt
    const = lambda a: pl.BlockSpec(a.shape, lambda i, j: (0,) * a.ndim)
    tok = lambda n: pl.BlockSpec((1, tt, n), lambda i, j: (i, j, 0))
    mod_spec = (pl.BlockSpec((6, tt, d), lambda i, j: (0, j, 0)) if per_row
                else pl.BlockSpec((1, 6, d), lambda i, j: (i, 0, 0)))
    tok_shape = lambda n, dt: jax.ShapeDtypeStruct((b, t, n), dt)
    return pl.pallas_call(
        functools.partial(_out_kernel, tt=tt, alpha=alpha, per_row=per_row),
        grid=(b, nt),
        in_specs=[tok(d), tok(conv_out.shape[2]), tok(att.shape[2]), mod_spec, const(bc), const(ba), const(woc), const(woa),
                  const(g1), const(b1), const(wr), const(br), const(cnt0)],
        out_specs=[tok(d), pl.BlockSpec((tt * SUBLANES, LANES), lambda i, j: (i * nt + j, 0)),
                   tok(LANES), tok(LANES), tok(LANES), pl.BlockSpec((1, LANES), lambda i, j: (0, 0))],
        out_shape=[tok_shape(d, F32), jax.ShapeDtypeStruct((b * t * SUBLANES, LANES), F32), tok_shape(LANES, I32),
                   tok_shape(LANES, F32), tok_shape(LANES, I32),
                   jax.ShapeDtypeStruct((1, LANES), F32)],
        scratch_shapes=[pltpu.VMEM((1, LANES), F32)],
        compiler_params=_cparams("arbitrary", "arbitrary"),
        name="out_proj_router",
    )(x, conv_out, att, mod, bc, ba, woc, woa, g1, b1, wr, br, cnt0)


def _tile_rows(x):
    return [x[:, j * LANES:(j + 1) * LANES] for j in range(x.shape[1] // LANES)]


def _store_tiled(ref, at, x):
    rows = x.shape[0]
    for j, piece in enumerate(_tile_rows(x)):
        ref[pl.ds(at + j, rows, stride=SUBLANES), :] = piece


def _load_tiled(ref, at, rows):
    return jnp.concatenate([ref[pl.ds(at + j, rows, stride=SUBLANES), :] for j in range(SUBLANES)], axis=1)


def _gather_pipeline(i, n, src_ref, cur_ref, nxt_ref, bufs, sem, rows, compute):
    def row_copy(src, buf, s, r):
        at = r * SUBLANES if isinstance(r, int) else pl.multiple_of(r * SUBLANES, SUBLANES)
        return pltpu.make_async_copy(src_ref.at[pl.ds(pl.multiple_of(src, SUBLANES), SUBLANES)],
                                     buf.at[pl.ds(at, SUBLANES)], sem.at[s])

    def whole(buf, s):
        return pltpu.make_async_copy(src_ref.at[pl.ds(0, rows * SUBLANES)], buf, sem.at[s])

    @pl.when(i == 0)
    def _():
        def body(r, carry):
            row_copy(cur_ref[0, 0, r], bufs[0], 0, r).start()
            return carry
        lax.fori_loop(0, rows, body, 0)

    def step(s):
        whole(bufs[s], s).wait()
        for r in range(rows):
            row_copy(nxt_ref[0, 0, r], bufs[1 - s], 1 - s, r).start(priority=r % 2)
        compute(bufs[s])

        @pl.when(i == n - 1)
        def _():
            whole(bufs[1 - s], 1 - s).wait()

    for s in range(2):
        pl.when((i % 2 == s) & (i < n))(functools.partial(step, s))


def _expert_kernel(be_ref, nu_ref, cur_ref, nxt_ref, u_ref, wgu_ref, bgu_ref, wdn_ref, bdn_ref, y_ref, xb0, xb1, sem, *, bm, d_ff):
    def compute(buf):
        gu = jnp.dot(_load_tiled(buf, 0, bm).astype(BF16), wgu_ref[0].astype(BF16), preferred_element_type=F32) + bgu_ref[0]
        gl = jnp.minimum(gu[:, :d_ff], SWIGLU_LIMIT)
        lin = jnp.clip(gu[:, d_ff:], -SWIGLU_LIMIT, SWIGLU_LIMIT)
        act = gl * _sigmoid(SWIGLU_ALPHA * gl) * (lin + 1.0)
        _store_tiled(y_ref, 0, jnp.dot(act.astype(BF16), wdn_ref[0].astype(BF16), preferred_element_type=F32) + bdn_ref[0])

    _gather_pipeline(pl.program_id(0), nu_ref[0], u_ref, cur_ref, nxt_ref, (xb0, xb1), sem, bm, compute)

    @pl.when(pl.program_id(0) >= nu_ref[0])
    def _():
        y_ref[...] = jnp.zeros(y_ref.shape, F32)


def _expert_ffn(u2, slot_tok, blk_e, n_used, wgu, bgu, wdn, bdn, *, bm):
    n_blocks = slot_tok.shape[0]
    d = wgu.shape[1]
    assert d == SUBLANES * LANES
    d_ff = wdn.shape[1]
    slot_tok = slot_tok.reshape(n_blocks, 1, bm)
    smem = lambda f: pl.BlockSpec((1, 1, bm), f, memory_space=pltpu.SMEM)
    grid_spec = pltpu.PrefetchScalarGridSpec(
        num_scalar_prefetch=2, grid=(n_blocks,),
        in_specs=[smem(lambda i, be, nu: (i, 0, 0)),
                  smem(lambda i, be, nu: (jnp.minimum(i + 1, n_blocks - 1), 0, 0)),
                  pl.BlockSpec(memory_space=pl.ANY),
                  pl.BlockSpec((1, d, 2 * d_ff), lambda i, be, nu: (be[i], 0, 0)),
                  pl.BlockSpec((1, 1, 2 * d_ff), lambda i, be, nu: (be[i], 0, 0)),
                  pl.BlockSpec((1, d_ff, d), lambda i, be, nu: (be[i], 0, 0)),
                  pl.BlockSpec((1, 1, d), lambda i, be, nu: (be[i], 0, 0))],
        out_specs=pl.BlockSpec((bm * SUBLANES, LANES), lambda i, be, nu: (i, 0)),
        scratch_shapes=[pltpu.VMEM((bm * SUBLANES, LANES), F32), pltpu.VMEM((bm * SUBLANES, LANES), F32),
                        pltpu.SemaphoreType.DMA((2,))])
    return pl.pallas_call(
        functools.partial(_expert_kernel, bm=bm, d_ff=d_ff),
        grid_spec=grid_spec,
        out_shape=jax.ShapeDtypeStruct((n_blocks * bm * SUBLANES, LANES), F32),
        compiler_params=_cparams("arbitrary"),
        name="expert_ffn",
    )(blk_e, n_used, slot_tok, slot_tok, u2, wgu, bgu, wdn, bdn)


def _combine_kernel(cur_ref, nxt_ref, ys_ref, w_ref, x1_ref, mod_ref, g2_ref, b2_ref, y_ref, gb0, gb1, sem,
                    *, tc, alpha, per_row):
    def compute(buf):
        w = w_ref[...]
        f = w[:, 0:1] * _load_tiled(buf, 0, tc)
        for k in range(1, TOP_K):
            f = f + w[:, k:k + 1] * _load_tiled(buf, k * tc * SUBLANES, tc)
        gate2 = mod_ref[5] if per_row else mod_ref[0, 5:6, :]
        y_ref[...] = _layer_norm(alpha * x1_ref[...] + gate2 * f, g2_ref[...], b2_ref[...])

    _gather_pipeline(pl.program_id(0), pl.num_programs(0), ys_ref, cur_ref, nxt_ref, (gb0, gb1), sem, TOP_K * tc, compute)


def _combine(ys, dest_t, top_w, x1, mod, g2, b2, *, tc, alpha, per_row, tiles_per_batch):
    n, d = x1.shape
    n_tiles = n // tc
    dest_t = dest_t.reshape(n_tiles, 1, TOP_K * tc)
    smem = lambda f: pl.BlockSpec((1, 1, TOP_K * tc), f, memory_space=pltpu.SMEM)
    const = lambda a: pl.BlockSpec(a.shape, lambda i: (0,) * a.ndim)
    mod_spec = (pl.BlockSpec((6, tc, d), lambda i: (0, i, 0)) if per_row
                else pl.BlockSpec((1, 6, d), lambda i: (i // tiles_per_batch, 0, 0)))
    return pl.pallas_call(
        functools.partial(_combine_kernel, tc=tc, alpha=alpha, per_row=per_row),
        grid=(n_tiles,),
        in_specs=[smem(lambda i: (i, 0, 0)), smem(lambda i: (jnp.minimum(i + 1, n_tiles - 1), 0, 0)),
                  pl.BlockSpec(memory_space=pl.ANY),
                  pl.BlockSpec((tc, LANES), lambda i: (i, 0)), pl.BlockSpec((tc, d), lambda i: (i, 0)),
                  mod_spec, const(g2), const(b2)],
        out_specs=pl.BlockSpec((tc, d), lambda i: (i, 0)),
        out_shape=jax.ShapeDtypeStruct((n, d), F32),
        scratch_shapes=[pltpu.VMEM((TOP_K * tc * SUBLANES, LANES), F32), pltpu.VMEM((TOP_K * tc * SUBLANES, LANES), F32),
                        pltpu.SemaphoreType.DMA((2,))],
        compiler_params=_cparams("arbitrary"),
        name="combine_ln2",
    )(dest_t, dest_t, ys, top_w, x1, mod, g2, b2)


def _moe_experts(u2, top_i, pos, counts, experts, *, bm):
    n = top_i.shape[0]
    wgu, bgu, wdn, bdn = experts
    n_exp = wgu.shape[0]
    n_blocks = -(-n * TOP_K // bm) + n_exp
    counts = counts.astype(I32)
    padded = (counts + bm - 1) // bm * bm
    pad_end = jnp.cumsum(padded)
    pad_start = pad_end - padded
    dest = pad_start[top_i] + pos
    tok_row = jnp.broadcast_to((jnp.arange(n, dtype=I32) * SUBLANES)[:, None], (n, TOP_K))
    slot_tok = jnp.zeros((n_blocks * bm,), I32).at[dest.reshape(-1)].set(tok_row.reshape(-1), unique_indices=True,
                                                                        mode="promise_in_bounds")
    blk_start = jnp.arange(n_blocks, dtype=I32) * bm
    blk_e = jnp.minimum(jnp.sum((pad_end[None, :] <= blk_start[:, None]).astype(I32), axis=1), n_exp - 1)
    n_used = jnp.maximum(pad_end[-1:] // bm, 1)
    ys = _expert_ffn(u2, slot_tok.reshape(n_blocks, bm), blk_e, n_used, wgu, bgu, wdn, bdn, bm=bm)
    return ys, dest * SUBLANES


def _moe_combine(ys, dest, top_w, x1, mod, g2, b2, *, tc, alpha, per_row, tiles_per_batch):
    n = x1.shape[0]
    dest_t = dest.reshape(n // tc, tc, TOP_K).transpose(0, 2, 1).reshape(n // tc, TOP_K * tc)
    return _combine(ys, dest_t, top_w, x1, mod, g2, b2, tc=tc, alpha=alpha, per_row=per_row, tiles_per_batch=tiles_per_batch)


def _pick_tile(n, target):
    t = min(n, target)
    while n % t:
        t -= 1
    return t


def kernel(x_prompt, x_sample, c_prompt, c_sample, cache_cmp_k, cache_cmp_v, cache_slc_k, cache_slc_v, state_win_k, state_win_v, state_conv, page_table, w_ada, b_ada, w_in, w_dw, b_dw, conv_ln_g, conv_ln_b, w_ck1, w_ck2, pe_k, w_cv1, w_cv2, pe_v, beta_conv, beta_attn, w_out, ln1_g, ln1_b, w_router, b_router, w_gate_up, b_gate_up, w_down, b_down, ln2_g, ln2_b):
    depth = w_ada.shape[0]
    assert depth == 1, "single-layer trunk"
    alpha = (2 * depth) ** 0.25
    bp, seq, d = x_prompt.shape
    bs, tq, _ = x_sample.shape
    c_conv = w_dw.shape[2]
    att_dim = N_HEADS * HEAD_DIM
    n_pool, page = cache_cmp_k.shape[1], cache_cmp_k.shape[2]
    assert state_conv.shape[2] == CONV_K - 1 and page % CMP_STRIDE == 0

    wi = w_in[0]
    cuts = [2 * c_conv, 2 * c_conv + att_dim, 2 * c_conv + att_dim + 6 * KV_DIM]
    wa = wi[:, :cuts[0]].astype(BF16)
    wq = wi[:, cuts[0]:cuts[1]].astype(BF16)
    wkv = wi[:, cuts[1]:cuts[2]].astype(BF16)
    wg = jnp.pad(wi[:, cuts[2]:], ((0, 0), (0, LANES - N_HEADS * N_GATES))).astype(BF16)
    in_w = (wa, wq, wkv, wg)
    wdw = jnp.pad(w_dw[0], ((0, CONV_HIST - CONV_K), (0, 0)))
    row = lambda a: a.reshape(1, -1)
    conv_vecs = (wdw, row(b_dw[0]), row(conv_ln_g[0]), row(conv_ln_b[0]))
    wk_c = _compress_weights(w_ck1[0], w_ck2[0], pe_k[0])
    wv_c = _compress_weights(w_cv1[0], w_cv2[0], pe_v[0])
    out_vecs = (row(beta_conv[0]), row(beta_attn[0]), row(ln1_g[0]), row(ln1_b[0]),
                jnp.pad(row(b_router[0]), ((0, 0), (0, LANES - N_EXPERTS)), constant_values=NEG_INF))
    out_w = (w_out[0][:c_conv].astype(BF16), w_out[0][c_conv:].astype(BF16),
             jnp.pad(w_router[0], ((0, 0), (0, LANES - N_EXPERTS))))
    experts = (w_gate_up[0], b_gate_up[0][:, None, :], w_down[0], b_down[0][:, None, :])
    g2, b2 = row(ln2_g[0]), row(ln2_b[0])

    mod = _ada(jnp.concatenate([c_prompt, c_sample], axis=0), w_ada[0], row(b_ada[0])).reshape(bp + bs, 6, d)
    mod_p = mod[:bp]
    mod_s = jnp.repeat(mod[bp:], tq, axis=0).transpose(1, 0, 2)

    tt = _pick_tile(seq, 512)
    hist0 = jnp.zeros((bp, CONV_HIST, c_conv), F32)
    (conv_p, tail_p, q_p, kc_p, vc_p, kct_p, vct_p, kst_p, vst_p, kwt_p, vwt_p, gates_p) = _in_proj(
        x_prompt, mod_p, in_w, (hist0,) + conv_vecs, tt=tt, per_row=False)
    n_chunk = seq // CMP_STRIDE
    ck_p, cv_p = _compress_prompt(kc_p.reshape(bp, n_chunk, CHUNK_ROW), vc_p.reshape(bp, n_chunk, CHUNK_ROW), wk_c, wv_c)
    att_p = _attn_prompt(q_p, gates_p, ck_p, cv_p, kst_p, vst_p, kwt_p, vwt_p)
    to = _pick_tile(seq, 512)
    cnt0 = jnp.zeros((1, LANES), F32)
    n_p, n_s = bp * seq, bs * tq
    x1_p, u2_p, ti_p, tw_p, pos_p, cnt_p = _out_proj(x_prompt, conv_p, att_p, mod_p, out_vecs, out_w, cnt0,
                                                     tt=to, alpha=alpha, per_row=False)
    picks = lambda a: a.reshape(-1, LANES)[:, :TOP_K]
    ys_p, dest_p = _moe_experts(u2_p, picks(ti_p), picks(pos_p), cnt_p[0, :N_EXPERTS], experts, bm=_pick_tile(n_p * TOP_K, 512))
    tc_p = _pick_tile(seq, 128)
    y_p = _moe_combine(ys_p, dest_p, tw_p.reshape(n_p, LANES), x1_p.reshape(n_p, d), mod_p, g2, b2,
                       tc=tc_p, alpha=alpha, per_row=False, tiles_per_batch=seq // tc_p).reshape(bp, seq, d)

    (glu_s, q_s, kc_s, vc_s, ks_s, vs_s, kw_s, vw_s, gates_s) = _in_proj(
        x_sample.reshape(1, n_s, d), mod_s, in_w, None, tt=n_s, per_row=True)
    unflat = lambda a: a.reshape(bs, tq, a.shape[-1])
    glu_s, q_s, kc_s, vc_s, ks_s, vs_s, kw_s, vw_s, gates_s = map(unflat, (glu_s, q_s, kc_s, vc_s, ks_s, vs_s, kw_s, vw_s, gates_s))
    xc_s = jnp.concatenate([jnp.zeros((bs, CONV_HIST - (CONV_K - 1), c_conv), F32), state_conv[0], glu_s], axis=1)
    conv_s = _conv_small(xc_s, *conv_vecs, tt=tq)
    dim_major = lambda a: a.transpose(0, 2, 3, 1).reshape(a.shape[0], KV_DIM, a.shape[1])
    ck_s, cv_s = _compress_decode(page_table, dim_major(cache_cmp_k[0]), dim_major(cache_cmp_v[0]), wk_c, wv_c)
    att_s = _attn_decode(page_table, q_s, gates_s, ck_s, cv_s, ks_s, vs_s, kw_s, vw_s, dim_major(state_win_k[0]),
                         dim_major(state_win_v[0]), dim_major(cache_slc_k[0]), dim_major(cache_slc_v[0]))
    flat = lambda a: a.reshape(1, n_s, a.shape[-1])
    x1_s, u2_s, ti_s, tw_s, pos_s, cnt_s = _out_proj(flat(x_sample), flat(conv_s), flat(att_s), mod_s, out_vecs, out_w, cnt0,
                                                     tt=n_s, alpha=alpha, per_row=True)
    bm_s = max(2 * SUBLANES, min(512, n_s * TOP_K // N_EXPERTS // SUBLANES * SUBLANES))
    ys_s, dest_s = _moe_experts(u2_s, picks(ti_s), picks(pos_s), cnt_s[0, :N_EXPERTS], experts, bm=bm_s)
    y_s = _moe_combine(ys_s, dest_s, tw_s.reshape(n_s, LANES), x1_s.reshape(n_s, d), mod_s, g2, b2,
                       tc=n_s, alpha=alpha, per_row=True, tiles_per_batch=1).reshape(bs, tq, d)

    heads = lambda a: a.reshape(1, a.shape[0], a.shape[1], N_KV, HEAD_DIM)
    win_keep = min(WINDOW, seq)
    heads_t = lambda a: a.reshape(1, a.shape[0], N_KV, HEAD_DIM, a.shape[2]).transpose(0, 1, 4, 2, 3)
    new_p = (heads_t(kct_p), heads_t(vct_p), heads_t(kst_p), heads_t(vst_p), heads_t(kwt_p[:, :, seq - win_keep:]),
             heads_t(vwt_p[:, :, seq - win_keep:]), tail_p[None, :, CONV_HIST - (CONV_K - 1):])
    win_buf = state_win_k.shape[2]
    keep = lambda old, new: jnp.concatenate([old[0], new.reshape(bs, tq, N_KV, HEAD_DIM)], axis=1)[None, :, -win_buf:]
    new_s = (heads(kc_s), heads(vc_s), heads(ks_s), heads(vs_s), keep(state_win_k, kw_s), keep(state_win_v, vw_s),
             jnp.concatenate([state_conv[0], glu_s], axis=1)[None, :, -(CONV_K - 1):])
    return (y_p, y_s) + new_p + new_s
```

```python
import functools

import jax
import jax.numpy as jnp
from jax import lax
from jax.experimental import pallas as pl
from jax.experimental.pallas import tpu as pltpu

F32 = jnp.float32
BF16 = jnp.bfloat16
I32 = jnp.int32

N_HEADS = 8
HEAD_DIM = 64
N_KV = 2
Q_PER_KV = N_HEADS // N_KV
KV_DIM = N_KV * HEAD_DIM
CMP_LEN = 32
CMP_STRIDE = 16
CMP_HID = 2 * HEAD_DIM
SEL_LEN = 64
SEL_RATIO = SEL_LEN // CMP_STRIDE
N_SEL = 16
WINDOW = 512
Q_BLOCK = 128
N_GATES = 3
N_EXPERTS = 32
TOP_K = 4
CONV_K = 31
SWIGLU_LIMIT = 7.0
SWIGLU_ALPHA = 1.702
LN_EPS = 1e-5
NEG_INF = -1e30
FORCE = 1e9
SLOPES = tuple(2.0 ** (-8.0 * (h + 1) / N_HEADS) for h in range(N_HEADS))

LANES = 128
SUBLANES = 8
VMEM_LIMIT_BYTES = 56 * 1024 * 1024

CONV_HIST = 32
CHUNK_ROW = CMP_STRIDE * KV_DIM
SLC_CHUNK = 512
DEC_CHUNK = 2048
DEC_Q = 8
V_AUG = HEAD_DIM + 16
LOG2E = 1.4426950408889634
HIGHEST = lax.Precision.HIGHEST


def _cparams(*sem):
    return pltpu.CompilerParams(dimension_semantics=sem, vmem_limit_bytes=VMEM_LIMIT_BYTES)


def _sigmoid(x):
    return 1.0 / (1.0 + jnp.exp(-x))


def _bdot(a, b):
    return jnp.dot(a.astype(BF16), b.astype(BF16), preferred_element_type=F32)


def _bdot_nt(a, b):
    return lax.dot_general(a.astype(BF16), b.astype(BF16), (((1,), (1,)), ((), ())), preferred_element_type=F32)


def _layer_norm(x, g, b):
    mu = jnp.mean(x, -1, keepdims=True)
    xc = x - mu
    var = jnp.mean(xc * xc, -1, keepdims=True)
    return xc * lax.rsqrt(var + LN_EPS) * g + b


def _gelu_tanh(x):
    return 0.5 * x * (1.0 + jnp.tanh(0.7978845608028654 * (x + 0.044715 * (x * x * x))))


def _masked_softmax(s, mask):
    s = jnp.where(mask, s, NEG_INF)
    e = jnp.where(mask, jnp.exp(s - jnp.max(s, -1, keepdims=True)), 0.0)
    return e * (1.0 / jnp.maximum(jnp.sum(e, -1, keepdims=True), 1e-30))


def _ada_kernel(c_ref, w_ref, b_ref, o_ref):
    c = c_ref[...]
    o_ref[...] = _bdot(c * _sigmoid(c), w_ref[...]) + b_ref[...]


def _ada(c_all, w_ada, b_ada):
    nb, d = c_all.shape
    n = w_ada.shape[1]
    bn = d
    return pl.pallas_call(
        _ada_kernel,
        grid=(n // bn,),
        in_specs=[pl.BlockSpec((nb, d), lambda i: (0, 0)),
                  pl.BlockSpec((d, bn), lambda i: (0, i)),
                  pl.BlockSpec((1, bn), lambda i: (0, i))],
        out_specs=pl.BlockSpec((nb, bn), lambda i: (0, i)),
        out_shape=jax.ShapeDtypeStruct((nb, n), F32),
        compiler_params=_cparams("arbitrary"),
        name="ada",
    )(c_all, w_ada, b_ada)


def _mod_rows(mod_ref, i, per_row):
    return mod_ref[i] if per_row else mod_ref[0, i:i + 1, :]


def _conv_ln_silu(xc_ref, row0, n, wdw_ref, bdw, lng, lnb, shifted_ref=None):
    acc = jnp.zeros((n, xc_ref.shape[1]), F32) + bdw
    off = CONV_HIST - (CONV_K - 1)
    for k in range(CONV_K):
        at = row0 + off + k
        if shifted_ref is None or at % SUBLANES == 0:
            x = xc_ref[pl.ds(at, n), :]
        else:
            x = shifted_ref[at % SUBLANES - 1, pl.ds(at - at % SUBLANES, n), :]
        acc = acc + x * wdw_ref[k:k + 1, :]
    y = _layer_norm(acc, lng, lnb)
    return y * _sigmoid(y)


def _in_kernel(*refs, tt, c_conv, fuse_conv, per_row):
    if fuse_conv:
        (x_ref, mod_ref, wa_ref, wq_ref, wkv_ref, wg_ref, hist_ref, wdw_ref, bdw_ref, lng_ref, lnb_ref,
         conv_ref, tail_ref, q_ref, kc_ref, vc_ref, *kvt_refs, gates_ref, xc_ref, xs_ref) = refs
    else:
        (x_ref, mod_ref, wa_ref, wq_ref, wkv_ref, wg_ref,
         glu_ref, q_ref, kc_ref, vc_ref, ks_ref, vs_ref, kw_ref, vw_ref, gates_ref) = refs
    x = x_ref[0]
    shift1 = _mod_rows(mod_ref, 0, per_row)
    scale1 = _mod_rows(mod_ref, 1, per_row)
    u = (x * (1.0 + scale1) + shift1).astype(BF16)
    a = jnp.dot(u, wa_ref[...], preferred_element_type=F32)
    glu = a[:, :c_conv] * _sigmoid(a[:, c_conv:])
    q_ref[0] = jnp.dot(u, wq_ref[...], preferred_element_type=F32)
    kv = jnp.dot(u, wkv_ref[...], preferred_element_type=F32)
    gates_ref[0] = _sigmoid(jnp.dot(u, wg_ref[...], preferred_element_type=F32))
    if not fuse_conv:
        for i, r in enumerate((kc_ref, vc_ref, ks_ref, vs_ref, kw_ref, vw_ref)):
            r[0] = kv[:, i * KV_DIM:(i + 1) * KV_DIM]
        glu_ref[0] = glu
        return
    kc_ref[0] = kv[:, 0:KV_DIM]
    vc_ref[0] = kv[:, KV_DIM:2 * KV_DIM]
    for i, r in enumerate(kvt_refs):
        r[0] = jnp.transpose(kv[:, i * KV_DIM:(i + 1) * KV_DIM])

    @pl.when(pl.program_id(1) == 0)
    def _():
        xc_ref[0:CONV_HIST, :] = hist_ref[0]

    xc_ref[CONV_HIST:CONV_HIST + tt, :] = glu
    rows = min(tt, 64)
    assert rows % SUBLANES == 0
    keep = CONV_HIST + tt - SUBLANES
    for b in range(1, SUBLANES):
        xs_ref[b - 1, 0:keep, :] = xc_ref[b:b + keep, :]
    for r0 in range(0, tt, rows):
        conv_ref[0, r0:r0 + rows, :] = _conv_ln_silu(xc_ref, r0, rows, wdw_ref, bdw_ref[...], lng_ref[...], lnb_ref[...], xs_ref)
    tail = xc_ref[tt:tt + CONV_HIST, :]
    xc_ref[0:CONV_HIST, :] = tail
    tail_ref[0] = tail


def _in_proj(x, mod, wts, conv=None, *, tt, per_row):
    b, t, d = x.shape
    wa, wq, wkv, wg = wts
    c_conv = wa.shape[1] // 2
    fuse = conv is not None
    nt = t // tt
    const = lambda shape: pl.BlockSpec(shape, lambda i, j: (0,) * len(shape))
    mod_spec = (pl.BlockSpec((6, tt, d), lambda i, j: (0, j, 0)) if per_row
                else pl.BlockSpec((1, 6, d), lambda i, j: (i, 0, 0)))
    in_specs = [pl.BlockSpec((1, tt, d), lambda i, j: (i, j, 0)), mod_spec,
                const(wa.shape), const(wq.shape), const(wkv.shape), const(wg.shape)]
    args = [x, mod, wa, wq, wkv, wg]
    tok = lambda n: pl.BlockSpec((1, tt, n), lambda i, j: (i, j, 0))
    tok_shape = lambda n: jax.ShapeDtypeStruct((b, t, n), F32)
    out_specs, out_shape, scratch = [], [], []
    if fuse:
        hist, wdw, bdw, lng, lnb = conv
        in_specs += [pl.BlockSpec((1, CONV_HIST, c_conv), lambda i, j: (i, 0, 0)),
                     const(wdw.shape), const(bdw.shape), const(lng.shape), const(lnb.shape)]
        args += [hist, wdw, bdw, lng, lnb]
        out_specs += [tok(c_conv), pl.BlockSpec((1, CONV_HIST, c_conv), lambda i, j: (i, 0, 0))]
        out_shape += [tok_shape(c_conv), jax.ShapeDtypeStruct((b, CONV_HIST, c_conv), F32)]
        scratch = [pltpu.VMEM((CONV_HIST + tt, c_conv), F32), pltpu.VMEM((SUBLANES - 1, CONV_HIST + tt, c_conv), F32)]
    else:
        out_specs += [tok(c_conv)]
        out_shape += [tok_shape(c_conv)]
    if fuse:
        kv_specs = [tok(KV_DIM)] * 2 + [pl.BlockSpec((1, KV_DIM, tt), lambda i, j: (i, 0, j))] * 6
        kv_shape = [tok_shape(KV_DIM)] * 2 + [jax.ShapeDtypeStruct((b, KV_DIM, t), F32)] * 6
    else:
        kv_specs, kv_shape = [tok(KV_DIM)] * 6, [tok_shape(KV_DIM)] * 6
    out_specs += [tok(wq.shape[1])] + kv_specs + [tok(wg.shape[1])]
    out_shape += [tok_shape(wq.shape[1])] + kv_shape + [tok_shape(wg.shape[1])]
    return pl.pallas_call(
        functools.partial(_in_kernel, tt=tt, c_conv=c_conv, fuse_conv=fuse, per_row=per_row),
        grid=(b, nt), in_specs=in_specs, out_specs=out_specs, out_shape=out_shape, scratch_shapes=scratch,
        compiler_params=_cparams("arbitrary", "arbitrary"),
        name="in_proj_conv" if fuse else "in_proj",
    )(*args)


def _conv_small_kernel(xc_ref, wdw_ref, bdw_ref, lng_ref, lnb_ref, o_ref, *, tt):
    o_ref[0] = _conv_ln_silu(xc_ref.at[0], 0, tt, wdw_ref, bdw_ref[...], lng_ref[...], lnb_ref[...])


def _conv_small(xc, wdw, bdw, lng, lnb, *, tt):
    b, rows, c = xc.shape
    const = lambda shape: pl.BlockSpec(shape, lambda i: (0,) * len(shape))
    return pl.pallas_call(
        functools.partial(_conv_small_kernel, tt=tt),
        grid=(b,),
        in_specs=[pl.BlockSpec((1, rows, c), lambda i: (i, 0, 0)),
                  const(wdw.shape), const(bdw.shape), const(lng.shape), const(lnb.shape)],
        out_specs=pl.BlockSpec((1, tt, c), lambda i: (i, 0, 0)),
        out_shape=jax.ShapeDtypeStruct((b, tt, c), F32),
        compiler_params=_cparams("arbitrary"),
        name="conv_small",
    )(xc, wdw, bdw, lng, lnb)


def _compress_weights(w1, w2, pe):
    half = CMP_STRIDE * HEAD_DIM

    def expand(w):
        w = w.reshape(CMP_STRIDE, HEAD_DIM, CMP_HID)
        z = jnp.zeros_like(w)
        g0 = jnp.concatenate([w, z], axis=1).reshape(CHUNK_ROW, CMP_HID)
        g1 = jnp.concatenate([z, w], axis=1).reshape(CHUNK_ROW, CMP_HID)
        return jnp.concatenate([g0, g1], axis=1)

    w1p = jnp.concatenate([expand(w1[:half]), expand(w1[half:])], axis=1).astype(BF16)
    z2 = jnp.zeros_like(w2)
    w2p = jnp.concatenate([jnp.concatenate([w2, z2], axis=1), jnp.concatenate([z2, w2], axis=1)], axis=0).astype(BF16)
    return w1p, w1.astype(BF16), pe.reshape(1, -1), w2p


def _compress_tail(r, w1_ref, pe_ref, w2p_ref):
    n = r.shape[0]
    pe8 = jnp.broadcast_to(pe_ref[...], (SUBLANES, pe_ref.shape[1]))
    c = _bdot(pe8, w1_ref[...])[0:1]
    c2 = jnp.concatenate([c, c], axis=1)
    h = r[:, :2 * CMP_HID] + pltpu.roll(r[:, 2 * CMP_HID:], n - 1, 0) + c2
    return _bdot(_gelu_tanh(h), w2p_ref[...])


def _compress_prompt_kernel(ak_ref, av_ref, w1pk, w1k, pek, w2pk, w1pv, w1v, pev, w2pv, ck_ref, cv_ref):
    ck_ref[0] = _compress_tail(_bdot(ak_ref[0], w1pk[...]), w1k, pek, w2pk)
    cv_ref[0] = _compress_tail(_bdot(av_ref[0], w1pv[...]), w1v, pev, w2pv)


def _compress_prompt(ak, av, wk, wv):
    b, n, _ = ak.shape
    const = lambda a: pl.BlockSpec(a.shape, lambda i: (0,) * a.ndim)
    row = pl.BlockSpec((1, n, CHUNK_ROW), lambda i: (i, 0, 0))
    out = pl.BlockSpec((1, n, KV_DIM), lambda i: (i, 0, 0))
    return pl.pallas_call(
        _compress_prompt_kernel,
        grid=(b,),
        in_specs=[row, row] + [const(a) for a in wk] + [const(a) for a in wv],
        out_specs=[out, out],
        out_shape=[jax.ShapeDtypeStruct((b, n, KV_DIM), F32)] * 2,
        compiler_params=_cparams("arbitrary"),
        name="compress_prompt",
    )(ak, av, *wk, *wv)


def _compress_decode_kernel(pt_ref, pk_ref, pv_ref, w1pk, w1k, pek, w2pk, w1pv, w1v, pev, w2pv, ck_ref, cv_ref,
                            slab_k, slab_v, rows_buf, rbuf, sem, *, n_pages, page):
    j = pl.program_id(0)
    n_jobs = pl.num_programs(0)
    past = n_pages * page
    n = past // CMP_STRIDE
    step = min(n, 256)
    tcols = min(past, 2048)

    def fetch(job, pool_ref, slab, s, wait):
        def body(pg, carry):
            pid = pt_ref[(job // 2) * n_pages + pg]
            cp = pltpu.make_async_copy(pool_ref.at[pid], slab.at[:, pl.ds(pl.multiple_of(pg * page, page), page)], sem.at[s])
            cp.wait() if wait else cp.start()
            return carry
        lax.fori_loop(0, n_pages, body, 0)

    def project(slab, w1p):
        def to_rows(i, carry):
            at = pl.ds(pl.multiple_of(i * tcols, tcols), tcols)
            rows_buf[at, :] = jnp.transpose(slab[:, at])
            return carry
        lax.fori_loop(0, past // tcols, to_rows, 0)

        def body(i, carry):
            base = pl.multiple_of(i * (step * CMP_STRIDE), step * CMP_STRIDE)
            a = jnp.concatenate([rows_buf[pl.ds(base + p, step, stride=CMP_STRIDE), :] for p in range(CMP_STRIDE)], axis=1)
            rbuf[pl.ds(pl.multiple_of(i * step, step), step), :] = _bdot(a, w1p[...])
            return carry
        lax.fori_loop(0, n // step, body, 0)
        return rbuf[...]

    @pl.when(j == 0)
    def _():
        fetch(j, pk_ref, slab_k, 0, False)

    @pl.when(j % 2 == 0)
    def _():
        fetch(j + 1, pv_ref, slab_v, 1, False)
        fetch(j, pk_ref, slab_k, 0, True)
        ck_ref[0] = _compress_tail(project(slab_k, w1pk), w1k, pek, w2pk)

    @pl.when(j % 2 == 1)
    def _():
        @pl.when(j + 1 < n_jobs)
        def _():
            fetch(j + 1, pk_ref, slab_k, 0, False)
        fetch(j, pv_ref, slab_v, 1, True)
        cv_ref[0] = _compress_tail(project(slab_v, w1pv), w1v, pev, w2pv)


def _compress_decode(page_table, pool_k, pool_v, wk, wv):
    nb, n_pages = page_table.shape
    page = pool_k.shape[2]
    past = n_pages * page
    n = past // CMP_STRIDE
    assert page % LANES == 0
    const = lambda a: pl.BlockSpec(a.shape, lambda i, pt: (0,) * a.ndim)
    any_spec = pl.BlockSpec(memory_space=pl.ANY)
    out = pl.BlockSpec((1, n, KV_DIM), lambda i, pt: (i // 2, 0, 0))
    grid_spec = pltpu.PrefetchScalarGridSpec(
        num_scalar_prefetch=1, grid=(2 * nb,),
        in_specs=[any_spec, any_spec] + [const(a) for a in wk] + [const(a) for a in wv],
        out_specs=[out, out],
        scratch_shapes=[pltpu.VMEM((KV_DIM, past), F32), pltpu.VMEM((KV_DIM, past), F32), pltpu.VMEM((past, KV_DIM), F32),
                        pltpu.VMEM((n, 4 * CMP_HID), F32), pltpu.SemaphoreType.DMA((2,))])
    return pl.pallas_call(
        functools.partial(_compress_decode_kernel, n_pages=n_pages, page=page),
        grid_spec=grid_spec,
        out_shape=[jax.ShapeDtypeStruct((nb, n, KV_DIM), F32)] * 2,
        compiler_params=_cparams("arbitrary"),
        name="compress_decode",
    )(page_table.reshape(-1), pool_k, pool_v, *wk, *wv)


def _importance_matrix(n_cmp_rows, n_cmp, n_blk, cols):
    c = jnp.arange(n_cmp_rows)[:, None]
    b = jnp.arange(cols)[None, :]
    d = c - (SEL_RATIO * b - 1)
    return ((d >= 0) & (d <= SEL_RATIO) & (c < n_cmp) & (b < n_blk)).astype(F32)


def _expand_matrix(n_chunks, chunk, blk0):
    key = jnp.arange(n_chunks)[:, None, None] * chunk + jnp.arange(chunk)[None, None, :]
    blk = jnp.asarray(blk0)[:, None, None] + jnp.arange(LANES)[None, :, None]
    return (key // SEL_LEN == blk).astype(BF16)


def _aug_values(vt):
    r = lax.broadcasted_iota(I32, (V_AUG - HEAD_DIM, vt.shape[1]), 0)
    return jnp.concatenate([vt, jnp.where(r == 0, 1.0, 0.0)], axis=0).astype(BF16)


def _flash_step(qh, kt, vta, bias, m, acc):
    s = jnp.dot(qh, kt, preferred_element_type=F32) + bias
    m_new = jnp.maximum(m, jnp.max(s, -1, keepdims=True))
    p = jnp.exp2(s - m_new).astype(BF16)
    pv = lax.dot_general(p, vta, (((1,), (1,)), ((), ())), preferred_element_type=F32)
    return m_new, jnp.exp2(m - m_new) * acc + pv


def _flash_init(rows):
    return jnp.full((rows, 1), NEG_INF, F32), jnp.zeros((rows, V_AUG), F32)


def _flash_out(acc):
    return acc[:, :HEAD_DIM] * (1.0 / jnp.maximum(acc[:, HEAD_DIM:HEAD_DIM + 1], 1e-30))


def _attn_prompt_kernel(q_ref, g_ref, ck_ref, cv_ref, kst_ref, vst_ref, kwt_ref, vwt_ref, mimp_ref, esel_ref, o_ref,
                        *, seq, n_blk, n_pick, win_len):
    qb = Q_BLOCK
    qs = pl.program_id(1) * qb
    t_col = qs + lax.broadcasted_iota(I32, (qb, 1), 0)
    gates = g_ref[0]
    scale = HEAD_DIM ** -0.5
    q_heads = [q_ref[0, :, h * HEAD_DIM:(h + 1) * HEAD_DIM] for h in range(N_HEADS)]
    nc = ck_ref.shape[1]
    cmp_end = lax.broadcasted_iota(I32, (1, nc), 1) * CMP_STRIDE + (CMP_LEN - 1)
    dist_c = t_col - cmp_end
    mask_c = dist_c >= 0
    dist_cf = dist_c.astype(F32)
    blk_pad = mimp_ref.shape[1]

    blk_i = lax.broadcasted_iota(I32, (n_blk, qb), 0)
    t_row = qs + lax.broadcasted_iota(I32, (n_blk, qb), 1)
    cur = t_row // SEL_LEN
    forced = (blk_i == 0) | (blk_i == cur) | (blk_i == cur - 1)
    future = blk_i * SEL_LEN > t_row

    win_start = pl.multiple_of(jnp.clip(qs - WINDOW, 0, seq - win_len), qb)
    dist_w = t_col - (win_start + lax.broadcasted_iota(I32, (1, win_len), 1))
    maskbias_w = jnp.where((dist_w >= 0) & (dist_w <= WINDOW), 0.0, NEG_INF)
    n_chunks = (qs + qb + SLC_CHUNK - 1) // SLC_CHUNK
    rows = Q_PER_KV * qb
    stack = lambda parts: jnp.concatenate(parts, axis=0)
    heads_of = lambda a: a.reshape(Q_PER_KV, qb, a.shape[1])

    def flash(qk, alibi, maskbias, off, vta, m, acc):
        s = (heads_of(qk + alibi) + maskbias[None]).reshape(rows, qk.shape[1])
        m_new = jnp.maximum(m, jnp.max(s, -1, keepdims=True) + off)
        p = jnp.exp2(s - (m_new - off)).astype(BF16)
        pv = lax.dot_general(p, vta, (((1,), (1,)), ((), ())), preferred_element_type=F32)
        return m_new, jnp.exp2(m - m_new) * acc + pv

    lanes_of = lambda g: slice(g * HEAD_DIM, (g + 1) * HEAD_DIM)
    heads_in = lambda g: range(g * Q_PER_KV, (g + 1) * Q_PER_KV)
    qg2s, slope2s, sels, o_cmps = [], [], [], []
    for g in range(N_KV):
        lanes, heads = lanes_of(g), heads_in(g)
        slope = stack([jnp.full((qb, 1), SLOPES[h], F32) for h in heads])
        qg = stack([(q_heads[h] * scale).astype(BF16) for h in heads])
        qg2s.append(stack([(q_heads[h] * (scale * LOG2E)).astype(BF16) for h in heads]))
        slope2s.append(slope * LOG2E)
        s_c = heads_of(_bdot_nt(qg, ck_ref[0, :, lanes]) - slope * stack([dist_cf] * Q_PER_KV))
        p_c = _masked_softmax(s_c, mask_c[None])
        p_sum = jnp.sum(p_c, axis=0)
        o_cmp = heads_of(_bdot(p_c.reshape(rows, nc), cv_ref[0, :, lanes]))
        imp = jnp.dot(p_sum, mimp_ref[...], precision=HIGHEST, preferred_element_type=F32)
        imp_t = jnp.transpose(imp)[0:n_blk, :]
        imp_t = jnp.where(future, -FORCE, jnp.where(forced, FORCE, imp_t))
        cnt = jnp.zeros((n_blk, qb), F32)
        for b2 in range(n_blk):
            row = imp_t[b2:b2 + 1, :]
            ahead = (row > imp_t) | ((row == imp_t) & (blk_i > b2))
            cnt = cnt + jnp.where(ahead, 1.0, 0.0)
        sel_t = jnp.where(cnt < n_pick, 1.0, 0.0)
        if n_blk < blk_pad:
            sel_t = jnp.concatenate([sel_t, jnp.zeros((blk_pad - n_blk, qb), F32)], axis=0)
        sels.append(jnp.transpose(sel_t).astype(BF16))
        o_cmps.append(o_cmp)

    key_slc = lax.broadcasted_iota(I32, (1, SLC_CHUNK), 1)
    key_win = lax.broadcasted_iota(I32, (1, win_len), 1).astype(F32)
    alibi_slc = [s2 * key_slc.astype(F32) for s2 in slope2s]

    def slc_step(c, carry):
        k0 = pl.multiple_of(c * SLC_CHUNK, SLC_CHUNK)
        causal = t_col - (k0 + key_slc) >= 0
        out = []
        for g in range(N_KV):
            kt = kst_ref[0, lanes_of(g), pl.ds(k0, SLC_CHUNK)].astype(BF16)
            vta = _aug_values(vst_ref[0, lanes_of(g), pl.ds(k0, SLC_CHUNK)])
            msel = jnp.dot(sels[g], esel_ref[c], preferred_element_type=F32)
            maskbias = jnp.where(causal & (msel > 0.5), 0.0, NEG_INF)
            qk = jnp.dot(qg2s[g], kt, preferred_element_type=F32)
            out += flash(qk, alibi_slc[g], maskbias, slope2s[g] * k0.astype(F32), vta, *carry[2 * g:2 * g + 2])
        return tuple(out)

    fin = lax.fori_loop(0, n_chunks, slc_step, _flash_init(rows) * N_KV)
    for g in range(N_KV):
        o_slc = heads_of(_flash_out(fin[2 * g + 1]))
        kwt = kwt_ref[0, lanes_of(g), pl.ds(win_start, win_len)].astype(BF16)
        vwta = _aug_values(vwt_ref[0, lanes_of(g), pl.ds(win_start, win_len)])
        qk = jnp.dot(qg2s[g], kwt, preferred_element_type=F32)
        o_win = heads_of(_flash_out(flash(qk, slope2s[g] * key_win, maskbias_w, 0.0, vwta, *_flash_init(rows))[1]))
        for i, h in enumerate(heads_in(g)):
            c0 = h * N_GATES
            o_ref[0, :, h * HEAD_DIM:(h + 1) * HEAD_DIM] = (gates[:, c0:c0 + 1] * o_cmps[g][i] + gates[:, c0 + 1:c0 + 2] * o_slc[i]
                                                           + gates[:, c0 + 2:c0 + 3] * o_win[i])


def _attn_prompt(q, gates, ck, cv, ks, vs, kw, vw):
    b, t, att = q.shape
    assert t % SLC_CHUNK == 0 and t % Q_BLOCK == 0
    n_blk = t // SEL_LEN
    n_cmp = t // CMP_STRIDE - 1
    assert n_blk <= LANES and ck.shape[1] == t // CMP_STRIDE
    n_pick = min(N_SEL, n_blk)
    win_len = min(WINDOW + Q_BLOCK, t)
    mimp = _importance_matrix(ck.shape[1], n_cmp, n_blk, LANES)
    n_ch = t // SLC_CHUNK
    esel = _expand_matrix(n_ch, SLC_CHUNK, [0] * n_ch)
    tok = lambda n: pl.BlockSpec((1, Q_BLOCK, n), lambda i, j: (i, j, 0))
    full = lambda a: pl.BlockSpec((1,) + a.shape[1:], lambda i, j: (i, 0, 0))
    const = lambda a: pl.BlockSpec(a.shape, lambda i, j: (0,) * a.ndim)
    return pl.pallas_call(
        functools.partial(_attn_prompt_kernel, seq=t, n_blk=n_blk, n_pick=n_pick, win_len=win_len),
        grid=(b, t // Q_BLOCK),
        in_specs=[tok(att), tok(gates.shape[2]), full(ck), full(cv), full(ks), full(vs), full(kw), full(vw),
                  const(mimp), const(esel)],
        out_specs=tok(att),
        out_shape=jax.ShapeDtypeStruct((b, t, att), F32),
        compiler_params=_cparams("arbitrary", "arbitrary"),
        name="attn_prompt",
    )(q, gates, ck, cv, ks, vs, kw, vw, mimp, esel)


def _attn_decode_kernel(pt_ref, q_ref, g_ref, ck_ref, cv_ref, ksn_ref, vsn_ref, kwn_ref, vwn_ref, wk_ref, wv_ref,
                        pk_ref, pv_ref, mimp_ref, esel_ref, o_ref, slab_k, slab_v, sem,
                        *, n_pages, page, past, n_blk, n_pick, tq):
    b = pl.program_id(0)
    slot = b % 2

    def fetch(bb, s, wait):
        def body(pg, carry):
            pid = pt_ref[bb * n_pages + pg]
            dst = pl.ds(pl.multiple_of(pg * page, page), page)
            for pool, slab, i in ((pk_ref, slab_k, 0), (pv_ref, slab_v, 1)):
                cp = pltpu.make_async_copy(pool.at[pid], slab.at[s, :, dst], sem.at[s, i])
                cp.wait() if wait else cp.start()
            return carry
        lax.fori_loop(0, n_pages, body, 0)

    @pl.when(b == 0)
    def _():
        fetch(b, slot, False)

    @pl.when(b + 1 < pl.num_programs(0))
    def _():
        fetch(b + 1, 1 - slot, False)

    rows = Q_PER_KV * DEC_Q
    qi = lax.broadcasted_iota(I32, (rows, 1), 0) % DEC_Q
    t_col = past + qi % tq
    hl = lax.broadcasted_iota(I32, (rows, 1), 0) // DEC_Q
    nc = ck_ref.shape[1]
    cmp_end = lax.broadcasted_iota(I32, (1, nc), 1) * CMP_STRIDE + (CMP_LEN - 1)
    dist_c = t_col - cmp_end
    mask_c = dist_c >= 0
    dist_cf = dist_c.astype(F32)
    blk_pad = mimp_ref.shape[1]
    t8 = past + lax.broadcasted_iota(I32, (DEC_Q, 1), 0) % tq
    blk_l = lax.broadcasted_iota(I32, (DEC_Q, blk_pad), 1)
    cur = t8 // SEL_LEN
    forced = (blk_l == 0) | (blk_l == cur) | (blk_l == cur - 1)
    future = blk_l * SEL_LEN > t8
    pad_blk = blk_l >= n_blk
    b_sub = lax.broadcasted_iota(I32, (blk_pad, blk_pad), 0)
    b_lane = lax.broadcasted_iota(I32, (blk_pad, blk_pad), 1)
    n_chunks = past // DEC_CHUNK
    new_rows = ksn_ref.shape[2]

    sels, slopes, o_cmps = [], [], []
    for g in range(N_KV):
        lanes = slice(g * HEAD_DIM, (g + 1) * HEAD_DIM)
        slope = jnp.zeros((rows, 1), F32)
        for i in range(Q_PER_KV):
            slope = jnp.where(hl == i, SLOPES[g * Q_PER_KV + i], slope)
        slopes.append(slope)
        qg = (q_ref[0, g] * (HEAD_DIM ** -0.5)).astype(BF16)
        p = _masked_softmax(_bdot_nt(qg, ck_ref[0, :, lanes]) - slope * dist_cf, mask_c)
        o_cmps.append(_bdot(p, cv_ref[0, :, lanes]))
        p_sum = p[0:DEC_Q]
        for i in range(1, Q_PER_KV):
            p_sum = p_sum + p[i * DEC_Q:(i + 1) * DEC_Q]
        imp = jnp.dot(p_sum, mimp_ref[...], precision=HIGHEST, preferred_element_type=F32)
        imp = jnp.where(pad_blk, -3e38, jnp.where(future, -FORCE, jnp.where(forced, FORCE, imp)))
        imp_t = jnp.transpose(jnp.concatenate([imp, jnp.zeros((LANES - DEC_Q, blk_pad), F32)], axis=0))
        sel_rows = []
        for r in range(DEC_Q):
            col = imp_t[:, r:r + 1]
            row = imp[r:r + 1, :]
            ahead = (col > row) | ((col == row) & (b_sub < b_lane))
            cnt = jnp.sum(jnp.where(ahead, 1.0, 0.0), axis=0, keepdims=True)
            sel_rows.append(jnp.where(cnt < n_pick, 1.0, 0.0))
        sel8 = jnp.concatenate(sel_rows, axis=0)
        sels.append(jnp.concatenate([sel8] * Q_PER_KV, axis=0))

    fetch(b, slot, True)

    for g in range(N_KV):
        lanes = slice(g * HEAD_DIM, (g + 1) * HEAD_DIM)
        slope2 = slopes[g] * LOG2E
        qg2 = (q_ref[0, g] * (HEAD_DIM ** -0.5 * LOG2E)).astype(BF16)
        sel = sels[g]
        sel_b = sel.astype(BF16)

        def bias_of(dist, visible):
            return jnp.where(visible, 0.0, NEG_INF) - slope2 * dist.astype(F32)

        m, acc = _flash_init(rows)
        per_mat = LANES * SEL_LEN // DEC_CHUNK
        for c in range(n_chunks):
            keys = slice(c * DEC_CHUNK, (c + 1) * DEC_CHUNK)
            lane0 = (c // per_mat) * LANES
            msel = jnp.dot(sel_b[:, lane0:lane0 + LANES], esel_ref[c % per_mat], preferred_element_type=F32)
            dist = t_col - (c * DEC_CHUNK + lax.broadcasted_iota(I32, (1, DEC_CHUNK), 1))
            m, acc = _flash_step(qg2, slab_k[slot, lanes, keys].astype(BF16), _aug_values(slab_v[slot, lanes, keys]),
                                 bias_of(dist, (dist >= 0) & (msel > 0.5)), m, acc)
        dist_n = t_col - (past + lax.broadcasted_iota(I32, (1, new_rows), 1))
        cur_blk = past // SEL_LEN
        m, acc = _flash_step(qg2, ksn_ref[0, lanes, :].astype(BF16), _aug_values(vsn_ref[0, lanes, :]),
                             bias_of(dist_n, (dist_n >= 0) & (sel[:, cur_blk:cur_blk + 1] > 0.5)), m, acc)
        o_slc = _flash_out(acc)
        wlen = wk_ref.shape[2]
        dist = t_col - (past - wlen + lax.broadcasted_iota(I32, (1, wlen), 1))
        m, acc = _flash_step(qg2, wk_ref[0, lanes, :].astype(BF16), _aug_values(wv_ref[0, lanes, :]),
                             bias_of(dist, (dist >= 0) & (dist <= WINDOW)), *_flash_init(rows))
        m, acc = _flash_step(qg2, kwn_ref[0, lanes, :].astype(BF16), _aug_values(vwn_ref[0, lanes, :]),
                             bias_of(dist_n, (dist_n >= 0) & (dist_n <= WINDOW)), m, acc)
        o_win = _flash_out(acc)
        gt = g_ref[0, g]
        o_ref[0, g] = gt[:, 0:1] * o_cmps[g] + gt[:, 1:2] * o_slc + gt[:, 2:3] * o_win


def _attn_decode(page_table, q, gates, ck, cv, ksn, vsn, kwn, vwn, win_k, win_v, pool_k, pool_v):
    nb, tq, att = q.shape
    n_pages = page_table.shape[1]
    page = pool_k.shape[2]
    past = n_pages * page
    assert tq <= DEC_Q and past % DEC_CHUNK == 0 and past % SEL_LEN == 0 and (past + tq - 1) // SEL_LEN == past // SEL_LEN
    assert win_k.shape[2] == WINDOW and ck.shape[1] * CMP_STRIDE == past and page % LANES == 0
    t_pad = -(-(past + tq) // SEL_LEN) * SEL_LEN
    n_blk = t_pad // SEL_LEN
    n_cmp = t_pad // CMP_STRIDE - 1
    n_pick = min(N_SEL, n_blk)
    blk_pad = -(-n_blk // LANES) * LANES
    mimp = _importance_matrix(ck.shape[1], n_cmp, n_blk, blk_pad)
    per_mat = LANES * SEL_LEN // DEC_CHUNK
    esel = _expand_matrix(per_mat, DEC_CHUNK, [0] * per_mat)

    def dec_rows(a, width):
        a = a.reshape(nb, tq, N_KV, Q_PER_KV, width)
        a = jnp.take(a, jnp.arange(DEC_Q) % tq, axis=1)
        return a.transpose(0, 2, 3, 1, 4).reshape(nb, N_KV, Q_PER_KV * DEC_Q, width)

    q_d = dec_rows(q, HEAD_DIM)
    g_d = dec_rows(gates[:, :, :N_HEADS * N_GATES], N_GATES)
    pad_new = lambda a: jnp.pad(a.transpose(0, 2, 1), ((0, 0), (0, 0), (0, LANES - tq)))
    ksn, vsn, kwn, vwn = [pad_new(a) for a in (ksn, vsn, kwn, vwn)]
    rows = Q_PER_KV * DEC_Q
    blk = lambda a: pl.BlockSpec((1,) + a.shape[1:], lambda i, pt: (i,) + (0,) * (a.ndim - 1))
    const = lambda a: pl.BlockSpec(a.shape, lambda i, pt: (0,) * a.ndim)
    any_spec = pl.BlockSpec(memory_space=pl.ANY)
    grid_spec = pltpu.PrefetchScalarGridSpec(
        num_scalar_prefetch=1, grid=(nb,),
        in_specs=[blk(q_d), blk(g_d), blk(ck), blk(cv), blk(ksn), blk(vsn), blk(kwn), blk(vwn), blk(win_k), blk(win_v),
                  any_spec, any_spec, const(mimp), const(esel)],
        out_specs=pl.BlockSpec((1, N_KV, rows, HEAD_DIM), lambda i, pt: (i, 0, 0, 0)),
        scratch_shapes=[pltpu.VMEM((2, KV_DIM, past), F32), pltpu.VMEM((2, KV_DIM, past), F32),
                        pltpu.SemaphoreType.DMA((2, 2))])
    o = pl.pallas_call(
        functools.partial(_attn_decode_kernel, n_pages=n_pages, page=page, past=past, n_blk=n_blk, n_pick=n_pick, tq=tq),
        grid_spec=grid_spec,
        out_shape=jax.ShapeDtypeStruct((nb, N_KV, rows, HEAD_DIM), F32),
        compiler_params=_cparams("arbitrary"),
        name="attn_decode",
    )(page_table.reshape(-1), q_d, g_d, ck, cv, ksn, vsn, kwn, vwn, win_k, win_v, pool_k, pool_v, mimp, esel)
    o = o.reshape(nb, N_KV, Q_PER_KV, DEC_Q, HEAD_DIM)[:, :, :, :tq]
    return o.transpose(0, 3, 1, 2, 4).reshape(nb, tq, att)


def _out_kernel(x_ref, conv_ref, att_ref, mod_ref, bc_ref, ba_ref, woc_ref, woa_ref, g1_ref, b1_ref, wr_ref, br_ref,
                cnt0_ref, x1_ref, u2_ref, idx_ref, w_ref, pos_ref, cnt_ref, carry_ref, *, tt, alpha, per_row):
    first = (pl.program_id(0) == 0) & (pl.program_id(1) == 0)

    @pl.when(first)
    def _():
        carry_ref[...] = cnt0_ref[...]

    gate1 = _mod_rows(mod_ref, 2, per_row)
    shift2 = _mod_rows(mod_ref, 3, per_row)
    scale2 = _mod_rows(mod_ref, 4, per_row)
    mix = _bdot(conv_ref[0] * bc_ref[...], woc_ref[...]) + _bdot(att_ref[0] * ba_ref[...], woa_ref[...])
    x1 = _layer_norm(alpha * x_ref[0] + gate1 * mix, g1_ref[...], b1_ref[...])
    x1_ref[0] = x1
    u2 = x1 * (1.0 + scale2) + shift2
    _store_tiled(u2_ref, 0, u2)
    logits = jnp.dot(u2, wr_ref[...], precision=HIGHEST, preferred_element_type=F32) + br_ref[...]
    lane = lax.broadcasted_iota(I32, logits.shape, 1)
    vals = logits
    top_v, top_i = [], []
    for _k in range(TOP_K):
        m = jnp.max(vals, axis=-1, keepdims=True)
        i = jnp.min(jnp.where(vals == m, lane, LANES), axis=-1, keepdims=True)
        top_v.append(m)
        top_i.append(i)
        vals = jnp.where(lane == i, -jnp.inf, vals)
    e = [jnp.exp(v - top_v[0]) for v in top_v]
    inv = 1.0 / (e[0] + e[1] + e[2] + e[3])
    onehot = [jnp.where(lane == i, 1.0, 0.0) for i in top_i]
    assigned = onehot[0] + onehot[1] + onehot[2] + onehot[3]
    r_i = lax.broadcasted_iota(I32, (tt, tt), 0)
    c_i = lax.broadcasted_iota(I32, (tt, tt), 1)
    before = jnp.where(r_i > c_i, 1.0, 0.0).astype(BF16)
    rank = carry_ref[...] + jnp.dot(before, assigned.astype(BF16), preferred_element_type=F32)
    idx_o = jnp.zeros(logits.shape, I32)
    w_o = jnp.zeros(logits.shape, F32)
    pos_o = jnp.zeros(logits.shape, F32)
    for k in range(TOP_K):
        idx_o = jnp.where(lane == k, top_i[k], idx_o)
        w_o = jnp.where(lane == k, e[k] * inv, w_o)
        pos_o = jnp.where(lane == k, jnp.sum(onehot[k] * rank, axis=-1, keepdims=True), pos_o)
    idx_ref[0] = idx_o
    w_ref[0] = w_o
    pos_ref[0] = pos_o.astype(I32)
    carry_ref[...] = carry_ref[...] + jnp.sum(assigned, axis=0, keepdims=True)
    cnt_ref[...] = carry_ref[...]


def _out_proj(x, conv_out, att, mod, vecs, wts, cnt0, *, tt, alpha, per_row):
    b, t, d = x.shape
    bc, ba, g1, b1, br = vecs
    woc, woa, wr = wts
    nt = t // tt
    const = lambda a: pl.BlockSpec(a.shape, lambda i, j: (0,) * a.ndim)
    tok = lambda n: pl.BlockSpec((1, tt, n), lambda i, j: (i, j, 0))
    mod_spec = (pl.BlockSpec((6, tt, d), lambda i, j: (0, j, 0)) if per_row
                else pl.BlockSpec((1, 6, d), lambda i, j: (i, 0, 0)))
    tok_shape = lambda n, dt: jax.ShapeDtypeStruct((b, t, n), dt)
    return pl.pallas_call(
        functools.partial(_out_kernel, tt=tt, alpha=alpha, per_row=per_row),
        grid=(b, nt),
        in_specs=[tok(d), tok(conv_out.shape[2]), tok(att.shape[2]), mod_spec, const(bc), const(ba), const(woc), const(woa),
                  const(g1), const(b1), const(wr), const(br), const(cnt0)],
        out_specs=[tok(d), pl.BlockSpec((tt * SUBLANES, LANES), lambda i, j: (i * nt + j, 0)),
                   tok(LANES), tok(LANES), tok(LANES), pl.BlockSpec((1, LANES), lambda i, j: (0, 0))],
        out_shape=[tok_shape(d, F32), jax.ShapeDtypeStruct((b * t * SUBLANES, LANES), F32), tok_shape(LANES, I32),
                   tok_shape(LANES, F32), tok_shape(LANES, I32),
                   jax.ShapeDtypeStruct((1, LANES), F32)],
        scratch_shapes=[pltpu.VMEM((1, LANES), F32)],
        compiler_params=_cparams("arbitrary", "arbitrary"),
        name="out_proj_router",
    )(x, conv_out, att, mod, bc, ba, woc, woa, g1, b1, wr, br, cnt0)


def _tile_rows(x):
    return [x[:, j * LANES:(j + 1) * LANES] for j in range(x.shape[1] // LANES)]


def _store_tiled(ref, at, x):
    rows = x.shape[0]
    for j, piece in enumerate(_tile_rows(x)):
        ref[pl.ds(at + j, rows, stride=SUBLANES), :] = piece


def _load_tiled(ref, at, rows):
    return jnp.concatenate([ref[pl.ds(at + j, rows, stride=SUBLANES), :] for j in range(SUBLANES)], axis=1)


def _gather_pipeline(i, n, src_ref, cur_ref, nxt_ref, bufs, sem, rows, compute):
    def row_copy(src, buf, s, r):
        at = r * SUBLANES if isinstance(r, int) else pl.multiple_of(r * SUBLANES, SUBLANES)
        return pltpu.make_async_copy(src_ref.at[pl.ds(pl.multiple_of(src, SUBLANES), SUBLANES)],
                                     buf.at[pl.ds(at, SUBLANES)], sem.at[s])

    def whole(buf, s):
        return pltpu.make_async_copy(src_ref.at[pl.ds(0, rows * SUBLANES)], buf, sem.at[s])

    @pl.when(i == 0)
    def _():
        def body(r, carry):
            row_copy(cur_ref[0, 0, r], bufs[0], 0, r).start()
            return carry
        lax.fori_loop(0, rows, body, 0)

    def step(s):
        whole(bufs[s], s).wait()
        for r in range(rows):
            row_copy(nxt_ref[0, 0, r], bufs[1 - s], 1 - s, r).start(priority=r % 2)
        compute(bufs[s])

        @pl.when(i == n - 1)
        def _():
            whole(bufs[1 - s], 1 - s).wait()

    for s in range(2):
        pl.when((i % 2 == s) & (i < n))(functools.partial(step, s))


def _expert_kernel(be_ref, nu_ref, cur_ref, nxt_ref, u_ref, wgu_ref, bgu_ref, wdn_ref, bdn_ref, y_ref, xb0, xb1, sem, *, bm, d_ff):
    def compute(buf):
        gu = jnp.dot(_load_tiled(buf, 0, bm).astype(BF16), wgu_ref[0].astype(BF16), preferred_element_type=F32) + bgu_ref[0]
        gl = jnp.minimum(gu[:, :d_ff], SWIGLU_LIMIT)
        lin = jnp.clip(gu[:, d_ff:], -SWIGLU_LIMIT, SWIGLU_LIMIT)
        act = gl * _sigmoid(SWIGLU_ALPHA * gl) * (lin + 1.0)
        _store_tiled(y_ref, 0, jnp.dot(act.astype(BF16), wdn_ref[0].astype(BF16), preferred_element_type=F32) + bdn_ref[0])

    _gather_pipeline(pl.program_id(0), nu_ref[0], u_ref, cur_ref, nxt_ref, (xb0, xb1), sem, bm, compute)

    @pl.when(pl.program_id(0) >= nu_ref[0])
    def _():
        y_ref[...] = jnp.zeros(y_ref.shape, F32)


def _expert_ffn(u2, slot_tok, blk_e, n_used, wgu, bgu, wdn, bdn, *, bm):
    n_blocks = slot_tok.shape[0]
    d = wgu.shape[1]
    assert d == SUBLANES * LANES
    d_ff = wdn.shape[1]
    slot_tok = slot_tok.reshape(n_blocks, 1, bm)
    smem = lambda f: pl.BlockSpec((1, 1, bm), f, memory_space=pltpu.SMEM)
    grid_spec = pltpu.PrefetchScalarGridSpec(
        num_scalar_prefetch=2, grid=(n_blocks,),
        in_specs=[smem(lambda i, be, nu: (i, 0, 0)),
                  smem(lambda i, be, nu: (jnp.minimum(i + 1, n_blocks - 1), 0, 0)),
                  pl.BlockSpec(memory_space=pl.ANY),
                  pl.BlockSpec((1, d, 2 * d_ff), lambda i, be, nu: (be[i], 0, 0)),
                  pl.BlockSpec((1, 1, 2 * d_ff), lambda i, be, nu: (be[i], 0, 0)),
                  pl.BlockSpec((1, d_ff, d), lambda i, be, nu: (be[i], 0, 0)),
                  pl.BlockSpec((1, 1, d), lambda i, be, nu: (be[i], 0, 0))],
        out_specs=pl.BlockSpec((bm * SUBLANES, LANES), lambda i, be, nu: (i, 0)),
        scratch_shapes=[pltpu.VMEM((bm * SUBLANES, LANES), F32), pltpu.VMEM((bm * SUBLANES, LANES), F32),
                        pltpu.SemaphoreType.DMA((2,))])
    return pl.pallas_call(
        functools.partial(_expert_kernel, bm=bm, d_ff=d_ff),
        grid_spec=grid_spec,
        out_shape=jax.ShapeDtypeStruct((n_blocks * bm * SUBLANES, LANES), F32),
        compiler_params=_cparams("arbitrary"),
        name="expert_ffn",
    )(blk_e, n_used, slot_tok, slot_tok, u2, wgu, bgu, wdn, bdn)


def _combine_kernel(cur_ref, nxt_ref, ys_ref, w_ref, x1_ref, mod_ref, g2_ref, b2_ref, y_ref, gb0, gb1, sem,
                    *, tc, alpha, per_row):
    def compute(buf):
        w = w_ref[...]
        f = w[:, 0:1] * _load_tiled(buf, 0, tc)
        for k in range(1, TOP_K):
            f = f + w[:, k:k + 1] * _load_tiled(buf, k * tc * SUBLANES, tc)
        gate2 = mod_ref[5] if per_row else mod_ref[0, 5:6, :]
        y_ref[...] = _layer_norm(alpha * x1_ref[...] + gate2 * f, g2_ref[...], b2_ref[...])

    _gather_pipeline(pl.program_id(0), pl.num_programs(0), ys_ref, cur_ref, nxt_ref, (gb0, gb1), sem, TOP_K * tc, compute)


def _combine(ys, dest_t, top_w, x1, mod, g2, b2, *, tc, alpha, per_row, tiles_per_batch):
    n, d = x1.shape
    n_tiles = n // tc
    dest_t = dest_t.reshape(n_tiles, 1, TOP_K * tc)
    smem = lambda f: pl.BlockSpec((1, 1, TOP_K * tc), f, memory_space=pltpu.SMEM)
    const = lambda a: pl.BlockSpec(a.shape, lambda i: (0,) * a.ndim)
    mod_spec = (pl.BlockSpec((6, tc, d), lambda i: (0, i, 0)) if per_row
                else pl.BlockSpec((1, 6, d), lambda i: (i // tiles_per_batch, 0, 0)))
    return pl.pallas_call(
        functools.partial(_combine_kernel, tc=tc, alpha=alpha, per_row=per_row),
        grid=(n_tiles,),
        in_specs=[smem(lambda i: (i, 0, 0)), smem(lambda i: (jnp.minimum(i + 1, n_tiles - 1), 0, 0)),
                  pl.BlockSpec(memory_space=pl.ANY),
                  pl.BlockSpec((tc, LANES), lambda i: (i, 0)), pl.BlockSpec((tc, d), lambda i: (i, 0)),
                  mod_spec, const(g2), const(b2)],
        out_specs=pl.BlockSpec((tc, d), lambda i: (i, 0)),
        out_shape=jax.ShapeDtypeStruct((n, d), F32),
        scratch_shapes=[pltpu.VMEM((TOP_K * tc * SUBLANES, LANES), F32), pltpu.VMEM((TOP_K * tc * SUBLANES, LANES), F32),
                        pltpu.SemaphoreType.DMA((2,))],
        compiler_params=_cparams("arbitrary"),
        name="combine_ln2",
    )(dest_t, dest_t, ys, top_w, x1, mod, g2, b2)


def _slot_table_kernel(dest_ref, init_ref, out_ref, sem, *, tt):
    i = pl.program_id(0)

    @pl.when(i == 0)
    def _():
        cp = pltpu.make_async_copy(init_ref, out_ref, sem.at[0])
        cp.start()
        cp.wait()

    unroll = min(tt, SUBLANES)

    def body(g, carry):
        for u in range(unroll):
            r = g * unroll + u
            for k in range(TOP_K):
                out_ref[dest_ref[0, 0, r * TOP_K + k]] = (i * tt + r) * SUBLANES
        return carry

    lax.fori_loop(0, tt // unroll, body, 0)


def _slot_table(dest, n_slots, *, tt):
    n = dest.shape[0]
    init = (jnp.arange(n_slots, dtype=I32) % n) * SUBLANES
    return pl.pallas_call(
        functools.partial(_slot_table_kernel, tt=tt),
        grid=(n // tt,),
        in_specs=[pl.BlockSpec((1, 1, tt * TOP_K), lambda i: (i, 0, 0), memory_space=pltpu.SMEM),
                  pl.BlockSpec(memory_space=pl.ANY)],
        out_specs=pl.BlockSpec(memory_space=pltpu.SMEM),
        out_shape=jax.ShapeDtypeStruct((n_slots,), I32),
        scratch_shapes=[pltpu.SemaphoreType.DMA((1,))],
        compiler_params=_cparams("arbitrary"),
        name="slot_table",
    )(dest.reshape(n // tt, 1, tt * TOP_K), init)


def _moe_experts(u2, top_i, pos, counts, experts, *, bm):
    n = top_i.shape[0]
    wgu, bgu, wdn, bdn = experts
    n_exp = wgu.shape[0]
    n_blocks = -(-n * TOP_K // bm) + n_exp
    counts = counts.astype(I32)
    padded = (counts + bm - 1) // bm * bm
    pad_end = jnp.cumsum(padded)
    pad_start = pad_end - padded
    dest = pad_start[top_i] + pos
    slot_tok = _slot_table(dest, n_blocks * bm, tt=_pick_tile(n, 2048))
    blk_start = jnp.arange(n_blocks, dtype=I32) * bm
    blk_e = jnp.minimum(jnp.sum((pad_end[None, :] <= blk_start[:, None]).astype(I32), axis=1), n_exp - 1)
    n_used = jnp.maximum(pad_end[-1:] // bm, 1)
    ys = _expert_ffn(u2, slot_tok.reshape(n_blocks, bm), blk_e, n_used, wgu, bgu, wdn, bdn, bm=bm)
    return ys, dest * SUBLANES


def _moe_combine(ys, dest, top_w, x1, mod, g2, b2, *, tc, alpha, per_row, tiles_per_batch):
    n = x1.shape[0]
    dest_t = dest.reshape(n // tc, tc, TOP_K).transpose(0, 2, 1).reshape(n // tc, TOP_K * tc)
    return _combine(ys, dest_t, top_w, x1, mod, g2, b2, tc=tc, alpha=alpha, per_row=per_row, tiles_per_batch=tiles_per_batch)


def _pick_tile(n, target):
    t = min(n, target)
    while n % t:
        t -= 1
    return t


def kernel(x_prompt, x_sample, c_prompt, c_sample, cache_cmp_k, cache_cmp_v, cache_slc_k, cache_slc_v, state_win_k, state_win_v, state_conv, page_table, w_ada, b_ada, w_in, w_dw, b_dw, conv_ln_g, conv_ln_b, w_ck1, w_ck2, pe_k, w_cv1, w_cv2, pe_v, beta_conv, beta_attn, w_out, ln1_g, ln1_b, w_router, b_router, w_gate_up, b_gate_up, w_down, b_down, ln2_g, ln2_b):
    depth = w_ada.shape[0]
    assert depth == 1, "single-layer trunk"
    alpha = (2 * depth) ** 0.25
    bp, seq, d = x_prompt.shape
    bs, tq, _ = x_sample.shape
    c_conv = w_dw.shape[2]
    att_dim = N_HEADS * HEAD_DIM
    n_pool, page = cache_cmp_k.shape[1], cache_cmp_k.shape[2]
    assert state_conv.shape[2] == CONV_K - 1 and page % CMP_STRIDE == 0

    wi = w_in[0]
    cuts = [2 * c_conv, 2 * c_conv + att_dim, 2 * c_conv + att_dim + 6 * KV_DIM]
    wa = wi[:, :cuts[0]].astype(BF16)
    wq = wi[:, cuts[0]:cuts[1]].astype(BF16)
    wkv = wi[:, cuts[1]:cuts[2]].astype(BF16)
    wg = jnp.pad(wi[:, cuts[2]:], ((0, 0), (0, LANES - N_HEADS * N_GATES))).astype(BF16)
    in_w = (wa, wq, wkv, wg)
    wdw = jnp.pad(w_dw[0], ((0, CONV_HIST - CONV_K), (0, 0)))
    row = lambda a: a.reshape(1, -1)
    conv_vecs = (wdw, row(b_dw[0]), row(conv_ln_g[0]), row(conv_ln_b[0]))
    wk_c = _compress_weights(w_ck1[0], w_ck2[0], pe_k[0])
    wv_c = _compress_weights(w_cv1[0], w_cv2[0], pe_v[0])
    out_vecs = (row(beta_conv[0]), row(beta_attn[0]), row(ln1_g[0]), row(ln1_b[0]),
                jnp.pad(row(b_router[0]), ((0, 0), (0, LANES - N_EXPERTS)), constant_values=NEG_INF))
    out_w = (w_out[0][:c_conv].astype(BF16), w_out[0][c_conv:].astype(BF16),
             jnp.pad(w_router[0], ((0, 0), (0, LANES - N_EXPERTS))))
    experts = (w_gate_up[0], b_gate_up[0][:, None, :], w_down[0], b_down[0][:, None, :])
    g2, b2 = row(ln2_g[0]), row(ln2_b[0])

    mod = _ada(jnp.concatenate([c_prompt, c_sample], axis=0), w_ada[0], row(b_ada[0])).reshape(bp + bs, 6, d)
    mod_p = mod[:bp]
    mod_s = jnp.repeat(mod[bp:], tq, axis=0).transpose(1, 0, 2)

    tt = _pick_tile(seq, 512)
    hist0 = jnp.zeros((bp, CONV_HIST, c_conv), F32)
    (conv_p, tail_p, q_p, kc_p, vc_p, kct_p, vct_p, kst_p, vst_p, kwt_p, vwt_p, gates_p) = _in_proj(
        x_prompt, mod_p, in_w, (hist0,) + conv_vecs, tt=tt, per_row=False)
    n_chunk = seq // CMP_STRIDE
    ck_p, cv_p = _compress_prompt(kc_p.reshape(bp, n_chunk, CHUNK_ROW), vc_p.reshape(bp, n_chunk, CHUNK_ROW), wk_c, wv_c)
    att_p = _attn_prompt(q_p, gates_p, ck_p, cv_p, kst_p, vst_p, kwt_p, vwt_p)
    to = _pick_tile(seq, 512)
    cnt0 = jnp.zeros((1, LANES), F32)
    n_p, n_s = bp * seq, bs * tq
    x1_p, u2_p, ti_p, tw_p, pos_p, cnt_p = _out_proj(x_prompt, conv_p, att_p, mod_p, out_vecs, out_w, cnt0,
                                                     tt=to, alpha=alpha, per_row=False)
    picks = lambda a: a.reshape(-1, LANES)[:, :TOP_K]
    ys_p, dest_p = _moe_experts(u2_p, picks(ti_p), picks(pos_p), cnt_p[0, :N_EXPERTS], experts, bm=_pick_tile(n_p * TOP_K, 512))
    tc_p = _pick_tile(seq, 128)
    y_p = _moe_combine(ys_p, dest_p, tw_p.reshape(n_p, LANES), x1_p.reshape(n_p, d), mod_p, g2, b2,
                       tc=tc_p, alpha=alpha, per_row=False, tiles_per_batch=seq // tc_p).reshape(bp, seq, d)

    (glu_s, q_s, kc_s, vc_s, ks_s, vs_s, kw_s, vw_s, gates_s) = _in_proj(
        x_sample.reshape(1, n_s, d), mod_s, in_w, None, tt=n_s, per_row=True)
    unflat = lambda a: a.reshape(bs, tq, a.shape[-1])
    glu_s, q_s, kc_s, vc_s, ks_s, vs_s, kw_s, vw_s, gates_s = map(unflat, (glu_s, q_s, kc_s, vc_s, ks_s, vs_s, kw_s, vw_s, gates_s))
    xc_s = jnp.concatenate([jnp.zeros((bs, CONV_HIST - (CONV_K - 1), c_conv), F32), state_conv[0], glu_s], axis=1)
    conv_s = _conv_small(xc_s, *conv_vecs, tt=tq)
    dim_major = lambda a: a.transpose(0, 2, 3, 1).reshape(a.shape[0], KV_DIM, a.shape[1])
    ck_s, cv_s = _compress_decode(page_table, dim_major(cache_cmp_k[0]), dim_major(cache_cmp_v[0]), wk_c, wv_c)
    att_s = _attn_decode(page_table, q_s, gates_s, ck_s, cv_s, ks_s, vs_s, kw_s, vw_s, dim_major(state_win_k[0]),
                         dim_major(state_win_v[0]), dim_major(cache_slc_k[0]), dim_major(cache_slc_v[0]))
    flat = lambda a: a.reshape(1, n_s, a.shape[-1])
    x1_s, u2_s, ti_s, tw_s, pos_s, cnt_s = _out_proj(flat(x_sample), flat(conv_s), flat(att_s), mod_s, out_vecs, out_w, cnt0,
                                                     tt=n_s, alpha=alpha, per_row=True)
    bm_s = max(2 * SUBLANES, min(512, n_s * TOP_K // N_EXPERTS // SUBLANES * SUBLANES))
    ys_s, dest_s = _moe_experts(u2_s, picks(ti_s), picks(pos_s), cnt_s[0, :N_EXPERTS], experts, bm=bm_s)
    y_s = _moe_combine(ys_s, dest_s, tw_s.reshape(n_s, LANES), x1_s.reshape(n_s, d), mod_s, g2, b2,
                       tc=n_s, alpha=alpha, per_row=True, tiles_per_batch=1).reshape(bs, tq, d)

    heads = lambda a: a.reshape(1, a.shape[0], a.shape[1], N_KV, HEAD_DIM)
    win_keep = min(WINDOW, seq)
    heads_t = lambda a: a.reshape(1, a.shape[0], N_KV, HEAD_DIM, a.shape[2]).transpose(0, 1, 4, 2, 3)
    new_p = (heads_t(kct_p), heads_t(vct_p), heads_t(kst_p), heads_t(vst_p), heads_t(kwt_p[:, :, seq - win_keep:]),
             heads_t(vwt_p[:, :, seq - win_keep:]), tail_p[None, :, CONV_HIST - (CONV_K - 1):])
    win_buf = state_win_k.shape[2]
    keep = lambda old, new: jnp.concatenate([old[0], new.reshape(bs, tq, N_KV, HEAD_DIM)], axis=1)[None, :, -win_buf:]
    new_s = (heads(kc_s), heads(vc_s), heads(ks_s), heads(vs_s), keep(state_win_k, kw_s), keep(state_win_v, vw_s),
             jnp.concatenate([state_conv[0], glu_s], axis=1)[None, :, -(CONV_K - 1):])
    return (y_p, y_s) + new_p + new_s
```

```python
import functools

import jax
import jax.numpy as jnp
from jax import lax
from jax.experimental import pallas as pl
from jax.experimental.pallas import tpu as pltpu

F32 = jnp.float32
BF16 = jnp.bfloat16
I32 = jnp.int32

N_HEADS = 8
HEAD_DIM = 64
N_KV = 2
Q_PER_KV = N_HEADS // N_KV
KV_DIM = N_KV * HEAD_DIM
CMP_LEN = 32
CMP_STRIDE = 16
CMP_HID = 2 * HEAD_DIM
SEL_LEN = 64
SEL_RATIO = SEL_LEN // CMP_STRIDE
N_SEL = 16
WINDOW = 512
Q_BLOCK = 128
N_GATES = 3
N_EXPERTS = 32
TOP_K = 4
CONV_K = 31
SWIGLU_LIMIT = 7.0
SWIGLU_ALPHA = 1.702
LN_EPS = 1e-5
NEG_INF = -1e30
FORCE = 1e9
SLOPES = tuple(2.0 ** (-8.0 * (h + 1) / N_HEADS) for h in range(N_HEADS))

LANES = 128
SUBLANES = 8
VMEM_LIMIT_BYTES = 56 * 1024 * 1024

CONV_HIST = 32
CHUNK_ROW = CMP_STRIDE * KV_DIM
SLC_CHUNK = 512
DEC_CHUNK = 2048
DEC_Q = 8
V_AUG = HEAD_DIM + 16
LOG2E = 1.4426950408889634
HIGHEST = lax.Precision.HIGHEST


def _cparams(*sem):
    return pltpu.CompilerParams(dimension_semantics=sem, vmem_limit_bytes=VMEM_LIMIT_BYTES)


def _sigmoid(x):
    return 1.0 / (1.0 + jnp.exp(-x))


def _bdot(a, b):
    return jnp.dot(a.astype(BF16), b.astype(BF16), preferred_element_type=F32)


def _bdot_nt(a, b):
    return lax.dot_general(a.astype(BF16), b.astype(BF16), (((1,), (1,)), ((), ())), preferred_element_type=F32)


def _layer_norm(x, g, b):
    mu = jnp.mean(x, -1, keepdims=True)
    xc = x - mu
    var = jnp.mean(xc * xc, -1, keepdims=True)
    return xc * lax.rsqrt(var + LN_EPS) * g + b


def _gelu_tanh(x):
    return 0.5 * x * (1.0 + jnp.tanh(0.7978845608028654 * (x + 0.044715 * (x * x * x))))


def _masked_softmax(s, mask):
    s = jnp.where(mask, s, NEG_INF)
    e = jnp.where(mask, jnp.exp(s - jnp.max(s, -1, keepdims=True)), 0.0)
    return e * (1.0 / jnp.maximum(jnp.sum(e, -1, keepdims=True), 1e-30))


def _ada_kernel(c_ref, w_ref, b_ref, o_ref):
    c = c_ref[...]
    o_ref[...] = _bdot(c * _sigmoid(c), w_ref[...]) + b_ref[...]


def _ada(c_all, w_ada, b_ada):
    nb, d = c_all.shape
    n = w_ada.shape[1]
    bn = d
    return pl.pallas_call(
        _ada_kernel,
        grid=(n // bn,),
        in_specs=[pl.BlockSpec((nb, d), lambda i: (0, 0)),
                  pl.BlockSpec((d, bn), lambda i: (0, i)),
                  pl.BlockSpec((1, bn), lambda i: (0, i))],
        out_specs=pl.BlockSpec((nb, bn), lambda i: (0, i)),
        out_shape=jax.ShapeDtypeStruct((nb, n), F32),
        compiler_params=_cparams("arbitrary"),
        name="ada",
    )(c_all, w_ada, b_ada)


def _mod_rows(mod_ref, i, per_row):
    return mod_ref[i] if per_row else mod_ref[0, i:i + 1, :]


def _conv_ln_silu(xc_ref, row0, n, wdw_ref, bdw, lng, lnb, shifted_ref=None):
    acc = jnp.zeros((n, xc_ref.shape[1]), F32) + bdw
    off = CONV_HIST - (CONV_K - 1)
    for k in range(CONV_K):
        at = row0 + off + k
        if shifted_ref is None or at % SUBLANES == 0:
            x = xc_ref[pl.ds(at, n), :]
        else:
            x = shifted_ref[at % SUBLANES - 1, pl.ds(at - at % SUBLANES, n), :]
        acc = acc + x * wdw_ref[k:k + 1, :]
    y = _layer_norm(acc, lng, lnb)
    return y * _sigmoid(y)


def _in_kernel(*refs, tt, c_conv, fuse_conv, per_row):
    if fuse_conv:
        (x_ref, mod_ref, wa_ref, wq_ref, wkv_ref, wg_ref, hist_ref, wdw_ref, bdw_ref, lng_ref, lnb_ref,
         conv_ref, tail_ref, q_ref, kc_ref, vc_ref, *kvt_refs, gates_ref, xc_ref, xs_ref) = refs
    else:
        (x_ref, mod_ref, wa_ref, wq_ref, wkv_ref, wg_ref,
         glu_ref, q_ref, kc_ref, vc_ref, ks_ref, vs_ref, kw_ref, vw_ref, gates_ref) = refs
    x = x_ref[0]
    shift1 = _mod_rows(mod_ref, 0, per_row)
    scale1 = _mod_rows(mod_ref, 1, per_row)
    u = (x * (1.0 + scale1) + shift1).astype(BF16)
    a = jnp.dot(u, wa_ref[...], preferred_element_type=F32)
    glu = a[:, :c_conv] * _sigmoid(a[:, c_conv:])
    q_ref[0] = jnp.dot(u, wq_ref[...], preferred_element_type=F32)
    kv = jnp.dot(u, wkv_ref[...], preferred_element_type=F32)
    gates_ref[0] = _sigmoid(jnp.dot(u, wg_ref[...], preferred_element_type=F32))
    if not fuse_conv:
        for i, r in enumerate((kc_ref, vc_ref, ks_ref, vs_ref, kw_ref, vw_ref)):
            r[0] = kv[:, i * KV_DIM:(i + 1) * KV_DIM]
        glu_ref[0] = glu
        return
    kc_ref[0] = kv[:, 0:KV_DIM]
    vc_ref[0] = kv[:, KV_DIM:2 * KV_DIM]
    for i, r in enumerate(kvt_refs):
        r[0] = jnp.transpose(kv[:, i * KV_DIM:(i + 1) * KV_DIM])

    @pl.when(pl.program_id(1) == 0)
    def _():
        xc_ref[0:CONV_HIST, :] = hist_ref[0]

    xc_ref[CONV_HIST:CONV_HIST + tt, :] = glu
    rows = min(tt, 64)
    assert rows % SUBLANES == 0
    keep = CONV_HIST + tt - SUBLANES
    for b in range(1, SUBLANES):
        xs_ref[b - 1, 0:keep, :] = xc_ref[b:b + keep, :]
    for r0 in range(0, tt, rows):
        conv_ref[0, r0:r0 + rows, :] = _conv_ln_silu(xc_ref, r0, rows, wdw_ref, bdw_ref[...], lng_ref[...], lnb_ref[...], xs_ref)
    tail = xc_ref[tt:tt + CONV_HIST, :]
    xc_ref[0:CONV_HIST, :] = tail
    tail_ref[0] = tail


def _in_proj(x, mod, wts, conv=None, *, tt, per_row):
    b, t, d = x.shape
    wa, wq, wkv, wg = wts
    c_conv = wa.shape[1] // 2
    fuse = conv is not None
    nt = t // tt
    const = lambda shape: pl.BlockSpec(shape, lambda i, j: (0,) * len(shape))
    mod_spec = (pl.BlockSpec((6, tt, d), lambda i, j: (0, j, 0)) if per_row
                else pl.BlockSpec((1, 6, d), lambda i, j: (i, 0, 0)))
    in_specs = [pl.BlockSpec((1, tt, d), lambda i, j: (i, j, 0)), mod_spec,
                const(wa.shape), const(wq.shape), const(wkv.shape), const(wg.shape)]
    args = [x, mod, wa, wq, wkv, wg]
    tok = lambda n: pl.BlockSpec((1, tt, n), lambda i, j: (i, j, 0))
    tok_shape = lambda n: jax.ShapeDtypeStruct((b, t, n), F32)
    out_specs, out_shape, scratch = [], [], []
    if fuse:
        hist, wdw, bdw, lng, lnb = conv
        in_specs += [pl.BlockSpec((1, CONV_HIST, c_conv), lambda i, j: (i, 0, 0)),
                     const(wdw.shape), const(bdw.shape), const(lng.shape), const(lnb.shape)]
        args += [hist, wdw, bdw, lng, lnb]
        out_specs += [tok(c_conv), pl.BlockSpec((1, CONV_HIST, c_conv), lambda i, j: (i, 0, 0))]
        out_shape += [tok_shape(c_conv), jax.ShapeDtypeStruct((b, CONV_HIST, c_conv), F32)]
        scratch = [pltpu.VMEM((CONV_HIST + tt, c_conv), F32), pltpu.VMEM((SUBLANES - 1, CONV_HIST + tt, c_conv), F32)]
    else:
        out_specs += [tok(c_conv)]
        out_shape += [tok_shape(c_conv)]
    if fuse:
        kv_specs = [tok(KV_DIM)] * 2 + [pl.BlockSpec((1, KV_DIM, tt), lambda i, j: (i, 0, j))] * 6
        kv_shape = [tok_shape(KV_DIM)] * 2 + [jax.ShapeDtypeStruct((b, KV_DIM, t), F32)] * 6
    else:
        kv_specs, kv_shape = [tok(KV_DIM)] * 6, [tok_shape(KV_DIM)] * 6
    out_specs += [tok(wq.shape[1])] + kv_specs + [tok(wg.shape[1])]
    out_shape += [tok_shape(wq.shape[1])] + kv_shape + [tok_shape(wg.shape[1])]
    return pl.pallas_call(
        functools.partial(_in_kernel, tt=tt, c_conv=c_conv, fuse_conv=fuse, per_row=per_row),
        grid=(b, nt), in_specs=in_specs, out_specs=out_specs, out_shape=out_shape, scratch_shapes=scratch,
        compiler_params=_cparams("arbitrary", "arbitrary"),
        name="in_proj_conv" if fuse else "in_proj",
    )(*args)


def _conv_small_kernel(xc_ref, wdw_ref, bdw_ref, lng_ref, lnb_ref, o_ref, *, tt):
    o_ref[0] = _conv_ln_silu(xc_ref.at[0], 0, tt, wdw_ref, bdw_ref[...], lng_ref[...], lnb_ref[...])


def _conv_small(xc, wdw, bdw, lng, lnb, *, tt):
    b, rows, c = xc.shape
    const = lambda shape: pl.BlockSpec(shape, lambda i: (0,) * len(shape))
    return pl.pallas_call(
        functools.partial(_conv_small_kernel, tt=tt),
        grid=(b,),
        in_specs=[pl.BlockSpec((1, rows, c), lambda i: (i, 0, 0)),
                  const(wdw.shape), const(bdw.shape), const(lng.shape), const(lnb.shape)],
        out_specs=pl.BlockSpec((1, tt, c), lambda i: (i, 0, 0)),
        out_shape=jax.ShapeDtypeStruct((b, tt, c), F32),
        compiler_params=_cparams("arbitrary"),
        name="conv_small",
    )(xc, wdw, bdw, lng, lnb)


def _compress_weights(w1, w2, pe):
    half = CMP_STRIDE * HEAD_DIM

    def expand(w):
        w = w.reshape(CMP_STRIDE, HEAD_DIM, CMP_HID)
        z = jnp.zeros_like(w)
        g0 = jnp.concatenate([w, z], axis=1).reshape(CHUNK_ROW, CMP_HID)
        g1 = jnp.concatenate([z, w], axis=1).reshape(CHUNK_ROW, CMP_HID)
        return jnp.concatenate([g0, g1], axis=1)

    w1p = jnp.concatenate([expand(w1[:half]), expand(w1[half:])], axis=1).astype(BF16)
    z2 = jnp.zeros_like(w2)
    w2p = jnp.concatenate([jnp.concatenate([w2, z2], axis=1), jnp.concatenate([z2, w2], axis=1)], axis=0).astype(BF16)
    return w1p, w1.astype(BF16), pe.reshape(1, -1), w2p


def _compress_tail(r, w1_ref, pe_ref, w2p_ref):
    n = r.shape[0]
    pe8 = jnp.broadcast_to(pe_ref[...], (SUBLANES, pe_ref.shape[1]))
    c = _bdot(pe8, w1_ref[...])[0:1]
    c2 = jnp.concatenate([c, c], axis=1)
    h = r[:, :2 * CMP_HID] + pltpu.roll(r[:, 2 * CMP_HID:], n - 1, 0) + c2
    return _bdot(_gelu_tanh(h), w2p_ref[...])


def _compress_prompt_kernel(ak_ref, av_ref, w1pk, w1k, pek, w2pk, w1pv, w1v, pev, w2pv, ck_ref, cv_ref):
    ck_ref[0] = _compress_tail(_bdot(ak_ref[0], w1pk[...]), w1k, pek, w2pk)
    cv_ref[0] = _compress_tail(_bdot(av_ref[0], w1pv[...]), w1v, pev, w2pv)


def _compress_prompt(ak, av, wk, wv):
    b, n, _ = ak.shape
    const = lambda a: pl.BlockSpec(a.shape, lambda i: (0,) * a.ndim)
    row = pl.BlockSpec((1, n, CHUNK_ROW), lambda i: (i, 0, 0))
    out = pl.BlockSpec((1, n, KV_DIM), lambda i: (i, 0, 0))
    return pl.pallas_call(
        _compress_prompt_kernel,
        grid=(b,),
        in_specs=[row, row] + [const(a) for a in wk] + [const(a) for a in wv],
        out_specs=[out, out],
        out_shape=[jax.ShapeDtypeStruct((b, n, KV_DIM), F32)] * 2,
        compiler_params=_cparams("arbitrary"),
        name="compress_prompt",
    )(ak, av, *wk, *wv)


def _compress_decode_kernel(pt_ref, pk_ref, pv_ref, w1pk, w1k, pek, w2pk, w1pv, w1v, pev, w2pv, ck_ref, cv_ref,
                            slab_k, slab_v, rows_buf, rbuf, sem, *, n_pages, page):
    j = pl.program_id(0)
    n_jobs = pl.num_programs(0)
    past = n_pages * page
    n = past // CMP_STRIDE
    step = min(n, 256)
    tcols = min(past, 2048)

    def fetch(job, pool_ref, slab, s, wait):
        def body(pg, carry):
            pid = pt_ref[(job // 2) * n_pages + pg]
            cp = pltpu.make_async_copy(pool_ref.at[pid], slab.at[:, pl.ds(pl.multiple_of(pg * page, page), page)], sem.at[s])
            cp.wait() if wait else cp.start()
            return carry
        lax.fori_loop(0, n_pages, body, 0)

    def project(slab, w1p):
        def to_rows(i, carry):
            at = pl.ds(pl.multiple_of(i * tcols, tcols), tcols)
            rows_buf[at, :] = jnp.transpose(slab[:, at])
            return carry
        lax.fori_loop(0, past // tcols, to_rows, 0)

        def body(i, carry):
            base = pl.multiple_of(i * (step * CMP_STRIDE), step * CMP_STRIDE)
            a = jnp.concatenate([rows_buf[pl.ds(base + p, step, stride=CMP_STRIDE), :] for p in range(CMP_STRIDE)], axis=1)
            rbuf[pl.ds(pl.multiple_of(i * step, step), step), :] = _bdot(a, w1p[...])
            return carry
        lax.fori_loop(0, n // step, body, 0)
        return rbuf[...]

    @pl.when(j == 0)
    def _():
        fetch(j, pk_ref, slab_k, 0, False)

    @pl.when(j % 2 == 0)
    def _():
        fetch(j + 1, pv_ref, slab_v, 1, False)
        fetch(j, pk_ref, slab_k, 0, True)
        ck_ref[0] = _compress_tail(project(slab_k, w1pk), w1k, pek, w2pk)

    @pl.when(j % 2 == 1)
    def _():
        @pl.when(j + 1 < n_jobs)
        def _():
            fetch(j + 1, pk_ref, slab_k, 0, False)
        fetch(j, pv_ref, slab_v, 1, True)
        cv_ref[0] = _compress_tail(project(slab_v, w1pv), w1v, pev, w2pv)


def _compress_decode(page_table, pool_k, pool_v, wk, wv):
    nb, n_pages = page_table.shape
    page = pool_k.shape[2]
    past = n_pages * page
    n = past // CMP_STRIDE
    assert page % LANES == 0
    const = lambda a: pl.BlockSpec(a.shape, lambda i, pt: (0,) * a.ndim)
    any_spec = pl.BlockSpec(memory_space=pl.ANY)
    out = pl.BlockSpec((1, n, KV_DIM), lambda i, pt: (i // 2, 0, 0))
    grid_spec = pltpu.PrefetchScalarGridSpec(
        num_scalar_prefetch=1, grid=(2 * nb,),
        in_specs=[any_spec, any_spec] + [const(a) for a in wk] + [const(a) for a in wv],
        out_specs=[out, out],
        scratch_shapes=[pltpu.VMEM((KV_DIM, past), F32), pltpu.VMEM((KV_DIM, past), F32), pltpu.VMEM((past, KV_DIM), F32),
                        pltpu.VMEM((n, 4 * CMP_HID), F32), pltpu.SemaphoreType.DMA((2,))])
    return pl.pallas_call(
        functools.partial(_compress_decode_kernel, n_pages=n_pages, page=page),
        grid_spec=grid_spec,
        out_shape=[jax.ShapeDtypeStruct((nb, n, KV_DIM), F32)] * 2,
        compiler_params=_cparams("arbitrary"),
        name="compress_decode",
    )(page_table.reshape(-1), pool_k, pool_v, *wk, *wv)


def _importance_matrix(n_cmp_rows, n_cmp, n_blk, cols):
    c = jnp.arange(n_cmp_rows)[:, None]
    b = jnp.arange(cols)[None, :]
    d = c - (SEL_RATIO * b - 1)
    return ((d >= 0) & (d <= SEL_RATIO) & (c < n_cmp) & (b < n_blk)).astype(F32)


def _expand_matrix(n_chunks, chunk, blk0):
    key = jnp.arange(n_chunks)[:, None, None] * chunk + jnp.arange(chunk)[None, None, :]
    blk = jnp.asarray(blk0)[:, None, None] + jnp.arange(LANES)[None, :, None]
    return (key // SEL_LEN == blk).astype(BF16)


def _aug_values(vt):
    r = lax.broadcasted_iota(I32, (V_AUG - HEAD_DIM, vt.shape[1]), 0)
    return jnp.concatenate([vt, jnp.where(r == 0, 1.0, 0.0)], axis=0).astype(BF16)


def _flash_step(qh, kt, vta, bias, m, acc):
    s = jnp.dot(qh, kt, preferred_element_type=F32) + bias
    m_new = jnp.maximum(m, jnp.max(s, -1, keepdims=True))
    p = jnp.exp2(s - m_new).astype(BF16)
    pv = lax.dot_general(p, vta, (((1,), (1,)), ((), ())), preferred_element_type=F32)
    return m_new, jnp.exp2(m - m_new) * acc + pv


def _flash_init(rows):
    return jnp.full((rows, 1), NEG_INF, F32), jnp.zeros((rows, V_AUG), F32)


def _flash_out(acc):
    return acc[:, :HEAD_DIM] * (1.0 / jnp.maximum(acc[:, HEAD_DIM:HEAD_DIM + 1], 1e-30))


def _attn_prompt_kernel(q_ref, g_ref, ck_ref, cv_ref, kst_ref, vst_ref, kwt_ref, vwt_ref, mimp_ref, esel_ref, o_ref,
                        *, seq, n_blk, n_pick, win_len):
    qb = Q_BLOCK
    qs = pl.program_id(1) * qb
    t_col = qs + lax.broadcasted_iota(I32, (qb, 1), 0)
    gates = g_ref[0]
    scale = HEAD_DIM ** -0.5
    q_heads = [q_ref[0, :, h * HEAD_DIM:(h + 1) * HEAD_DIM] for h in range(N_HEADS)]
    nc = ck_ref.shape[1]
    cmp_end = lax.broadcasted_iota(I32, (1, nc), 1) * CMP_STRIDE + (CMP_LEN - 1)
    dist_c = t_col - cmp_end
    mask_c = dist_c >= 0
    dist_cf = dist_c.astype(F32)
    blk_pad = mimp_ref.shape[1]

    blk_i = lax.broadcasted_iota(I32, (n_blk, qb), 0)
    t_row = qs + lax.broadcasted_iota(I32, (n_blk, qb), 1)
    cur = t_row // SEL_LEN
    forced = (blk_i == 0) | (blk_i == cur) | (blk_i == cur - 1)
    future = blk_i * SEL_LEN > t_row

    win_start = pl.multiple_of(jnp.clip(qs - WINDOW, 0, seq - win_len), qb)
    dist_w = t_col - (win_start + lax.broadcasted_iota(I32, (1, win_len), 1))
    maskbias_w = jnp.where((dist_w >= 0) & (dist_w <= WINDOW), 0.0, NEG_INF)
    n_chunks = (qs + qb + SLC_CHUNK - 1) // SLC_CHUNK
    rows = Q_PER_KV * qb
    stack = lambda parts: jnp.concatenate(parts, axis=0)
    heads_of = lambda a: a.reshape(Q_PER_KV, qb, a.shape[1])

    def flash(qk, alibi, maskbias, off, vta, m, acc):
        s = (heads_of(qk + alibi) + maskbias[None]).reshape(rows, qk.shape[1])
        m_new = jnp.maximum(m, jnp.max(s, -1, keepdims=True) + off)
        p = jnp.exp2(s - (m_new - off)).astype(BF16)
        pv = lax.dot_general(p, vta, (((1,), (1,)), ((), ())), preferred_element_type=F32)
        return m_new, jnp.exp2(m - m_new) * acc + pv

    lanes_of = lambda g: slice(g * HEAD_DIM, (g + 1) * HEAD_DIM)
    heads_in = lambda g: range(g * Q_PER_KV, (g + 1) * Q_PER_KV)
    qg2s, slope2s, sels, o_cmps = [], [], [], []
    for g in range(N_KV):
        lanes, heads = lanes_of(g), heads_in(g)
        slope = stack([jnp.full((qb, 1), SLOPES[h], F32) for h in heads])
        qg = stack([(q_heads[h] * scale).astype(BF16) for h in heads])
        qg2s.append(stack([(q_heads[h] * (scale * LOG2E)).astype(BF16) for h in heads]))
        slope2s.append(slope * LOG2E)
        s_c = heads_of(_bdot_nt(qg, ck_ref[0, :, lanes]) - slope * stack([dist_cf] * Q_PER_KV))
        p_c = _masked_softmax(s_c, mask_c[None])
        p_sum = jnp.sum(p_c, axis=0)
        o_cmp = heads_of(_bdot(p_c.reshape(rows, nc), cv_ref[0, :, lanes]))
        imp = jnp.dot(p_sum, mimp_ref[...], precision=HIGHEST, preferred_element_type=F32)
        imp_t = jnp.transpose(imp)[0:n_blk, :]
        imp_t = jnp.where(future, -FORCE, jnp.where(forced, FORCE, imp_t))
        cnt = jnp.zeros((n_blk, qb), F32)
        for b2 in range(n_blk):
            row = imp_t[b2:b2 + 1, :]
            ahead = (row > imp_t) | ((row == imp_t) & (blk_i > b2))
            cnt = cnt + jnp.where(ahead, 1.0, 0.0)
        sel_t = jnp.where(cnt < n_pick, 1.0, 0.0)
        if n_blk < blk_pad:
            sel_t = jnp.concatenate([sel_t, jnp.zeros((blk_pad - n_blk, qb), F32)], axis=0)
        sels.append(jnp.transpose(sel_t).astype(BF16))
        o_cmps.append(o_cmp)

    key_slc = lax.broadcasted_iota(I32, (1, SLC_CHUNK), 1)
    key_win = lax.broadcasted_iota(I32, (1, win_len), 1).astype(F32)
    alibi_slc = [s2 * key_slc.astype(F32) for s2 in slope2s]

    def slc_step(c, carry):
        k0 = pl.multiple_of(c * SLC_CHUNK, SLC_CHUNK)
        causal = t_col - (k0 + key_slc) >= 0
        out = []
        for g in range(N_KV):
            kt = kst_ref[0, lanes_of(g), pl.ds(k0, SLC_CHUNK)].astype(BF16)
            vta = _aug_values(vst_ref[0, lanes_of(g), pl.ds(k0, SLC_CHUNK)])
            msel = jnp.dot(sels[g], esel_ref[c], preferred_element_type=F32)
            maskbias = jnp.where(causal & (msel > 0.5), 0.0, NEG_INF)
            qk = jnp.dot(qg2s[g], kt, preferred_element_type=F32)
            out += flash(qk, alibi_slc[g], maskbias, slope2s[g] * k0.astype(F32), vta, *carry[2 * g:2 * g + 2])
        return tuple(out)

    fin = lax.fori_loop(0, n_chunks, slc_step, _flash_init(rows) * N_KV)
    for g in range(N_KV):
        o_slc = heads_of(_flash_out(fin[2 * g + 1]))
        kwt = kwt_ref[0, lanes_of(g), pl.ds(win_start, win_len)].astype(BF16)
        vwta = _aug_values(vwt_ref[0, lanes_of(g), pl.ds(win_start, win_len)])
        qk = jnp.dot(qg2s[g], kwt, preferred_element_type=F32)
        o_win = heads_of(_flash_out(flash(qk, slope2s[g] * key_win, maskbias_w, 0.0, vwta, *_flash_init(rows))[1]))
        for i, h in enumerate(heads_in(g)):
            c0 = h * N_GATES
            o_ref[0, :, h * HEAD_DIM:(h + 1) * HEAD_DIM] = (gates[:, c0:c0 + 1] * o_cmps[g][i] + gates[:, c0 + 1:c0 + 2] * o_slc[i]
                                                           + gates[:, c0 + 2:c0 + 3] * o_win[i])


def _attn_prompt(q, gates, ck, cv, ks, vs, kw, vw):
    b, t, att = q.shape
    assert t % SLC_CHUNK == 0 and t % Q_BLOCK == 0
    n_blk = t // SEL_LEN
    n_cmp = t // CMP_STRIDE - 1
    assert n_blk <= LANES and ck.shape[1] == t // CMP_STRIDE
    n_pick = min(N_SEL, n_blk)
    win_len = min(WINDOW + Q_BLOCK, t)
    mimp = _importance_matrix(ck.shape[1], n_cmp, n_blk, LANES)
    n_ch = t // SLC_CHUNK
    esel = _expand_matrix(n_ch, SLC_CHUNK, [0] * n_ch)
    tok = lambda n: pl.BlockSpec((1, Q_BLOCK, n), lambda i, j: (i, j, 0))
    full = lambda a: pl.BlockSpec((1,) + a.shape[1:], lambda i, j: (i, 0, 0))
    const = lambda a: pl.BlockSpec(a.shape, lambda i, j: (0,) * a.ndim)
    return pl.pallas_call(
        functools.partial(_attn_prompt_kernel, seq=t, n_blk=n_blk, n_pick=n_pick, win_len=win_len),
        grid=(b, t // Q_BLOCK),
        in_specs=[tok(att), tok(gates.shape[2]), full(ck), full(cv), full(ks), full(vs), full(kw), full(vw),
                  const(mimp), const(esel)],
        out_specs=tok(att),
        out_shape=jax.ShapeDtypeStruct((b, t, att), F32),
        compiler_params=_cparams("arbitrary", "arbitrary"),
        name="attn_prompt",
    )(q, gates, ck, cv, ks, vs, kw, vw, mimp, esel)


def _attn_decode_kernel(pt_ref, q_ref, g_ref, ck_ref, cv_ref, ksn_ref, vsn_ref, kwn_ref, vwn_ref, wk_ref, wv_ref,
                        pk_ref, pv_ref, mimp_ref, esel_ref, o_ref, slab_k, slab_v, sem,
                        *, n_pages, page, past, n_blk, n_pick, tq):
    b = pl.program_id(0)
    slot = b % 2

    def fetch(bb, s, wait):
        def body(pg, carry):
            pid = pt_ref[bb * n_pages + pg]
            dst = pl.ds(pl.multiple_of(pg * page, page), page)
            for pool, slab, i in ((pk_ref, slab_k, 0), (pv_ref, slab_v, 1)):
                cp = pltpu.make_async_copy(pool.at[pid], slab.at[s, :, dst], sem.at[s, i])
                cp.wait() if wait else cp.start()
            return carry
        lax.fori_loop(0, n_pages, body, 0)

    @pl.when(b == 0)
    def _():
        fetch(b, slot, False)

    @pl.when(b + 1 < pl.num_programs(0))
    def _():
        fetch(b + 1, 1 - slot, False)

    rows = Q_PER_KV * DEC_Q
    qi = lax.broadcasted_iota(I32, (rows, 1), 0) % DEC_Q
    t_col = past + qi % tq
    hl = lax.broadcasted_iota(I32, (rows, 1), 0) // DEC_Q
    nc = ck_ref.shape[1]
    cmp_end = lax.broadcasted_iota(I32, (1, nc), 1) * CMP_STRIDE + (CMP_LEN - 1)
    dist_c = t_col - cmp_end
    mask_c = dist_c >= 0
    dist_cf = dist_c.astype(F32)
    blk_pad = mimp_ref.shape[1]
    t8 = past + lax.broadcasted_iota(I32, (DEC_Q, 1), 0) % tq
    blk_l = lax.broadcasted_iota(I32, (DEC_Q, blk_pad), 1)
    cur = t8 // SEL_LEN
    forced = (blk_l == 0) | (blk_l == cur) | (blk_l == cur - 1)
    future = blk_l * SEL_LEN > t8
    pad_blk = blk_l >= n_blk
    b_sub = lax.broadcasted_iota(I32, (blk_pad, blk_pad), 0)
    b_lane = lax.broadcasted_iota(I32, (blk_pad, blk_pad), 1)
    n_chunks = past // DEC_CHUNK
    new_rows = ksn_ref.shape[2]

    sels, slopes, o_cmps = [], [], []
    for g in range(N_KV):
        lanes = slice(g * HEAD_DIM, (g + 1) * HEAD_DIM)
        slope = jnp.zeros((rows, 1), F32)
        for i in range(Q_PER_KV):
            slope = jnp.where(hl == i, SLOPES[g * Q_PER_KV + i], slope)
        slopes.append(slope)
        qg = (q_ref[0, g] * (HEAD_DIM ** -0.5)).astype(BF16)
        p = _masked_softmax(_bdot_nt(qg, ck_ref[0, :, lanes]) - slope * dist_cf, mask_c)
        o_cmps.append(_bdot(p, cv_ref[0, :, lanes]))
        p_sum = p[0:DEC_Q]
        for i in range(1, Q_PER_KV):
            p_sum = p_sum + p[i * DEC_Q:(i + 1) * DEC_Q]
        imp = jnp.dot(p_sum, mimp_ref[...], precision=HIGHEST, preferred_element_type=F32)
        imp = jnp.where(pad_blk, -3e38, jnp.where(future, -FORCE, jnp.where(forced, FORCE, imp)))
        imp_t = jnp.transpose(jnp.concatenate([imp, jnp.zeros((LANES - DEC_Q, blk_pad), F32)], axis=0))
        sel_rows = []
        for r in range(DEC_Q):
            col = imp_t[:, r:r + 1]
            row = imp[r:r + 1, :]
            ahead = (col > row) | ((col == row) & (b_sub < b_lane))
            cnt = jnp.sum(jnp.where(ahead, 1.0, 0.0), axis=0, keepdims=True)
            sel_rows.append(jnp.where(cnt < n_pick, 1.0, 0.0))
        sel8 = jnp.concatenate(sel_rows, axis=0)
        sels.append(jnp.concatenate([sel8] * Q_PER_KV, axis=0))

    fetch(b, slot, True)

    for g in range(N_KV):
        lanes = slice(g * HEAD_DIM, (g + 1) * HEAD_DIM)
        slope2 = slopes[g] * LOG2E
        qg2 = (q_ref[0, g] * (HEAD_DIM ** -0.5 * LOG2E)).astype(BF16)
        sel = sels[g]
        sel_b = sel.astype(BF16)

        def bias_of(dist, visible):
            return jnp.where(visible, 0.0, NEG_INF) - slope2 * dist.astype(F32)

        m, acc = _flash_init(rows)
        per_mat = LANES * SEL_LEN // DEC_CHUNK
        for c in range(n_chunks):
            keys = slice(c * DEC_CHUNK, (c + 1) * DEC_CHUNK)
            lane0 = (c // per_mat) * LANES
            msel = jnp.dot(sel_b[:, lane0:lane0 + LANES], esel_ref[c % per_mat], preferred_element_type=F32)
            dist = t_col - (c * DEC_CHUNK + lax.broadcasted_iota(I32, (1, DEC_CHUNK), 1))
            m, acc = _flash_step(qg2, slab_k[slot, lanes, keys].astype(BF16), _aug_values(slab_v[slot, lanes, keys]),
                                 bias_of(dist, (dist >= 0) & (msel > 0.5)), m, acc)
        dist_n = t_col - (past + lax.broadcasted_iota(I32, (1, new_rows), 1))
        cur_blk = past // SEL_LEN
        m, acc = _flash_step(qg2, ksn_ref[0, lanes, :].astype(BF16), _aug_values(vsn_ref[0, lanes, :]),
                             bias_of(dist_n, (dist_n >= 0) & (sel[:, cur_blk:cur_blk + 1] > 0.5)), m, acc)
        o_slc = _flash_out(acc)
        wlen = wk_ref.shape[2]
        dist = t_col - (past - wlen + lax.broadcasted_iota(I32, (1, wlen), 1))
        m, acc = _flash_step(qg2, wk_ref[0, lanes, :].astype(BF16), _aug_values(wv_ref[0, lanes, :]),
                             bias_of(dist, (dist >= 0) & (dist <= WINDOW)), *_flash_init(rows))
        m, acc = _flash_step(qg2, kwn_ref[0, lanes, :].astype(BF16), _aug_values(vwn_ref[0, lanes, :]),
                             bias_of(dist_n, (dist_n >= 0) & (dist_n <= WINDOW)), m, acc)
        o_win = _flash_out(acc)
        gt = g_ref[0, g]
        o_ref[0, g] = gt[:, 0:1] * o_cmps[g] + gt[:, 1:2] * o_slc + gt[:, 2:3] * o_win


def _attn_decode(page_table, q, gates, ck, cv, ksn, vsn, kwn, vwn, win_k, win_v, pool_k, pool_v):
    nb, tq, att = q.shape
    n_pages = page_table.shape[1]
    page = pool_k.shape[2]
    past = n_pages * page
    assert tq <= DEC_Q and past % DEC_CHUNK == 0 and past % SEL_LEN == 0 and (past + tq - 1) // SEL_LEN == past // SEL_LEN
    assert win_k.shape[2] == WINDOW and ck.shape[1] * CMP_STRIDE == past and page % LANES == 0
    t_pad = -(-(past + tq) // SEL_LEN) * SEL_LEN
    n_blk = t_pad // SEL_LEN
    n_cmp = t_pad // CMP_STRIDE - 1
    n_pick = min(N_SEL, n_blk)
    blk_pad = -(-n_blk // LANES) * LANES
    mimp = _importance_matrix(ck.shape[1], n_cmp, n_blk, blk_pad)
    per_mat = LANES * SEL_LEN // DEC_CHUNK
    esel = _expand_matrix(per_mat, DEC_CHUNK, [0] * per_mat)

    def dec_rows(a, width):
        a = a.reshape(nb, tq, N_KV, Q_PER_KV, width)
        a = jnp.take(a, jnp.arange(DEC_Q) % tq, axis=1)
        return a.transpose(0, 2, 3, 1, 4).reshape(nb, N_KV, Q_PER_KV * DEC_Q, width)

    q_d = dec_rows(q, HEAD_DIM)
    g_d = dec_rows(gates[:, :, :N_HEADS * N_GATES], N_GATES)
    pad_new = lambda a: jnp.pad(a.transpose(0, 2, 1), ((0, 0), (0, 0), (0, LANES - tq)))
    ksn, vsn, kwn, vwn = [pad_new(a) for a in (ksn, vsn, kwn, vwn)]
    rows = Q_PER_KV * DEC_Q
    blk = lambda a: pl.BlockSpec((1,) + a.shape[1:], lambda i, pt: (i,) + (0,) * (a.ndim - 1))
    const = lambda a: pl.BlockSpec(a.shape, lambda i, pt: (0,) * a.ndim)
    any_spec = pl.BlockSpec(memory_space=pl.ANY)
    grid_spec = pltpu.PrefetchScalarGridSpec(
        num_scalar_prefetch=1, grid=(nb,),
        in_specs=[blk(q_d), blk(g_d), blk(ck), blk(cv), blk(ksn), blk(vsn), blk(kwn), blk(vwn), blk(win_k), blk(win_v),
                  any_spec, any_spec, const(mimp), const(esel)],
        out_specs=pl.BlockSpec((1, N_KV, rows, HEAD_DIM), lambda i, pt: (i, 0, 0, 0)),
        scratch_shapes=[pltpu.VMEM((2, KV_DIM, past), F32), pltpu.VMEM((2, KV_DIM, past), F32),
                        pltpu.SemaphoreType.DMA((2, 2))])
    o = pl.pallas_call(
        functools.partial(_attn_decode_kernel, n_pages=n_pages, page=page, past=past, n_blk=n_blk, n_pick=n_pick, tq=tq),
        grid_spec=grid_spec,
        out_shape=jax.ShapeDtypeStruct((nb, N_KV, rows, HEAD_DIM), F32),
        compiler_params=_cparams("arbitrary"),
        name="attn_decode",
    )(page_table.reshape(-1), q_d, g_d, ck, cv, ksn, vsn, kwn, vwn, win_k, win_v, pool_k, pool_v, mimp, esel)
    o = o.reshape(nb, N_KV, Q_PER_KV, DEC_Q, HEAD_DIM)[:, :, :, :tq]
    return o.transpose(0, 3, 1, 2, 4).reshape(nb, tq, att)


def _out_kernel(x_ref, conv_ref, att_ref, mod_ref, bc_ref, ba_ref, woc_ref, woa_ref, g1_ref, b1_ref, wr_ref, br_ref,
                cnt0_ref, x1_ref, u2_ref, idx_ref, w_ref, pos_ref, cnt_ref, carry_ref, *, tt, alpha, per_row):
    first = (pl.program_id(0) == 0) & (pl.program_id(1) == 0)

    @pl.when(first)
    def _():
        carry_ref[...] = cnt0_ref[...]

    gate1 = _mod_rows(mod_ref, 2, per_row)
    shift2 = _mod_rows(mod_ref, 3, per_row)
    scale2 = _mod_rows(mod_ref, 4, per_row)
    mix = _bdot(conv_ref[0] * bc_ref[...], woc_ref[...]) + _bdot(att_ref[0] * ba_ref[...], woa_ref[...])
    x1 = _layer_norm(alpha * x_ref[0] + gate1 * mix, g1_ref[...], b1_ref[...])
    x1_ref[0] = x1
    u2 = x1 * (1.0 + scale2) + shift2
    _store_tiled(u2_ref, 0, u2)
    logits = jnp.dot(u2, wr_ref[...], precision=HIGHEST, preferred_element_type=F32) + br_ref[...]
    lane = lax.broadcasted_iota(I32, logits.shape, 1)
    vals = logits
    top_v, top_i = [], []
    for _k in range(TOP_K):
        m = jnp.max(vals, axis=-1, keepdims=True)
        i = jnp.min(jnp.where(vals == m, lane, LANES), axis=-1, keepdims=True)
        top_v.append(m)
        top_i.append(i)
        vals = jnp.where(lane == i, -jnp.inf, vals)
    e = [jnp.exp(v - top_v[0]) for v in top_v]
    inv = 1.0 / (e[0] + e[1] + e[2] + e[3])
    onehot = [jnp.where(lane == i, 1.0, 0.0) for i in top_i]
    assigned = onehot[0] + onehot[1] + onehot[2] + onehot[3]
    r_i = lax.broadcasted_iota(I32, (tt, tt), 0)
    c_i = lax.broadcasted_iota(I32, (tt, tt), 1)
    before = jnp.where(r_i > c_i, 1.0, 0.0).astype(BF16)
    rank = carry_ref[...] + jnp.dot(before, assigned.astype(BF16), preferred_element_type=F32)
    idx_o = jnp.zeros(logits.shape, I32)
    w_o = jnp.zeros(logits.shape, F32)
    pos_o = jnp.zeros(logits.shape, F32)
    for k in range(TOP_K):
        idx_o = jnp.where(lane == k, top_i[k], idx_o)
        w_o = jnp.where(lane == k, e[k] * inv, w_o)
        pos_o = jnp.where(lane == k, jnp.sum(onehot[k] * rank, axis=-1, keepdims=True), pos_o)
    idx_ref[0] = idx_o
    w_ref[0] = w_o
    pos_ref[0] = pos_o.astype(I32)
    carry_ref[...] = carry_ref[...] + jnp.sum(assigned, axis=0, keepdims=True)
    cnt_ref[...] = carry_ref[...]


def _out_proj(x, conv_out, att, mod, vecs, wts, cnt0, *, tt, alpha, per_row):
    b, t, d = x.shape
    bc, ba, g1, b1, br = vecs
    woc, woa, wr = wts
    nt = t // tt
    const = lambda a: pl.BlockSpec(a.shape, lambda i, j: (0,) * a.ndim)
    tok = lambda n: pl.BlockSpec((1, tt, n), lambda i, j: (i, j, 0))
    mod_spec = (pl.BlockSpec((6, tt, d), lambda i, j: (0, j, 0)) if per_row
                else pl.BlockSpec((1, 6, d), lambda i, j: (i, 0, 0)))
    tok_shape = lambda n, dt: jax.ShapeDtypeStruct((b, t, n), dt)
    return pl.pallas_call(
        functools.partial(_out_kernel, tt=tt, alpha=alpha, per_row=per_row),
        grid=(b, nt),
        in_specs=[tok(d), tok(conv_out.shape[2]), tok(att.shape[2]), mod_spec, const(bc), const(ba), const(woc), const(woa),
                  const(g1), const(b1), const(wr), const(br), const(cnt0)],
        out_specs=[tok(d), pl.BlockSpec((tt * SUBLANES, LANES), lambda i, j: (i * nt + j, 0)),
                   tok(LANES), tok(LANES), tok(LANES), pl.BlockSpec((1, LANES), lambda i, j: (0, 0))],
        out_shape=[tok_shape(d, F32), jax.ShapeDtypeStruct((b * t * SUBLANES, LANES), F32), tok_shape(LANES, I32),
                   tok_shape(LANES, F32), tok_shape(LANES, I32),
                   jax.ShapeDtypeStruct((1, LANES), F32)],
        scratch_shapes=[pltpu.VMEM((1, LANES), F32)],
        compiler_params=_cparams("arbitrary", "arbitrary"),
        name="out_proj_router",
    )(x, conv_out, att, mod, bc, ba, woc, woa, g1, b1, wr, br, cnt0)


def _tile_rows(x):
    return [x[:, j * LANES:(j + 1) * LANES] for j in range(x.shape[1] // LANES)]


def _store_tiled(ref, at, x):
    rows = x.shape[0]
    for j, piece in enumerate(_tile_rows(x)):
        ref[pl.ds(at + j, rows, stride=SUBLANES), :] = piece


def _load_tiled(ref, at, rows):
    return jnp.concatenate([ref[pl.ds(at + j, rows, stride=SUBLANES), :] for j in range(SUBLANES)], axis=1)


def _gather_pipeline(i, n, src_ref, cur_ref, nxt_ref, bufs, sem, rows, compute):
    def row_copy(src, buf, s, r):
        at = r * SUBLANES if isinstance(r, int) else pl.multiple_of(r * SUBLANES, SUBLANES)
        return pltpu.make_async_copy(src_ref.at[pl.ds(pl.multiple_of(src, SUBLANES), SUBLANES)],
                                     buf.at[pl.ds(at, SUBLANES)], sem.at[s])

    def whole(buf, s):
        return pltpu.make_async_copy(src_ref.at[pl.ds(0, rows * SUBLANES)], buf, sem.at[s])

    @pl.when(i == 0)
    def _():
        def body(r, carry):
            row_copy(cur_ref[0, 0, r], bufs[0], 0, r).start()
            return carry
        lax.fori_loop(0, rows, body, 0)

    def step(s):
        whole(bufs[s], s).wait()
        for r in range(rows):
            row_copy(nxt_ref[0, 0, r], bufs[1 - s], 1 - s, r).start(priority=r % 2)
        compute(bufs[s])

        @pl.when(i == n - 1)
        def _():
            whole(bufs[1 - s], 1 - s).wait()

    for s in range(2):
        pl.when((i % 2 == s) & (i < n))(functools.partial(step, s))


def _expert_kernel(be_ref, nu_ref, cur_ref, nxt_ref, u_ref, wgu_ref, bgu_ref, wdn_ref, bdn_ref, y_ref, xb0, xb1, sem, *, bm, d_ff):
    def compute(buf):
        gu = jnp.dot(_load_tiled(buf, 0, bm).astype(BF16), wgu_ref[0].astype(BF16), preferred_element_type=F32) + bgu_ref[0]
        gl = jnp.minimum(gu[:, :d_ff], SWIGLU_LIMIT)
        lin = jnp.clip(gu[:, d_ff:], -SWIGLU_LIMIT, SWIGLU_LIMIT)
        act = gl * _sigmoid(SWIGLU_ALPHA * gl) * (lin + 1.0)
        _store_tiled(y_ref, 0, jnp.dot(act.astype(BF16), wdn_ref[0].astype(BF16), preferred_element_type=F32) + bdn_ref[0])

    _gather_pipeline(pl.program_id(0), nu_ref[0], u_ref, cur_ref, nxt_ref, (xb0, xb1), sem, bm, compute)

    @pl.when(pl.program_id(0) >= nu_ref[0])
    def _():
        y_ref[...] = jnp.zeros(y_ref.shape, F32)


def _expert_ffn(u2, slot_tok, blk_e, n_used, wgu, bgu, wdn, bdn, *, bm):
    n_blocks = slot_tok.shape[0]
    d = wgu.shape[1]
    assert d == SUBLANES * LANES
    d_ff = wdn.shape[1]
    slot_tok = slot_tok.reshape(n_blocks, 1, bm)
    smem = lambda f: pl.BlockSpec((1, 1, bm), f, memory_space=pltpu.SMEM)
    grid_spec = pltpu.PrefetchScalarGridSpec(
        num_scalar_prefetch=2, grid=(n_blocks,),
        in_specs=[smem(lambda i, be, nu: (i, 0, 0)),
                  smem(lambda i, be, nu: (jnp.minimum(i + 1, n_blocks - 1), 0, 0)),
                  pl.BlockSpec(memory_space=pl.ANY),
                  pl.BlockSpec((1, d, 2 * d_ff), lambda i, be, nu: (be[i], 0, 0)),
                  pl.BlockSpec((1, 1, 2 * d_ff), lambda i, be, nu: (be[i], 0, 0)),
                  pl.BlockSpec((1, d_ff, d), lambda i, be, nu: (be[i], 0, 0)),
                  pl.BlockSpec((1, 1, d), lambda i, be, nu: (be[i], 0, 0))],
        out_specs=pl.BlockSpec((bm * SUBLANES, LANES), lambda i, be, nu: (i, 0)),
        scratch_shapes=[pltpu.VMEM((bm * SUBLANES, LANES), F32), pltpu.VMEM((bm * SUBLANES, LANES), F32),
                        pltpu.SemaphoreType.DMA((2,))])
    return pl.pallas_call(
        functools.partial(_expert_kernel, bm=bm, d_ff=d_ff),
        grid_spec=grid_spec,
        out_shape=jax.ShapeDtypeStruct((n_blocks * bm * SUBLANES, LANES), F32),
        compiler_params=_cparams("arbitrary"),
        name="expert_ffn",
    )(blk_e, n_used, slot_tok, slot_tok, u2, wgu, bgu, wdn, bdn)


def _combine_kernel(cur_ref, nxt_ref, ys_ref, w_ref, x1_ref, mod_ref, g2_ref, b2_ref, y_ref, gb0, gb1, sem,
                    *, tc, alpha, per_row):
    def compute(buf):
        w = w_ref[...]
        f = w[:, 0:1] * _load_tiled(buf, 0, tc)
        for k in range(1, TOP_K):
            f = f + w[:, k:k + 1] * _load_tiled(buf, k * tc * SUBLANES, tc)
        gate2 = mod_ref[5] if per_row else mod_ref[0, 5:6, :]
        y_ref[...] = _layer_norm(alpha * x1_ref[...] + gate2 * f, g2_ref[...], b2_ref[...])

    _gather_pipeline(pl.program_id(0), pl.num_programs(0), ys_ref, cur_ref, nxt_ref, (gb0, gb1), sem, TOP_K * tc, compute)


def _combine(ys, dest_t, top_w, x1, mod, g2, b2, *, tc, alpha, per_row, tiles_per_batch):
    n, d = x1.shape
    n_tiles = n // tc
    dest_t = dest_t.reshape(n_tiles, 1, TOP_K * tc)
    smem = lambda f: pl.BlockSpec((1, 1, TOP_K * tc), f, memory_space=pltpu.SMEM)
    const = lambda a: pl.BlockSpec(a.shape, lambda i: (0,) * a.ndim)
    mod_spec = (pl.BlockSpec((6, tc, d), lambda i: (0, i, 0)) if per_row
                else pl.BlockSpec((1, 6, d), lambda i: (i // tiles_per_batch, 0, 0)))
    return pl.pallas_call(
        functools.partial(_combine_kernel, tc=tc, alpha=alpha, per_row=per_row),
        grid=(n_tiles,),
        in_specs=[smem(lambda i: (i, 0, 0)), smem(lambda i: (jnp.minimum(i + 1, n_tiles - 1), 0, 0)),
                  pl.BlockSpec(memory_space=pl.ANY),
                  pl.BlockSpec((tc, LANES), lambda i: (i, 0)), pl.BlockSpec((tc, d), lambda i: (i, 0)),
                  mod_spec, const(g2), const(b2)],
        out_specs=pl.BlockSpec((tc, d), lambda i: (i, 0)),
        out_shape=jax.ShapeDtypeStruct((n, d), F32),
        scratch_shapes=[pltpu.VMEM((TOP_K * tc * SUBLANES, LANES), F32), pltpu.VMEM((TOP_K * tc * SUBLANES, LANES), F32),
                        pltpu.SemaphoreType.DMA((2,))],
        compiler_params=_cparams("arbitrary"),
        name="combine_ln2",
    )(dest_t, dest_t, ys, top_w, x1, mod, g2, b2)


def _slot_table_kernel(dest_ref, init_ref, out_ref, sem, *, tt):
    i = pl.program_id(0)

    @pl.when(i == 0)
    def _():
        cp = pltpu.make_async_copy(init_ref, out_ref, sem.at[0])
        cp.start()
        cp.wait()

    unroll = min(tt, SUBLANES)

    def body(g, carry):
        for u in range(unroll):
            r = g * unroll + u
            for k in range(TOP_K):
                out_ref[dest_ref[0, 0, k * tt + r]] = (i * tt + r) * SUBLANES
        return carry

    lax.fori_loop(0, tt // unroll, body, 0)


def _slot_table(dest, n_slots, *, tt):
    n = dest.shape[1]
    init = (jnp.arange(n_slots, dtype=I32) % n) * SUBLANES
    dest = dest.reshape(TOP_K, n // tt, tt).transpose(1, 0, 2)
    return pl.pallas_call(
        functools.partial(_slot_table_kernel, tt=tt),
        grid=(n // tt,),
        in_specs=[pl.BlockSpec((1, 1, tt * TOP_K), lambda i: (i, 0, 0), memory_space=pltpu.SMEM),
                  pl.BlockSpec(memory_space=pl.ANY)],
        out_specs=pl.BlockSpec(memory_space=pltpu.SMEM),
        out_shape=jax.ShapeDtypeStruct((n_slots,), I32),
        scratch_shapes=[pltpu.SemaphoreType.DMA((1,))],
        compiler_params=_cparams("arbitrary"),
        name="slot_table",
    )(dest.reshape(n // tt, 1, tt * TOP_K), init)


def _moe_experts(u2, top_i, pos, counts, experts, *, bm):
    n = top_i.shape[1]
    wgu, bgu, wdn, bdn = experts
    n_exp = wgu.shape[0]
    n_blocks = -(-n * TOP_K // bm) + n_exp
    counts = counts.astype(I32)
    padded = (counts + bm - 1) // bm * bm
    pad_end = jnp.cumsum(padded)
    pad_start = pad_end - padded
    dest = pos
    for e in range(n_exp):
        dest = dest + jnp.where(top_i == e, pad_start[e], 0)
    slot_tok = _slot_table(dest, n_blocks * bm, tt=_pick_tile(n, 2048))
    blk_start = jnp.arange(n_blocks, dtype=I32) * bm
    blk_e = jnp.minimum(jnp.sum((pad_end[None, :] <= blk_start[:, None]).astype(I32), axis=1), n_exp - 1)
    n_used = jnp.maximum(pad_end[-1:] // bm, 1)
    ys = _expert_ffn(u2, slot_tok.reshape(n_blocks, bm), blk_e, n_used, wgu, bgu, wdn, bdn, bm=bm)
    return ys, dest * SUBLANES


def _moe_combine(ys, dest, top_w, x1, mod, g2, b2, *, tc, alpha, per_row, tiles_per_batch):
    n = x1.shape[0]
    dest_t = dest.reshape(TOP_K, n // tc, tc).transpose(1, 0, 2).reshape(n // tc, TOP_K * tc)
    return _combine(ys, dest_t, top_w, x1, mod, g2, b2, tc=tc, alpha=alpha, per_row=per_row, tiles_per_batch=tiles_per_batch)


def _pick_tile(n, target):
    t = min(n, target)
    while n % t:
        t -= 1
    return t


def kernel(x_prompt, x_sample, c_prompt, c_sample, cache_cmp_k, cache_cmp_v, cache_slc_k, cache_slc_v, state_win_k, state_win_v, state_conv, page_table, w_ada, b_ada, w_in, w_dw, b_dw, conv_ln_g, conv_ln_b, w_ck1, w_ck2, pe_k, w_cv1, w_cv2, pe_v, beta_conv, beta_attn, w_out, ln1_g, ln1_b, w_router, b_router, w_gate_up, b_gate_up, w_down, b_down, ln2_g, ln2_b):
    depth = w_ada.shape[0]
    assert depth == 1, "single-layer trunk"
    alpha = (2 * depth) ** 0.25
    bp, seq, d = x_prompt.shape
    bs, tq, _ = x_sample.shape
    c_conv = w_dw.shape[2]
    att_dim = N_HEADS * HEAD_DIM
    n_pool, page = cache_cmp_k.shape[1], cache_cmp_k.shape[2]
    assert state_conv.shape[2] == CONV_K - 1 and page % CMP_STRIDE == 0

    wi = w_in[0]
    cuts = [2 * c_conv, 2 * c_conv + att_dim, 2 * c_conv + att_dim + 6 * KV_DIM]
    wa = wi[:, :cuts[0]].astype(BF16)
    wq = wi[:, cuts[0]:cuts[1]].astype(BF16)
    wkv = wi[:, cuts[1]:cuts[2]].astype(BF16)
    wg = jnp.pad(wi[:, cuts[2]:], ((0, 0), (0, LANES - N_HEADS * N_GATES))).astype(BF16)
    in_w = (wa, wq, wkv, wg)
    wdw = jnp.pad(w_dw[0], ((0, CONV_HIST - CONV_K), (0, 0)))
    row = lambda a: a.reshape(1, -1)
    conv_vecs = (wdw, row(b_dw[0]), row(conv_ln_g[0]), row(conv_ln_b[0]))
    wk_c = _compress_weights(w_ck1[0], w_ck2[0], pe_k[0])
    wv_c = _compress_weights(w_cv1[0], w_cv2[0], pe_v[0])
    out_vecs = (row(beta_conv[0]), row(beta_attn[0]), row(ln1_g[0]), row(ln1_b[0]),
                jnp.pad(row(b_router[0]), ((0, 0), (0, LANES - N_EXPERTS)), constant_values=NEG_INF))
    out_w = (w_out[0][:c_conv].astype(BF16), w_out[0][c_conv:].astype(BF16),
             jnp.pad(w_router[0], ((0, 0), (0, LANES - N_EXPERTS))))
    experts = (w_gate_up[0], b_gate_up[0][:, None, :], w_down[0], b_down[0][:, None, :])
    g2, b2 = row(ln2_g[0]), row(ln2_b[0])

    mod = _ada(jnp.concatenate([c_prompt, c_sample], axis=0), w_ada[0], row(b_ada[0])).reshape(bp + bs, 6, d)
    mod_p = mod[:bp]
    mod_s = jnp.repeat(mod[bp:], tq, axis=0).transpose(1, 0, 2)

    tt = _pick_tile(seq, 512)
    hist0 = jnp.zeros((bp, CONV_HIST, c_conv), F32)
    (conv_p, tail_p, q_p, kc_p, vc_p, kct_p, vct_p, kst_p, vst_p, kwt_p, vwt_p, gates_p) = _in_proj(
        x_prompt, mod_p, in_w, (hist0,) + conv_vecs, tt=tt, per_row=False)
    n_chunk = seq // CMP_STRIDE
    ck_p, cv_p = _compress_prompt(kc_p.reshape(bp, n_chunk, CHUNK_ROW), vc_p.reshape(bp, n_chunk, CHUNK_ROW), wk_c, wv_c)
    att_p = _attn_prompt(q_p, gates_p, ck_p, cv_p, kst_p, vst_p, kwt_p, vwt_p)
    to = _pick_tile(seq, 512)
    cnt0 = jnp.zeros((1, LANES), F32)
    n_p, n_s = bp * seq, bs * tq
    x1_p, u2_p, ti_p, tw_p, pos_p, cnt_p = _out_proj(x_prompt, conv_p, att_p, mod_p, out_vecs, out_w, cnt0,
                                                     tt=to, alpha=alpha, per_row=False)
    picks = lambda a: a.reshape(-1, LANES)[:, :TOP_K].T
    ys_p, dest_p = _moe_experts(u2_p, picks(ti_p), picks(pos_p), cnt_p[0, :N_EXPERTS], experts, bm=_pick_tile(n_p * TOP_K, 512))
    tc_p = _pick_tile(seq, 128)
    y_p = _moe_combine(ys_p, dest_p, tw_p.reshape(n_p, LANES), x1_p.reshape(n_p, d), mod_p, g2, b2,
                       tc=tc_p, alpha=alpha, per_row=False, tiles_per_batch=seq // tc_p).reshape(bp, seq, d)

    (glu_s, q_s, kc_s, vc_s, ks_s, vs_s, kw_s, vw_s, gates_s) = _in_proj(
        x_sample.reshape(1, n_s, d), mod_s, in_w, None, tt=n_s, per_row=True)
    unflat = lambda a: a.reshape(bs, tq, a.shape[-1])
    glu_s, q_s, kc_s, vc_s, ks_s, vs_s, kw_s, vw_s, gates_s = map(unflat, (glu_s, q_s, kc_s, vc_s, ks_s, vs_s, kw_s, vw_s, gates_s))
    xc_s = jnp.concatenate([jnp.zeros((bs, CONV_HIST - (CONV_K - 1), c_conv), F32), state_conv[0], glu_s], axis=1)
    conv_s = _conv_small(xc_s, *conv_vecs, tt=tq)
    dim_major = lambda a: a.transpose(0, 2, 3, 1).reshape(a.shape[0], KV_DIM, a.shape[1])
    ck_s, cv_s = _compress_decode(page_table, dim_major(cache_cmp_k[0]), dim_major(cache_cmp_v[0]), wk_c, wv_c)
    att_s = _attn_decode(page_table, q_s, gates_s, ck_s, cv_s, ks_s, vs_s, kw_s, vw_s, dim_major(state_win_k[0]),
                         dim_major(state_win_v[0]), dim_major(cache_slc_k[0]), dim_major(cache_slc_v[0]))
    flat = lambda a: a.reshape(1, n_s, a.shape[-1])
    x1_s, u2_s, ti_s, tw_s, pos_s, cnt_s = _out_proj(flat(x_sample), flat(conv_s), flat(att_s), mod_s, out_vecs, out_w, cnt0,
                                                     tt=n_s, alpha=alpha, per_row=True)
    bm_s = max(2 * SUBLANES, min(512, n_s * TOP_K // N_EXPERTS // SUBLANES * SUBLANES))
    ys_s, dest_s = _moe_experts(u2_s, picks(ti_s), picks(pos_s), cnt_s[0, :N_EXPERTS], experts, bm=bm_s)
    y_s = _moe_combine(ys_s, dest_s, tw_s.reshape(n_s, LANES), x1_s.reshape(n_s, d), mod_s, g2, b2,
                       tc=n_s, alpha=alpha, per_row=True, tiles_per_batch=1).reshape(bs, tq, d)

    heads = lambda a: a.reshape(1, a.shape[0], a.shape[1], N_KV, HEAD_DIM)
    win_keep = min(WINDOW, seq)
    heads_t = lambda a: a.reshape(1, a.shape[0], N_KV, HEAD_DIM, a.shape[2]).transpose(0, 1, 4, 2, 3)
    new_p = (heads_t(kct_p), heads_t(vct_p), heads_t(kst_p), heads_t(vst_p), heads_t(kwt_p[:, :, seq - win_keep:]),
             heads_t(vwt_p[:, :, seq - win_keep:]), tail_p[None, :, CONV_HIST - (CONV_K - 1):])
    win_buf = state_win_k.shape[2]
    keep = lambda old, new: jnp.concatenate([old[0], new.reshape(bs, tq, N_KV, HEAD_DIM)], axis=1)[None, :, -win_buf:]
    new_s = (heads(kc_s), heads(vc_s), heads(ks_s), heads(vs_s), keep(state_win_k, kw_s), keep(state_win_v, vw_s),
             jnp.concatenate([state_conv[0], glu_s], axis=1)[None, :, -(CONV_K - 1):])
    return (y_p, y_s) + new_p + new_s
```
